```python
import math
import jax, jax.numpy as jnp
from jax import lax
import numpy as np

D_MODEL = 1024
BATCH = 2
SEQ = 8192
DEPTH = 2
DEC_BATCH = 32
DEC_SEQ = 1
PAST_LEN = 16384
PAGE_SIZE = 128

MIX_WIDTH = D_MODEL
N_GROUPS = 4
GROUP_W = MIX_WIDTH // N_GROUPS
LRU_HEADS = 4
LRU_HD = GROUP_W // LRU_HEADS
LRU_CONV = 4
LRU_C = 8.0
RWKV_HEADS = 4
RWKV_HD = GROUP_W // RWKV_HEADS
DECAY_LORA = 64
AAA_LORA = 64
GATE_LORA = 128
RWKV_PROJ = 3 * GROUP_W + DECAY_LORA + AAA_LORA + GATE_LORA
RWKV_SPLITS = [GROUP_W, 2 * GROUP_W, 3 * GROUP_W, 3 * GROUP_W + DECAY_LORA,
               3 * GROUP_W + DECAY_LORA + AAA_LORA]
RWKV_LN_EPS = 64e-5
SCONV_W = 3
NSA_HEADS = 4
NSA_HD = GROUP_W // NSA_HEADS
CMP_LEN = 32
CMP_STRIDE = 16
CMP_HIDDEN = 2 * NSA_HD
SEL_BLOCK = 64
N_SELECT = 16
WINDOW = 512
Q_BLOCK = 128
FORCE_BONUS = 1e4
NEG_INF = -1e30
NSA_PROJ = NSA_HEADS * NSA_HD + 6 * NSA_HD + 3 * NSA_HEADS
NSA_SPLITS = [NSA_HEADS * NSA_HD + i * NSA_HD for i in range(7)]
D_FF = 2816
ROPE_THETA = 10000.0
EPS = 1e-6
PROJ_A = 2 * GROUP_W
PROJ_C = 3 * GROUP_W
PROJ_W = PROJ_A + RWKV_PROJ + PROJ_C + NSA_PROJ
GROUP_SPLITS = [PROJ_A, PROJ_A + RWKV_PROJ, PROJ_A + RWKV_PROJ + PROJ_C]

kernel_name = 'hymba_rglru_rwkv7_shortconv_nsa_step'


def _rmsnorm(x, g):
    xf = x.astype(jnp.float32)
    y = xf * lax.rsqrt(jnp.mean(xf * xf, axis=-1, keepdims=True) + EPS)
    return (y * g.astype(jnp.float32)).astype(x.dtype)


def _rope(x, pos):
    half = x.shape[-1] // 2
    inv = ROPE_THETA ** (-jnp.arange(half, dtype=jnp.float32) / half)
    ang = pos.astype(jnp.float32)[:, None] * inv[None, :]
    cos, sin = jnp.cos(ang), jnp.sin(ang)
    xf = x.astype(jnp.float32)
    x1, x2 = xf[..., :half], xf[..., half:]
    return jnp.concatenate([x1 * cos - x2 * sin, x2 * cos + x1 * sin], axis=-1).astype(x.dtype)


def _causal_conv(x, buf, w):
    K = w.shape[0]
    T = x.shape[1]
    xp = jnp.concatenate([buf.astype(x.dtype), x], axis=1)
    y = xp[:, 0:T] * w[0]
    for k in range(1, K):
        y = y + xp[:, k:k + T] * w[k]
    return y, xp[:, T:]


def _swiglu(x, wg, wu, wd):
    return (jax.nn.silu(x @ wg) * (x @ wu)) @ wd


def _masked_softmax(s, mask):
    s = jnp.where(mask, s.astype(jnp.float32), NEG_INF)
    m = jnp.max(s, axis=-1, keepdims=True)
    e = jnp.exp(s - m) * mask
    return e / jnp.maximum(jnp.sum(e, axis=-1, keepdims=True), 1e-30)


def _lin_combine(e1, e2):
    a1, b1 = e1
    a2, b2 = e2
    return a1 * a2, a2 * b1 + b2


def _rglru_mixer(xa, ga, conv_buf, h0, pos, w_conv, b_conv, w_r, b_r, w_i, b_i, lam):
    B, T, C = xa.shape
    xc, new_buf = _causal_conv(xa, conv_buf, w_conv)
    xc = xc + b_conv
    xh = xc.reshape(B, T, LRU_HEADS, LRU_HD)
    r = jax.nn.sigmoid((jnp.einsum('bthi,hij->bthj', xh, w_r).reshape(B, T, C) + b_r).astype(jnp.float32))
    ig = jax.nn.sigmoid((jnp.einsum('bthi,hij->bthj', xh, w_i).reshape(B, T, C) + b_i).astype(jnp.float32))
    log_a = -LRU_C * r * jax.nn.softplus(-lam.astype(jnp.float32))
    a = jnp.exp(log_a)
    mult = jnp.where((pos == 0)[None, :, None], 1.0, jnp.sqrt(-jnp.expm1(2.0 * log_a)))
    b = mult * ig * xc.astype(jnp.float32)
    b = b.at[:, 0].add(a[:, 0] * h0.astype(jnp.float32))
    _, h = lax.associative_scan(_lin_combine, (a, b), axis=1)
    y = jax.nn.gelu(ga.astype(jnp.float32)) * h
    return y.astype(xa.dtype), h[:, -1].astype(xa.dtype), new_buf


def _rwkv7_mixer(p, shift0, S0, mu, w0, w2, a0, a2, g2, k_k, k_a, r_k, ln_g, ln_b):
    B, T, _ = p.shape
    f32 = jnp.float32
    p_prev = jnp.concatenate([shift0[:, None].astype(p.dtype), p[:, :-1]], axis=1)
    m = p + (p_prev - p) * mu
    r, k, v, wl, al, gl = jnp.split(m, RWKV_SPLITS, axis=-1)
    w = -jax.nn.softplus(-(w0 + jnp.tanh(wl) @ w2).astype(f32)) - 0.5
    log_decay = -jnp.exp(w)
    a = jax.nn.sigmoid((a0 + al @ a2).astype(f32))
    g = jax.nn.sigmoid(gl) @ g2
    hd = lambda t: t.reshape(B, T, RWKV_HEADS, RWKV_HD)
    kk = hd((k * k_k).astype(f32))
    kk = kk * lax.rsqrt(jnp.maximum(jnp.sum(kk * kk, axis=-1, keepdims=True), 1e-24))
    kf = k.astype(f32) * (1.0 + (a - 1.0) * k_a)
    rh, kh, vh, ah, wh = hd(r.astype(f32)), hd(kf), hd(v.astype(f32)), hd(a), hd(log_decay)

    def step(S, inp):
        rt, wt, kt, vt, at_, bt = inp
        S = (S * jnp.exp(wt)[:, :, None, :]
             + jnp.einsum('bhvk,bhk->bhv', S, at_)[..., None] * bt[:, :, None, :]
             + vt[..., None] * kt[:, :, None, :])
        return S, jnp.einsum('bhvk,bhk->bhv', S, rt)

    seq = lambda t: jnp.moveaxis(t, 1, 0)
    S, y = lax.scan(step, S0.astype(f32), (seq(rh), seq(wh), seq(kh), seq(vh), seq(-kk), seq(kk * ah)))
    y = jnp.moveaxis(y, 0, 1)
    mean = jnp.mean(y, axis=-1, keepdims=True)
    var = jnp.mean(jnp.square(y - mean), axis=-1, keepdims=True)
    y = ((y - mean) * lax.rsqrt(var + RWKV_LN_EPS)).reshape(B, T, GROUP_W) * ln_g + ln_b
    bonus = jnp.sum(rh * kh * r_k, axis=-1, keepdims=True) * vh
    y = (y + bonus.reshape(B, T, GROUP_W)) * g
    return y.astype(p.dtype), p[:, -1], S.astype(p.dtype)


def _sconv_mixer(bg, cg, xin, buf, w):
    y, new_buf = _causal_conv(cg * xin, buf, w)
    return bg * y, new_buf


def _compress(rows, pe, w1, b1, w2, b2):
    B, P, hd = rows.shape
    n = (P - CMP_LEN) // CMP_STRIDE + 1
    idx = np.arange(n)[:, None] * CMP_STRIDE + np.arange(CMP_LEN)[None, :]
    blk = (rows[:, idx] + pe).reshape(B, n, CMP_LEN * hd)
    return jax.nn.gelu(blk @ w1 + b1) @ w2 + b2


def _nsa_core(q, pos0, kc, vc, ks, vs, kw, vw, win_pos0, gates, norm_kc, pe, w1, b1, w2, b2):
    B, H, T, hd = q.shape
    P = kc.shape[1]
    scale = hd ** -0.5
    n_cmp = (P - CMP_LEN) // CMP_STRIDE + 1
    cmp_end = np.arange(n_cmp) * CMP_STRIDE + CMP_LEN - 1
    k_cmp = _rope(_rmsnorm(_compress(kc, pe[0], w1[0], b1[0], w2[0], b2[0]), norm_kc), jnp.asarray(cmp_end))
    v_cmp = _compress(vc, pe[1], w1[1], b1[1], w2[1], b2[1])
    n_sel = -(-P // SEL_BLOCK)
    pad = n_sel * SEL_BLOCK - P
    ks_blk = jnp.pad(ks, ((0, 0), (0, pad), (0, 0))).reshape(B, n_sel, SEL_BLOCK, hd)
    vs_blk = jnp.pad(vs, ((0, 0), (0, pad), (0, 0))).reshape(B, n_sel, SEL_BLOCK, hd)
    cs = np.arange(n_cmp)[:, None] * CMP_STRIDE
    js = np.arange(n_sel)[None, :] * SEL_BLOCK
    overlap = jnp.asarray(((cs < js + SEL_BLOCK) & (cs + CMP_LEN > js)).astype(np.float32))
    n_top = min(N_SELECT, n_sel)
    kw_pad = jnp.pad(kw, ((0, 0), (WINDOW, 0), (0, 0)))
    vw_pad = jnp.pad(vw, ((0, 0), (WINDOW, 0), (0, 0)))
    qb = Q_BLOCK if T % Q_BLOCK == 0 else T
    nb = T // qb
    q_blocks = jnp.moveaxis(q.reshape(B, H, nb, qb, hd), 2, 0)
    g_blocks = jnp.moveaxis(gates.reshape(B, H, nb, qb, 3), 2, 0)
    take = jax.vmap(lambda blk, i: blk[i])

    def block(args):
        qblk, gblk, bi = args
        start = pos0 + bi * qb
        t = start + jnp.arange(qb)
        s = jnp.einsum('bhqd,bnd->bhqn', qblk, k_cmp) * scale
        p_cmp = _masked_softmax(s, cmp_end[None, :] <= t[:, None])
        o_cmp = jnp.einsum('bhqn,bnd->bhqd', p_cmp.astype(v_cmp.dtype), v_cmp)
        imp = jnp.einsum('bhqn,nj->bqj', p_cmp, overlap)
        jj = jnp.arange(n_sel)
        cur = (t // SEL_BLOCK)[:, None]
        forced = (jj == 0) | (jj == cur) | (jj == cur - 1)
        valid = jj * SEL_BLOCK <= t[:, None]
        score = jnp.where(valid, imp + jnp.where(forced, FORCE_BONUS, 0.0), -1.0)
        top_s, top_i = lax.top_k(score, n_top)
        k_sel = take(ks_blk, top_i).reshape(B, qb, n_top * SEL_BLOCK, hd)
        v_sel = take(vs_blk, top_i).reshape(B, qb, n_top * SEL_BLOCK, hd)
        kpos = (top_i[..., None] * SEL_BLOCK + jnp.arange(SEL_BLOCK)).reshape(B, qb, n_top * SEL_BLOCK)
        kvalid = jnp.repeat(top_s >= 0, SEL_BLOCK, axis=-1)
        smask = (kpos <= t[None, :, None]) & kvalid
        s = jnp.einsum('bhqd,bqkd->bhqk', qblk, k_sel) * scale
        o_sel = jnp.einsum('bhqk,bqkd->bhqd', _masked_softmax(s, smask[:, None]).astype(v_sel.dtype), v_sel)
        off = start - win_pos0
        k_win = lax.dynamic_slice_in_dim(kw_pad, off, WINDOW + qb, axis=1)
        v_win = lax.dynamic_slice_in_dim(vw_pad, off, WINDOW + qb, axis=1)
        wpos = start - WINDOW + jnp.arange(WINDOW + qb)
        rel = t[:, None] - wpos[None, :]
        wmask = (rel >= 0) & (rel < WINDOW) & (wpos[None, :] >= win_pos0)
        s = jnp.einsum('bhqd,bkd->bhqk', qblk, k_win) * scale
        o_win = jnp.einsum('bhqk,bkd->bhqd', _masked_softmax(s, wmask).astype(v_win.dtype), v_win)
        return gblk[..., 0:1] * o_cmp + gblk[..., 1:2] * o_sel + gblk[..., 2:3] * o_win

    out = lax.map(block, (q_blocks, g_blocks, jnp.arange(nb)))
    return jnp.moveaxis(out, 0, 2).reshape(B, H, T, hd)


def _nsa_mixer(pd, pos0, past_cmp, past_slc, win_buf, win_pos0, norm_q, norm_k, pe, w1, b1, w2, b2):
    B, T, _ = pd.shape
    pos = pos0 + jnp.arange(T)
    q, kc, vc, ks, vs, kw, vw, gl = jnp.split(pd, NSA_SPLITS, axis=-1)
    q = _rope(_rmsnorm(q.reshape(B, T, NSA_HEADS, NSA_HD), norm_q).transpose(0, 2, 1, 3), pos)
    ks = _rope(_rmsnorm(ks, norm_k[1]), pos)
    kw = _rope(_rmsnorm(kw, norm_k[2]), pos)
    new_cmp = jnp.concatenate([kc, vc], axis=-1)
    new_slc = jnp.concatenate([ks, vs], axis=-1)
    new_win = jnp.concatenate([kw, vw], axis=-1)
    if past_cmp is None:
        full_cmp, full_slc, win_rows = new_cmp, new_slc, new_win
    else:
        full_cmp = jnp.concatenate([past_cmp.astype(pd.dtype), new_cmp], axis=1)
        full_slc = jnp.concatenate([past_slc.astype(pd.dtype), new_slc], axis=1)
        win_rows = jnp.concatenate([win_buf.astype(pd.dtype), new_win], axis=1)
    gates = jax.nn.sigmoid(gl.reshape(B, T, NSA_HEADS, 3).astype(jnp.float32)).transpose(0, 2, 1, 3)
    o = _nsa_core(q, pos0, full_cmp[..., :NSA_HD], full_cmp[..., NSA_HD:],
                  full_slc[..., :NSA_HD], full_slc[..., NSA_HD:],
                  win_rows[..., :NSA_HD], win_rows[..., NSA_HD:], win_pos0, gates,
                  norm_k[0], pe, w1, b1, w2, b2)
    y = o.transpose(0, 2, 1, 3).reshape(B, T, GROUP_W).astype(pd.dtype)
    new_win_state = win_rows[:, -min(WINDOW, win_rows.shape[1]):]
    return y, new_cmp, new_slc, new_win_state


def _trunk_layer(x, pos0, lp, lru_h, lru_conv, rwkv_S, rwkv_shift, sconv_buf, win_buf, win_pos0, past_cmp, past_slc):
    B, T, _ = x.shape
    pos = pos0 + jnp.arange(T)
    h = _rmsnorm(x, lp['norm_ffn'][0])
    x = x + 0.5 * _swiglu(h, lp['ffn_w_gate'][0], lp['ffn_w_up'][0], lp['ffn_w_down'][0])
    h = _rmsnorm(x, lp['norm_mix'])
    proj = h @ lp['w_in']
    pa, pb, pc, pd = jnp.split(proj, GROUP_SPLITS, axis=-1)
    xa, ga = jnp.split(pa, 2, axis=-1)
    ya, lru_h, lru_conv = _rglru_mixer(xa, ga, lru_conv, lru_h, pos, lp['lru_conv_w'], lp['lru_conv_b'],
                                       lp['lru_w_r'], lp['lru_b_r'], lp['lru_w_i'], lp['lru_b_i'], lp['lru_lambda'])
    yb, rwkv_shift, rwkv_S = _rwkv7_mixer(pb, rwkv_shift, rwkv_S, lp['rwkv_mu'], lp['rwkv_w0'], lp['rwkv_w2'],
                                          lp['rwkv_a0'], lp['rwkv_a2'], lp['rwkv_g2'], lp['rwkv_k_k'],
                                          lp['rwkv_k_a'], lp['rwkv_r_k'], lp['rwkv_ln_g'], lp['rwkv_ln_b'])
    bg, cg, xin = jnp.split(pc, 3, axis=-1)
    yc, sconv_buf = _sconv_mixer(bg, cg, xin, sconv_buf, lp['sconv_w'])
    yd, new_cmp, new_slc, win_buf = _nsa_mixer(pd, pos0, past_cmp, past_slc, win_buf, win_pos0,
                                               lp['nsa_norm_q'], lp['nsa_norm_k'], lp['nsa_cmp_pe'],
                                               lp['nsa_cmp_w1'], lp['nsa_cmp_b1'], lp['nsa_cmp_w2'], lp['nsa_cmp_b2'])
    y = jnp.concatenate([ya, yb, yc, yd], axis=-1).reshape(B, T, N_GROUPS, GROUP_W)
    y = _rmsnorm(y, lp['out_norm'].reshape(N_GROUPS, GROUP_W)).reshape(B, T, MIX_WIDTH)
    x = x + y @ lp['w_out']
    h = _rmsnorm(x, lp['norm_ffn'][1])
    x = x + 0.5 * _swiglu(h, lp['ffn_w_gate'][1], lp['ffn_w_up'][1], lp['ffn_w_down'][1])
    return x, (lru_h, lru_conv, rwkv_S, rwkv_shift, sconv_buf, win_buf, new_cmp, new_slc)


def setup_inputs(seed: int = 0) -> dict:
    key = jax.random.key(seed)
    keys = iter(jax.random.split(key, 64))

    def nrm(shape, scale):
        return scale * jax.random.normal(next(keys), shape, jnp.float32)

    def gain(shape):
        return 1.0 + nrm(shape, 0.02)

    n_pages = PAST_LEN // PAGE_SIZE
    n_used = DEC_BATCH * n_pages
    n_phys = n_used + max(1, n_used // 4)
    win_len = min(WINDOW, PAST_LEN)
    kv_w = 2 * NSA_HD
    perm = jax.random.permutation(next(keys), n_phys)
    page_table = perm[:n_used].reshape(DEC_BATCH, n_pages).astype(jnp.int32)
    u = jax.random.uniform(next(keys), (DEPTH, GROUP_W), jnp.float32, 0.9, 0.999)
    a_base = u ** (1.0 / LRU_C)
    lru_lambda = jnp.log(a_base) - jnp.log1p(-a_base)
    rwkv_mu = jax.random.uniform(next(keys), (DEPTH, RWKV_PROJ), jnp.float32, 0.0, 1.0)
    rwkv_w0 = jax.random.uniform(next(keys), (DEPTH, GROUP_W), jnp.float32, -3.0, 0.0)
    return {
        'x_prompt': nrm((BATCH, SEQ, D_MODEL), 1.0),
        'x_sample': nrm((DEC_BATCH, DEC_SEQ, D_MODEL), 1.0),
        'state_lru_h': nrm((DEPTH, DEC_BATCH, GROUP_W), 0.5),
        'state_lru_conv': nrm((DEPTH, DEC_BATCH, LRU_CONV - 1, GROUP_W), 1.0),
        'state_rwkv_S': nrm((DEPTH, DEC_BATCH, RWKV_HEADS, RWKV_HD, RWKV_HD), 0.2),
        'state_rwkv_shift': nrm((DEPTH, DEC_BATCH, RWKV_PROJ), 1.0),
        'state_sconv': nrm((DEPTH, DEC_BATCH, SCONV_W - 1, GROUP_W), 1.0),
        'state_nsa_win': nrm((DEPTH, DEC_BATCH, win_len, kv_w), 1.0),
        'cache_nsa_cmp': nrm((DEPTH, n_phys, PAGE_SIZE, kv_w), 1.0),
        'cache_nsa_slc': nrm((DEPTH, n_phys, PAGE_SIZE, kv_w), 1.0),
        'page_table': page_table,
        'norm_ffn': gain((DEPTH, 2, D_MODEL)),
        'ffn_w_gate': nrm((DEPTH, 2, D_MODEL, D_FF), D_MODEL ** -0.5),
        'ffn_w_up': nrm((DEPTH, 2, D_MODEL, D_FF), D_MODEL ** -0.5),
        'ffn_w_down': nrm((DEPTH, 2, D_FF, D_MODEL), D_FF ** -0.5),
        'norm_mix': gain((DEPTH, D_MODEL)),
        'w_in': nrm((DEPTH, D_MODEL, PROJ_W), D_MODEL ** -0.5),
        'lru_conv_w': nrm((DEPTH, LRU_CONV, GROUP_W), 0.5),
        'lru_conv_b': nrm((DEPTH, GROUP_W), 0.02),
        'lru_w_r': nrm((DEPTH, LRU_HEADS, LRU_HD, LRU_HD), LRU_HD ** -0.5),
        'lru_b_r': nrm((DEPTH, GROUP_W), 0.02),
        'lru_w_i': nrm((DEPTH, LRU_HEADS, LRU_HD, LRU_HD), LRU_HD ** -0.5),
        'lru_b_i': nrm((DEPTH, GROUP_W), 0.02),
        'lru_lambda': lru_lambda,
        'rwkv_mu': rwkv_mu,
        'rwkv_w0': rwkv_w0,
        'rwkv_w2': nrm((DEPTH, DECAY_LORA, GROUP_W), 0.1),
        'rwkv_a0': nrm((DEPTH, GROUP_W), 0.1),
        'rwkv_a2': nrm((DEPTH, AAA_LORA, GROUP_W), AAA_LORA ** -0.5),
        'rwkv_g2': nrm((DEPTH, GATE_LORA, GROUP_W), GATE_LORA ** -0.5),
        'rwkv_k_k': 0.85 + nrm((DEPTH, GROUP_W), 0.02),
        'rwkv_k_a': gain((DEPTH, GROUP_W)),
        'rwkv_r_k': nrm((DEPTH, RWKV_HEADS, RWKV_HD), 0.1),
        'rwkv_ln_g': gain((DEPTH, GROUP_W)),
        'rwkv_ln_b': nrm((DEPTH, GROUP_W), 0.02),
        'sconv_w': nrm((DEPTH, SCONV_W, GROUP_W), 0.5),
        'nsa_norm_q': gain((DEPTH, NSA_HD)),
        'nsa_norm_k': gain((DEPTH, 3, NSA_HD)),
        'nsa_cmp_pe': nrm((DEPTH, 2, CMP_LEN, NSA_HD), 0.02),
        'nsa_cmp_w1': nrm((DEPTH, 2, CMP_LEN * NSA_HD, CMP_HIDDEN), (CMP_LEN * NSA_HD) ** -0.5),
        'nsa_cmp_b1': nrm((DEPTH, 2, CMP_HIDDEN), 0.02),
        'nsa_cmp_w2': nrm((DEPTH, 2, CMP_HIDDEN, NSA_HD), CMP_HIDDEN ** -0.5),
        'nsa_cmp_b2': nrm((DEPTH, 2, NSA_HD), 0.02),
        'out_norm': gain((DEPTH, MIX_WIDTH)),
        'w_out': nrm((DEPTH, MIX_WIDTH, D_MODEL), MIX_WIDTH ** -0.5),
    }


def reference(x_prompt, x_sample, state_lru_h, state_lru_conv, state_rwkv_S, state_rwkv_shift, state_sconv,
              state_nsa_win, cache_nsa_cmp, cache_nsa_slc, page_table, norm_ffn, ffn_w_gate, ffn_w_up, ffn_w_down,
              norm_mix, w_in, lru_conv_w, lru_conv_b, lru_w_r, lru_b_r, lru_w_i, lru_b_i, lru_lambda, rwkv_mu,
              rwkv_w0, rwkv_w2, rwkv_a0, rwkv_a2, rwkv_g2, rwkv_k_k, rwkv_k_a, rwkv_r_k, rwkv_ln_g, rwkv_ln_b,
              sconv_w, nsa_norm_q, nsa_norm_k, nsa_cmp_pe, nsa_cmp_w1, nsa_cmp_b1, nsa_cmp_w2, nsa_cmp_b2,
              out_norm, w_out):
    params = dict(norm_ffn=norm_ffn, ffn_w_gate=ffn_w_gate, ffn_w_up=ffn_w_up, ffn_w_down=ffn_w_down,
                  norm_mix=norm_mix, w_in=w_in, lru_conv_w=lru_conv_w, lru_conv_b=lru_conv_b, lru_w_r=lru_w_r,
                  lru_b_r=lru_b_r, lru_w_i=lru_w_i, lru_b_i=lru_b_i, lru_lambda=lru_lambda, rwkv_mu=rwkv_mu,
                  rwkv_w0=rwkv_w0, rwkv_w2=rwkv_w2, rwkv_a0=rwkv_a0, rwkv_a2=rwkv_a2, rwkv_g2=rwkv_g2,
                  rwkv_k_k=rwkv_k_k, rwkv_k_a=rwkv_k_a, rwkv_r_k=rwkv_r_k, rwkv_ln_g=rwkv_ln_g, rwkv_ln_b=rwkv_ln_b,
                  sconv_w=sconv_w, nsa_norm_q=nsa_norm_q, nsa_norm_k=nsa_norm_k, nsa_cmp_pe=nsa_cmp_pe,
                  nsa_cmp_w1=nsa_cmp_w1, nsa_cmp_b1=nsa_cmp_b1, nsa_cmp_w2=nsa_cmp_w2, nsa_cmp_b2=nsa_cmp_b2,
                  out_norm=out_norm, w_out=w_out)
    Bp, Tp, _ = x_prompt.shape
    Bs, Ts, _ = x_sample.shape
    dt = x_prompt.dtype
    n_pages = page_table.shape[1]
    past_len = n_pages * PAGE_SIZE
    win_len = state_nsa_win.shape[2]
    kv_w = 2 * NSA_HD
    yp, ys = x_prompt, x_sample
    sp_all, ss_all = [], []
    for l in range(DEPTH):
        lp = {name: arr[l] for name, arr in params.items()}
        yp, sp = _trunk_layer(yp, 0, lp,
                              jnp.zeros((Bp, GROUP_W), dt),
                              jnp.zeros((Bp, LRU_CONV - 1, GROUP_W), dt),
                              jnp.zeros((Bp, RWKV_HEADS, RWKV_HD, RWKV_HD), dt),
                              jnp.zeros((Bp, RWKV_PROJ), dt),
                              jnp.zeros((Bp, SCONV_W - 1, GROUP_W), dt),
                              None, 0, None, None)
        past_cmp = cache_nsa_cmp[l][page_table].reshape(Bs, past_len, kv_w)
        past_slc = cache_nsa_slc[l][page_table].reshape(Bs, past_len, kv_w)
        ys, ss = _trunk_layer(ys, past_len, lp, state_lru_h[l], state_lru_conv[l], state_rwkv_S[l],
                              state_rwkv_shift[l], state_sconv[l], state_nsa_win[l], past_len - win_len,
                              past_cmp, past_slc)
        sp_all.append(sp)
        ss_all.append(ss)
    lru_h_p = jnp.stack([s[0] for s in sp_all])
    lru_h_s = jnp.stack([s[0] for s in ss_all])
    lru_conv_p = jnp.stack([s[1] for s in sp_all])
    lru_conv_s = jnp.stack([s[1] for s in ss_all])
    rwkv_S_p = jnp.stack([s[2] for s in sp_all])
    rwkv_S_s = jnp.stack([s[2] for s in ss_all])
    rwkv_shift_p = jnp.stack([s[3] for s in sp_all])
    rwkv_shift_s = jnp.stack([s[3] for s in ss_all])
    sconv_p = jnp.stack([s[4] for s in sp_all])
    sconv_s = jnp.stack([s[4] for s in ss_all])
    nsa_win_p = jnp.stack([s[5] for s in sp_all])
    nsa_win_s = jnp.stack([s[5] for s in ss_all])
    nsa_cmp_p = jnp.stack([s[6] for s in sp_all]).reshape(DEPTH, Bp, Tp // PAGE_SIZE, PAGE_SIZE, kv_w)
    nsa_cmp_s = jnp.stack([s[6] for s in ss_all])
    nsa_slc_p = jnp.stack([s[7] for s in sp_all]).reshape(DEPTH, Bp, Tp // PAGE_SIZE, PAGE_SIZE, kv_w)
    nsa_slc_s = jnp.stack([s[7] for s in ss_all])
    return (yp, ys, lru_h_p, lru_h_s, lru_conv_p, lru_conv_s, rwkv_S_p, rwkv_S_s, rwkv_shift_p, rwkv_shift_s,
            sconv_p, sconv_s, nsa_win_p, nsa_win_s, nsa_cmp_p, nsa_cmp_s, nsa_slc_p, nsa_slc_s)
```

```python
import functools
import math

import numpy as np
import jax
import jax.numpy as jnp
from jax import lax
from jax.experimental import pallas as pl
from jax.experimental.pallas import tpu as pltpu

F32 = jnp.float32
BF16 = jnp.bfloat16

D_MODEL = 1024
GROUP_W = 256
N_HEADS = 4
HEAD_D = 64
LRU_C = 8.0
RWKV_LN_EPS = 64e-5
CMP_LEN = 32
CMP_STRIDE = 16
SEL_BLOCK = 64
N_SELECT = 16
WINDOW = 512
FORCE_BONUS = 1e4
NEG_INF = -1e30
ROPE_THETA = 10000.0
EPS = 1e-6
PAGE_SIZE = 128
PROJ_PAD = 3072
ATT_SCALE = HEAD_D ** -0.5

SUBLANES = 8
LANES = 128
VMEM_LIMIT = 56 * 1024 * 1024

COL_XA, COL_GA, COL_BG, COL_CG, COL_XIN, COL_Q = 4, 5, 6, 7, 8, 9
COL_CMP, COL_SLC, COL_WIN, COL_GL = 20, 21, 22, 23


def _params(*sem):
    return pltpu.CompilerParams(dimension_semantics=sem, vmem_limit_bytes=VMEM_LIMIT)


def _const_spec(shape):
    nd = len(shape)
    return pl.BlockSpec(shape, lambda *_: (0,) * nd)


def _dot(a, b):
    return jnp.dot(a, b, preferred_element_type=F32)


def _dot_nt(a, b):
    return lax.dot_general(a, b, (((1,), (1,)), ((), ())), preferred_element_type=F32)


def _dot_tn(a, b):
    return lax.dot_general(a, b, (((0,), (0,)), ((), ())), preferred_element_type=F32)


def _split3(x):
    h1 = x.astype(BF16)
    r1 = x - h1.astype(F32)
    h2 = r1.astype(BF16)
    h3 = (r1 - h2.astype(F32)).astype(BF16)
    return h1, h2, h3


def _dot_exact_rhs(x, m_bf16):
    h1, h2, h3 = _split3(x)
    return _dot(h1, m_bf16) + _dot(h2, m_bf16) + _dot(h3, m_bf16)


def _gelu_tanh(x):
    return x * (0.5 * (1.0 + jnp.tanh(math.sqrt(2.0 / math.pi) * (x + 0.044715 * (x * x * x)))))


def _sigmoid(x):
    return 1.0 / (1.0 + jnp.exp(-x))


def _softplus(x):
    return jnp.maximum(x, 0.0) + jnp.log1p(jnp.exp(-jnp.abs(x)))


def _rmsnorm_rows(x, g):
    ms = jnp.mean(x * x, axis=-1, keepdims=True)
    return x * lax.rsqrt(ms + EPS) * g


def _masked_softmax_rows(s, mask):
    s = jnp.where(mask, s, NEG_INF)
    m = jnp.max(s, axis=-1, keepdims=True)
    e = jnp.exp(s - m) * mask.astype(F32)
    return e / jnp.maximum(jnp.sum(e, axis=-1, keepdims=True), 1e-30)


def _rope_lanes(x, cos, sin_signed, lane):
    swapped = jnp.where((lane & (HEAD_D - 1)) < HEAD_D // 2,
                        pltpu.roll(x, LANES - HEAD_D // 2, 1), pltpu.roll(x, HEAD_D // 2, 1))
    return x * cos + swapped * sin_signed


def _ffn_body(x_ref, g_ref, wg_ref, wu_ref, wd_ref, o_ref, act_ref, *, f_chunk):
    x = x_ref[...]
    h = _rmsnorm_rows(x, g_ref[...]).astype(BF16)
    d_ff = wg_ref.shape[1]
    for c in range(d_ff // f_chunk):
        sl = slice(c * f_chunk, (c + 1) * f_chunk)
        gate = _dot(h, wg_ref[:, sl])
        up = _dot(h, wu_ref[:, sl])
        act_ref[:, sl] = (gate * _sigmoid(gate) * up).astype(BF16)
    o_ref[...] = x + 0.5 * _dot(act_ref[...], wd_ref[...])


def _ffn(x, g, wg, wu, wd, tm):
    m, d = x.shape
    d_ff = wg.shape[1]
    return pl.pallas_call(
        functools.partial(_ffn_body, f_chunk=256),
        grid=(m // tm,),
        in_specs=[pl.BlockSpec((tm, d), lambda i: (i, 0)), _const_spec((1, d)),
                  _const_spec((d, d_ff)), _const_spec((d, d_ff)), _const_spec((d_ff, d))],
        out_specs=pl.BlockSpec((tm, d), lambda i: (i, 0)),
        out_shape=jax.ShapeDtypeStruct((m, d), F32),
        scratch_shapes=[pltpu.VMEM((tm, d_ff), BF16)],
        compiler_params=_params("parallel"),
        name="ffn",
    )(x, g, wg, wu, wd)


def _proj_body(x_ref, g_ref, w_ref, o_ref, *, n_chunk):
    h = _rmsnorm_rows(x_ref[...], g_ref[...]).astype(BF16)
    for c in range(w_ref.shape[1] // n_chunk):
        sl = slice(c * n_chunk, (c + 1) * n_chunk)
        o_ref[:, sl] = _dot(h, w_ref[:, sl])


def _proj(x, g, w, tm):
    m, d = x.shape
    n = w.shape[1]
    return pl.pallas_call(
        functools.partial(_proj_body, n_chunk=512),
        grid=(m // tm,),
        in_specs=[pl.BlockSpec((tm, d), lambda i: (i, 0)), _const_spec((1, d)), _const_spec((d, n))],
        out_specs=pl.BlockSpec((tm, n), lambda i: (i, 0)),
        out_shape=jax.ShapeDtypeStruct((m, n), F32),
        compiler_params=_params("parallel"),
        name="proj",
    )(x, g, w)


def _lru_sconv_body(xa_ref, ga_ref, bg_ref, cg_ref, xin_ref, cbuf_ref, h0_ref, sbuf_ref,
                    cw_ref, cb_ref, wri_ref, bri_ref, lam_ref, sw_ref,
                    ya_ref, yc_ref, hlast_ref, ulast_ref,
                    extx_ref, extu_ref, h_ref, *, tt, pos0):
    t = pl.program_id(1)
    nt = pl.num_programs(1)

    @pl.when(t == 0)
    def _():
        extx_ref[0:SUBLANES, :] = cbuf_ref[...]
        extu_ref[0:SUBLANES, :] = sbuf_ref[...]
        h_ref[...] = h0_ref[...]

    x = xa_ref[...]
    extx_ref[SUBLANES:SUBLANES + tt, :] = x
    cw = cw_ref[...]
    xc = extx_ref[pl.ds(SUBLANES - 3, tt), :] * cw[0:1]
    xc = xc + extx_ref[pl.ds(SUBLANES - 2, tt), :] * cw[1:2]
    xc = xc + extx_ref[pl.ds(SUBLANES - 1, tt), :] * cw[2:3]
    xc = xc + x * cw[3:4]
    xc = xc + cb_ref[...]
    extx_ref[0:SUBLANES, :] = x[tt - SUBLANES:tt]

    gates = _dot(xc.astype(BF16), wri_ref[...]) + bri_ref[...]
    r = _sigmoid(gates[:, :GROUP_W])
    ig = _sigmoid(gates[:, GROUP_W:])
    log_a = -LRU_C * r * _softplus(-lam_ref[...])
    a = jnp.exp(log_a)
    rows = lax.broadcasted_iota(jnp.int32, (tt, GROUP_W), 0)
    mult = jnp.where(rows + (pos0 + t * tt) == 0, 1.0, jnp.sqrt(1.0 - jnp.exp(2.0 * log_a)))
    b = mult * ig * xc

    s = 1
    while s < tt:
        keep = rows >= s
        a_sh = jnp.where(keep, pltpu.roll(a, s, 0), 1.0)
        b_sh = jnp.where(keep, pltpu.roll(b, s, 0), 0.0)
        b = a * b_sh + b
        a = a * a_sh
        s *= 2
    h = a * h_ref[...] + b
    h_ref[...] = h[tt - 1:tt]
    ya_ref[...] = _gelu_tanh(ga_ref[...]) * h

    u = cg_ref[...] * xin_ref[...]
    extu_ref[SUBLANES:SUBLANES + tt, :] = u
    sw = sw_ref[...]
    yv = extu_ref[pl.ds(SUBLANES - 2, tt), :] * sw[0:1]
    yv = yv + extu_ref[pl.ds(SUBLANES - 1, tt), :] * sw[1:2]
    yv = yv + u * sw[2:3]
    yc_ref[...] = bg_ref[...] * yv
    extu_ref[0:SUBLANES, :] = u[tt - SUBLANES:tt]

    @pl.when(t == nt - 1)
    def _():
        hlast_ref[...] = h[tt - SUBLANES:tt]
        ulast_ref[...] = u[tt - SUBLANES:tt]


def _lru_sconv(proj3, cbuf8, h0, sbuf8, cw, cb, wri, bri, lam, sw, tt, pos0):
    nb, tp, _ = proj3.shape
    w = GROUP_W

    def col(c):
        return pl.BlockSpec((None, tt, w), lambda b, t, c=c: (b, t, c))

    state8 = pl.BlockSpec((None, SUBLANES, w), lambda b, t: (b, 0, 0))
    return pl.pallas_call(
        functools.partial(_lru_sconv_body, tt=tt, pos0=pos0),
        grid=(nb, tp // tt),
        in_specs=[col(COL_XA), col(COL_GA), col(COL_BG), col(COL_CG), col(COL_XIN),
                  state8, pl.BlockSpec((None, 1, w), lambda b, t: (b, 0, 0)), state8,
                  _const_spec((4, w)), _const_spec((1, w)), _const_spec((w, 2 * w)), _const_spec((1, 2 * w)),
                  _const_spec((1, w)), _const_spec((3, w))],
        out_specs=[pl.BlockSpec((None, tt, w), lambda b, t: (b, t, 0)),
                   pl.BlockSpec((None, tt, w), lambda b, t: (b, t, 0)), state8, state8],
        out_shape=[jax.ShapeDtypeStruct((nb, tp, w), F32), jax.ShapeDtypeStruct((nb, tp, w), F32),
                   jax.ShapeDtypeStruct((nb, SUBLANES, w), F32), jax.ShapeDtypeStruct((nb, SUBLANES, w), F32)],
        scratch_shapes=[pltpu.VMEM((tt + SUBLANES, w), F32), pltpu.VMEM((tt + SUBLANES, w), F32),
                        pltpu.VMEM((1, w), F32)],
        compiler_params=_params("parallel", "arbitrary"),
        name="lru_sconv",
    )(proj3, proj3, proj3, proj3, proj3, cbuf8, h0, sbuf8, cw, cb, wri, bri, lam, sw)


def _rwkv_body(p_ref, shift_ref, sin_ref, mu_ref, w0_ref, w2_ref, a0_ref, a2_ref, g2_ref, kk_ref, ka_ref,
               rk_ref, lng_ref, lnb_ref, j4_ref,
               y_ref, sout_ref,
               ext_ref, s_ref, w_buf, r_buf, k_buf, v_buf, a_buf, b_buf, y_buf, *, tt, chunk, t_valid):
    t = pl.program_id(1)
    nt = pl.num_programs(1)
    w = GROUP_W
    nh = N_HEADS

    @pl.when(t == 0)
    def _():
        ext_ref[0:SUBLANES, :] = shift_ref[...]
        s_ref[...] = sin_ref[...]

    p = p_ref[...]
    ext_ref[SUBLANES:SUBLANES + tt, :] = p
    m = p + (ext_ref[pl.ds(SUBLANES - 1, tt), :] - p) * mu_ref[...]
    ext_ref[0:SUBLANES, :] = p[tt - SUBLANES:tt]
    r = m[:, 0:w]
    k = m[:, w:2 * w]
    v = m[:, 2 * w:3 * w]
    wa = m[:, 3 * w:3 * w + LANES]
    gl = m[:, 3 * w + LANES:4 * w]
    wlin = w0_ref[...] + _dot(jnp.tanh(wa).astype(BF16), w2_ref[...])
    log_decay = -jnp.exp(-_softplus(-wlin) - 0.5)
    ag = _sigmoid(a0_ref[...] + _dot(wa.astype(BF16), a2_ref[...]))
    g = _dot(_sigmoid(gl).astype(BF16), g2_ref[...])
    j4 = j4_ref[...]
    kk = k * kk_ref[...]
    kk = kk * lax.rsqrt(jnp.maximum(_dot_exact_rhs(kk * kk, j4), 1e-24))
    kf = k * (1.0 + (ag - 1.0) * ka_ref[...])
    rows = lax.broadcasted_iota(jnp.int32, (tt, w), 0)
    if t_valid % tt != 0:
        live = rows + t * tt < t_valid
        log_decay = jnp.where(live, log_decay, 0.0)
        kk = jnp.where(live, kk, 0.0)
        kf = jnp.where(live, kf, 0.0)
        v = jnp.where(live, v, 0.0)

    rin = rows & (chunk - 1)
    cl = log_decay
    s = 1
    while s < chunk:
        cl = cl + jnp.where(rin >= s, pltpu.roll(cl, s, 0), 0.0)
        s *= 2
    e_neg = jnp.exp(-cl)
    w_buf[...] = cl
    r_buf[...] = r * jnp.exp(cl)
    a_buf[...] = -kk * jnp.exp(cl - log_decay)
    b_buf[...] = kk * ag * e_neg
    k_buf[...] = kf * e_neg
    v_buf[...] = v

    sl = nh * chunk
    hm_rows = lax.broadcasted_iota(jnp.int32, (sl, w), 0) // chunk
    hm_cols = lax.broadcasted_iota(jnp.int32, (sl, w), 1) // HEAD_D
    head_mask = (hm_rows == hm_cols).astype(F32)
    ri = lax.broadcasted_iota(jnp.int32, (sl, sl), 0)
    ci = lax.broadcasted_iota(jnp.int32, (sl, sl), 1)
    same = (ri // chunk) == (ci // chunk)
    strict = (same & ((ri & (chunk - 1)) > (ci & (chunk - 1)))).astype(F32)
    incl = (same & ((ri & (chunk - 1)) >= (ci & (chunk - 1)))).astype(F32)
    eye = (ri == ci).astype(F32)

    def stack(x):
        return jnp.concatenate([x] * nh, axis=0) * head_mask

    def chunk_step(c, carry):
        off = pl.multiple_of(c * chunk, chunk)
        cs = pl.ds(off, chunk)
        a_s = stack(a_buf[cs, :]).astype(BF16)
        r_s = stack(r_buf[cs, :]).astype(BF16)
        b_s = stack(b_buf[cs, :]).astype(BF16)
        k_s = stack(k_buf[cs, :]).astype(BF16)
        v_s = stack(v_buf[cs, :])
        v_sb = v_s.astype(BF16)
        n_mat = _dot_nt(a_s, b_s) * strict
        m_mat = _dot_nt(a_s, k_s) * strict
        p_mat = _dot_nt(r_s, b_s) * incl
        q_mat = _dot_nt(r_s, k_s) * incl
        t_mat = eye + n_mat
        x = n_mat
        step = 2
        while step < chunk:
            xb = x.astype(BF16)
            x = _dot(xb, xb)
            t_mat = t_mat + _dot(t_mat.astype(BF16), x.astype(BF16))
            step *= 2
        t_b = t_mat.astype(BF16)
        w_eff = _dot(t_b, a_s)
        z = _dot(t_b, _dot(m_mat.astype(BF16), v_sb).astype(BF16))
        s0 = s_ref[...]
        s0b = s0.astype(BF16)
        u = _dot_nt(w_eff.astype(BF16), s0b) + z
        ub = u.astype(BF16)
        y_s = _dot_nt(r_s, s0b) + _dot(p_mat.astype(BF16), ub) + _dot(q_mat.astype(BF16), v_sb)
        yc = y_s[0:chunk]
        for hh in range(1, nh):
            yc = yc + y_s[hh * chunk:(hh + 1) * chunk]
        y_buf[cs, :] = yc
        c_last = jnp.exp(w_buf[pl.ds(off + chunk - 1, 1), :])
        s_ref[...] = (s0 + _dot_tn(ub, b_s) + _dot_tn(v_sb, k_s)) * c_last
        return carry

    lax.fori_loop(0, tt // chunk, chunk_step, 0)

    y = y_buf[...]
    inv_hd = 1.0 / HEAD_D
    mean = _dot_exact_rhs(y, j4) * inv_hd
    yc = y - mean
    var = _dot_exact_rhs(yc * yc, j4) * inv_hd
    yn = yc * lax.rsqrt(var + RWKV_LN_EPS) * lng_ref[...] + lnb_ref[...]
    bonus = _dot_exact_rhs(r * kf * rk_ref[...], j4) * v
    y_ref[...] = (yn + bonus) * g

    @pl.when(t == nt - 1)
    def _():
        sout_ref[...] = s_ref[...]


def _rwkv(proj3, shift8, s_bd, mu, w0, w2p, a0, a2p, g2, kk, ka, rk, lng, lnb, j4, tt, chunk, t_valid):
    nb, tp, _ = proj3.shape
    w = GROUP_W
    pw = 4 * w
    vec = _const_spec((1, w))
    return pl.pallas_call(
        functools.partial(_rwkv_body, tt=tt, chunk=chunk, t_valid=t_valid),
        grid=(nb, tp // tt),
        in_specs=[pl.BlockSpec((None, tt, pw), lambda b, t: (b, t, 0)),
                  pl.BlockSpec((None, SUBLANES, pw), lambda b, t: (b, 0, 0)),
                  pl.BlockSpec((None, w, w), lambda b, t: (b, 0, 0)),
                  _const_spec((1, pw)), vec, _const_spec((LANES, w)), vec, _const_spec((LANES, w)),
                  _const_spec((LANES, w)), vec, vec, vec, vec, vec, _const_spec((w, w))],
        out_specs=[pl.BlockSpec((None, tt, w), lambda b, t: (b, t, 0)),
                   pl.BlockSpec((None, w, w), lambda b, t: (b, 0, 0))],
        out_shape=[jax.ShapeDtypeStruct((nb, tp, w), F32), jax.ShapeDtypeStruct((nb, w, w), F32)],
        scratch_shapes=[pltpu.VMEM((tt + SUBLANES, pw), F32), pltpu.VMEM((w, w), F32)]
        + [pltpu.VMEM((tt, w), F32)] * 7,
        compiler_params=_params("parallel", "arbitrary"),
        name="rwkv7",
    )(proj3, shift8, s_bd, mu, w0, w2p, a0, a2p, g2, kk, ka, rk, lng, lnb, j4)


def _nsa_prep_body(q_ref, cmp_ref, slc_ref, win_ref, gl_ref, cos_ref, sin_ref, gq_ref, gk_ref, j2_ref,
                   qo_ref, cmpo_ref, slco_ref, wino_ref, gate_ref):
    cos = cos_ref[...]
    sin = sin_ref[...]
    j2 = j2_ref[...]
    lane = lax.broadcasted_iota(jnp.int32, cos.shape, 1)
    inv_hd = 1.0 / HEAD_D
    gq = gq_ref[...]
    q = q_ref[...]
    halves = []
    for c in range(2):
        x = q[:, c * LANES:(c + 1) * LANES]
        ms = _dot_exact_rhs(x * x, j2) * inv_hd
        halves.append(_rope_lanes(x * lax.rsqrt(ms + EPS) * gq, cos, sin, lane))
    qo_ref[...] = jnp.concatenate(halves, axis=1)
    cmpo_ref[...] = cmp_ref[...]
    gk = gk_ref[...]
    is_key = lane < HEAD_D
    for src, dst, row in ((slc_ref, slco_ref, 0), (win_ref, wino_ref, 1)):
        x = src[...]
        ms = _dot_exact_rhs(x * x, j2) * inv_hd
        roped = _rope_lanes(x * lax.rsqrt(ms + EPS) * gk[row:row + 1], cos, sin, lane)
        dst[...] = jnp.where(is_key, roped, x)
    gate_ref[...] = _sigmoid(gl_ref[...])


def _nsa_prep(proj, cos, sin, gq, gk2, j2, tm, rows_per_table):
    m = proj.shape[0]
    nt_tab = rows_per_table // tm

    def col(c, wdt):
        return pl.BlockSpec((tm, wdt), lambda i, c=c: (i, c))

    tab = pl.BlockSpec((tm, LANES), lambda i: (i % nt_tab, 0))
    o128 = pl.BlockSpec((tm, LANES), lambda i: (i, 0))
    return pl.pallas_call(
        _nsa_prep_body,
        grid=(m // tm,),
        in_specs=[col(COL_Q, GROUP_W), col(COL_CMP, LANES), col(COL_SLC, LANES), col(COL_WIN, LANES),
                  col(COL_GL, LANES), tab, tab, _const_spec((1, LANES)), _const_spec((2, LANES)),
                  _const_spec((LANES, LANES))],
        out_specs=[pl.BlockSpec((tm, GROUP_W), lambda i: (i, 0)), o128, o128, o128, o128],
        out_shape=[jax.ShapeDtypeStruct((m, GROUP_W), F32)] + [jax.ShapeDtypeStruct((m, LANES), F32)] * 4,
        compiler_params=_params("parallel"),
        name="nsa_prep",
    )(proj, proj, proj, proj, proj, cos, sin, gq, gk2, j2)


def _compress_tail(hid_lo, hid_hi_next, b1_ref, w2_ref, b2_ref, gk_ref, cos_ref, sin_ref, j2_ref):
    hidden = _gelu_tanh(hid_lo + hid_hi_next + b1_ref[...])
    kv = _dot(hidden.astype(BF16), w2_ref[...]) + b2_ref[...]
    lane = lax.broadcasted_iota(jnp.int32, kv.shape, 1)
    ms = _dot_exact_rhs(kv * kv, j2_ref[...]) * (1.0 / HEAD_D)
    roped = _rope_lanes(kv * lax.rsqrt(ms + EPS) * gk_ref[...], cos_ref[...], sin_ref[...], lane)
    return jnp.where(lane < HEAD_D, roped, kv)


def _compress_body(x_ref, pelo_ref, pehi_ref, wlo_ref, whi_ref, b1_ref, w2_ref, b2_ref, gk_ref,
                   cos_ref, sin_ref, j2_ref, o_ref):
    x = x_ref[...]
    n_grp = x.shape[0]
    lo = _dot((x + pelo_ref[...]).astype(BF16), wlo_ref[...])
    hi = _dot((x + pehi_ref[...]).astype(BF16), whi_ref[...])
    hi_next = pltpu.roll(hi, n_grp - 1, 0)
    o_ref[...] = _compress_tail(lo, hi_next, b1_ref, w2_ref, b2_ref, gk_ref, cos_ref, sin_ref, j2_ref)


def _compress(x16, cw, cos, sin, j2):
    nb, n_grp, wid = x16.shape
    return pl.pallas_call(
        _compress_body,
        grid=(nb,),
        in_specs=[pl.BlockSpec((None, n_grp, wid), lambda b: (b, 0, 0)),
                  _const_spec((1, wid)), _const_spec((1, wid)),
                  _const_spec((wid, GROUP_W)), _const_spec((wid, GROUP_W)), _const_spec((1, GROUP_W)),
                  _const_spec((GROUP_W, LANES)), _const_spec((1, LANES)), _const_spec((1, LANES)),
                  _const_spec((n_grp, LANES)), _const_spec((n_grp, LANES)), _const_spec((LANES, LANES))],
        out_specs=pl.BlockSpec((None, n_grp, LANES), lambda b: (b, 0, 0)),
        out_shape=jax.ShapeDtypeStruct((nb, n_grp, LANES), F32),
        compiler_params=_params("parallel"),
        name="nsa_compress",
    )(x16, cw["pe_lo"], cw["pe_hi"], cw["w_lo"], cw["w_hi"], cw["b1"], cw["w2"], cw["b2"], cw["gk"], cos, sin, j2)


def _stack_heads(q, lane):
    parts = []
    for h in range(N_HEADS):
        blk = q[:, (h // 2) * LANES:(h // 2 + 1) * LANES]
        if h % 2 == 1:
            blk = pltpu.roll(blk, HEAD_D, 1)
        parts.append(jnp.where(lane < HEAD_D, blk, 0.0))
    return jnp.concatenate(parts, axis=0)


def _unstack_heads(parts, lane):
    b01 = jnp.where(lane < HEAD_D, pltpu.roll(parts[0], HEAD_D, 1), parts[1])
    b23 = jnp.where(lane < HEAD_D, pltpu.roll(parts[2], HEAD_D, 1), parts[3])
    return jnp.concatenate([b01, b23], axis=1)


def _nsa_attn_body(q_ref, gate_ref, kvc_ref, slc_ref, win_ref, ovl_ref, exp_ref, o_ref,
                   sc_ref, cnt_ref, m_ref, l_ref, acc_ref, *, tq, tk, n_cmp, n_sel, n_top, win_span):
    i = pl.program_id(1)
    start = i * tq
    rows4 = N_HEADS * tq
    lane = lax.broadcasted_iota(jnp.int32, (tq, LANES), 1)
    q4 = _stack_heads(q_ref[...], lane).astype(BF16)
    trow = start + (lax.broadcasted_iota(jnp.int32, (rows4, 1), 0) & (tq - 1))

    kvc = kvc_ref[...].astype(BF16)
    ng = kvc.shape[0]
    s = _dot_nt(q4, kvc) * ATT_SCALE
    nidx = lax.broadcasted_iota(jnp.int32, (rows4, ng), 1)
    p = _masked_softmax_rows(s, (nidx * CMP_STRIDE + (CMP_LEN - 1) <= trow) & (nidx < n_cmp))
    o_cmp = _dot(p.astype(BF16), kvc)
    psum = p[0:tq]
    for h in range(1, N_HEADS):
        psum = psum + p[h * tq:(h + 1) * tq]
    ph = psum.astype(BF16)
    pl_ = (psum - ph.astype(F32)).astype(BF16)
    ovl = ovl_ref[...]
    imp_t = _dot_nt(ovl, ph) + _dot_nt(ovl, pl_)

    nsp = imp_t.shape[0]
    jblk = lax.broadcasted_iota(jnp.int32, (nsp, tq), 0)
    tcol = start + lax.broadcasted_iota(jnp.int32, (nsp, tq), 1)
    cur = tcol // SEL_BLOCK
    forced = (jblk == 0) | (jblk == cur) | (jblk == cur - 1)
    valid = (jblk * SEL_BLOCK <= tcol) & (jblk < n_sel)
    score = jnp.where(valid, imp_t + jnp.where(forced, FORCE_BONUS, 0.0), -1.0)
    sc_ref[...] = score
    cnt_ref[...] = jnp.zeros((nsp, tq), F32)
    n_live = jnp.minimum((start + tq - 1) // SEL_BLOCK + 1, n_sel)

    def rank_step(r_, carry):
        row = sc_ref[pl.ds(r_, 1), :]
        sc = sc_ref[...]
        beats = (row > sc) | ((row == sc) & (r_ < jblk))
        cnt_ref[...] = cnt_ref[...] + jnp.where(beats, 1.0, 0.0)
        return carry

    lax.fori_loop(0, n_live, rank_step, 0)
    sel_t = jnp.where((cnt_ref[...] < n_top) & (score >= 0.0), 1.0, 0.0)
    sel = sel_t.T.astype(BF16)

    m_ref[...] = jnp.full((rows4, LANES), NEG_INF, F32)
    l_ref[...] = jnp.zeros((rows4, LANES), F32)
    acc_ref[...] = jnp.zeros((rows4, LANES), F32)
    n_kt = (start + tq + tk - 1) // tk

    def kt_step(kt, carry):
        off = pl.multiple_of(kt * tk, tk)
        kv = slc_ref[pl.ds(off, tk), :].astype(BF16)
        sk = _dot_nt(q4, kv) * ATT_SCALE
        km = _dot(sel, exp_ref[:, pl.ds(off, tk)])
        km4 = jnp.concatenate([km] * N_HEADS, axis=0)
        kpos = off + lax.broadcasted_iota(jnp.int32, (rows4, tk), 1)
        msk = (km4 > 0.5) & (kpos <= trow)
        sk = jnp.where(msk, sk, NEG_INF)
        m_prev = m_ref[:, 0:1]
        m_new = jnp.maximum(m_prev, jnp.max(sk, axis=-1, keepdims=True))
        e = jnp.exp(sk - m_new) * msk.astype(F32)
        alpha = jnp.exp(m_prev - m_new)
        l_ref[...] = alpha * l_ref[...] + jnp.sum(e, axis=-1, keepdims=True)
        acc_ref[...] = alpha * acc_ref[...] + _dot(e.astype(BF16), kv)
        m_ref[...] = jnp.broadcast_to(m_new, (rows4, LANES))
        return carry

    lax.fori_loop(0, n_kt, kt_step, 0)
    o_sel = acc_ref[...] / jnp.maximum(l_ref[...], 1e-30)

    start0 = pl.multiple_of(jnp.maximum(start + tq - win_span, 0), tq)
    kvw = win_ref[pl.ds(start0, win_span), :].astype(BF16)
    sw = _dot_nt(q4, kvw) * ATT_SCALE
    rel = trow - (start0 + lax.broadcasted_iota(jnp.int32, (rows4, win_span), 1))
    pw = _masked_softmax_rows(sw, (rel >= 0) & (rel < WINDOW))
    o_win = _dot(pw.astype(BF16), kvw)

    g = gate_ref[...]
    outs = []
    for h in range(N_HEADS):
        rs = slice(h * tq, (h + 1) * tq)
        outs.append(g[:, 3 * h:3 * h + 1] * o_cmp[rs] + g[:, 3 * h + 1:3 * h + 2] * o_sel[rs]
                    + g[:, 3 * h + 2:3 * h + 3] * o_win[rs])
    o_ref[...] = _unstack_heads(outs, lane)


def _nsa_attn(q, gates, kvc, slc, win, ovl_t, expand, tq, tk, n_cmp, n_sel):
    nb, t, _ = q.shape
    ng = kvc.shape[1]
    nsp = ovl_t.shape[0]
    rows4 = N_HEADS * tq
    win_span = WINDOW + tq
    return pl.pallas_call(
        functools.partial(_nsa_attn_body, tq=tq, tk=tk, n_cmp=n_cmp, n_sel=n_sel,
                          n_top=min(N_SELECT, n_sel), win_span=win_span),
        grid=(nb, t // tq),
        in_specs=[pl.BlockSpec((None, tq, GROUP_W), lambda b, i: (b, i, 0)),
                  pl.BlockSpec((None, tq, LANES), lambda b, i: (b, i, 0)),
                  pl.BlockSpec((None, ng, LANES), lambda b, i: (b, 0, 0)),
                  pl.BlockSpec((None, t, LANES), lambda b, i: (b, 0, 0)),
                  pl.BlockSpec((None, t, LANES), lambda b, i: (b, 0, 0)),
                  _const_spec((nsp, ng)), _const_spec((nsp, t))],
        out_specs=pl.BlockSpec((None, tq, GROUP_W), lambda b, i: (b, i, 0)),
        out_shape=jax.ShapeDtypeStruct((nb, t, GROUP_W), F32),
        scratch_shapes=[pltpu.VMEM((nsp, tq), F32), pltpu.VMEM((nsp, tq), F32),
                        pltpu.VMEM((rows4, LANES), F32), pltpu.VMEM((rows4, LANES), F32),
                        pltpu.VMEM((rows4, LANES), F32)],
        compiler_params=_params("parallel", "arbitrary"),
        name="nsa_attn",
    )(q, gates, kvc, slc, win, ovl_t, expand)


def _samp_cmp_body(*refs, pages_per_step, n_grp, n_cmp, n_sel, n_top, t_pos):
    pt_ref = refs[0]
    page_refs = refs[1:1 + pages_per_step]
    (pelo_ref, pehi_ref, wlo_ref, whi_ref, b1_ref, w2_ref, b2_ref, gk_ref, cos_ref, sin_ref, j2_ref,
     q4_ref, ovl_ref, ocmp_ref, idx_ref, xcat_ref, lo_ref, hi_ref) = refs[1 + pages_per_step:]
    del pt_ref
    s = pl.program_id(1)
    ns = pl.num_programs(1)
    grp_per_page = PAGE_SIZE // CMP_STRIDE
    rows_step = pages_per_step * grp_per_page

    @pl.when(s == 0)
    def _():
        hi_ref[n_grp:n_grp + SUBLANES, :] = jnp.zeros((SUBLANES, GROUP_W), F32)

    for j in range(pages_per_step):
        xcat_ref[j * grp_per_page:(j + 1) * grp_per_page, :] = page_refs[j][...]
    x = xcat_ref[...]
    off = pl.multiple_of(s * rows_step, rows_step)
    lo_ref[pl.ds(off, rows_step), :] = _dot((x + pelo_ref[...]).astype(BF16), wlo_ref[...])
    hi_ref[pl.ds(off, rows_step), :] = _dot((x + pehi_ref[...]).astype(BF16), whi_ref[...])

    @pl.when(s == ns - 1)
    def _():
        kvc = _compress_tail(lo_ref[...], hi_ref[pl.ds(1, n_grp), :], b1_ref, w2_ref, b2_ref, gk_ref,
                             cos_ref, sin_ref, j2_ref).astype(BF16)
        q4 = q4_ref[...].astype(BF16)
        sc = _dot_nt(q4, kvc) * ATT_SCALE
        nidx = lax.broadcasted_iota(jnp.int32, sc.shape, 1)
        p = _masked_softmax_rows(sc, (nidx * CMP_STRIDE + (CMP_LEN - 1) <= t_pos) & (nidx < n_cmp))
        ocmp_ref[...] = _dot(p.astype(BF16), kvc)
        hrow = lax.broadcasted_iota(jnp.int32, p.shape, 0)
        psum = jnp.sum(jnp.where(hrow < N_HEADS, p, 0.0), axis=0, keepdims=True)
        psum8 = jnp.broadcast_to(psum, p.shape)
        ph = psum8.astype(BF16)
        pl_ = (psum8 - ph.astype(F32)).astype(BF16)
        ovl = ovl_ref[...]
        imp = (_dot(ph, ovl) + _dot(pl_, ovl))[0:1]
        nsp = imp.shape[1]
        jrow = lax.broadcasted_iota(jnp.int32, (1, nsp), 1)
        cur = t_pos // SEL_BLOCK
        forced = (jrow == 0) | (jrow == cur) | (jrow == cur - 1)
        valid = (jrow * SEL_BLOCK <= t_pos) & (jrow < n_sel)
        score = jnp.where(valid, imp + jnp.where(forced, FORCE_BONUS, 0.0), -1.0)
        score = jnp.where(jrow < n_sel, score, -2.0)
        s_row = jnp.broadcast_to(score, (nsp, nsp))
        s_col = s_row.T
        ii = lax.broadcasted_iota(jnp.int32, (nsp, nsp), 0)
        jj = lax.broadcasted_iota(jnp.int32, (nsp, nsp), 1)
        beats = (s_col > s_row) | ((s_col == s_row) & (ii < jj))
        rank = jnp.sum(jnp.where(beats, 1.0, 0.0), axis=0, keepdims=True)
        chosen = (rank < n_top) & (score >= 0.0)
        slot = lax.broadcasted_iota(jnp.int32, (N_SELECT, nsp), 0).astype(F32)
        hit = (jnp.broadcast_to(rank, (N_SELECT, nsp)) == slot) & jnp.broadcast_to(chosen, (N_SELECT, nsp))
        jcol = lax.broadcasted_iota(jnp.int32, (N_SELECT, nsp), 1).astype(F32)
        blk = jnp.sum(jnp.where(hit, jcol, 0.0), axis=1, keepdims=True)
        cnt = jnp.sum(jnp.where(hit, 1.0, 0.0), axis=1, keepdims=True)
        blk = jnp.where(cnt > 0.5, blk, -1.0)
        idx_ref[...] = jnp.broadcast_to(blk, (N_SELECT, LANES)).astype(jnp.int32)


def _samp_cmp(cache16, page_table, cw, cos, sin, j2, q4, ovl, n_cmp, n_sel, t_pos, pages_per_step):
    nb, n_pages = page_table.shape
    grp_per_page = PAGE_SIZE // CMP_STRIDE
    wid = cache16.shape[2]
    n_grp = n_pages * grp_per_page
    nsp = ovl.shape[1]
    rows_step = pages_per_step * grp_per_page

    def page_spec(j):
        return pl.BlockSpec((None, grp_per_page, wid),
                            lambda b, s, pt, j=j: (pt[b, s * pages_per_step + j], 0, 0))

    def cst(shape):
        nd = len(shape)
        return pl.BlockSpec(shape, lambda b, s, pt: (0,) * nd)

    grid_spec = pltpu.PrefetchScalarGridSpec(
        num_scalar_prefetch=1,
        grid=(nb, n_pages // pages_per_step),
        in_specs=[page_spec(j) for j in range(pages_per_step)] + [
            cst((1, wid)), cst((1, wid)), cst((wid, GROUP_W)), cst((wid, GROUP_W)), cst((1, GROUP_W)),
            cst((GROUP_W, LANES)), cst((1, LANES)), cst((1, LANES)),
            cst((n_grp, LANES)), cst((n_grp, LANES)), cst((LANES, LANES)),
            pl.BlockSpec((None, SUBLANES, LANES), lambda b, s, pt: (b, 0, 0)), cst((n_grp, nsp))],
        out_specs=[pl.BlockSpec((None, SUBLANES, LANES), lambda b, s, pt: (b, 0, 0)),
                   pl.BlockSpec((None, N_SELECT, LANES), lambda b, s, pt: (b, 0, 0))],
        scratch_shapes=[pltpu.VMEM((rows_step, wid), F32),
                        pltpu.VMEM((n_grp, GROUP_W), F32), pltpu.VMEM((n_grp + SUBLANES, GROUP_W), F32)],
    )
    return pl.pallas_call(
        functools.partial(_samp_cmp_body, pages_per_step=pages_per_step, n_grp=n_grp, n_cmp=n_cmp,
                          n_sel=n_sel, n_top=min(N_SELECT, n_sel), t_pos=t_pos),
        grid_spec=grid_spec,
        out_shape=[jax.ShapeDtypeStruct((nb, SUBLANES, LANES), F32),
                   jax.ShapeDtypeStruct((nb, N_SELECT, LANES), jnp.int32)],
        compiler_params=_params("parallel", "arbitrary"),
        name="nsa_decode_compress",
    )(page_table, *([cache16] * pages_per_step), cw["pe_lo"], cw["pe_hi"], cw["w_lo"], cw["w_hi"], cw["b1"],
      cw["w2"], cw["b2"], cw["gk"], cos, sin, j2, q4, ovl)


def _samp_sel_body(pt_ref, idx_ref, blk_ref, q4_ref, nslc_ref, win_ref, nwin_ref, gate_ref, ocmp_ref, o_ref,
                   m_ref, l_ref, acc_ref, *, n_past_blk, t_pos, win_pos0):
    del pt_ref
    b = pl.program_id(0)
    k = pl.program_id(1)
    nk = pl.num_programs(1)
    q4 = q4_ref[...].astype(BF16)

    @pl.when(k == 0)
    def _():
        m_ref[...] = jnp.full(m_ref.shape, NEG_INF, F32)
        l_ref[...] = jnp.zeros(l_ref.shape, F32)
        acc_ref[...] = jnp.zeros(acc_ref.shape, F32)

    j = idx_ref[b, k]
    blk = blk_ref[...]
    first = jnp.where(j == n_past_blk, nslc_ref[...], blk[0:SUBLANES])
    kv = jnp.concatenate([first, blk[SUBLANES:]], axis=0).astype(BF16)
    sk = _dot_nt(q4, kv) * ATT_SCALE
    kpos = j * SEL_BLOCK + lax.broadcasted_iota(jnp.int32, sk.shape, 1)
    msk = (kpos <= t_pos) & (j >= 0)
    sk = jnp.where(msk, sk, NEG_INF)
    m_prev = m_ref[:, 0:1]
    m_new = jnp.maximum(m_prev, jnp.max(sk, axis=-1, keepdims=True))
    e = jnp.exp(sk - m_new) * msk.astype(F32)
    alpha = jnp.exp(m_prev - m_new)
    l_ref[...] = alpha * l_ref[...] + jnp.sum(e, axis=-1, keepdims=True)
    acc_ref[...] = alpha * acc_ref[...] + _dot(e.astype(BF16), kv)
    m_ref[...] = jnp.broadcast_to(m_new, m_ref.shape)

    @pl.when(k == nk - 1)
    def _():
        o_sel = acc_ref[...] / jnp.maximum(l_ref[...], 1e-30)
        wb = win_ref[...].astype(BF16)
        nw = nwin_ref[...].astype(BF16)
        s_w = _dot_nt(q4, wb) * ATT_SCALE
        s_n = _dot_nt(q4, nw) * ATT_SCALE
        wpos = win_pos0 + lax.broadcasted_iota(jnp.int32, s_w.shape, 1)
        rel = t_pos - wpos
        m_w = (rel >= 0) & (rel < WINDOW) & (wpos >= win_pos0)
        m_n = lax.broadcasted_iota(jnp.int32, s_n.shape, 1) == 0
        s_w = jnp.where(m_w, s_w, NEG_INF)
        s_n = jnp.where(m_n, s_n, NEG_INF)
        mx = jnp.maximum(jnp.max(s_w, axis=-1, keepdims=True), jnp.max(s_n, axis=-1, keepdims=True))
        e_w = jnp.exp(s_w - mx) * m_w.astype(F32)
        e_n = jnp.exp(s_n - mx) * m_n.astype(F32)
        den = jnp.sum(e_w, axis=-1, keepdims=True) + jnp.sum(e_n, axis=-1, keepdims=True)
        o_win = (_dot(e_w.astype(BF16), wb) + _dot(e_n.astype(BF16), nw)) / jnp.maximum(den, 1e-30)
        o_cmp = ocmp_ref[...]
        g = gate_ref[...]
        lane = lax.broadcasted_iota(jnp.int32, (1, LANES), 1)
        outs = []
        for h in range(N_HEADS):
            outs.append(g[0:1, 3 * h:3 * h + 1] * o_cmp[h:h + 1] + g[0:1, 3 * h + 1:3 * h + 2] * o_sel[h:h + 1]
                        + g[0:1, 3 * h + 2:3 * h + 3] * o_win[h:h + 1])
        o_ref[...] = jnp.broadcast_to(_unstack_heads(outs, lane), o_ref.shape)


def _samp_sel(cache_blk, page_table, top_idx, q4, nslc8, win, nwin8, gates8, ocmp, t_pos, win_pos0):
    nb, n_pages = page_table.shape
    blk_per_page = PAGE_SIZE // SEL_BLOCK
    n_past_blk = n_pages * blk_per_page
    wlen = win.shape[1]

    def blk_map(b, k, pt, idx):
        j = jnp.clip(idx[b, k], 0, n_past_blk - 1)
        return (pt[b, j // blk_per_page], j % blk_per_page, 0, 0)

    row8 = pl.BlockSpec((None, SUBLANES, LANES), lambda b, k, pt, idx: (b, 0, 0))
    grid_spec = pltpu.PrefetchScalarGridSpec(
        num_scalar_prefetch=2,
        grid=(nb, N_SELECT),
        in_specs=[pl.BlockSpec((None, None, SEL_BLOCK, LANES), blk_map), row8, row8,
                  pl.BlockSpec((None, wlen, LANES), lambda b, k, pt, idx: (b, 0, 0)), row8, row8, row8],
        out_specs=pl.BlockSpec((None, SUBLANES, GROUP_W), lambda b, k, pt, idx: (b, 0, 0)),
        scratch_shapes=[pltpu.VMEM((SUBLANES, LANES), F32)] * 3,
    )
    return pl.pallas_call(
        functools.partial(_samp_sel_body, n_past_blk=n_past_blk, t_pos=t_pos, win_pos0=win_pos0),
        grid_spec=grid_spec,
        out_shape=jax.ShapeDtypeStruct((nb, SUBLANES, GROUP_W), F32),
        compiler_params=_params("parallel", "arbitrary"),
        name="nsa_decode_select",
    )(page_table, top_idx, cache_blk, q4, nslc8, win, nwin8, gates8, ocmp)


def _mix_out_body(x_ref, ya_ref, yb_ref, yc_ref, yd_ref, g_ref, w_ref, o_ref):
    acc = x_ref[...]
    g = g_ref[...]
    for gi, y_ref in enumerate((ya_ref, yb_ref, yc_ref, yd_ref)):
        yn = _rmsnorm_rows(y_ref[...], g[gi:gi + 1]).astype(BF16)
        acc = acc + _dot(yn, w_ref[gi * GROUP_W:(gi + 1) * GROUP_W, :])
    o_ref[...] = acc


def _mix_out(x, ya, yb, yc, yd, g4, w, tm):
    m, d = x.shape
    yspec = pl.BlockSpec((tm, GROUP_W), lambda i: (i, 0))
    return pl.pallas_call(
        _mix_out_body,
        grid=(m // tm,),
        in_specs=[pl.BlockSpec((tm, d), lambda i: (i, 0)), yspec, yspec, yspec, yspec,
                  _const_spec((N_HEADS, GROUP_W)), _const_spec((N_HEADS * GROUP_W, d))],
        out_specs=pl.BlockSpec((tm, d), lambda i: (i, 0)),
        out_shape=jax.ShapeDtypeStruct((m, d), F32),
        compiler_params=_params("parallel"),
        name="mix_out",
    )(x, ya, yb, yc, yd, g4, w)


def _block_diag(blocks):
    n, a, b = blocks.shape
    eye = jnp.eye(n, dtype=blocks.dtype)
    return jnp.einsum("nab,nm->namb", blocks, eye).reshape(n * a, n * b)


def _rope_tables(pos):
    half = HEAD_D // 2
    inv = ROPE_THETA ** (-jnp.arange(half, dtype=F32) / half)
    ang = pos.astype(F32)[:, None] * inv[None, :]
    cos, sin = jnp.cos(ang), jnp.sin(ang)
    cos128 = jnp.concatenate([cos, cos, cos, cos], axis=1)
    sin128 = jnp.concatenate([-sin, sin, -sin, sin], axis=1)
    return cos128, sin128


def _seg_ones(n_seg):
    return _block_diag(jnp.ones((n_seg, HEAD_D, HEAD_D), F32)).astype(BF16)


def _layer_weights(lp):
    w_in = lp["w_in"]
    a_w, b_w, c_w = 2 * GROUP_W, 4 * GROUP_W, 3 * GROUP_W
    w_perm = jnp.concatenate(
        [w_in[:, a_w:a_w + b_w], w_in[:, :a_w], w_in[:, a_w + b_w:a_w + b_w + c_w], w_in[:, a_w + b_w + c_w:],
         jnp.zeros((w_in.shape[0], PROJ_PAD - w_in.shape[1]), F32)], axis=1).astype(BF16)
    lora = HEAD_D
    w1 = lp["nsa_cmp_w1"].reshape(2, 2, CMP_STRIDE, HEAD_D, 2 * HEAD_D)
    pe = lp["nsa_cmp_pe"].reshape(2, 2, CMP_STRIDE, HEAD_D)

    def cmp_half(hf):
        wk = jnp.zeros((CMP_STRIDE, 2 * HEAD_D, 4 * HEAD_D), F32)
        wk = wk.at[:, :HEAD_D, :2 * HEAD_D].set(w1[0, hf]).at[:, HEAD_D:, 2 * HEAD_D:].set(w1[1, hf])
        pk = jnp.concatenate([pe[0, hf], pe[1, hf]], axis=1)
        return wk.reshape(CMP_STRIDE * 2 * HEAD_D, 4 * HEAD_D).astype(BF16), pk.reshape(1, CMP_STRIDE * 2 * HEAD_D)

    w_lo, pe_lo = cmp_half(0)
    w_hi, pe_hi = cmp_half(1)
    ones64 = jnp.ones((HEAD_D,), F32)
    cw = dict(
        w_lo=w_lo, w_hi=w_hi, pe_lo=pe_lo, pe_hi=pe_hi,
        b1=lp["nsa_cmp_b1"].reshape(1, 4 * HEAD_D),
        w2=_block_diag(lp["nsa_cmp_w2"]).astype(BF16),
        b2=lp["nsa_cmp_b2"].reshape(1, 2 * HEAD_D),
        gk=jnp.concatenate([lp["nsa_norm_k"][0], ones64]).reshape(1, LANES),
    )
    zeros_lora = jnp.zeros((lora, GROUP_W), F32)
    return dict(
        norm_ffn=lp["norm_ffn"].reshape(2, 1, D_MODEL),
        wg=lp["ffn_w_gate"].astype(BF16), wu=lp["ffn_w_up"].astype(BF16), wd=lp["ffn_w_down"].astype(BF16),
        norm_mix=lp["norm_mix"].reshape(1, D_MODEL), w_in=w_perm,
        lru_cw=lp["lru_conv_w"], lru_cb=lp["lru_conv_b"].reshape(1, GROUP_W),
        lru_wri=jnp.concatenate([_block_diag(lp["lru_w_r"]), _block_diag(lp["lru_w_i"])], axis=1).astype(BF16),
        lru_bri=jnp.concatenate([lp["lru_b_r"], lp["lru_b_i"]]).reshape(1, 2 * GROUP_W),
        lru_lam=lp["lru_lambda"].reshape(1, GROUP_W), sconv_w=lp["sconv_w"],
        mu=lp["rwkv_mu"].reshape(1, 4 * GROUP_W), w0=lp["rwkv_w0"].reshape(1, GROUP_W),
        w2p=jnp.concatenate([lp["rwkv_w2"], zeros_lora], axis=0).astype(BF16),
        a0=lp["rwkv_a0"].reshape(1, GROUP_W),
        a2p=jnp.concatenate([zeros_lora, lp["rwkv_a2"]], axis=0).astype(BF16),
        g2=lp["rwkv_g2"].astype(BF16),
        k_k=lp["rwkv_k_k"].reshape(1, GROUP_W), k_a=lp["rwkv_k_a"].reshape(1, GROUP_W),
        r_k=lp["rwkv_r_k"].reshape(1, GROUP_W),
        ln_g=lp["rwkv_ln_g"].reshape(1, GROUP_W), ln_b=lp["rwkv_ln_b"].reshape(1, GROUP_W),
        gq=jnp.tile(lp["nsa_norm_q"], 2).reshape(1, LANES),
        gk2=jnp.stack([jnp.concatenate([lp["nsa_norm_k"][1], ones64]),
                       jnp.concatenate([lp["nsa_norm_k"][2], ones64])]),
        cw=cw,
        out_norm=lp["out_norm"].reshape(N_HEADS, GROUP_W), w_out=lp["w_out"].astype(BF16),
    )


def _overlap(n_grp, n_sel_pad, n_sel):
    cs = jnp.arange(n_grp)[:, None] * CMP_STRIDE
    js = jnp.arange(n_sel_pad)[None, :] * SEL_BLOCK
    ov = (cs < js + SEL_BLOCK) & (cs + CMP_LEN > js) & (jnp.arange(n_sel_pad)[None, :] < n_sel)
    return ov.astype(BF16)


def _state_to_bd(s):
    nb = s.shape[0]
    eye = jnp.eye(N_HEADS, dtype=s.dtype)
    return jnp.einsum("bhvk,hg->bhvgk", s, eye).reshape(nb, GROUP_W, GROUP_W)


def _bd_to_state(sbd):
    nb = sbd.shape[0]
    s5 = sbd.reshape(nb, N_HEADS, HEAD_D, N_HEADS, HEAD_D)
    return jnp.stack([s5[:, h, :, h, :] for h in range(N_HEADS)], axis=1)


def _pad_rows_front(x, rows):
    return jnp.pad(x, ((0, 0), (rows - x.shape[1], 0), (0, 0)))


def _round_up(x, m):
    return (x + m - 1) // m * m


def _mixers_recurrent(lw, proj3, cbuf8, h0, sbuf8, shift8, s_bd, j4, tt, chunk, pos0, t_valid):
    ya, yc, hlast, ulast = _lru_sconv(proj3, cbuf8, h0, sbuf8, lw["lru_cw"], lw["lru_cb"], lw["lru_wri"],
                                      lw["lru_bri"], lw["lru_lam"], lw["sconv_w"], tt, pos0)
    yb, s_out = _rwkv(proj3, shift8, s_bd, lw["mu"], lw["w0"], lw["w2p"], lw["a0"], lw["a2p"], lw["g2"],
                      lw["k_k"], lw["k_a"], lw["r_k"], lw["ln_g"], lw["ln_b"], j4, tt, chunk, t_valid)
    return ya, yb, yc, hlast, ulast, s_out


def _prompt_layer(lw, x, nb, t, consts):
    m = nb * t
    tm = 512
    j2, j4 = consts["j2"], consts["j4"]
    x1 = _ffn(x, lw["norm_ffn"][0], lw["wg"][0], lw["wu"][0], lw["wd"][0], tm)
    proj = _proj(x1, lw["norm_mix"], lw["w_in"], tm)
    proj3 = proj.reshape(nb, t, PROJ_PAD)
    zeros8 = jnp.zeros((nb, SUBLANES, GROUP_W), F32)
    ya, yb, yc, hlast, ulast, s_out = _mixers_recurrent(
        lw, proj3, zeros8, jnp.zeros((nb, 1, GROUP_W), F32), zeros8,
        jnp.zeros((nb, SUBLANES, 4 * GROUP_W), F32), jnp.zeros((nb, GROUP_W, GROUP_W), F32), j4,
        tt=256, chunk=HEAD_D, pos0=0, t_valid=t)
    q_r, ncmp, nslc, nwin, gates = _nsa_prep(proj, consts["cos_p"], consts["sin_p"], lw["gq"], lw["gk2"], j2,
                                             tm, rows_per_table=t)
    n_grp = t // CMP_STRIDE
    kvc = _compress(ncmp.reshape(nb, n_grp, CMP_STRIDE * LANES), lw["cw"], consts["cos_cp"], consts["sin_cp"], j2)
    n_cmp = (t - CMP_LEN) // CMP_STRIDE + 1
    n_sel = -(-t // SEL_BLOCK)
    yd = _nsa_attn(q_r.reshape(nb, t, GROUP_W), gates.reshape(nb, t, LANES), kvc, nslc.reshape(nb, t, LANES),
                   nwin.reshape(nb, t, LANES), consts["ovl_t_p"], consts["expand_p"], tq=128, tk=512,
                   n_cmp=n_cmp, n_sel=n_sel)
    x2 = _mix_out(x1, ya.reshape(m, GROUP_W), yb.reshape(m, GROUP_W), yc.reshape(m, GROUP_W),
                  yd.reshape(m, GROUP_W), lw["out_norm"], lw["w_out"], tm)
    x3 = _ffn(x2, lw["norm_ffn"][1], lw["wg"][1], lw["wu"][1], lw["wd"][1], tm)
    xa = proj3[:, :, COL_XA * GROUP_W:(COL_XA + 1) * GROUP_W]
    wlen = min(WINDOW, t)
    states = (hlast[:, SUBLANES - 1], xa[:, t - 3:], _bd_to_state(s_out), proj3[:, t - 1, :4 * GROUP_W],
              ulast[:, SUBLANES - 2:], nwin.reshape(nb, t, LANES)[:, t - wlen:],
              ncmp.reshape(nb, t // PAGE_SIZE, PAGE_SIZE, LANES), nslc.reshape(nb, t // PAGE_SIZE, PAGE_SIZE, LANES))
    return x3, states


def _sample_layer(lw, x, st, caches, page_table, consts):
    nb = x.shape[0]
    n_pages = page_table.shape[1]
    past_len = n_pages * PAGE_SIZE
    j2, j4 = consts["j2"], consts["j4"]
    lru_h, lru_conv, rwkv_s, rwkv_shift, sconv, win = st
    cache_cmp, cache_slc = caches
    x1 = _ffn(x, lw["norm_ffn"][0], lw["wg"][0], lw["wu"][0], lw["wd"][0], nb)
    proj = _proj(x1, lw["norm_mix"], lw["w_in"], nb)
    proj3 = jnp.pad(proj[:, None, :], ((0, 0), (0, SUBLANES - 1), (0, 0)))
    ya, yb, yc, hlast, ulast, s_out = _mixers_recurrent(
        lw, proj3, _pad_rows_front(lru_conv, SUBLANES), lru_h[:, None, :], _pad_rows_front(sconv, SUBLANES),
        _pad_rows_front(rwkv_shift[:, None, :], SUBLANES), _state_to_bd(rwkv_s), j4,
        tt=SUBLANES, chunk=SUBLANES, pos0=past_len, t_valid=1)
    q_r, ncmp, nslc, nwin, gates = _nsa_prep(proj, consts["cos_s"], consts["sin_s"], lw["gq"], lw["gk2"], j2,
                                             nb, rows_per_table=nb)

    def row8(a):
        return jnp.pad(a[:, None, :], ((0, 0), (0, SUBLANES - 1), (0, 0)))

    q4 = jnp.pad(q_r.reshape(nb, N_HEADS, HEAD_D), ((0, 0), (0, SUBLANES - N_HEADS), (0, LANES - HEAD_D)))
    grp_per_page = PAGE_SIZE // CMP_STRIDE
    n_phys = cache_cmp.shape[0]
    n_cmp = (past_len + 1 - CMP_LEN) // CMP_STRIDE + 1
    n_sel = -(-(past_len + 1) // SEL_BLOCK)
    ocmp, top_idx = _samp_cmp(cache_cmp.reshape(n_phys, grp_per_page, CMP_STRIDE * LANES), page_table, lw["cw"],
                              consts["cos_cs"], consts["sin_cs"], j2, q4, consts["ovl_s"], n_cmp, n_sel,
                              t_pos=past_len, pages_per_step=consts["pages_per_step"])
    wlen = win.shape[1]
    yd8 = _samp_sel(cache_slc.reshape(n_phys, PAGE_SIZE // SEL_BLOCK, SEL_BLOCK, LANES), page_table,
                    top_idx[:, :, 0], q4, row8(nslc), win, row8(nwin), row8(gates), ocmp,
                    t_pos=past_len, win_pos0=past_len - wlen)
    x2 = _mix_out(x1, ya[:, 0], yb[:, 0], yc[:, 0], yd8[:, 0], lw["out_norm"], lw["w_out"], nb)
    x3 = _ffn(x2, lw["norm_ffn"][1], lw["wg"][1], lw["wu"][1], lw["wd"][1], nb)
    xa = proj[:, COL_XA * GROUP_W:(COL_XA + 1) * GROUP_W]
    new_win = jnp.concatenate([win, nwin[:, None, :]], axis=1)
    states = (hlast[:, 0], jnp.concatenate([lru_conv[:, 1:], xa[:, None, :]], axis=1), _bd_to_state(s_out),
              proj[:, :4 * GROUP_W], jnp.concatenate([sconv[:, 1:], ulast[:, 0:1]], axis=1),
              new_win[:, -min(WINDOW, wlen + 1):], ncmp[:, None, :], nslc[:, None, :])
    return x3, states


def kernel(x_prompt, x_sample, state_lru_h, state_lru_conv, state_rwkv_S, state_rwkv_shift, state_sconv, state_nsa_win, cache_nsa_cmp, cache_nsa_slc, page_table, norm_ffn, ffn_w_gate, ffn_w_up, ffn_w_down, norm_mix, w_in, lru_conv_w, lru_conv_b, lru_w_r, lru_b_r, lru_w_i, lru_b_i, lru_lambda, rwkv_mu, rwkv_w0, rwkv_w2, rwkv_a0, rwkv_a2, rwkv_g2, rwkv_k_k, rwkv_k_a, rwkv_r_k, rwkv_ln_g, rwkv_ln_b, sconv_w, nsa_norm_q, nsa_norm_k, nsa_cmp_pe, nsa_cmp_w1, nsa_cmp_b1, nsa_cmp_w2, nsa_cmp_b2, out_norm, w_out):
    params = dict(norm_ffn=norm_ffn, ffn_w_gate=ffn_w_gate, ffn_w_up=ffn_w_up, ffn_w_down=ffn_w_down,
                  norm_mix=norm_mix, w_in=w_in, lru_conv_w=lru_conv_w, lru_conv_b=lru_conv_b, lru_w_r=lru_w_r,
                  lru_b_r=lru_b_r, lru_w_i=lru_w_i, lru_b_i=lru_b_i, lru_lambda=lru_lambda, rwkv_mu=rwkv_mu,
                  rwkv_w0=rwkv_w0, rwkv_w2=rwkv_w2, rwkv_a0=rwkv_a0, rwkv_a2=rwkv_a2, rwkv_g2=rwkv_g2,
                  rwkv_k_k=rwkv_k_k, rwkv_k_a=rwkv_k_a, rwkv_r_k=rwkv_r_k, rwkv_ln_g=rwkv_ln_g,
                  rwkv_ln_b=rwkv_ln_b, sconv_w=sconv_w, nsa_norm_q=nsa_norm_q, nsa_norm_k=nsa_norm_k,
                  nsa_cmp_pe=nsa_cmp_pe, nsa_cmp_w1=nsa_cmp_w1, nsa_cmp_b1=nsa_cmp_b1, nsa_cmp_w2=nsa_cmp_w2,
                  nsa_cmp_b2=nsa_cmp_b2, out_norm=out_norm, w_out=w_out)
    depth = norm_mix.shape[0]
    bp, tp, d = x_prompt.shape
    bs, ts, _ = x_sample.shape
    n_pages = page_table.shape[1]
    past_len = n_pages * PAGE_SIZE
    assert d == D_MODEL and ts == 1 and tp % 512 == 0 and tp >= WINDOW + 128 and past_len >= WINDOW
    assert bs % SUBLANES == 0 or bs < SUBLANES

    n_grp_p = tp // CMP_STRIDE
    n_sel_p = -(-tp // SEL_BLOCK)
    nsp_p = _round_up(n_sel_p, LANES)
    n_grp_s = past_len // CMP_STRIDE
    n_sel_s = -(-(past_len + 1) // SEL_BLOCK)
    nsp_s = _round_up(n_sel_s, LANES)
    cos_p, sin_p = _rope_tables(jnp.arange(tp))
    cos_s, sin_s = _rope_tables(jnp.full((bs,), past_len))
    cos_cp, sin_cp = _rope_tables(jnp.arange(n_grp_p) * CMP_STRIDE + CMP_LEN - 1)
    cos_cs, sin_cs = _rope_tables(jnp.arange(n_grp_s) * CMP_STRIDE + CMP_LEN - 1)
    expand_p = (jnp.arange(nsp_p)[:, None] == (jnp.arange(tp)[None, :] // SEL_BLOCK)).astype(BF16)
    pages_per_step = math.gcd(n_pages, 16)
    consts = dict(
        j2=_seg_ones(2), j4=_seg_ones(N_HEADS), cos_p=cos_p, sin_p=sin_p, cos_s=cos_s, sin_s=sin_s,
        cos_cp=cos_cp, sin_cp=sin_cp, cos_cs=cos_cs, sin_cs=sin_cs,
        ovl_t_p=_overlap(n_grp_p, nsp_p, n_sel_p).T,
        expand_p=expand_p,
        ovl_s=_overlap(n_grp_s, nsp_s, n_sel_s),
        pages_per_step=pages_per_step,
    )

    yp = x_prompt.reshape(bp * tp, d)
    ys = x_sample.reshape(bs, d)
    sp_all, ss_all = [], []
    for l in range(depth):
        lw = _layer_weights({name: arr[l] for name, arr in params.items()})
        yp, sp = _prompt_layer(lw, yp, bp, tp, consts)
        ys, ss = _sample_layer(lw, ys, (state_lru_h[l], state_lru_conv[l], state_rwkv_S[l], state_rwkv_shift[l],
                                        state_sconv[l], state_nsa_win[l]),
                               (cache_nsa_cmp[l], cache_nsa_slc[l]), page_table, consts)
        sp_all.append(sp)
        ss_all.append(ss)
    outs = [yp.reshape(bp, tp, d), ys.reshape(bs, ts, d)]
    for i in range(8):
        outs.append(jnp.stack([s[i] for s in sp_all]))
        outs.append(jnp.stack([s[i] for s in ss_all]))
    return tuple(outs)
```

```python
import functools
import math

import numpy as np
import jax
import jax.numpy as jnp
from jax import lax
from jax.experimental import pallas as pl
from jax.experimental.pallas import tpu as pltpu

F32 = jnp.float32
BF16 = jnp.bfloat16

D_MODEL = 1024
GROUP_W = 256
N_HEADS = 4
HEAD_D = 64
LRU_C = 8.0
RWKV_LN_EPS = 64e-5
CMP_LEN = 32
CMP_STRIDE = 16
SEL_BLOCK = 64
N_SELECT = 16
WINDOW = 512
FORCE_BONUS = 1e4
NEG_INF = -1e30
ROPE_THETA = 10000.0
EPS = 1e-6
PAGE_SIZE = 128
PROJ_PAD = 3072
ATT_SCALE = HEAD_D ** -0.5

SUBLANES = 8
LANES = 128
VMEM_LIMIT = 56 * 1024 * 1024

COL_XA, COL_GA, COL_BG, COL_CG, COL_XIN, COL_Q = 4, 5, 6, 7, 8, 9
COL_CMP, COL_SLC, COL_WIN, COL_GL = 20, 21, 22, 23


def _params(*sem):
    return pltpu.CompilerParams(dimension_semantics=sem, vmem_limit_bytes=VMEM_LIMIT)


def _const_spec(shape):
    nd = len(shape)
    return pl.BlockSpec(shape, lambda *_: (0,) * nd)


def _dot(a, b):
    return jnp.dot(a, b, preferred_element_type=F32)


def _dot_nt(a, b):
    return lax.dot_general(a, b, (((1,), (1,)), ((), ())), preferred_element_type=F32)


def _dot_tn(a, b):
    return lax.dot_general(a, b, (((0,), (0,)), ((), ())), preferred_element_type=F32)


def _split3(x):
    h1 = x.astype(BF16)
    r1 = x - h1.astype(F32)
    h2 = r1.astype(BF16)
    h3 = (r1 - h2.astype(F32)).astype(BF16)
    return h1, h2, h3


def _dot_exact_rhs(x, m_bf16):
    h1, h2, h3 = _split3(x)
    return _dot(h1, m_bf16) + _dot(h2, m_bf16) + _dot(h3, m_bf16)


def _gelu_tanh(x):
    return x * (0.5 * (1.0 + jnp.tanh(math.sqrt(2.0 / math.pi) * (x + 0.044715 * (x * x * x)))))


def _sigmoid(x):
    return 1.0 / (1.0 + jnp.exp(-x))


def _softplus(x):
    return jnp.maximum(x, 0.0) + jnp.log1p(jnp.exp(-jnp.abs(x)))


def _rmsnorm_rows(x, g):
    ms = jnp.mean(x * x, axis=-1, keepdims=True)
    return x * lax.rsqrt(ms + EPS) * g


def _masked_softmax_rows(s, mask):
    s = jnp.where(mask, s, NEG_INF)
    m = jnp.max(s, axis=-1, keepdims=True)
    e = jnp.exp(s - m)
    norm = 1.0 / jnp.maximum(jnp.sum(e, axis=-1, keepdims=True), 1e-30)
    return e * jnp.where(m > 0.5 * NEG_INF, norm, 0.0)


def _rope_lanes(x, cos, sin_signed, lane):
    swapped = jnp.where((lane & (HEAD_D - 1)) < HEAD_D // 2,
                        pltpu.roll(x, LANES - HEAD_D // 2, 1), pltpu.roll(x, HEAD_D // 2, 1))
    return x * cos + swapped * sin_signed


def _ffn_body(x_ref, g_ref, wg_ref, wu_ref, wd_ref, o_ref, act_ref, *, f_chunk):
    x = x_ref[...]
    h = _rmsnorm_rows(x, g_ref[...]).astype(BF16)
    d_ff = wg_ref.shape[1]
    for c in range(d_ff // f_chunk):
        sl = slice(c * f_chunk, (c + 1) * f_chunk)
        gate = _dot(h, wg_ref[:, sl])
        up = _dot(h, wu_ref[:, sl])
        act_ref[:, sl] = (gate * _sigmoid(gate) * up).astype(BF16)
    o_ref[...] = x + 0.5 * _dot(act_ref[...], wd_ref[...])


def _ffn(x, g, wg, wu, wd, tm):
    m, d = x.shape
    d_ff = wg.shape[1]
    return pl.pallas_call(
        functools.partial(_ffn_body, f_chunk=256),
        grid=(m // tm,),
        in_specs=[pl.BlockSpec((tm, d), lambda i: (i, 0)), _const_spec((1, d)),
                  _const_spec((d, d_ff)), _const_spec((d, d_ff)), _const_spec((d_ff, d))],
        out_specs=pl.BlockSpec((tm, d), lambda i: (i, 0)),
        out_shape=jax.ShapeDtypeStruct((m, d), F32),
        scratch_shapes=[pltpu.VMEM((tm, d_ff), BF16)],
        compiler_params=_params("parallel"),
        name="ffn",
    )(x, g, wg, wu, wd)


def _proj_body(x_ref, g_ref, w_ref, o_ref, *, n_chunk):
    h = _rmsnorm_rows(x_ref[...], g_ref[...]).astype(BF16)
    for c in range(w_ref.shape[1] // n_chunk):
        sl = slice(c * n_chunk, (c + 1) * n_chunk)
        o_ref[:, sl] = _dot(h, w_ref[:, sl])


def _proj(x, g, w, tm):
    m, d = x.shape
    n = w.shape[1]
    return pl.pallas_call(
        functools.partial(_proj_body, n_chunk=512),
        grid=(m // tm,),
        in_specs=[pl.BlockSpec((tm, d), lambda i: (i, 0)), _const_spec((1, d)), _const_spec((d, n))],
        out_specs=pl.BlockSpec((tm, n), lambda i: (i, 0)),
        out_shape=jax.ShapeDtypeStruct((m, n), F32),
        compiler_params=_params("parallel"),
        name="proj",
    )(x, g, w)


def _lru_sconv_body(xa_ref, ga_ref, bg_ref, cg_ref, xin_ref, cbuf_ref, h0_ref, sbuf_ref,
                    cw_ref, cb_ref, wri_ref, bri_ref, lam_ref, sw_ref,
                    ya_ref, yc_ref, hlast_ref, ulast_ref,
                    extx_ref, extu_ref, h_ref, *, tt, pos0):
    t = pl.program_id(1)
    nt = pl.num_programs(1)

    @pl.when(t == 0)
    def _():
        extx_ref[0:SUBLANES, :] = cbuf_ref[...]
        extu_ref[0:SUBLANES, :] = sbuf_ref[...]
        h_ref[...] = h0_ref[...]

    x = xa_ref[...]
    extx_ref[SUBLANES:SUBLANES + tt, :] = x
    cw = cw_ref[...]
    xc = extx_ref[pl.ds(SUBLANES - 3, tt), :] * cw[0:1]
    xc = xc + extx_ref[pl.ds(SUBLANES - 2, tt), :] * cw[1:2]
    xc = xc + extx_ref[pl.ds(SUBLANES - 1, tt), :] * cw[2:3]
    xc = xc + x * cw[3:4]
    xc = xc + cb_ref[...]
    extx_ref[0:SUBLANES, :] = x[tt - SUBLANES:tt]

    gates = _dot(xc.astype(BF16), wri_ref[...]) + bri_ref[...]
    r = _sigmoid(gates[:, :GROUP_W])
    ig = _sigmoid(gates[:, GROUP_W:])
    log_a = -LRU_C * r * _softplus(-lam_ref[...])
    a = jnp.exp(log_a)
    rows = lax.broadcasted_iota(jnp.int32, (tt, GROUP_W), 0)
    mult = jnp.where(rows + (pos0 + t * tt) == 0, 1.0, jnp.sqrt(1.0 - jnp.exp(2.0 * log_a)))
    b = mult * ig * xc

    s = 1
    while s < tt:
        keep = rows >= s
        a_sh = jnp.where(keep, pltpu.roll(a, s, 0), 1.0)
        b_sh = jnp.where(keep, pltpu.roll(b, s, 0), 0.0)
        b = a * b_sh + b
        a = a * a_sh
        s *= 2
    h = a * h_ref[...] + b
    h_ref[...] = h[tt - 1:tt]
    ya_ref[...] = _gelu_tanh(ga_ref[...]) * h

    u = cg_ref[...] * xin_ref[...]
    extu_ref[SUBLANES:SUBLANES + tt, :] = u
    sw = sw_ref[...]
    yv = extu_ref[pl.ds(SUBLANES - 2, tt), :] * sw[0:1]
    yv = yv + extu_ref[pl.ds(SUBLANES - 1, tt), :] * sw[1:2]
    yv = yv + u * sw[2:3]
    yc_ref[...] = bg_ref[...] * yv
    extu_ref[0:SUBLANES, :] = u[tt - SUBLANES:tt]

    @pl.when(t == nt - 1)
    def _():
        hlast_ref[...] = h[tt - SUBLANES:tt]
        ulast_ref[...] = u[tt - SUBLANES:tt]


def _lru_sconv(proj3, cbuf8, h0, sbuf8, cw, cb, wri, bri, lam, sw, tt, pos0):
    nb, tp, _ = proj3.shape
    w = GROUP_W

    def col(c):
        return pl.BlockSpec((None, tt, w), lambda b, t, c=c: (b, t, c))

    state8 = pl.BlockSpec((None, SUBLANES, w), lambda b, t: (b, 0, 0))
    return pl.pallas_call(
        functools.partial(_lru_sconv_body, tt=tt, pos0=pos0),
        grid=(nb, tp // tt),
        in_specs=[col(COL_XA), col(COL_GA), col(COL_BG), col(COL_CG), col(COL_XIN),
                  state8, pl.BlockSpec((None, 1, w), lambda b, t: (b, 0, 0)), state8,
                  _const_spec((4, w)), _const_spec((1, w)), _const_spec((w, 2 * w)), _const_spec((1, 2 * w)),
                  _const_spec((1, w)), _const_spec((3, w))],
        out_specs=[pl.BlockSpec((None, tt, w), lambda b, t: (b, t, 0)),
                   pl.BlockSpec((None, tt, w), lambda b, t: (b, t, 0)), state8, state8],
        out_shape=[jax.ShapeDtypeStruct((nb, tp, w), F32), jax.ShapeDtypeStruct((nb, tp, w), F32),
                   jax.ShapeDtypeStruct((nb, SUBLANES, w), F32), jax.ShapeDtypeStruct((nb, SUBLANES, w), F32)],
        scratch_shapes=[pltpu.VMEM((tt + SUBLANES, w), F32), pltpu.VMEM((tt + SUBLANES, w), F32),
                        pltpu.VMEM((1, w), F32)],
        compiler_params=_params("parallel", "arbitrary"),
        name="lru_sconv",
    )(proj3, proj3, proj3, proj3, proj3, cbuf8, h0, sbuf8, cw, cb, wri, bri, lam, sw)


def _rwkv_body(p_ref, shift_ref, sin_ref, mu_ref, w0_ref, w2_ref, a0_ref, a2_ref, g2_ref, kk_ref, ka_ref,
               rk_ref, lng_ref, lnb_ref, j4_ref,
               y_ref, sout_ref,
               ext_ref, s_ref, w_buf, r_buf, k_buf, v_buf, a_buf, b_buf, y_buf, *, tt, chunk, t_valid):
    t = pl.program_id(1)
    nt = pl.num_programs(1)
    w = GROUP_W
    nh = N_HEADS

    @pl.when(t == 0)
    def _():
        ext_ref[0:SUBLANES, :] = shift_ref[...]
        s_ref[...] = sin_ref[...]

    p = p_ref[...]
    ext_ref[SUBLANES:SUBLANES + tt, :] = p
    m = p + (ext_ref[pl.ds(SUBLANES - 1, tt), :] - p) * mu_ref[...]
    ext_ref[0:SUBLANES, :] = p[tt - SUBLANES:tt]
    r = m[:, 0:w]
    k = m[:, w:2 * w]
    v = m[:, 2 * w:3 * w]
    wa = m[:, 3 * w:3 * w + LANES]
    gl = m[:, 3 * w + LANES:4 * w]
    wlin = w0_ref[...] + _dot(jnp.tanh(wa).astype(BF16), w2_ref[...])
    log_decay = -jnp.exp(-_softplus(-wlin) - 0.5)
    ag = _sigmoid(a0_ref[...] + _dot(wa.astype(BF16), a2_ref[...]))
    g = _dot(_sigmoid(gl).astype(BF16), g2_ref[...])
    j4 = j4_ref[...]
    kk = k * kk_ref[...]
    kk = kk * lax.rsqrt(jnp.maximum(_dot_exact_rhs(kk * kk, j4), 1e-24))
    kf = k * (1.0 + (ag - 1.0) * ka_ref[...])
    rows = lax.broadcasted_iota(jnp.int32, (tt, w), 0)
    if t_valid % tt != 0:
        live = rows + t * tt < t_valid
        log_decay = jnp.where(live, log_decay, 0.0)
        kk = jnp.where(live, kk, 0.0)
        kf = jnp.where(live, kf, 0.0)
        v = jnp.where(live, v, 0.0)

    rin = rows & (chunk - 1)
    cl = log_decay
    s = 1
    while s < chunk:
        cl = cl + jnp.where(rin >= s, pltpu.roll(cl, s, 0), 0.0)
        s *= 2
    e_neg = jnp.exp(-cl)
    w_buf[...] = cl
    r_buf[...] = r * jnp.exp(cl)
    a_buf[...] = -kk * jnp.exp(cl - log_decay)
    b_buf[...] = kk * ag * e_neg
    k_buf[...] = kf * e_neg
    v_buf[...] = v

    sl = nh * chunk
    hm_rows = lax.broadcasted_iota(jnp.int32, (sl, w), 0) // chunk
    hm_cols = lax.broadcasted_iota(jnp.int32, (sl, w), 1) // HEAD_D
    head_mask = (hm_rows == hm_cols).astype(F32)
    ri = lax.broadcasted_iota(jnp.int32, (sl, sl), 0)
    ci = lax.broadcasted_iota(jnp.int32, (sl, sl), 1)
    same = (ri // chunk) == (ci // chunk)
    strict = (same & ((ri & (chunk - 1)) > (ci & (chunk - 1)))).astype(F32)
    incl = (same & ((ri & (chunk - 1)) >= (ci & (chunk - 1)))).astype(F32)
    eye = (ri == ci).astype(F32)

    def stack(x):
        return jnp.concatenate([x] * nh, axis=0) * head_mask

    def chunk_step(c):
        off = c * chunk
        cs = pl.ds(off, chunk)
        a_s = stack(a_buf[cs, :]).astype(BF16)
        r_s = stack(r_buf[cs, :]).astype(BF16)
        b_s = stack(b_buf[cs, :]).astype(BF16)
        k_s = stack(k_buf[cs, :]).astype(BF16)
        v_s = stack(v_buf[cs, :])
        v_sb = v_s.astype(BF16)
        n_mat = _dot_nt(a_s, b_s) * strict
        m_mat = _dot_nt(a_s, k_s) * strict
        p_mat = _dot_nt(r_s, b_s) * incl
        q_mat = _dot_nt(r_s, k_s) * incl
        t_mat = eye + n_mat
        x = n_mat
        step = 2
        while step < chunk:
            xb = x.astype(BF16)
            x = _dot(xb, xb)
            t_mat = t_mat + _dot(t_mat.astype(BF16), x.astype(BF16))
            step *= 2
        t_b = t_mat.astype(BF16)
        w_eff = _dot(t_b, a_s)
        z = _dot(t_b, _dot(m_mat.astype(BF16), v_sb).astype(BF16))
        s0 = s_ref[...]
        s0b = s0.astype(BF16)
        u = _dot_nt(w_eff.astype(BF16), s0b) + z
        ub = u.astype(BF16)
        y_s = _dot_nt(r_s, s0b) + _dot(p_mat.astype(BF16), ub) + _dot(q_mat.astype(BF16), v_sb)
        yc = y_s[0:chunk]
        for hh in range(1, nh):
            yc = yc + y_s[hh * chunk:(hh + 1) * chunk]
        y_buf[cs, :] = yc
        c_last = jnp.exp(w_buf[pl.ds(off + chunk - 1, 1), :])
        s_ref[...] = (s0 + _dot_tn(ub, b_s) + _dot_tn(v_sb, k_s)) * c_last

    for c in range(tt // chunk):
        chunk_step(c)

    y = y_buf[...]
    inv_hd = 1.0 / HEAD_D
    mean = _dot_exact_rhs(y, j4) * inv_hd
    yc = y - mean
    var = _dot_exact_rhs(yc * yc, j4) * inv_hd
    yn = yc * lax.rsqrt(var + RWKV_LN_EPS) * lng_ref[...] + lnb_ref[...]
    bonus = _dot_exact_rhs(r * kf * rk_ref[...], j4) * v
    y_ref[...] = (yn + bonus) * g

    @pl.when(t == nt - 1)
    def _():
        sout_ref[...] = s_ref[...]


def _rwkv(proj3, shift8, s_bd, mu, w0, w2p, a0, a2p, g2, kk, ka, rk, lng, lnb, j4, tt, chunk, t_valid):
    nb, tp, _ = proj3.shape
    w = GROUP_W
    pw = 4 * w
    vec = _const_spec((1, w))
    return pl.pallas_call(
        functools.partial(_rwkv_body, tt=tt, chunk=chunk, t_valid=t_valid),
        grid=(nb, tp // tt),
        in_specs=[pl.BlockSpec((None, tt, pw), lambda b, t: (b, t, 0)),
                  pl.BlockSpec((None, SUBLANES, pw), lambda b, t: (b, 0, 0)),
                  pl.BlockSpec((None, w, w), lambda b, t: (b, 0, 0)),
                  _const_spec((1, pw)), vec, _const_spec((LANES, w)), vec, _const_spec((LANES, w)),
                  _const_spec((LANES, w)), vec, vec, vec, vec, vec, _const_spec((w, w))],
        out_specs=[pl.BlockSpec((None, tt, w), lambda b, t: (b, t, 0)),
                   pl.BlockSpec((None, w, w), lambda b, t: (b, 0, 0))],
        out_shape=[jax.ShapeDtypeStruct((nb, tp, w), F32), jax.ShapeDtypeStruct((nb, w, w), F32)],
        scratch_shapes=[pltpu.VMEM((tt + SUBLANES, pw), F32), pltpu.VMEM((w, w), F32)]
        + [pltpu.VMEM((tt, w), F32)] * 7,
        compiler_params=_params("parallel", "arbitrary"),
        name="rwkv7",
    )(proj3, shift8, s_bd, mu, w0, w2p, a0, a2p, g2, kk, ka, rk, lng, lnb, j4)


def _nsa_prep_body(q_ref, cmp_ref, slc_ref, win_ref, gl_ref, cos_ref, sin_ref, gq_ref, gk_ref, j2_ref,
                   qo_ref, cmpo_ref, slco_ref, wino_ref, gate_ref, *bf_refs):
    cos = cos_ref[...]
    sin = sin_ref[...]
    j2 = j2_ref[...]
    lane = lax.broadcasted_iota(jnp.int32, cos.shape, 1)
    inv_hd = 1.0 / HEAD_D
    gq = gq_ref[...]
    q = q_ref[...]
    halves = []
    for c in range(2):
        x = q[:, c * LANES:(c + 1) * LANES]
        ms = _dot_exact_rhs(x * x, j2) * inv_hd
        halves.append(_rope_lanes(x * lax.rsqrt(ms + EPS) * gq, cos, sin, lane))
    qo_ref[...] = jnp.concatenate(halves, axis=1)
    cmpo_ref[...] = cmp_ref[...]
    gk = gk_ref[...]
    is_key = lane < HEAD_D
    for src, dst, row in ((slc_ref, slco_ref, 0), (win_ref, wino_ref, 1)):
        x = src[...]
        ms = _dot_exact_rhs(x * x, j2) * inv_hd
        roped = _rope_lanes(x * lax.rsqrt(ms + EPS) * gk[row:row + 1], cos, sin, lane)
        kv = jnp.where(is_key, roped, x)
        dst[...] = kv
        if bf_refs:
            bf_refs[row][...] = kv.astype(BF16)
            bf_refs[2 + row][...] = kv.T.astype(BF16)
    gate_ref[...] = _sigmoid(gl_ref[...])


def _nsa_prep(proj, cos, sin, gq, gk2, j2, tm, rows_per_table, with_bf16):
    m = proj.shape[0]
    nt_tab = rows_per_table // tm
    o128t = pl.BlockSpec((LANES, tm), lambda i: (0, i))
    bf_specs = [pl.BlockSpec((tm, LANES), lambda i: (i, 0))] * 2 + [o128t] * 2 if with_bf16 else []
    bf_shapes = ([jax.ShapeDtypeStruct((m, LANES), BF16)] * 2 + [jax.ShapeDtypeStruct((LANES, m), BF16)] * 2
                 if with_bf16 else [])

    def col(c, wdt):
        return pl.BlockSpec((tm, wdt), lambda i, c=c: (i, c))

    tab = pl.BlockSpec((tm, LANES), lambda i: (i % nt_tab, 0))
    o128 = pl.BlockSpec((tm, LANES), lambda i: (i, 0))
    return pl.pallas_call(
        _nsa_prep_body,
        grid=(m // tm,),
        in_specs=[col(COL_Q, GROUP_W), col(COL_CMP, LANES), col(COL_SLC, LANES), col(COL_WIN, LANES),
                  col(COL_GL, LANES), tab, tab, _const_spec((1, LANES)), _const_spec((2, LANES)),
                  _const_spec((LANES, LANES))],
        out_specs=[pl.BlockSpec((tm, GROUP_W), lambda i: (i, 0)), o128, o128, o128, o128] + bf_specs,
        out_shape=[jax.ShapeDtypeStruct((m, GROUP_W), F32)] + [jax.ShapeDtypeStruct((m, LANES), F32)] * 4
        + bf_shapes,
        compiler_params=_params("parallel"),
        name="nsa_prep",
    )(proj, proj, proj, proj, proj, cos, sin, gq, gk2, j2)


def _compress_tail(hid_lo, hid_hi_next, b1_ref, w2_ref, b2_ref, gk_ref, cos_ref, sin_ref, j2_ref):
    hidden = _gelu_tanh(hid_lo + hid_hi_next + b1_ref[...])
    kv = _dot(hidden.astype(BF16), w2_ref[...]) + b2_ref[...]
    lane = lax.broadcasted_iota(jnp.int32, kv.shape, 1)
    ms = _dot_exact_rhs(kv * kv, j2_ref[...]) * (1.0 / HEAD_D)
    roped = _rope_lanes(kv * lax.rsqrt(ms + EPS) * gk_ref[...], cos_ref[...], sin_ref[...], lane)
    return jnp.where(lane < HEAD_D, roped, kv)


def _compress_body(x_ref, pelo_ref, pehi_ref, wlo_ref, whi_ref, b1_ref, w2_ref, b2_ref, gk_ref,
                   cos_ref, sin_ref, j2_ref, o_ref, ot_ref):
    x = x_ref[...]
    n_grp = x.shape[0]
    lo = _dot((x + pelo_ref[...]).astype(BF16), wlo_ref[...])
    hi = _dot((x + pehi_ref[...]).astype(BF16), whi_ref[...])
    hi_next = pltpu.roll(hi, n_grp - 1, 0)
    kv = _compress_tail(lo, hi_next, b1_ref, w2_ref, b2_ref, gk_ref, cos_ref, sin_ref, j2_ref)
    o_ref[...] = kv.astype(BF16)
    ot_ref[...] = kv.T.astype(BF16)


def _compress(x16, cw, cos, sin, j2):
    nb, n_grp, wid = x16.shape
    return pl.pallas_call(
        _compress_body,
        grid=(nb,),
        in_specs=[pl.BlockSpec((None, n_grp, wid), lambda b: (b, 0, 0)),
                  _const_spec((1, wid)), _const_spec((1, wid)),
                  _const_spec((wid, GROUP_W)), _const_spec((wid, GROUP_W)), _const_spec((1, GROUP_W)),
                  _const_spec((GROUP_W, LANES)), _const_spec((1, LANES)), _const_spec((1, LANES)),
                  _const_spec((n_grp, LANES)), _const_spec((n_grp, LANES)), _const_spec((LANES, LANES))],
        out_specs=[pl.BlockSpec((None, n_grp, LANES), lambda b: (b, 0, 0)),
                   pl.BlockSpec((LANES, n_grp), lambda b: (0, b))],
        out_shape=[jax.ShapeDtypeStruct((nb, n_grp, LANES), BF16),
                   jax.ShapeDtypeStruct((LANES, nb * n_grp), BF16)],
        compiler_params=_params("parallel"),
        name="nsa_compress",
    )(x16, cw["pe_lo"], cw["pe_hi"], cw["w_lo"], cw["w_hi"], cw["b1"], cw["w2"], cw["b2"], cw["gk"], cos, sin, j2)


def _stack_heads(q, lane):
    parts = []
    for h in range(N_HEADS):
        blk = q[:, (h // 2) * LANES:(h // 2 + 1) * LANES]
        if h % 2 == 1:
            blk = pltpu.roll(blk, HEAD_D, 1)
        parts.append(jnp.where(lane < HEAD_D, blk, 0.0))
    return jnp.concatenate(parts, axis=0)


def _unstack_heads(parts, lane):
    b01 = jnp.where(lane < HEAD_D, pltpu.roll(parts[0], HEAD_D, 1), parts[1])
    b23 = jnp.where(lane < HEAD_D, pltpu.roll(parts[2], HEAD_D, 1), parts[3])
    return jnp.concatenate([b01, b23], axis=1)


def _softmax_cols(st, mask):
    st = jnp.where(mask, st, NEG_INF)
    m = jnp.max(st, axis=0, keepdims=True)
    e = jnp.exp(st - m)
    norm = 1.0 / jnp.maximum(jnp.sum(e, axis=0, keepdims=True), 1e-30)
    return e * jnp.where(m > 0.5 * NEG_INF, norm, 0.0)


def _nsa_attn_t_body(q_ref, gate_ref, kvc_ref, kvct_ref, slc_ref, slct_ref, win_ref, wint_ref, ovl_ref, exp_ref,
                     o_ref, key_ref, keyp_ref, acc_ref, *, tq, tk, n_cmp, n_sel, n_top, win_span):
    i = pl.program_id(1)
    start = i * tq
    cols = N_HEADS * tq
    lane = lax.broadcasted_iota(jnp.int32, (tq, LANES), 1)
    q4 = _stack_heads(q_ref[...] * ATT_SCALE, lane).astype(BF16)
    tlane = start + (lax.broadcasted_iota(jnp.int32, (1, cols), 1) & (tq - 1))

    kvc = kvc_ref[...]
    ng = kvc.shape[0]
    ncol = lax.broadcasted_iota(jnp.int32, (ng, 1), 0)
    cend = jnp.where(ncol < n_cmp, ncol * CMP_STRIDE + (CMP_LEN - 1), jnp.iinfo(jnp.int32).max)
    pt = _softmax_cols(_dot_nt(kvc, q4), cend <= tlane)
    o_cmp = _dot(kvct_ref[...], pt.astype(BF16))
    psum = pt[:, 0:tq]
    for h in range(1, N_HEADS):
        psum = psum + pt[:, h * tq:(h + 1) * tq]
    ph = psum.astype(BF16)
    pl_ = (psum - ph.astype(F32)).astype(BF16)
    ovl = ovl_ref[...]
    imp_t = _dot(ovl, ph) + _dot(ovl, pl_)

    start0 = pl.multiple_of(jnp.maximum(start + tq - win_span, 0), tq)
    rel = tlane - (start0 + lax.broadcasted_iota(jnp.int32, (win_span, 1), 0))
    pw = _softmax_cols(_dot_nt(win_ref[pl.ds(start0, win_span), :], q4), (rel >= 0) & (rel < WINDOW))
    o_win = _dot(wint_ref[:, pl.ds(start0, win_span)], pw.astype(BF16))

    nsp = imp_t.shape[0]
    jblk = lax.broadcasted_iota(jnp.int32, (nsp, tq), 0)
    tcol = start + lax.broadcasted_iota(jnp.int32, (nsp, tq), 1)
    cur = tcol // SEL_BLOCK
    forced = (jblk == 0) | (jblk == cur) | (jblk == cur - 1)
    valid = (jblk * SEL_BLOCK <= tcol) & (jblk < n_sel)
    score = jnp.where(valid, imp_t + jnp.where(forced, FORCE_BONUS, 0.0), -1.0)
    key = pltpu.bitcast(score, jnp.int32)
    key_ref[...] = key
    keyp_ref[...] = key + 1
    n_live = jnp.minimum((start + tq - 1) // SEL_BLOCK + 1, n_sel)

    def rank_step(r_, cnt):
        row = key_ref[pl.ds(r_, 1), :]
        thr = jnp.where(jblk <= r_, keyp_ref[...], key_ref[...])
        return cnt + jnp.where(row >= thr, 1, 0)

    cnt = lax.fori_loop(0, n_live, rank_step, jnp.zeros((nsp, tq), jnp.int32))
    bias_t = jnp.where((cnt < n_top) & (score >= 0.0), 0.0, NEG_INF)
    bias = bias_t.T.astype(BF16)
    lhs = jnp.concatenate([q4, jnp.concatenate([bias] * N_HEADS, axis=0)], axis=1)

    acc_ref[...] = jnp.zeros((LANES, cols), F32)
    n_kt = (start + tq + tk - 1) // tk

    def scores(kt, causal_lane):
        off = pl.multiple_of(kt * tk, tk)
        k_aug = jnp.concatenate([slc_ref[pl.ds(off, tk), :], exp_ref[pl.ds(off, tk), :]], axis=1)
        st = _dot_nt(k_aug, lhs)
        if causal_lane is not None:
            kpos = off + lax.broadcasted_iota(jnp.int32, (tk, 1), 0)
            st = jnp.where(kpos <= causal_lane, st, NEG_INF)
        return st

    def consume(kt, st, m_prev, l_prev):
        off = pl.multiple_of(kt * tk, tk)
        m_new = jnp.maximum(m_prev, jnp.max(st, axis=0, keepdims=True))
        e = jnp.exp(st - m_new[0:1])
        alpha = jnp.exp(m_prev - m_new)
        l_new = alpha * l_prev + jnp.sum(e.reshape(tk // SUBLANES, SUBLANES, cols), axis=0)
        acc_ref[...] = alpha[0:1] * acc_ref[...] + _dot(slct_ref[:, pl.ds(off, tk)], e.astype(BF16))
        return m_new, l_new

    def steady(kt, carry):
        st, m_run, l_run = carry
        st_next = scores(kt + 1, None)
        m_run, l_run = consume(kt, st, m_run, l_run)
        return st_next, m_run, l_run

    def boundary(_, carry):
        st, m_run, l_run = carry
        st_next = scores(n_kt - 1, tlane)
        m_run, l_run = consume(n_kt - 2, st, m_run, l_run)
        return st_next, m_run, l_run

    first_lane = tlane + jnp.where(n_kt > 1, jnp.int32(1 << 30), 0)
    carry = (scores(0, first_lane), jnp.full((SUBLANES, cols), NEG_INF, F32), jnp.zeros((SUBLANES, cols), F32))
    carry = lax.fori_loop(0, n_kt - 2, steady, carry)
    carry = lax.fori_loop(0, jnp.minimum(n_kt - 1, 1), boundary, carry)
    _, l_part = consume(n_kt - 1, *carry)
    o_sel = acc_ref[...] / jnp.maximum(jnp.sum(l_part, axis=0, keepdims=True), 1e-30)

    gt = gate_ref[...].T
    outs = []
    for h in range(N_HEADS):
        cs = slice(h * tq, (h + 1) * tq)
        outs.append(gt[3 * h:3 * h + 1] * o_cmp[HEAD_D:, cs] + gt[3 * h + 1:3 * h + 2] * o_sel[HEAD_D:, cs]
                    + gt[3 * h + 2:3 * h + 3] * o_win[HEAD_D:, cs])
    o_ref[...] = jnp.concatenate(outs, axis=0).T


def _nsa_attn_t(q, gates, kvc, kvc_t, slc, slc_t, win, win_t, ovl_t, expand, tq, tk, n_cmp, n_sel):
    nb, t, _ = q.shape
    ng = kvc.shape[1]
    nsp = ovl_t.shape[0]
    cols = N_HEADS * tq
    win_span = WINDOW + tq
    per_b = lambda b, i: (b, 0, 0)
    per_b_t = lambda b, i: (0, b)
    return pl.pallas_call(
        functools.partial(_nsa_attn_t_body, tq=tq, tk=tk, n_cmp=n_cmp, n_sel=n_sel,
                          n_top=min(N_SELECT, n_sel), win_span=win_span),
        grid=(nb, t // tq),
        in_specs=[pl.BlockSpec((None, tq, GROUP_W), lambda b, i: (b, i, 0)),
                  pl.BlockSpec((None, tq, LANES), lambda b, i: (b, i, 0)),
                  pl.BlockSpec((None, ng, LANES), per_b), pl.BlockSpec((LANES, ng), per_b_t),
                  pl.BlockSpec((None, t, LANES), per_b), pl.BlockSpec((LANES, t), per_b_t),
                  pl.BlockSpec((None, t, LANES), per_b), pl.BlockSpec((LANES, t), per_b_t),
                  _const_spec((nsp, ng)), _const_spec((t, nsp))],
        out_specs=pl.BlockSpec((None, tq, GROUP_W), lambda b, i: (b, i, 0)),
        out_shape=jax.ShapeDtypeStruct((nb, t, GROUP_W), F32),
        scratch_shapes=[pltpu.VMEM((nsp, tq), jnp.int32), pltpu.VMEM((nsp, tq), jnp.int32),
                        pltpu.VMEM((LANES, cols), F32)],
        compiler_params=_params("parallel", "arbitrary"),
        name="nsa_attn",
    )(q, gates, kvc, kvc_t, slc, slc_t, win, win_t, ovl_t, expand)


def _samp_cmp_body(*refs, pages_per_step, n_grp, n_cmp, n_sel, n_top, t_pos):
    pt_ref = refs[0]
    page_refs = refs[1:1 + pages_per_step]
    (pelo_ref, pehi_ref, wlo_ref, whi_ref, b1_ref, w2_ref, b2_ref, gk_ref, cos_ref, sin_ref, j2_ref,
     q4_ref, ovl_ref, ocmp_ref, idx_ref, xcat_ref, lo_ref, hi_ref) = refs[1 + pages_per_step:]
    del pt_ref
    s = pl.program_id(1)
    ns = pl.num_programs(1)
    grp_per_page = PAGE_SIZE // CMP_STRIDE
    rows_step = pages_per_step * grp_per_page

    @pl.when(s == 0)
    def _():
        hi_ref[n_grp:n_grp + SUBLANES, :] = jnp.zeros((SUBLANES, GROUP_W), F32)

    for j in range(pages_per_step):
        xcat_ref[j * grp_per_page:(j + 1) * grp_per_page, :] = page_refs[j][...]
    x = xcat_ref[...]
    off = pl.multiple_of(s * rows_step, rows_step)
    lo_ref[pl.ds(off, rows_step), :] = _dot((x + pelo_ref[...]).astype(BF16), wlo_ref[...])
    hi_ref[pl.ds(off, rows_step), :] = _dot((x + pehi_ref[...]).astype(BF16), whi_ref[...])

    @pl.when(s == ns - 1)
    def _():
        kvc = _compress_tail(lo_ref[...], hi_ref[pl.ds(1, n_grp), :], b1_ref, w2_ref, b2_ref, gk_ref,
                             cos_ref, sin_ref, j2_ref).astype(BF16)
        q4 = q4_ref[...].astype(BF16)
        sc = _dot_nt(q4, kvc) * ATT_SCALE
        nidx = lax.broadcasted_iota(jnp.int32, sc.shape, 1)
        p = _masked_softmax_rows(sc, (nidx * CMP_STRIDE + (CMP_LEN - 1) <= t_pos) & (nidx < n_cmp))
        ocmp_ref[...] = _dot(p.astype(BF16), kvc)
        hrow = lax.broadcasted_iota(jnp.int32, p.shape, 0)
        psum = jnp.sum(jnp.where(hrow < N_HEADS, p, 0.0), axis=0, keepdims=True)
        psum8 = jnp.broadcast_to(psum, p.shape)
        ph = psum8.astype(BF16)
        pl_ = (psum8 - ph.astype(F32)).astype(BF16)
        ovl = ovl_ref[...]
        imp = (_dot(ph, ovl) + _dot(pl_, ovl))[0:1]
        nsp = imp.shape[1]
        jrow = lax.broadcasted_iota(jnp.int32, (1, nsp), 1)
        cur = t_pos // SEL_BLOCK
        forced = (jrow == 0) | (jrow == cur) | (jrow == cur - 1)
        valid = (jrow * SEL_BLOCK <= t_pos) & (jrow < n_sel)
        score = jnp.where(valid, imp + jnp.where(forced, FORCE_BONUS, 0.0), -1.0)
        score = jnp.where(jrow < n_sel, score, -2.0)
        s_row = jnp.broadcast_to(score, (nsp, nsp))
        s_col = s_row.T
        ii = lax.broadcasted_iota(jnp.int32, (nsp, nsp), 0)
        jj = lax.broadcasted_iota(jnp.int32, (nsp, nsp), 1)
        beats = (s_col > s_row) | ((s_col == s_row) & (ii < jj))
        rank = jnp.sum(jnp.where(beats, 1.0, 0.0), axis=0, keepdims=True)
        chosen = (rank < n_top) & (score >= 0.0)
        slot = lax.broadcasted_iota(jnp.int32, (N_SELECT, nsp), 0).astype(F32)
        hit = (jnp.broadcast_to(rank, (N_SELECT, nsp)) == slot) & jnp.broadcast_to(chosen, (N_SELECT, nsp))
        jcol = lax.broadcasted_iota(jnp.int32, (N_SELECT, nsp), 1).astype(F32)
        blk = jnp.sum(jnp.where(hit, jcol, 0.0), axis=1, keepdims=True)
        cnt = jnp.sum(jnp.where(hit, 1.0, 0.0), axis=1, keepdims=True)
        blk = jnp.where(cnt > 0.5, blk, -1.0)
        idx_ref[...] = jnp.broadcast_to(blk, (N_SELECT, LANES)).astype(jnp.int32)


def _samp_cmp(cache16, layer, page_table, cw, cos, sin, j2, q4, ovl, n_cmp, n_sel, t_pos, pages_per_step):
    nb, n_pages = page_table.shape
    grp_per_page = PAGE_SIZE // CMP_STRIDE
    wid = cache16.shape[3]
    n_grp = n_pages * grp_per_page
    nsp = ovl.shape[1]
    rows_step = pages_per_step * grp_per_page

    def page_spec(j):
        return pl.BlockSpec((None, None, grp_per_page, wid),
                            lambda b, s, pt, j=j: (layer, pt[b, s * pages_per_step + j], 0, 0))

    def cst(shape):
        nd = len(shape)
        return pl.BlockSpec(shape, lambda b, s, pt: (0,) * nd)

    grid_spec = pltpu.PrefetchScalarGridSpec(
        num_scalar_prefetch=1,
        grid=(nb, n_pages // pages_per_step),
        in_specs=[page_spec(j) for j in range(pages_per_step)] + [
            cst((1, wid)), cst((1, wid)), cst((wid, GROUP_W)), cst((wid, GROUP_W)), cst((1, GROUP_W)),
            cst((GROUP_W, LANES)), cst((1, LANES)), cst((1, LANES)),
            cst((n_grp, LANES)), cst((n_grp, LANES)), cst((LANES, LANES)),
            pl.BlockSpec((None, SUBLANES, LANES), lambda b, s, pt: (b, 0, 0)), cst((n_grp, nsp))],
        out_specs=[pl.BlockSpec((None, SUBLANES, LANES), lambda b, s, pt: (b, 0, 0)),
                   pl.BlockSpec((None, N_SELECT, LANES), lambda b, s, pt: (b, 0, 0))],
        scratch_shapes=[pltpu.VMEM((rows_step, wid), F32),
                        pltpu.VMEM((n_grp, GROUP_W), F32), pltpu.VMEM((n_grp + SUBLANES, GROUP_W), F32)],
    )
    return pl.pallas_call(
        functools.partial(_samp_cmp_body, pages_per_step=pages_per_step, n_grp=n_grp, n_cmp=n_cmp,
                          n_sel=n_sel, n_top=min(N_SELECT, n_sel), t_pos=t_pos),
        grid_spec=grid_spec,
        out_shape=[jax.ShapeDtypeStruct((nb, SUBLANES, LANES), F32),
                   jax.ShapeDtypeStruct((nb, N_SELECT, LANES), jnp.int32)],
        compiler_params=_params("parallel", "arbitrary"),
        name="nsa_decode_compress",
    )(page_table, *([cache16] * pages_per_step), cw["pe_lo"], cw["pe_hi"], cw["w_lo"], cw["w_hi"], cw["b1"],
      cw["w2"], cw["b2"], cw["gk"], cos, sin, j2, q4, ovl)


def _samp_sel_body(pt_ref, idx_ref, *refs, n_past_blk, t_pos, win_pos0):
    del pt_ref
    blk_refs = refs[:N_SELECT]
    q4_ref, nslc_ref, win_ref, nwin_ref, gate_ref, ocmp_ref, o_ref = refs[N_SELECT:]
    b = pl.program_id(0)
    q4 = q4_ref[...].astype(BF16)

    n_keys = N_SELECT * SEL_BLOCK
    lane_k = lax.broadcasted_iota(jnp.int32, (1, n_keys), 1)
    jvec = jnp.zeros((1, n_keys), jnp.int32)
    parts = []
    for k in range(N_SELECT):
        j = idx_ref[b, k]
        blk = blk_refs[k][...]
        parts.append(jnp.where(j == n_past_blk, nslc_ref[...], blk[0:SUBLANES]))
        parts.append(blk[SUBLANES:])
        jvec = jnp.where(lane_k // SEL_BLOCK == k, j, jvec)
    kv = jnp.concatenate(parts, axis=0).astype(BF16)
    kpos = jvec * SEL_BLOCK + (lane_k & (SEL_BLOCK - 1))
    p_sel = _masked_softmax_rows(_dot_nt(q4, kv) * ATT_SCALE, (kpos <= t_pos) & (jvec >= 0))
    o_sel = _dot(p_sel.astype(BF16), kv)

    wb = win_ref[...].astype(BF16)
    nw = nwin_ref[...].astype(BF16)
    s_w = _dot_nt(q4, wb) * ATT_SCALE
    s_n = _dot_nt(q4, nw) * ATT_SCALE
    wpos = win_pos0 + lax.broadcasted_iota(jnp.int32, s_w.shape, 1)
    rel = t_pos - wpos
    m_w = (rel >= 0) & (rel < WINDOW) & (wpos >= win_pos0)
    m_n = lax.broadcasted_iota(jnp.int32, s_n.shape, 1) == 0
    s_w = jnp.where(m_w, s_w, NEG_INF)
    s_n = jnp.where(m_n, s_n, NEG_INF)
    mx = jnp.maximum(jnp.max(s_w, axis=-1, keepdims=True), jnp.max(s_n, axis=-1, keepdims=True))
    e_w = jnp.exp(s_w - mx) * m_w.astype(F32)
    e_n = jnp.exp(s_n - mx) * m_n.astype(F32)
    den = jnp.sum(e_w, axis=-1, keepdims=True) + jnp.sum(e_n, axis=-1, keepdims=True)
    o_win = (_dot(e_w.astype(BF16), wb) + _dot(e_n.astype(BF16), nw)) / jnp.maximum(den, 1e-30)
    o_cmp = ocmp_ref[...]
    g = gate_ref[...]
    lane = lax.broadcasted_iota(jnp.int32, (1, LANES), 1)
    outs = []
    for h in range(N_HEADS):
        outs.append(g[0:1, 3 * h:3 * h + 1] * o_cmp[h:h + 1] + g[0:1, 3 * h + 1:3 * h + 2] * o_sel[h:h + 1]
                    + g[0:1, 3 * h + 2:3 * h + 3] * o_win[h:h + 1])
    o_ref[...] = jnp.broadcast_to(_unstack_heads(outs, lane), o_ref.shape)


def _samp_sel(cache_blk, layer, page_table, top_idx, q4, nslc8, win, nwin8, gates8, ocmp, t_pos, win_pos0):
    nb, n_pages = page_table.shape
    blk_per_page = PAGE_SIZE // SEL_BLOCK
    n_past_blk = n_pages * blk_per_page
    wlen = win.shape[1]

    def blk_spec(k):
        def blk_map(b, pt, idx):
            j = jnp.clip(idx[b, k], 0, n_past_blk - 1)
            return (layer, pt[b, j // blk_per_page], j % blk_per_page, 0, 0)
        return pl.BlockSpec((None, None, None, SEL_BLOCK, LANES), blk_map)

    row8 = pl.BlockSpec((None, SUBLANES, LANES), lambda b, pt, idx: (b, 0, 0))
    grid_spec = pltpu.PrefetchScalarGridSpec(
        num_scalar_prefetch=2,
        grid=(nb,),
        in_specs=[blk_spec(k) for k in range(N_SELECT)] + [
            row8, row8, pl.BlockSpec((None, wlen, LANES), lambda b, pt, idx: (b, 0, 0)), row8, row8, row8],
        out_specs=pl.BlockSpec((None, SUBLANES, GROUP_W), lambda b, pt, idx: (b, 0, 0)),
    )
    return pl.pallas_call(
        functools.partial(_samp_sel_body, n_past_blk=n_past_blk, t_pos=t_pos, win_pos0=win_pos0),
        grid_spec=grid_spec,
        out_shape=jax.ShapeDtypeStruct((nb, SUBLANES, GROUP_W), F32),
        compiler_params=_params("parallel"),
        name="nsa_decode_select",
    )(page_table, top_idx, *([cache_blk] * N_SELECT), q4, nslc8, win, nwin8, gates8, ocmp)


def _mix_out_body(x_ref, ya_ref, yb_ref, yc_ref, yd_ref, g_ref, w_ref, o_ref):
    acc = x_ref[...]
    g = g_ref[...]
    for gi, y_ref in enumerate((ya_ref, yb_ref, yc_ref, yd_ref)):
        yn = _rmsnorm_rows(y_ref[...], g[gi:gi + 1]).astype(BF16)
        acc = acc + _dot(yn, w_ref[gi * GROUP_W:(gi + 1) * GROUP_W, :])
    o_ref[...] = acc


def _mix_out(x, ya, yb, yc, yd, g4, w, tm):
    m, d = x.shape
    yspec = pl.BlockSpec((tm, GROUP_W), lambda i: (i, 0))
    return pl.pallas_call(
        _mix_out_body,
        grid=(m // tm,),
        in_specs=[pl.BlockSpec((tm, d), lambda i: (i, 0)), yspec, yspec, yspec, yspec,
                  _const_spec((N_HEADS, GROUP_W)), _const_spec((N_HEADS * GROUP_W, d))],
        out_specs=pl.BlockSpec((tm, d), lambda i: (i, 0)),
        out_shape=jax.ShapeDtypeStruct((m, d), F32),
        compiler_params=_params("parallel"),
        name="mix_out",
    )(x, ya, yb, yc, yd, g4, w)


def _block_diag(blocks):
    n, a, b = blocks.shape
    eye = jnp.eye(n, dtype=blocks.dtype)
    return jnp.einsum("nab,nm->namb", blocks, eye).reshape(n * a, n * b)


def _rope_tables(pos):
    half = HEAD_D // 2
    inv = ROPE_THETA ** (-jnp.arange(half, dtype=F32) / half)
    ang = pos.astype(F32)[:, None] * inv[None, :]
    cos, sin = jnp.cos(ang), jnp.sin(ang)
    cos128 = jnp.concatenate([cos, cos, cos, cos], axis=1)
    sin128 = jnp.concatenate([-sin, sin, -sin, sin], axis=1)
    return cos128, sin128


def _seg_ones(n_seg):
    return _block_diag(jnp.ones((n_seg, HEAD_D, HEAD_D), F32)).astype(BF16)


def _layer_weights(lp):
    w_in = lp["w_in"]
    a_w, b_w, c_w = 2 * GROUP_W, 4 * GROUP_W, 3 * GROUP_W
    w_perm = jnp.concatenate(
        [w_in[:, a_w:a_w + b_w], w_in[:, :a_w], w_in[:, a_w + b_w:a_w + b_w + c_w], w_in[:, a_w + b_w + c_w:],
         jnp.zeros((w_in.shape[0], PROJ_PAD - w_in.shape[1]), F32)], axis=1).astype(BF16)
    lora = HEAD_D
    w1 = lp["nsa_cmp_w1"].reshape(2, 2, CMP_STRIDE, HEAD_D, 2 * HEAD_D)
    pe = lp["nsa_cmp_pe"].reshape(2, 2, CMP_STRIDE, HEAD_D)

    def cmp_half(hf):
        wk = jnp.zeros((CMP_STRIDE, 2 * HEAD_D, 4 * HEAD_D), F32)
        wk = wk.at[:, :HEAD_D, :2 * HEAD_D].set(w1[0, hf]).at[:, HEAD_D:, 2 * HEAD_D:].set(w1[1, hf])
        pk = jnp.concatenate([pe[0, hf], pe[1, hf]], axis=1)
        return wk.reshape(CMP_STRIDE * 2 * HEAD_D, 4 * HEAD_D).astype(BF16), pk.reshape(1, CMP_STRIDE * 2 * HEAD_D)

    w_lo, pe_lo = cmp_half(0)
    w_hi, pe_hi = cmp_half(1)
    ones64 = jnp.ones((HEAD_D,), F32)
    cw = dict(
        w_lo=w_lo, w_hi=w_hi, pe_lo=pe_lo, pe_hi=pe_hi,
        b1=lp["nsa_cmp_b1"].reshape(1, 4 * HEAD_D),
        w2=_block_diag(lp["nsa_cmp_w2"]).astype(BF16),
        b2=lp["nsa_cmp_b2"].reshape(1, 2 * HEAD_D),
        gk=jnp.concatenate([lp["nsa_norm_k"][0], ones64]).reshape(1, LANES),
    )
    zeros_lora = jnp.zeros((lora, GROUP_W), F32)
    return dict(
        norm_ffn=lp["norm_ffn"].reshape(2, 1, D_MODEL),
        wg=lp["ffn_w_gate"].astype(BF16), wu=lp["ffn_w_up"].astype(BF16), wd=lp["ffn_w_down"].astype(BF16),
        norm_mix=lp["norm_mix"].reshape(1, D_MODEL), w_in=w_perm,
        lru_cw=lp["lru_conv_w"], lru_cb=lp["lru_conv_b"].reshape(1, GROUP_W),
        lru_wri=jnp.concatenate([_block_diag(lp["lru_w_r"]), _block_diag(lp["lru_w_i"])], axis=1).astype(BF16),
        lru_bri=jnp.concatenate([lp["lru_b_r"], lp["lru_b_i"]]).reshape(1, 2 * GROUP_W),
        lru_lam=lp["lru_lambda"].reshape(1, GROUP_W), sconv_w=lp["sconv_w"],
        mu=lp["rwkv_mu"].reshape(1, 4 * GROUP_W), w0=lp["rwkv_w0"].reshape(1, GROUP_W),
        w2p=jnp.concatenate([lp["rwkv_w2"], zeros_lora], axis=0).astype(BF16),
        a0=lp["rwkv_a0"].reshape(1, GROUP_W),
        a2p=jnp.concatenate([zeros_lora, lp["rwkv_a2"]], axis=0).astype(BF16),
        g2=lp["rwkv_g2"].astype(BF16),
        k_k=lp["rwkv_k_k"].reshape(1, GROUP_W), k_a=lp["rwkv_k_a"].reshape(1, GROUP_W),
        r_k=lp["rwkv_r_k"].reshape(1, GROUP_W),
        ln_g=lp["rwkv_ln_g"].reshape(1, GROUP_W), ln_b=lp["rwkv_ln_b"].reshape(1, GROUP_W),
        gq=jnp.tile(lp["nsa_norm_q"], 2).reshape(1, LANES),
        gk2=jnp.stack([jnp.concatenate([lp["nsa_norm_k"][1], ones64]),
                       jnp.concatenate([lp["nsa_norm_k"][2], ones64])]),
        cw=cw,
        out_norm=lp["out_norm"].reshape(N_HEADS, GROUP_W), w_out=lp["w_out"].astype(BF16),
    )


def _overlap(n_grp, n_sel_pad, n_sel):
    cs = jnp.arange(n_grp)[:, None] * CMP_STRIDE
    js = jnp.arange(n_sel_pad)[None, :] * SEL_BLOCK
    ov = (cs < js + SEL_BLOCK) & (cs + CMP_LEN > js) & (jnp.arange(n_sel_pad)[None, :] < n_sel)
    return ov.astype(BF16)


def _state_to_bd(s):
    nb = s.shape[0]
    eye = jnp.eye(N_HEADS, dtype=s.dtype)
    return jnp.einsum("bhvk,hg->bhvgk", s, eye).reshape(nb, GROUP_W, GROUP_W)


def _bd_to_state(sbd):
    nb = sbd.shape[0]
    s5 = sbd.reshape(nb, N_HEADS, HEAD_D, N_HEADS, HEAD_D)
    return jnp.stack([s5[:, h, :, h, :] for h in range(N_HEADS)], axis=1)


def _pad_rows_front(x, rows):
    return jnp.pad(x, ((0, 0), (rows - x.shape[1], 0), (0, 0)))


def _round_up(x, m):
    return (x + m - 1) // m * m


def _mixers_recurrent(lw, proj3, cbuf8, h0, sbuf8, shift8, s_bd, j4, tt, chunk, pos0, t_valid):
    ya, yc, hlast, ulast = _lru_sconv(proj3, cbuf8, h0, sbuf8, lw["lru_cw"], lw["lru_cb"], lw["lru_wri"],
                                      lw["lru_bri"], lw["lru_lam"], lw["sconv_w"], tt, pos0)
    yb, s_out = _rwkv(proj3, shift8, s_bd, lw["mu"], lw["w0"], lw["w2p"], lw["a0"], lw["a2p"], lw["g2"],
                      lw["k_k"], lw["k_a"], lw["r_k"], lw["ln_g"], lw["ln_b"], j4, tt, chunk, t_valid)
    return ya, yb, yc, hlast, ulast, s_out


def _prompt_layer(lw, x, nb, t, consts):
    m = nb * t
    tm = 512
    j2, j4 = consts["j2"], consts["j4"]
    x1 = _ffn(x, lw["norm_ffn"][0], lw["wg"][0], lw["wu"][0], lw["wd"][0], tm)
    proj = _proj(x1, lw["norm_mix"], lw["w_in"], tm)
    proj3 = proj.reshape(nb, t, PROJ_PAD)
    zeros8 = jnp.zeros((nb, SUBLANES, GROUP_W), F32)
    ya, yb, yc, hlast, ulast, s_out = _mixers_recurrent(
        lw, proj3, zeros8, jnp.zeros((nb, 1, GROUP_W), F32), zeros8,
        jnp.zeros((nb, SUBLANES, 4 * GROUP_W), F32), jnp.zeros((nb, GROUP_W, GROUP_W), F32), j4,
        tt=256, chunk=HEAD_D, pos0=0, t_valid=t)
    q_r, ncmp, nslc, nwin, gates, slc_bf, win_bf, slc_t, win_t = _nsa_prep(
        proj, consts["cos_p"], consts["sin_p"], lw["gq"], lw["gk2"], j2, tm, rows_per_table=t, with_bf16=True)
    n_grp = t // CMP_STRIDE
    kvc, kvc_t = _compress(ncmp.reshape(nb, n_grp, CMP_STRIDE * LANES), lw["cw"], consts["cos_cp"],
                           consts["sin_cp"], j2)
    n_cmp = (t - CMP_LEN) // CMP_STRIDE + 1
    n_sel = -(-t // SEL_BLOCK)
    yd = _nsa_attn_t(q_r.reshape(nb, t, GROUP_W), gates.reshape(nb, t, LANES), kvc, kvc_t,
                     slc_bf.reshape(nb, t, LANES), slc_t, win_bf.reshape(nb, t, LANES), win_t,
                     consts["ovl_t_p"], consts["expand_p"], tq=128, tk=512, n_cmp=n_cmp, n_sel=n_sel)
    x2 = _mix_out(x1, ya.reshape(m, GROUP_W), yb.reshape(m, GROUP_W), yc.reshape(m, GROUP_W),
                  yd.reshape(m, GROUP_W), lw["out_norm"], lw["w_out"], tm)
    x3 = _ffn(x2, lw["norm_ffn"][1], lw["wg"][1], lw["wu"][1], lw["wd"][1], tm)
    xa = proj3[:, :, COL_XA * GROUP_W:(COL_XA + 1) * GROUP_W]
    wlen = min(WINDOW, t)
    states = (hlast[:, SUBLANES - 1], xa[:, t - 3:], _bd_to_state(s_out), proj3[:, t - 1, :4 * GROUP_W],
              ulast[:, SUBLANES - 2:], nwin.reshape(nb, t, LANES)[:, t - wlen:],
              ncmp.reshape(nb, t // PAGE_SIZE, PAGE_SIZE, LANES), nslc.reshape(nb, t // PAGE_SIZE, PAGE_SIZE, LANES))
    return x3, states


def _sample_layer(lw, layer, x, st, caches, page_table, consts):
    nb = x.shape[0]
    n_pages = page_table.shape[1]
    past_len = n_pages * PAGE_SIZE
    j2, j4 = consts["j2"], consts["j4"]
    lru_h, lru_conv, rwkv_s, rwkv_shift, sconv, win = st
    cache_cmp, cache_slc = caches
    x1 = _ffn(x, lw["norm_ffn"][0], lw["wg"][0], lw["wu"][0], lw["wd"][0], nb)
    proj = _proj(x1, lw["norm_mix"], lw["w_in"], nb)
    proj3 = jnp.pad(proj[:, None, :], ((0, 0), (0, SUBLANES - 1), (0, 0)))
    ya, yb, yc, hlast, ulast, s_out = _mixers_recurrent(
        lw, proj3, _pad_rows_front(lru_conv, SUBLANES), lru_h[:, None, :], _pad_rows_front(sconv, SUBLANES),
        _pad_rows_front(rwkv_shift[:, None, :], SUBLANES), _state_to_bd(rwkv_s), j4,
        tt=SUBLANES, chunk=SUBLANES, pos0=past_len, t_valid=1)
    q_r, ncmp, nslc, nwin, gates = _nsa_prep(proj, consts["cos_s"], consts["sin_s"], lw["gq"], lw["gk2"], j2,
                                             nb, rows_per_table=nb, with_bf16=False)

    def row8(a):
        return jnp.pad(a[:, None, :], ((0, 0), (0, SUBLANES - 1), (0, 0)))

    q4 = jnp.pad(q_r.reshape(nb, N_HEADS, HEAD_D), ((0, 0), (0, SUBLANES - N_HEADS), (0, LANES - HEAD_D)))
    grp_per_page = PAGE_SIZE // CMP_STRIDE
    depth, n_phys = cache_cmp.shape[:2]
    n_cmp = (past_len + 1 - CMP_LEN) // CMP_STRIDE + 1
    n_sel = -(-(past_len + 1) // SEL_BLOCK)
    ocmp, top_idx = _samp_cmp(cache_cmp.reshape(depth, n_phys, grp_per_page, CMP_STRIDE * LANES), layer, page_table,
                              lw["cw"], consts["cos_cs"], consts["sin_cs"], j2, q4, consts["ovl_s"], n_cmp, n_sel,
                              t_pos=past_len, pages_per_step=consts["pages_per_step"])
    wlen = win.shape[1]
    yd8 = _samp_sel(cache_slc.reshape(depth, n_phys, PAGE_SIZE // SEL_BLOCK, SEL_BLOCK, LANES), layer, page_table,
                    top_idx[:, :, 0], q4, row8(nslc), win, row8(nwin), row8(gates), ocmp,
                    t_pos=past_len, win_pos0=past_len - wlen)
    x2 = _mix_out(x1, ya[:, 0], yb[:, 0], yc[:, 0], yd8[:, 0], lw["out_norm"], lw["w_out"], nb)
    x3 = _ffn(x2, lw["norm_ffn"][1], lw["wg"][1], lw["wu"][1], lw["wd"][1], nb)
    xa = proj[:, COL_XA * GROUP_W:(COL_XA + 1) * GROUP_W]
    new_win = jnp.concatenate([win, nwin[:, None, :]], axis=1)
    states = (hlast[:, 0], jnp.concatenate([lru_conv[:, 1:], xa[:, None, :]], axis=1), _bd_to_state(s_out),
              proj[:, :4 * GROUP_W], jnp.concatenate([sconv[:, 1:], ulast[:, 0:1]], axis=1),
              new_win[:, -min(WINDOW, wlen + 1):], ncmp[:, None, :], nslc[:, None, :])
    return x3, states


def kernel(x_prompt, x_sample, state_lru_h, state_lru_conv, state_rwkv_S, state_rwkv_shift, state_sconv, state_nsa_win, cache_nsa_cmp, cache_nsa_slc, page_table, norm_ffn, ffn_w_gate, ffn_w_up, ffn_w_down, norm_mix, w_in, lru_conv_w, lru_conv_b, lru_w_r, lru_b_r, lru_w_i, lru_b_i, lru_lambda, rwkv_mu, rwkv_w0, rwkv_w2, rwkv_a0, rwkv_a2, rwkv_g2, rwkv_k_k, rwkv_k_a, rwkv_r_k, rwkv_ln_g, rwkv_ln_b, sconv_w, nsa_norm_q, nsa_norm_k, nsa_cmp_pe, nsa_cmp_w1, nsa_cmp_b1, nsa_cmp_w2, nsa_cmp_b2, out_norm, w_out):
    params = dict(norm_ffn=norm_ffn, ffn_w_gate=ffn_w_gate, ffn_w_up=ffn_w_up, ffn_w_down=ffn_w_down,
                  norm_mix=norm_mix, w_in=w_in, lru_conv_w=lru_conv_w, lru_conv_b=lru_conv_b, lru_w_r=lru_w_r,
                  lru_b_r=lru_b_r, lru_w_i=lru_w_i, lru_b_i=lru_b_i, lru_lambda=lru_lambda, rwkv_mu=rwkv_mu,
                  rwkv_w0=rwkv_w0, rwkv_w2=rwkv_w2, rwkv_a0=rwkv_a0, rwkv_a2=rwkv_a2, rwkv_g2=rwkv_g2,
                  rwkv_k_k=rwkv_k_k, rwkv_k_a=rwkv_k_a, rwkv_r_k=rwkv_r_k, rwkv_ln_g=rwkv_ln_g,
                  rwkv_ln_b=rwkv_ln_b, sconv_w=sconv_w, nsa_norm_q=nsa_norm_q, nsa_norm_k=nsa_norm_k,
                  nsa_cmp_pe=nsa_cmp_pe, nsa_cmp_w1=nsa_cmp_w1, nsa_cmp_b1=nsa_cmp_b1, nsa_cmp_w2=nsa_cmp_w2,
                  nsa_cmp_b2=nsa_cmp_b2, out_norm=out_norm, w_out=w_out)
    depth = norm_mix.shape[0]
    bp, tp, d = x_prompt.shape
    bs, ts, _ = x_sample.shape
    n_pages = page_table.shape[1]
    past_len = n_pages * PAGE_SIZE
    assert d == D_MODEL and ts == 1 and tp % 512 == 0 and tp >= WINDOW + 128 and past_len >= WINDOW
    assert bs % SUBLANES == 0 or bs < SUBLANES

    n_grp_p = tp // CMP_STRIDE
    n_sel_p = -(-tp // SEL_BLOCK)
    nsp_p = _round_up(n_sel_p, LANES)
    n_grp_s = past_len // CMP_STRIDE
    n_sel_s = -(-(past_len + 1) // SEL_BLOCK)
    nsp_s = _round_up(n_sel_s, LANES)
    cos_p, sin_p = _rope_tables(jnp.arange(tp))
    cos_s, sin_s = _rope_tables(jnp.full((bs,), past_len))
    cos_cp, sin_cp = _rope_tables(jnp.arange(n_grp_p) * CMP_STRIDE + CMP_LEN - 1)
    cos_cs, sin_cs = _rope_tables(jnp.arange(n_grp_s) * CMP_STRIDE + CMP_LEN - 1)
    expand_p = ((jnp.arange(tp)[:, None] // SEL_BLOCK) == jnp.arange(nsp_p)[None, :]).astype(BF16)
    pages_per_step = math.gcd(n_pages, 32)
    consts = dict(
        j2=_seg_ones(2), j4=_seg_ones(N_HEADS), cos_p=cos_p, sin_p=sin_p, cos_s=cos_s, sin_s=sin_s,
        cos_cp=cos_cp, sin_cp=sin_cp, cos_cs=cos_cs, sin_cs=sin_cs,
        ovl_t_p=_overlap(n_grp_p, nsp_p, n_sel_p).T,
        expand_p=expand_p,
        ovl_s=_overlap(n_grp_s, nsp_s, n_sel_s),
        pages_per_step=pages_per_step,
    )

    yp = x_prompt.reshape(bp * tp, d)
    ys = x_sample.reshape(bs, d)
    sp_all, ss_all = [], []
    for l in range(depth):
        lw = _layer_weights({name: arr[l] for name, arr in params.items()})
        yp, sp = _prompt_layer(lw, yp, bp, tp, consts)
        ys, ss = _sample_layer(lw, l, ys, (state_lru_h[l], state_lru_conv[l], state_rwkv_S[l], state_rwkv_shift[l],
                                           state_sconv[l], state_nsa_win[l]),
                               (cache_nsa_cmp, cache_nsa_slc), page_table, consts)
        sp_all.append(sp)
        ss_all.append(ss)
    outs = [yp.reshape(bp, tp, d), ys.reshape(bs, ts, d)]
    for i in range(8):
        outs.append(jnp.stack([s[i] for s in sp_all]))
        outs.append(jnp.stack([s[i] for s in ss_all]))
    return tuple(outs)
```

```python
import functools
import math

import numpy as np
import jax
import jax.numpy as jnp
from jax import lax
from jax.experimental import pallas as pl
from jax.experimental.pallas import tpu as pltpu

F32 = jnp.float32
BF16 = jnp.bfloat16

D_MODEL = 1024
GROUP_W = 256
N_HEADS = 4
HEAD_D = 64
LRU_C = 8.0
RWKV_LN_EPS = 64e-5
CMP_LEN = 32
CMP_STRIDE = 16
SEL_BLOCK = 64
N_SELECT = 16
WINDOW = 512
FORCE_BONUS = 1e4
NEG_INF = -1e30
ROPE_THETA = 10000.0
EPS = 1e-6
PAGE_SIZE = 128
PROJ_PAD = 3072
ATT_SCALE = HEAD_D ** -0.5
RWKV_CHAINS = 8

SUBLANES = 8
LANES = 128
VMEM_LIMIT = 56 * 1024 * 1024

COL_XA, COL_GA, COL_BG, COL_CG, COL_XIN, COL_Q = 4, 5, 6, 7, 8, 9
COL_CMP, COL_SLC, COL_WIN, COL_GL = 20, 21, 22, 23


def _params(*sem):
    return pltpu.CompilerParams(dimension_semantics=sem, vmem_limit_bytes=VMEM_LIMIT)


def _const_spec(shape):
    nd = len(shape)
    return pl.BlockSpec(shape, lambda *_: (0,) * nd)


def _dot(a, b):
    return jnp.dot(a, b, preferred_element_type=F32)


def _dot_nt(a, b):
    return lax.dot_general(a, b, (((1,), (1,)), ((), ())), preferred_element_type=F32)


def _dot_tn(a, b):
    return lax.dot_general(a, b, (((0,), (0,)), ((), ())), preferred_element_type=F32)


def _split3(x):
    h1 = x.astype(BF16)
    r1 = x - h1.astype(F32)
    h2 = r1.astype(BF16)
    h3 = (r1 - h2.astype(F32)).astype(BF16)
    return h1, h2, h3


def _dot_exact_rhs(x, m_bf16):
    h1, h2, h3 = _split3(x)
    return _dot(h1, m_bf16) + _dot(h2, m_bf16) + _dot(h3, m_bf16)


def _gelu_tanh(x):
    return x * (0.5 * (1.0 + jnp.tanh(math.sqrt(2.0 / math.pi) * (x + 0.044715 * (x * x * x)))))


def _sigmoid(x):
    return 1.0 / (1.0 + jnp.exp(-x))


def _softplus(x):
    return jnp.maximum(x, 0.0) + jnp.log1p(jnp.exp(-jnp.abs(x)))


def _rmsnorm_rows(x, g):
    ms = jnp.mean(x * x, axis=-1, keepdims=True)
    return x * lax.rsqrt(ms + EPS) * g


def _masked_softmax_rows(s, mask):
    s = jnp.where(mask, s, NEG_INF)
    m = jnp.max(s, axis=-1, keepdims=True)
    e = jnp.exp(s - m)
    norm = 1.0 / jnp.maximum(jnp.sum(e, axis=-1, keepdims=True), 1e-30)
    return e * jnp.where(m > 0.5 * NEG_INF, norm, 0.0)


def _rope_lanes(x, cos, sin_signed, lane):
    swapped = jnp.where((lane & (HEAD_D - 1)) < HEAD_D // 2,
                        pltpu.roll(x, LANES - HEAD_D // 2, 1), pltpu.roll(x, HEAD_D // 2, 1))
    return x * cos + swapped * sin_signed


def _ffn_body(x_ref, g_ref, wg_ref, wu_ref, wd_ref, o_ref, act_ref, *, f_chunk):
    x = x_ref[...]
    h = _rmsnorm_rows(x, g_ref[...]).astype(BF16)
    d_ff = wg_ref.shape[1]
    for c in range(d_ff // f_chunk):
        sl = slice(c * f_chunk, (c + 1) * f_chunk)
        gate = _dot(h, wg_ref[:, sl])
        up = _dot(h, wu_ref[:, sl])
        act_ref[:, sl] = (gate * _sigmoid(gate) * up).astype(BF16)
    o_ref[...] = x + 0.5 * _dot(act_ref[...], wd_ref[...])


def _ffn(x, g, wg, wu, wd, tm):
    m, d = x.shape
    d_ff = wg.shape[1]
    return pl.pallas_call(
        functools.partial(_ffn_body, f_chunk=256),
        grid=(m // tm,),
        in_specs=[pl.BlockSpec((tm, d), lambda i: (i, 0)), _const_spec((1, d)),
                  _const_spec((d, d_ff)), _const_spec((d, d_ff)), _const_spec((d_ff, d))],
        out_specs=pl.BlockSpec((tm, d), lambda i: (i, 0)),
        out_shape=jax.ShapeDtypeStruct((m, d), F32),
        scratch_shapes=[pltpu.VMEM((tm, d_ff), BF16)],
        compiler_params=_params("parallel"),
        name="ffn",
    )(x, g, wg, wu, wd)


def _proj_body(x_ref, g_ref, w_ref, o_ref, *, n_chunk):
    h = _rmsnorm_rows(x_ref[...], g_ref[...]).astype(BF16)
    for c in range(w_ref.shape[1] // n_chunk):
        sl = slice(c * n_chunk, (c + 1) * n_chunk)
        o_ref[:, sl] = _dot(h, w_ref[:, sl])


def _proj(x, g, w, tm):
    m, d = x.shape
    n = w.shape[1]
    return pl.pallas_call(
        functools.partial(_proj_body, n_chunk=512),
        grid=(m // tm,),
        in_specs=[pl.BlockSpec((tm, d), lambda i: (i, 0)), _const_spec((1, d)), _const_spec((d, n))],
        out_specs=pl.BlockSpec((tm, n), lambda i: (i, 0)),
        out_shape=jax.ShapeDtypeStruct((m, n), F32),
        compiler_params=_params("parallel"),
        name="proj",
    )(x, g, w)


def _lru_sconv_body(xa_ref, ga_ref, bg_ref, cg_ref, xin_ref, cbuf_ref, h0_ref, sbuf_ref,
                    cw_ref, cb_ref, wri_ref, bri_ref, lam_ref, sw_ref,
                    ya_ref, yc_ref, hlast_ref, ulast_ref,
                    extx_ref, extu_ref, h_ref, *, tt, pos0):
    t = pl.program_id(1)
    nt = pl.num_programs(1)

    @pl.when(t == 0)
    def _():
        extx_ref[0:SUBLANES, :] = cbuf_ref[...]
        extu_ref[0:SUBLANES, :] = sbuf_ref[...]
        h_ref[...] = h0_ref[...]

    x = xa_ref[...]
    extx_ref[SUBLANES:SUBLANES + tt, :] = x
    cw = cw_ref[...]
    xc = extx_ref[pl.ds(SUBLANES - 3, tt), :] * cw[0:1]
    xc = xc + extx_ref[pl.ds(SUBLANES - 2, tt), :] * cw[1:2]
    xc = xc + extx_ref[pl.ds(SUBLANES - 1, tt), :] * cw[2:3]
    xc = xc + x * cw[3:4]
    xc = xc + cb_ref[...]
    extx_ref[0:SUBLANES, :] = x[tt - SUBLANES:tt]

    gates = _dot(xc.astype(BF16), wri_ref[...]) + bri_ref[...]
    r = _sigmoid(gates[:, :GROUP_W])
    ig = _sigmoid(gates[:, GROUP_W:])
    log_a = -LRU_C * r * _softplus(-lam_ref[...])
    a = jnp.exp(log_a)
    rows = lax.broadcasted_iota(jnp.int32, (tt, GROUP_W), 0)
    mult = jnp.where(rows + (pos0 + t * tt) == 0, 1.0, jnp.sqrt(1.0 - jnp.exp(2.0 * log_a)))
    b = mult * ig * xc

    s = 1
    while s < tt:
        keep = rows >= s
        a_sh = jnp.where(keep, pltpu.roll(a, s, 0), 1.0)
        b_sh = jnp.where(keep, pltpu.roll(b, s, 0), 0.0)
        b = a * b_sh + b
        a = a * a_sh
        s *= 2
    h = a * h_ref[...] + b
    h_ref[...] = h[tt - 1:tt]
    ya_ref[...] = _gelu_tanh(ga_ref[...]) * h

    u = cg_ref[...] * xin_ref[...]
    extu_ref[SUBLANES:SUBLANES + tt, :] = u
    sw = sw_ref[...]
    yv = extu_ref[pl.ds(SUBLANES - 2, tt), :] * sw[0:1]
    yv = yv + extu_ref[pl.ds(SUBLANES - 1, tt), :] * sw[1:2]
    yv = yv + u * sw[2:3]
    yc_ref[...] = bg_ref[...] * yv
    extu_ref[0:SUBLANES, :] = u[tt - SUBLANES:tt]

    @pl.when(t == nt - 1)
    def _():
        hlast_ref[...] = h[tt - SUBLANES:tt]
        ulast_ref[...] = u[tt - SUBLANES:tt]


def _lru_sconv(proj3, cbuf8, h0, sbuf8, cw, cb, wri, bri, lam, sw, tt, pos0):
    nb, tp, _ = proj3.shape
    w = GROUP_W

    def col(c):
        return pl.BlockSpec((None, tt, w), lambda b, t, c=c: (b, t, c))

    state8 = pl.BlockSpec((None, SUBLANES, w), lambda b, t: (b, 0, 0))
    return pl.pallas_call(
        functools.partial(_lru_sconv_body, tt=tt, pos0=pos0),
        grid=(nb, tp // tt),
        in_specs=[col(COL_XA), col(COL_GA), col(COL_BG), col(COL_CG), col(COL_XIN),
                  state8, pl.BlockSpec((None, 1, w), lambda b, t: (b, 0, 0)), state8,
                  _const_spec((4, w)), _const_spec((1, w)), _const_spec((w, 2 * w)), _const_spec((1, 2 * w)),
                  _const_spec((1, w)), _const_spec((3, w))],
        out_specs=[pl.BlockSpec((None, tt, w), lambda b, t: (b, t, 0)),
                   pl.BlockSpec((None, tt, w), lambda b, t: (b, t, 0)), state8, state8],
        out_shape=[jax.ShapeDtypeStruct((nb, tp, w), F32), jax.ShapeDtypeStruct((nb, tp, w), F32),
                   jax.ShapeDtypeStruct((nb, SUBLANES, w), F32), jax.ShapeDtypeStruct((nb, SUBLANES, w), F32)],
        scratch_shapes=[pltpu.VMEM((tt + SUBLANES, w), F32), pltpu.VMEM((tt + SUBLANES, w), F32),
                        pltpu.VMEM((1, w), F32)],
        compiler_params=_params("parallel", "arbitrary"),
        name="lru_sconv",
    )(proj3, proj3, proj3, proj3, proj3, cbuf8, h0, sbuf8, cw, cb, wri, bri, lam, sw)


def _rwkv_tokens(t, p_ref, mu_ref, w0_ref, w2_ref, a0_ref, a2_ref, g2_ref, kk_ref, ka_ref, rk_ref, j4,
                 ext_ref, w_buf, r_buf, k_buf, v_buf, a_buf, b_buf, g_buf, bon_buf, *, tt, chunk, t_valid):
    w = GROUP_W
    p = p_ref[...]
    ext_ref[SUBLANES:SUBLANES + tt, :] = p
    m = p + (ext_ref[pl.ds(SUBLANES - 1, tt), :] - p) * mu_ref[...]
    ext_ref[0:SUBLANES, :] = p[tt - SUBLANES:tt]
    r = m[:, 0:w]
    k = m[:, w:2 * w]
    v = m[:, 2 * w:3 * w]
    wa = m[:, 3 * w:3 * w + LANES]
    gl = m[:, 3 * w + LANES:4 * w]
    wlin = w0_ref[...] + _dot(jnp.tanh(wa).astype(BF16), w2_ref[...])
    log_decay = -jnp.exp(-_softplus(-wlin) - 0.5)
    ag = _sigmoid(a0_ref[...] + _dot(wa.astype(BF16), a2_ref[...]))
    g_buf[...] = _dot(_sigmoid(gl).astype(BF16), g2_ref[...])
    kk = k * kk_ref[...]
    kk = kk * lax.rsqrt(jnp.maximum(_dot_exact_rhs(kk * kk, j4), 1e-24))
    kf = k * (1.0 + (ag - 1.0) * ka_ref[...])
    rows = lax.broadcasted_iota(jnp.int32, (tt, w), 0)
    if t_valid % tt != 0:
        live = rows + t * tt < t_valid
        log_decay = jnp.where(live, log_decay, 0.0)
        kk = jnp.where(live, kk, 0.0)
        kf = jnp.where(live, kf, 0.0)
        v = jnp.where(live, v, 0.0)
    bon_buf[...] = _dot_exact_rhs(r * kf * rk_ref[...], j4) * v

    rin = rows & (chunk - 1)
    cl = log_decay
    s = 1
    while s < chunk:
        cl = cl + jnp.where(rin >= s, pltpu.roll(cl, s, 0), 0.0)
        s *= 2
    e_neg = jnp.exp(-cl)
    w_buf[...] = cl
    r_buf[...] = r * jnp.exp(cl)
    a_buf[...] = -kk * jnp.exp(cl - log_decay)
    b_buf[...] = kk * ag * e_neg
    k_buf[...] = kf * e_neg
    v_buf[...] = v


def _rwkv2_body(p_ref, shift_ref, sin_ref, mu_ref, w0_ref, w2_ref, a0_ref, a2_ref, g2_ref, kk_ref, ka_ref,
                rk_ref, lng_ref, lnb_ref, j4_ref, y_ref, sout_ref,
                ext_ref, s_ref, w_buf, r_buf, k_buf, v_buf, a_buf, b_buf, g_buf, bon_buf, y_buf,
                *, n_seq, tt, chunk, t_valid):
    t = pl.program_id(1)
    w = GROUP_W
    nh = N_HEADS
    nc = tt // chunk
    j4 = j4_ref[...]

    @pl.when(t == 0)
    def _():
        for i in range(n_seq):
            ext_ref[i, 0:SUBLANES, :] = shift_ref[i]
            s_ref[i] = sin_ref[i]

    for i in range(n_seq):
        _rwkv_tokens(t, p_ref.at[i], mu_ref, w0_ref, w2_ref, a0_ref, a2_ref, g2_ref, kk_ref, ka_ref, rk_ref, j4,
                     ext_ref.at[i], w_buf.at[i], r_buf.at[i], k_buf.at[i], v_buf.at[i], a_buf.at[i], b_buf.at[i],
                     g_buf.at[i], bon_buf.at[i], tt=tt, chunk=chunk, t_valid=t_valid)

    sl = nh * chunk
    hm_rows = lax.broadcasted_iota(jnp.int32, (sl, w), 0) // chunk
    hm_cols = lax.broadcasted_iota(jnp.int32, (sl, w), 1) // HEAD_D
    head_mask = (hm_rows == hm_cols).astype(F32)
    ri = lax.broadcasted_iota(jnp.int32, (sl, sl), 0)
    ci = lax.broadcasted_iota(jnp.int32, (sl, sl), 1)
    same = (ri // chunk) == (ci // chunk)
    strict = (same & ((ri & (chunk - 1)) > (ci & (chunk - 1)))).astype(F32)
    incl = (same & ((ri & (chunk - 1)) >= (ci & (chunk - 1)))).astype(F32)
    eye = (ri == ci).astype(F32)

    def stacked(buf, i, c):
        return jnp.concatenate([buf[i, pl.ds(c * chunk, chunk), :]] * nh, axis=0) * head_mask

    chains = [(i, c) for c in range(nc) for i in range(n_seq)]
    a_s = [stacked(a_buf, i, c).astype(BF16) for i, c in chains]
    r_s = [stacked(r_buf, i, c).astype(BF16) for i, c in chains]
    b_s = [stacked(b_buf, i, c).astype(BF16) for i, c in chains]
    k_s = [stacked(k_buf, i, c).astype(BF16) for i, c in chains]
    v_s = [stacked(v_buf, i, c).astype(BF16) for i, c in chains]
    n_mat = [_dot_nt(a, b) * strict for a, b in zip(a_s, b_s)]
    m_mat = [(_dot_nt(a, k) * strict).astype(BF16) for a, k in zip(a_s, k_s)]
    p_mat = [(_dot_nt(r, b) * incl).astype(BF16) for r, b in zip(r_s, b_s)]
    q_mat = [(_dot_nt(r, k) * incl).astype(BF16) for r, k in zip(r_s, k_s)]
    t_mat = [eye + n for n in n_mat]
    x = n_mat
    step = 2
    while step < chunk:
        xb = [xi.astype(BF16) for xi in x]
        x = [_dot(b, b) for b in xb]
        t_mat = [tm + _dot(tm.astype(BF16), xi.astype(BF16)) for tm, xi in zip(t_mat, x)]
        step *= 2
    t_b = [tm.astype(BF16) for tm in t_mat]
    w_eff = [_dot(tb, a).astype(BF16) for tb, a in zip(t_b, a_s)]
    mv = [_dot(mm, v).astype(BF16) for mm, v in zip(m_mat, v_s)]
    z = [_dot(tb, x_) for tb, x_ in zip(t_b, mv)]
    qv = [_dot(qm, v) for qm, v in zip(q_mat, v_s)]

    for n, (i, c) in enumerate(chains):
        s0 = s_ref[i]
        s0b = s0.astype(BF16)
        ub = (_dot_nt(w_eff[n], s0b) + z[n]).astype(BF16)
        y_s = _dot_nt(r_s[n], s0b) + _dot(p_mat[n], ub) + qv[n]
        yc = y_s[0:chunk]
        for hh in range(1, nh):
            yc = yc + y_s[hh * chunk:(hh + 1) * chunk]
        y_buf[i, pl.ds(c * chunk, chunk), :] = yc
        c_last = jnp.exp(w_buf[i, pl.ds((c + 1) * chunk - 1, 1), :])
        s_ref[i] = (s0 + _dot_tn(ub, b_s[n]) + _dot_tn(v_s[n], k_s[n])) * c_last

    inv_hd = 1.0 / HEAD_D
    for i in range(n_seq):
        y = y_buf[i]
        mean = _dot_exact_rhs(y, j4) * inv_hd
        yc = y - mean
        var = _dot_exact_rhs(yc * yc, j4) * inv_hd
        yn = yc * lax.rsqrt(var + RWKV_LN_EPS) * lng_ref[...] + lnb_ref[...]
        y_ref[i] = (yn + bon_buf[i]) * g_buf[i]
        sout_ref[i] = s_ref[i]


def _rwkv2(proj3, shift8, s_bd, mu, w0, w2p, a0, a2p, g2, kk, ka, rk, lng, lnb, j4, tt, chunk, t_valid, n_seq):
    nb, tp, _ = proj3.shape
    w = GROUP_W
    pw = 4 * w
    vec = _const_spec((1, w))
    return pl.pallas_call(
        functools.partial(_rwkv2_body, n_seq=n_seq, tt=tt, chunk=chunk, t_valid=t_valid),
        grid=(nb // n_seq, tp // tt),
        in_specs=[pl.BlockSpec((n_seq, tt, pw), lambda b, t: (b, t, 0)),
                  pl.BlockSpec((n_seq, SUBLANES, pw), lambda b, t: (b, 0, 0)),
                  pl.BlockSpec((n_seq, w, w), lambda b, t: (b, 0, 0)),
                  _const_spec((1, pw)), vec, _const_spec((LANES, w)), vec, _const_spec((LANES, w)),
                  _const_spec((LANES, w)), vec, vec, vec, vec, vec, _const_spec((w, w))],
        out_specs=[pl.BlockSpec((n_seq, tt, w), lambda b, t: (b, t, 0)),
                   pl.BlockSpec((n_seq, w, w), lambda b, t: (b, 0, 0))],
        out_shape=[jax.ShapeDtypeStruct((nb, tp, w), F32), jax.ShapeDtypeStruct((nb, w, w), F32)],
        scratch_shapes=[pltpu.VMEM((n_seq, tt + SUBLANES, pw), F32), pltpu.VMEM((n_seq, w, w), F32)]
        + [pltpu.VMEM((n_seq, tt, w), F32)] * 9,
        compiler_params=_params("parallel", "arbitrary"),
        name="rwkv7",
    )(proj3, shift8, s_bd, mu, w0, w2p, a0, a2p, g2, kk, ka, rk, lng, lnb, j4)


def _nsa_prep_body(q_ref, cmp_ref, slc_ref, win_ref, gl_ref, cos_ref, sin_ref, gq_ref, gk_ref, j2_ref,
                   qo_ref, cmpo_ref, slco_ref, wino_ref, gate_ref, *bf_refs):
    cos = cos_ref[...]
    sin = sin_ref[...]
    j2 = j2_ref[...]
    lane = lax.broadcasted_iota(jnp.int32, cos.shape, 1)
    inv_hd = 1.0 / HEAD_D
    gq = gq_ref[...]
    q = q_ref[...]
    halves = []
    for c in range(2):
        x = q[:, c * LANES:(c + 1) * LANES]
        ms = _dot_exact_rhs(x * x, j2) * inv_hd
        halves.append(_rope_lanes(x * lax.rsqrt(ms + EPS) * gq, cos, sin, lane))
    qo_ref[...] = jnp.concatenate(halves, axis=1)
    cmpo_ref[...] = cmp_ref[...]
    gk = gk_ref[...]
    is_key = lane < HEAD_D
    for src, dst, row in ((slc_ref, slco_ref, 0), (win_ref, wino_ref, 1)):
        x = src[...]
        ms = _dot_exact_rhs(x * x, j2) * inv_hd
        roped = _rope_lanes(x * lax.rsqrt(ms + EPS) * gk[row:row + 1], cos, sin, lane)
        kv = jnp.where(is_key, roped, x)
        dst[...] = kv
        if bf_refs:
            bf_refs[row][...] = kv.astype(BF16)
            bf_refs[2 + row][...] = kv.T.astype(BF16)
    gate_ref[...] = _sigmoid(gl_ref[...])


def _nsa_prep(proj, cos, sin, gq, gk2, j2, tm, rows_per_table, with_bf16):
    m = proj.shape[0]
    nt_tab = rows_per_table // tm
    o128t = pl.BlockSpec((LANES, tm), lambda i: (0, i))
    bf_specs = [pl.BlockSpec((tm, LANES), lambda i: (i, 0))] * 2 + [o128t] * 2 if with_bf16 else []
    bf_shapes = ([jax.ShapeDtypeStruct((m, LANES), BF16)] * 2 + [jax.ShapeDtypeStruct((LANES, m), BF16)] * 2
                 if with_bf16 else [])

    def col(c, wdt):
        return pl.BlockSpec((tm, wdt), lambda i, c=c: (i, c))

    tab = pl.BlockSpec((tm, LANES), lambda i: (i % nt_tab, 0))
    o128 = pl.BlockSpec((tm, LANES), lambda i: (i, 0))
    return pl.pallas_call(
        _nsa_prep_body,
        grid=(m // tm,),
        in_specs=[col(COL_Q, GROUP_W), col(COL_CMP, LANES), col(COL_SLC, LANES), col(COL_WIN, LANES),
                  col(COL_GL, LANES), tab, tab, _const_spec((1, LANES)), _const_spec((2, LANES)),
                  _const_spec((LANES, LANES))],
        out_specs=[pl.BlockSpec((tm, GROUP_W), lambda i: (i, 0)), o128, o128, o128, o128] + bf_specs,
        out_shape=[jax.ShapeDtypeStruct((m, GROUP_W), F32)] + [jax.ShapeDtypeStruct((m, LANES), F32)] * 4
        + bf_shapes,
        compiler_params=_params("parallel"),
        name="nsa_prep",
    )(proj, proj, proj, proj, proj, cos, sin, gq, gk2, j2)


def _compress_tail(hid_lo, hid_hi_next, b1_ref, w2_ref, b2_ref, gk_ref, cos_ref, sin_ref, j2_ref):
    hidden = _gelu_tanh(hid_lo + hid_hi_next + b1_ref[...])
    kv = _dot(hidden.astype(BF16), w2_ref[...]) + b2_ref[...]
    lane = lax.broadcasted_iota(jnp.int32, kv.shape, 1)
    ms = _dot_exact_rhs(kv * kv, j2_ref[...]) * (1.0 / HEAD_D)
    roped = _rope_lanes(kv * lax.rsqrt(ms + EPS) * gk_ref[...], cos_ref[...], sin_ref[...], lane)
    return jnp.where(lane < HEAD_D, roped, kv)


def _compress_hidden(read_rows, pelo_ref, pehi_ref, wlo_ref, whi_ref):
    pair = 2 * LANES
    lo = hi = None
    for p in range(CMP_STRIDE // 2):
        x = jnp.concatenate([read_rows(2 * p), read_rows(2 * p + 1)], axis=1)
        cs = slice(p * pair, (p + 1) * pair)
        d_lo = _dot((x + pelo_ref[:, cs]).astype(BF16), wlo_ref[cs, :])
        d_hi = _dot((x + pehi_ref[:, cs]).astype(BF16), whi_ref[cs, :])
        lo = d_lo if lo is None else lo + d_lo
        hi = d_hi if hi is None else hi + d_hi
    return lo, hi


def _compress_body(x_ref, pelo_ref, pehi_ref, wlo_ref, whi_ref, b1_ref, w2_ref, b2_ref, gk_ref,
                   cos_ref, sin_ref, j2_ref, o_ref, ot_ref):
    n_grp = x_ref.shape[0] // CMP_STRIDE
    lo, hi = _compress_hidden(lambda j: x_ref[pl.ds(j, n_grp, stride=CMP_STRIDE), :],
                              pelo_ref, pehi_ref, wlo_ref, whi_ref)
    hi_next = pltpu.roll(hi, n_grp - 1, 0)
    kv = _compress_tail(lo, hi_next, b1_ref, w2_ref, b2_ref, gk_ref, cos_ref, sin_ref, j2_ref)
    o_ref[...] = kv.astype(BF16)
    ot_ref[...] = kv.T.astype(BF16)


def _compress(rows, cw, cos, sin, j2):
    nb, t, _ = rows.shape
    n_grp = t // CMP_STRIDE
    wid = CMP_STRIDE * LANES
    return pl.pallas_call(
        _compress_body,
        grid=(nb,),
        in_specs=[pl.BlockSpec((None, t, LANES), lambda b: (b, 0, 0)),
                  _const_spec((1, wid)), _const_spec((1, wid)),
                  _const_spec((wid, GROUP_W)), _const_spec((wid, GROUP_W)), _const_spec((1, GROUP_W)),
                  _const_spec((GROUP_W, LANES)), _const_spec((1, LANES)), _const_spec((1, LANES)),
                  _const_spec((n_grp, LANES)), _const_spec((n_grp, LANES)), _const_spec((LANES, LANES))],
        out_specs=[pl.BlockSpec((None, n_grp, LANES), lambda b: (b, 0, 0)),
                   pl.BlockSpec((LANES, n_grp), lambda b: (0, b))],
        out_shape=[jax.ShapeDtypeStruct((nb, n_grp, LANES), BF16),
                   jax.ShapeDtypeStruct((LANES, nb * n_grp), BF16)],
        compiler_params=_params("parallel"),
        name="nsa_compress",
    )(rows, cw["pe_lo"], cw["pe_hi"], cw["w_lo"], cw["w_hi"], cw["b1"], cw["w2"], cw["b2"], cw["gk"], cos, sin, j2)


def _stack_heads(q, lane):
    parts = []
    for h in range(N_HEADS):
        blk = q[:, (h // 2) * LANES:(h // 2 + 1) * LANES]
        if h % 2 == 1:
            blk = pltpu.roll(blk, HEAD_D, 1)
        parts.append(jnp.where(lane < HEAD_D, blk, 0.0))
    return jnp.concatenate(parts, axis=0)


def _unstack_heads(parts, lane):
    b01 = jnp.where(lane < HEAD_D, pltpu.roll(parts[0], HEAD_D, 1), parts[1])
    b23 = jnp.where(lane < HEAD_D, pltpu.roll(parts[2], HEAD_D, 1), parts[3])
    return jnp.concatenate([b01, b23], axis=1)


def _softmax_cols(st, mask):
    st = jnp.where(mask, st, NEG_INF)
    m = jnp.max(st, axis=0, keepdims=True)
    e = jnp.exp(st - m)
    norm = 1.0 / jnp.maximum(jnp.sum(e, axis=0, keepdims=True), 1e-30)
    return e * jnp.where(m > 0.5 * NEG_INF, norm, 0.0)


def _nsa_attn_t_body(q_ref, gate_ref, kvc_ref, kvct_ref, slc_ref, slct_ref, win_ref, wint_ref, ovl_ref, exp_ref,
                     o_ref, acc_ref, st_ref, e_ref, *, tq, tk, n_cmp, n_sel, n_top, win_span):
    i = pl.program_id(1)
    start = i * tq
    cols = N_HEADS * tq
    lane = lax.broadcasted_iota(jnp.int32, (tq, LANES), 1)
    q4 = _stack_heads(q_ref[...] * ATT_SCALE, lane).astype(BF16)
    tlane = start + (lax.broadcasted_iota(jnp.int32, (1, cols), 1) & (tq - 1))

    kvc = kvc_ref[...]
    ng = kvc.shape[0]
    ncol = lax.broadcasted_iota(jnp.int32, (ng, 1), 0)
    cend = jnp.where(ncol < n_cmp, ncol * CMP_STRIDE + (CMP_LEN - 1), jnp.iinfo(jnp.int32).max)
    pt = _softmax_cols(_dot_nt(kvc, q4), cend <= tlane)
    o_cmp = _dot(kvct_ref[...], pt.astype(BF16))
    psum = pt[:, 0:tq]
    for h in range(1, N_HEADS):
        psum = psum + pt[:, h * tq:(h + 1) * tq]
    ph = psum.astype(BF16)
    pl_ = (psum - ph.astype(F32)).astype(BF16)
    ovl = ovl_ref[...]
    imp_t = _dot(ovl, ph) + _dot(ovl, pl_)

    start0 = pl.multiple_of(jnp.maximum(start + tq - win_span, 0), tq)
    rel = tlane - (start0 + lax.broadcasted_iota(jnp.int32, (win_span, 1), 0))
    pw = _softmax_cols(_dot_nt(win_ref[pl.ds(start0, win_span), :], q4), (rel >= 0) & (rel < WINDOW))
    o_win = _dot(wint_ref[:, pl.ds(start0, win_span)], pw.astype(BF16))

    nsp = imp_t.shape[0]
    jblk = lax.broadcasted_iota(jnp.int32, (nsp, tq), 0)
    tcol = start + lax.broadcasted_iota(jnp.int32, (nsp, tq), 1)
    cur = tcol // SEL_BLOCK
    forced = (jblk == 0) | (jblk == cur) | (jblk == cur - 1)
    valid = (jblk * SEL_BLOCK <= tcol) & (jblk < n_sel)
    score = jnp.where(valid, imp_t + jnp.where(forced, FORCE_BONUS, 0.0), -1.0)
    taken = -3.0
    jf = jblk.astype(F32)
    left = score
    for _ in range(n_top):
        best = jnp.max(left, axis=0, keepdims=True)
        first = jnp.min(jnp.where(left == best, jf, float(nsp)), axis=0, keepdims=True)
        left = jnp.where(jf == first, taken, left)
    bias_t = jnp.where((left == taken) & (score >= 0.0), 0.0, NEG_INF)
    bias = bias_t.T.astype(BF16)
    lhs = jnp.concatenate([q4, jnp.concatenate([bias] * N_HEADS, axis=0)], axis=1)

    acc_ref[...] = jnp.zeros((LANES, cols), F32)
    n_kt = (start + tq + tk - 1) // tk

    def stage_scores(kt, causal):
        off = pl.multiple_of(kt * tk, tk)
        k_aug = jnp.concatenate([slc_ref[pl.ds(off, tk), :], exp_ref[pl.ds(off, tk), :]], axis=1)
        st = _dot_nt(k_aug, lhs)
        if causal:
            kpos = off + lax.broadcasted_iota(jnp.int32, (tk, 1), 0)
            st = jnp.where(kpos <= tlane, st, NEG_INF)
        return st, jnp.max(st.reshape(tk // SUBLANES, SUBLANES, cols), axis=0)

    def stage_exp(st, mt, m_prev, l_prev):
        m_new = jnp.maximum(m_prev, jnp.max(mt, axis=0, keepdims=True))
        m_sub = jnp.where(m_new > 0.5 * NEG_INF, m_new, 0.0)
        e = jnp.exp(st - m_sub[0:1])
        alpha = jnp.exp(m_prev - m_new)
        l_new = alpha * l_prev + jnp.sum(e.reshape(tk // SUBLANES, SUBLANES, cols), axis=0)
        return e.astype(BF16), alpha, m_new, l_new

    def stage_values(kt, e, alpha):
        off = pl.multiple_of(jnp.maximum(kt, 0) * tk, tk)
        acc_ref[...] = alpha[0:1] * acc_ref[...] + _dot(slct_ref[:, pl.ds(off, tk)], e)

    def trip(i, carry, causal):
        mt, alpha, m_run, l_run = carry
        stage_values(i - 2, e_ref[...], alpha)
        e, alpha, m_run, l_run = stage_exp(st_ref[...], mt, m_run, l_run)
        e_ref[...] = e
        st, mt = stage_scores(i, causal)
        st_ref[...] = st
        return mt, alpha, m_run, l_run

    st_ref[...] = jnp.full((tk, cols), NEG_INF, F32)
    e_ref[...] = jnp.zeros((tk, cols), BF16)
    neg8 = st_ref[0:SUBLANES, :]
    zero8 = acc_ref[0:SUBLANES, :]
    carry = (neg8, zero8 + 1.0, neg8, zero8)
    carry = lax.fori_loop(0, n_kt - 1, functools.partial(trip, causal=False), carry)
    mt, alpha, m_run, l_run = trip(n_kt - 1, carry, True)
    stage_values(n_kt - 2, e_ref[...], alpha)
    e, alpha, _, l_part = stage_exp(st_ref[...], mt, m_run, l_run)
    stage_values(n_kt - 1, e, alpha)
    o_sel = acc_ref[...] / jnp.maximum(jnp.sum(l_part, axis=0, keepdims=True), 1e-30)

    gt = gate_ref[...].T
    outs = []
    for h in range(N_HEADS):
        cs = slice(h * tq, (h + 1) * tq)
        outs.append(gt[3 * h:3 * h + 1] * o_cmp[HEAD_D:, cs] + gt[3 * h + 1:3 * h + 2] * o_sel[HEAD_D:, cs]
                    + gt[3 * h + 2:3 * h + 3] * o_win[HEAD_D:, cs])
    o_ref[...] = jnp.concatenate(outs, axis=0).T


def _nsa_attn_t(q, gates, kvc, kvc_t, slc, slc_t, win, win_t, ovl_t, expand, tq, tk, n_cmp, n_sel):
    nb, t, _ = q.shape
    ng = kvc.shape[1]
    nsp = ovl_t.shape[0]
    cols = N_HEADS * tq
    win_span = WINDOW + tq
    per_b = lambda b, i: (b, 0, 0)
    per_b_t = lambda b, i: (0, b)
    return pl.pallas_call(
        functools.partial(_nsa_attn_t_body, tq=tq, tk=tk, n_cmp=n_cmp, n_sel=n_sel,
                          n_top=min(N_SELECT, n_sel), win_span=win_span),
        grid=(nb, t // tq),
        in_specs=[pl.BlockSpec((None, tq, GROUP_W), lambda b, i: (b, i, 0)),
                  pl.BlockSpec((None, tq, LANES), lambda b, i: (b, i, 0)),
                  pl.BlockSpec((None, ng, LANES), per_b), pl.BlockSpec((LANES, ng), per_b_t),
                  pl.BlockSpec((None, t, LANES), per_b), pl.BlockSpec((LANES, t), per_b_t),
                  pl.BlockSpec((None, t, LANES), per_b), pl.BlockSpec((LANES, t), per_b_t),
                  _const_spec((nsp, ng)), _const_spec((t, nsp))],
        out_specs=pl.BlockSpec((None, tq, GROUP_W), lambda b, i: (b, i, 0)),
        out_shape=jax.ShapeDtypeStruct((nb, t, GROUP_W), F32),
        scratch_shapes=[pltpu.VMEM((LANES, cols), F32), pltpu.VMEM((tk, cols), F32), pltpu.VMEM((tk, cols), BF16)],
        compiler_params=_params("parallel", "arbitrary"),
        name="nsa_attn",
    )(q, gates, kvc, kvc_t, slc, slc_t, win, win_t, ovl_t, expand)


def _samp_cmp_body(*refs, pages_per_step, n_grp, n_cmp, n_sel, n_top, t_pos):
    pt_ref = refs[0]
    page_refs = refs[1:1 + pages_per_step]
    (pelo_ref, pehi_ref, wlo_ref, whi_ref, b1_ref, w2_ref, b2_ref, gk_ref, cos_ref, sin_ref, j2_ref,
     q4_ref, ovl_ref, ocmp_ref, idx_ref, lo_ref, hi_ref) = refs[1 + pages_per_step:]
    del pt_ref
    s = pl.program_id(1)
    ns = pl.num_programs(1)
    grp_per_page = PAGE_SIZE // CMP_STRIDE
    rows_step = pages_per_step * grp_per_page

    @pl.when(s == 0)
    def _():
        hi_ref[n_grp:n_grp + SUBLANES, :] = jnp.zeros((SUBLANES, GROUP_W), F32)

    def read_rows(j):
        return jnp.concatenate([pg[pl.ds(j, grp_per_page, stride=CMP_STRIDE), :] for pg in page_refs], axis=0)

    lo, hi = _compress_hidden(read_rows, pelo_ref, pehi_ref, wlo_ref, whi_ref)
    off = pl.multiple_of(s * rows_step, rows_step)
    lo_ref[pl.ds(off, rows_step), :] = lo
    hi_ref[pl.ds(off, rows_step), :] = hi

    @pl.when(s == ns - 1)
    def _():
        kvc = _compress_tail(lo_ref[...], hi_ref[pl.ds(1, n_grp), :], b1_ref, w2_ref, b2_ref, gk_ref,
                             cos_ref, sin_ref, j2_ref).astype(BF16)
        q4 = q4_ref[...].astype(BF16)
        sc = _dot_nt(q4, kvc) * ATT_SCALE
        nidx = lax.broadcasted_iota(jnp.int32, sc.shape, 1)
        p = _masked_softmax_rows(sc, (nidx * CMP_STRIDE + (CMP_LEN - 1) <= t_pos) & (nidx < n_cmp))
        ocmp_ref[...] = _dot(p.astype(BF16), kvc)
        hrow = lax.broadcasted_iota(jnp.int32, p.shape, 0)
        psum = jnp.sum(jnp.where(hrow < N_HEADS, p, 0.0), axis=0, keepdims=True)
        psum8 = jnp.broadcast_to(psum, p.shape)
        ph = psum8.astype(BF16)
        pl_ = (psum8 - ph.astype(F32)).astype(BF16)
        ovl = ovl_ref[...]
        imp = (_dot(ph, ovl) + _dot(pl_, ovl))[0:1]
        nsp = imp.shape[1]
        jrow = lax.broadcasted_iota(jnp.int32, (1, nsp), 1)
        cur = t_pos // SEL_BLOCK
        forced = (jrow == 0) | (jrow == cur) | (jrow == cur - 1)
        valid = (jrow * SEL_BLOCK <= t_pos) & (jrow < n_sel)
        score = jnp.where(valid, imp + jnp.where(forced, FORCE_BONUS, 0.0), -1.0)
        score = jnp.where(jrow < n_sel, score, -2.0)
        s_row = jnp.broadcast_to(score, (nsp, nsp))
        s_col = s_row.T
        ii = lax.broadcasted_iota(jnp.int32, (nsp, nsp), 0)
        jj = lax.broadcasted_iota(jnp.int32, (nsp, nsp), 1)
        beats = (s_col > s_row) | ((s_col == s_row) & (ii < jj))
        rank = jnp.sum(jnp.where(beats, 1.0, 0.0), axis=0, keepdims=True)
        chosen = (rank < n_top) & (score >= 0.0)
        slot = lax.broadcasted_iota(jnp.int32, (N_SELECT, nsp), 0).astype(F32)
        hit = (jnp.broadcast_to(rank, (N_SELECT, nsp)) == slot) & jnp.broadcast_to(chosen, (N_SELECT, nsp))
        jcol = lax.broadcasted_iota(jnp.int32, (N_SELECT, nsp), 1).astype(F32)
        blk = jnp.sum(jnp.where(hit, jcol, 0.0), axis=1, keepdims=True)
        cnt = jnp.sum(jnp.where(hit, 1.0, 0.0), axis=1, keepdims=True)
        blk = jnp.where(cnt > 0.5, blk, -1.0)
        idx_ref[...] = jnp.broadcast_to(blk, (N_SELECT, LANES)).astype(jnp.int32)


def _samp_cmp(cache, layer, page_table, cw, cos, sin, j2, q4, ovl, n_cmp, n_sel, t_pos, pages_per_step):
    nb, n_pages = page_table.shape
    grp_per_page = PAGE_SIZE // CMP_STRIDE
    wid = CMP_STRIDE * LANES
    n_grp = n_pages * grp_per_page
    nsp = ovl.shape[1]

    def page_spec(j):
        return pl.BlockSpec((None, None, PAGE_SIZE, LANES),
                            lambda b, s, pt, j=j: (layer, pt[b, s * pages_per_step + j], 0, 0))

    def cst(shape):
        nd = len(shape)
        return pl.BlockSpec(shape, lambda b, s, pt: (0,) * nd)

    grid_spec = pltpu.PrefetchScalarGridSpec(
        num_scalar_prefetch=1,
        grid=(nb, n_pages // pages_per_step),
        in_specs=[page_spec(j) for j in range(pages_per_step)] + [
            cst((1, wid)), cst((1, wid)), cst((wid, GROUP_W)), cst((wid, GROUP_W)), cst((1, GROUP_W)),
            cst((GROUP_W, LANES)), cst((1, LANES)), cst((1, LANES)),
            cst((n_grp, LANES)), cst((n_grp, LANES)), cst((LANES, LANES)),
            pl.BlockSpec((None, SUBLANES, LANES), lambda b, s, pt: (b, 0, 0)), cst((n_grp, nsp))],
        out_specs=[pl.BlockSpec((None, SUBLANES, LANES), lambda b, s, pt: (b, 0, 0)),
                   pl.BlockSpec((None, N_SELECT, LANES), lambda b, s, pt: (b, 0, 0))],
        scratch_shapes=[pltpu.VMEM((n_grp, GROUP_W), F32), pltpu.VMEM((n_grp + SUBLANES, GROUP_W), F32)],
    )
    return pl.pallas_call(
        functools.partial(_samp_cmp_body, pages_per_step=pages_per_step, n_grp=n_grp, n_cmp=n_cmp,
                          n_sel=n_sel, n_top=min(N_SELECT, n_sel), t_pos=t_pos),
        grid_spec=grid_spec,
        out_shape=[jax.ShapeDtypeStruct((nb, SUBLANES, LANES), F32),
                   jax.ShapeDtypeStruct((nb, N_SELECT, LANES), jnp.int32)],
        compiler_params=_params("parallel", "arbitrary"),
        name="nsa_decode_compress",
    )(page_table, *([cache] * pages_per_step), cw["pe_lo"], cw["pe_hi"], cw["w_lo"], cw["w_hi"], cw["b1"],
      cw["w2"], cw["b2"], cw["gk"], cos, sin, j2, q4, ovl)


def _samp_sel_body(pt_ref, idx_ref, *refs, n_past_blk, t_pos, win_pos0):
    del pt_ref
    blk_refs = refs[:N_SELECT]
    q4_ref, nslc_ref, win_ref, nwin_ref, gate_ref, ocmp_ref, o_ref = refs[N_SELECT:]
    b = pl.program_id(0)
    q4 = q4_ref[...].astype(BF16)

    n_keys = N_SELECT * SEL_BLOCK
    lane_k = lax.broadcasted_iota(jnp.int32, (1, n_keys), 1)
    jvec = jnp.zeros((1, n_keys), jnp.int32)
    parts = []
    for k in range(N_SELECT):
        j = idx_ref[b, k]
        blk = blk_refs[k][...]
        parts.append(jnp.where(j == n_past_blk, nslc_ref[...], blk[0:SUBLANES]))
        parts.append(blk[SUBLANES:])
        jvec = jnp.where(lane_k // SEL_BLOCK == k, j, jvec)
    kv = jnp.concatenate(parts, axis=0).astype(BF16)
    kpos = jvec * SEL_BLOCK + (lane_k & (SEL_BLOCK - 1))
    p_sel = _masked_softmax_rows(_dot_nt(q4, kv) * ATT_SCALE, (kpos <= t_pos) & (jvec >= 0))
    o_sel = _dot(p_sel.astype(BF16), kv)

    wb = win_ref[...].astype(BF16)
    nw = nwin_ref[...].astype(BF16)
    s_w = _dot_nt(q4, wb) * ATT_SCALE
    s_n = _dot_nt(q4, nw) * ATT_SCALE
    wpos = win_pos0 + lax.broadcasted_iota(jnp.int32, s_w.shape, 1)
    rel = t_pos - wpos
    m_w = (rel >= 0) & (rel < WINDOW) & (wpos >= win_pos0)
    m_n = lax.broadcasted_iota(jnp.int32, s_n.shape, 1) == 0
    s_w = jnp.where(m_w, s_w, NEG_INF)
    s_n = jnp.where(m_n, s_n, NEG_INF)
    mx = jnp.maximum(jnp.max(s_w, axis=-1, keepdims=True), jnp.max(s_n, axis=-1, keepdims=True))
    e_w = jnp.exp(s_w - mx) * m_w.astype(F32)
    e_n = jnp.exp(s_n - mx) * m_n.astype(F32)
    den = jnp.sum(e_w, axis=-1, keepdims=True) + jnp.sum(e_n, axis=-1, keepdims=True)
    o_win = (_dot(e_w.astype(BF16), wb) + _dot(e_n.astype(BF16), nw)) / jnp.maximum(den, 1e-30)
    o_cmp = ocmp_ref[...]
    g = gate_ref[...]
    lane = lax.broadcasted_iota(jnp.int32, (1, LANES), 1)
    outs = []
    for h in range(N_HEADS):
        outs.append(g[0:1, 3 * h:3 * h + 1] * o_cmp[h:h + 1] + g[0:1, 3 * h + 1:3 * h + 2] * o_sel[h:h + 1]
                    + g[0:1, 3 * h + 2:3 * h + 3] * o_win[h:h + 1])
    o_ref[...] = jnp.broadcast_to(_unstack_heads(outs, lane), o_ref.shape)


def _samp_sel(cache_blk, layer, page_table, top_idx, q4, nslc8, win, nwin8, gates8, ocmp, t_pos, win_pos0):
    nb, n_pages = page_table.shape
    blk_per_page = PAGE_SIZE // SEL_BLOCK
    n_past_blk = n_pages * blk_per_page
    wlen = win.shape[1]

    def blk_spec(k):
        def blk_map(b, pt, idx):
            j = jnp.clip(idx[b, k], 0, n_past_blk - 1)
            return (layer, pt[b, j // blk_per_page], j % blk_per_page, 0, 0)
        return pl.BlockSpec((None, None, None, SEL_BLOCK, LANES), blk_map)

    row8 = pl.BlockSpec((None, SUBLANES, LANES), lambda b, pt, idx: (b, 0, 0))
    grid_spec = pltpu.PrefetchScalarGridSpec(
        num_scalar_prefetch=2,
        grid=(nb,),
        in_specs=[blk_spec(k) for k in range(N_SELECT)] + [
            row8, row8, pl.BlockSpec((None, wlen, LANES), lambda b, pt, idx: (b, 0, 0)), row8, row8, row8],
        out_specs=pl.BlockSpec((None, SUBLANES, GROUP_W), lambda b, pt, idx: (b, 0, 0)),
    )
    return pl.pallas_call(
        functools.partial(_samp_sel_body, n_past_blk=n_past_blk, t_pos=t_pos, win_pos0=win_pos0),
        grid_spec=grid_spec,
        out_shape=jax.ShapeDtypeStruct((nb, SUBLANES, GROUP_W), F32),
        compiler_params=_params("parallel"),
        name="nsa_decode_select",
    )(page_table, top_idx, *([cache_blk] * N_SELECT), q4, nslc8, win, nwin8, gates8, ocmp)


def _mix_out_body(x_ref, ya_ref, yb_ref, yc_ref, yd_ref, g_ref, w_ref, o_ref):
    acc = x_ref[...]
    g = g_ref[...]
    for gi, y_ref in enumerate((ya_ref, yb_ref, yc_ref, yd_ref)):
        yn = _rmsnorm_rows(y_ref[...], g[gi:gi + 1]).astype(BF16)
        acc = acc + _dot(yn, w_ref[gi * GROUP_W:(gi + 1) * GROUP_W, :])
    o_ref[...] = acc


def _mix_out(x, ya, yb, yc, yd, g4, w, tm):
    m, d = x.shape
    yspec = pl.BlockSpec((tm, GROUP_W), lambda i: (i, 0))
    return pl.pallas_call(
        _mix_out_body,
        grid=(m // tm,),
        in_specs=[pl.BlockSpec((tm, d), lambda i: (i, 0)), yspec, yspec, yspec, yspec,
                  _const_spec((N_HEADS, GROUP_W)), _const_spec((N_HEADS * GROUP_W, d))],
        out_specs=pl.BlockSpec((tm, d), lambda i: (i, 0)),
        out_shape=jax.ShapeDtypeStruct((m, d), F32),
        compiler_params=_params("parallel"),
        name="mix_out",
    )(x, ya, yb, yc, yd, g4, w)


def _block_diag(blocks):
    n, a, b = blocks.shape
    eye = jnp.eye(n, dtype=blocks.dtype)
    return jnp.einsum("nab,nm->namb", blocks, eye).reshape(n * a, n * b)


def _rope_tables(pos):
    half = HEAD_D // 2
    inv = ROPE_THETA ** (-jnp.arange(half, dtype=F32) / half)
    ang = pos.astype(F32)[:, None] * inv[None, :]
    cos, sin = jnp.cos(ang), jnp.sin(ang)
    cos128 = jnp.concatenate([cos, cos, cos, cos], axis=1)
    sin128 = jnp.concatenate([-sin, sin, -sin, sin], axis=1)
    return cos128, sin128


def _seg_ones(n_seg):
    return _block_diag(jnp.ones((n_seg, HEAD_D, HEAD_D), F32)).astype(BF16)


def _layer_weights(lp):
    w_in = lp["w_in"]
    a_w, b_w, c_w = 2 * GROUP_W, 4 * GROUP_W, 3 * GROUP_W
    w_perm = jnp.concatenate(
        [w_in[:, a_w:a_w + b_w], w_in[:, :a_w], w_in[:, a_w + b_w:a_w + b_w + c_w], w_in[:, a_w + b_w + c_w:],
         jnp.zeros((w_in.shape[0], PROJ_PAD - w_in.shape[1]), F32)], axis=1).astype(BF16)
    lora = HEAD_D
    w1 = lp["nsa_cmp_w1"].reshape(2, 2, CMP_STRIDE, HEAD_D, 2 * HEAD_D)
    pe = lp["nsa_cmp_pe"].reshape(2, 2, CMP_STRIDE, HEAD_D)

    def cmp_half(hf):
        wk = jnp.zeros((CMP_STRIDE, 2 * HEAD_D, 4 * HEAD_D), F32)
        wk = wk.at[:, :HEAD_D, :2 * HEAD_D].set(w1[0, hf]).at[:, HEAD_D:, 2 * HEAD_D:].set(w1[1, hf])
        pk = jnp.concatenate([pe[0, hf], pe[1, hf]], axis=1)
        return wk.reshape(CMP_STRIDE * 2 * HEAD_D, 4 * HEAD_D).astype(BF16), pk.reshape(1, CMP_STRIDE * 2 * HEAD_D)

    w_lo, pe_lo = cmp_half(0)
    w_hi, pe_hi = cmp_half(1)
    ones64 = jnp.ones((HEAD_D,), F32)
    cw = dict(
        w_lo=w_lo, w_hi=w_hi, pe_lo=pe_lo, pe_hi=pe_hi,
        b1=lp["nsa_cmp_b1"].reshape(1, 4 * HEAD_D),
        w2=_block_diag(lp["nsa_cmp_w2"]).astype(BF16),
        b2=lp["nsa_cmp_b2"].reshape(1, 2 * HEAD_D),
        gk=jnp.concatenate([lp["nsa_norm_k"][0], ones64]).reshape(1, LANES),
    )
    zeros_lora = jnp.zeros((lora, GROUP_W), F32)
    return dict(
        norm_ffn=lp["norm_ffn"].reshape(2, 1, D_MODEL),
        wg=lp["ffn_w_gate"].astype(BF16), wu=lp["ffn_w_up"].astype(BF16), wd=lp["ffn_w_down"].astype(BF16),
        norm_mix=lp["norm_mix"].reshape(1, D_MODEL), w_in=w_perm,
        lru_cw=lp["lru_conv_w"], lru_cb=lp["lru_conv_b"].reshape(1, GROUP_W),
        lru_wri=jnp.concatenate([_block_diag(lp["lru_w_r"]), _block_diag(lp["lru_w_i"])], axis=1).astype(BF16),
        lru_bri=jnp.concatenate([lp["lru_b_r"], lp["lru_b_i"]]).reshape(1, 2 * GROUP_W),
        lru_lam=lp["lru_lambda"].reshape(1, GROUP_W), sconv_w=lp["sconv_w"],
        mu=lp["rwkv_mu"].reshape(1, 4 * GROUP_W), w0=lp["rwkv_w0"].reshape(1, GROUP_W),
        w2p=jnp.concatenate([lp["rwkv_w2"], zeros_lora], axis=0).astype(BF16),
        a0=lp["rwkv_a0"].reshape(1, GROUP_W),
        a2p=jnp.concatenate([zeros_lora, lp["rwkv_a2"]], axis=0).astype(BF16),
        g2=lp["rwkv_g2"].astype(BF16),
        k_k=lp["rwkv_k_k"].reshape(1, GROUP_W), k_a=lp["rwkv_k_a"].reshape(1, GROUP_W),
        r_k=lp["rwkv_r_k"].reshape(1, GROUP_W),
        ln_g=lp["rwkv_ln_g"].reshape(1, GROUP_W), ln_b=lp["rwkv_ln_b"].reshape(1, GROUP_W),
        gq=jnp.tile(lp["nsa_norm_q"], 2).reshape(1, LANES),
        gk2=jnp.stack([jnp.concatenate([lp["nsa_norm_k"][1], ones64]),
                       jnp.concatenate([lp["nsa_norm_k"][2], ones64])]),
        cw=cw,
        out_norm=lp["out_norm"].reshape(N_HEADS, GROUP_W), w_out=lp["w_out"].astype(BF16),
    )


def _overlap(n_grp, n_sel_pad, n_sel):
    cs = jnp.arange(n_grp)[:, None] * CMP_STRIDE
    js = jnp.arange(n_sel_pad)[None, :] * SEL_BLOCK
    ov = (cs < js + SEL_BLOCK) & (cs + CMP_LEN > js) & (jnp.arange(n_sel_pad)[None, :] < n_sel)
    return ov.astype(BF16)


def _state_to_bd(s):
    nb = s.shape[0]
    eye = jnp.eye(N_HEADS, dtype=s.dtype)
    return jnp.einsum("bhvk,hg->bhvgk", s, eye).reshape(nb, GROUP_W, GROUP_W)


def _bd_to_state(sbd):
    nb = sbd.shape[0]
    s5 = sbd.reshape(nb, N_HEADS, HEAD_D, N_HEADS, HEAD_D)
    return jnp.stack([s5[:, h, :, h, :] for h in range(N_HEADS)], axis=1)


def _pad_rows_front(x, rows):
    return jnp.pad(x, ((0, 0), (rows - x.shape[1], 0), (0, 0)))


def _round_up(x, m):
    return (x + m - 1) // m * m


def _mixers_recurrent(lw, proj3, cbuf8, h0, sbuf8, shift8, s_bd, j4, tt, chunk, pos0, t_valid):
    ya, yc, hlast, ulast = _lru_sconv(proj3, cbuf8, h0, sbuf8, lw["lru_cw"], lw["lru_cb"], lw["lru_wri"],
                                      lw["lru_bri"], lw["lru_lam"], lw["sconv_w"], tt, pos0)
    yb, s_out = _rwkv2(proj3, shift8, s_bd, lw["mu"], lw["w0"], lw["w2p"], lw["a0"], lw["a2p"], lw["g2"],
                      lw["k_k"], lw["k_a"], lw["r_k"], lw["ln_g"], lw["ln_b"], j4, tt, chunk, t_valid,
                      n_seq=math.gcd(proj3.shape[0], max(1, RWKV_CHAINS * chunk // tt)))
    return ya, yb, yc, hlast, ulast, s_out


def _prompt_layer(lw, x, nb, t, consts):
    m = nb * t
    tm = 512
    j2, j4 = consts["j2"], consts["j4"]
    x1 = _ffn(x, lw["norm_ffn"][0], lw["wg"][0], lw["wu"][0], lw["wd"][0], tm)
    proj = _proj(x1, lw["norm_mix"], lw["w_in"], tm)
    proj3 = proj.reshape(nb, t, PROJ_PAD)
    zeros8 = jnp.zeros((nb, SUBLANES, GROUP_W), F32)
    ya, yb, yc, hlast, ulast, s_out = _mixers_recurrent(
        lw, proj3, zeros8, jnp.zeros((nb, 1, GROUP_W), F32), zeros8,
        jnp.zeros((nb, SUBLANES, 4 * GROUP_W), F32), jnp.zeros((nb, GROUP_W, GROUP_W), F32), j4,
        tt=256, chunk=HEAD_D, pos0=0, t_valid=t)
    q_r, ncmp, nslc, nwin, gates, slc_bf, win_bf, slc_t, win_t = _nsa_prep(
        proj, consts["cos_p"], consts["sin_p"], lw["gq"], lw["gk2"], j2, tm, rows_per_table=t, with_bf16=True)
    kvc, kvc_t = _compress(ncmp.reshape(nb, t, LANES), lw["cw"], consts["cos_cp"], consts["sin_cp"], j2)
    n_cmp = (t - CMP_LEN) // CMP_STRIDE + 1
    n_sel = -(-t // SEL_BLOCK)
    yd = _nsa_attn_t(q_r.reshape(nb, t, GROUP_W), gates.reshape(nb, t, LANES), kvc, kvc_t,
                     slc_bf.reshape(nb, t, LANES), slc_t, win_bf.reshape(nb, t, LANES), win_t,
                     consts["ovl_t_p"], consts["expand_p"], tq=128, tk=512, n_cmp=n_cmp, n_sel=n_sel)
    x2 = _mix_out(x1, ya.reshape(m, GROUP_W), yb.reshape(m, GROUP_W), yc.reshape(m, GROUP_W),
                  yd.reshape(m, GROUP_W), lw["out_norm"], lw["w_out"], tm)
    x3 = _ffn(x2, lw["norm_ffn"][1], lw["wg"][1], lw["wu"][1], lw["wd"][1], tm)
    xa = proj3[:, :, COL_XA * GROUP_W:(COL_XA + 1) * GROUP_W]
    wlen = min(WINDOW, t)
    states = (hlast[:, SUBLANES - 1], xa[:, t - 3:], _bd_to_state(s_out), proj3[:, t - 1, :4 * GROUP_W],
              ulast[:, SUBLANES - 2:], nwin.reshape(nb, t, LANES)[:, t - wlen:],
              ncmp.reshape(nb, t // PAGE_SIZE, PAGE_SIZE, LANES), nslc.reshape(nb, t // PAGE_SIZE, PAGE_SIZE, LANES))
    return x3, states


def _sample_layer(lw, layer, x, st, caches, page_table, consts):
    nb = x.shape[0]
    n_pages = page_table.shape[1]
    past_len = n_pages * PAGE_SIZE
    j2, j4 = consts["j2"], consts["j4"]
    lru_h, lru_conv, rwkv_s, rwkv_shift, sconv, win = st
    cache_cmp, cache_slc = caches
    x1 = _ffn(x, lw["norm_ffn"][0], lw["wg"][0], lw["wu"][0], lw["wd"][0], nb)
    proj = _proj(x1, lw["norm_mix"], lw["w_in"], nb)
    proj3 = jnp.pad(proj[:, None, :], ((0, 0), (0, SUBLANES - 1), (0, 0)))
    ya, yb, yc, hlast, ulast, s_out = _mixers_recurrent(
        lw, proj3, _pad_rows_front(lru_conv, SUBLANES), lru_h[:, None, :], _pad_rows_front(sconv, SUBLANES),
        _pad_rows_front(rwkv_shift[:, None, :], SUBLANES), _state_to_bd(rwkv_s), j4,
        tt=SUBLANES, chunk=SUBLANES, pos0=past_len, t_valid=1)
    q_r, ncmp, nslc, nwin, gates = _nsa_prep(proj, consts["cos_s"], consts["sin_s"], lw["gq"], lw["gk2"], j2,
                                             nb, rows_per_table=nb, with_bf16=False)

    def row8(a):
        return jnp.pad(a[:, None, :], ((0, 0), (0, SUBLANES - 1), (0, 0)))

    q4 = jnp.pad(q_r.reshape(nb, N_HEADS, HEAD_D), ((0, 0), (0, SUBLANES - N_HEADS), (0, LANES - HEAD_D)))
    grp_per_page = PAGE_SIZE // CMP_STRIDE
    depth, n_phys = cache_cmp.shape[:2]
    n_cmp = (past_len + 1 - CMP_LEN) // CMP_STRIDE + 1
    n_sel = -(-(past_len + 1) // SEL_BLOCK)
    ocmp, top_idx = _samp_cmp(cache_cmp, layer, page_table, lw["cw"], consts["cos_cs"], consts["sin_cs"], j2, q4,
                              consts["ovl_s"], n_cmp, n_sel, t_pos=past_len,
                              pages_per_step=consts["pages_per_step"])
    wlen = win.shape[1]
    yd8 = _samp_sel(cache_slc.reshape(depth, n_phys, PAGE_SIZE // SEL_BLOCK, SEL_BLOCK, LANES), layer, page_table,
                    top_idx[:, :, 0], q4, row8(nslc), win, row8(nwin), row8(gates), ocmp,
                    t_pos=past_len, win_pos0=past_len - wlen)
    x2 = _mix_out(x1, ya[:, 0], yb[:, 0], yc[:, 0], yd8[:, 0], lw["out_norm"], lw["w_out"], nb)
    x3 = _ffn(x2, lw["norm_ffn"][1], lw["wg"][1], lw["wu"][1], lw["wd"][1], nb)
    xa = proj[:, COL_XA * GROUP_W:(COL_XA + 1) * GROUP_W]
    new_win = jnp.concatenate([win, nwin[:, None, :]], axis=1)
    states = (hlast[:, 0], jnp.concatenate([lru_conv[:, 1:], xa[:, None, :]], axis=1), _bd_to_state(s_out),
              proj[:, :4 * GROUP_W], jnp.concatenate([sconv[:, 1:], ulast[:, 0:1]], axis=1),
              new_win[:, -min(WINDOW, wlen + 1):], ncmp[:, None, :], nslc[:, None, :])
    return x3, states


def kernel(x_prompt, x_sample, state_lru_h, state_lru_conv, state_rwkv_S, state_rwkv_shift, state_sconv, state_nsa_win, cache_nsa_cmp, cache_nsa_slc, page_table, norm_ffn, ffn_w_gate, ffn_w_up, ffn_w_down, norm_mix, w_in, lru_conv_w, lru_conv_b, lru_w_r, lru_b_r, lru_w_i, lru_b_i, lru_lambda, rwkv_mu, rwkv_w0, rwkv_w2, rwkv_a0, rwkv_a2, rwkv_g2, rwkv_k_k, rwkv_k_a, rwkv_r_k, rwkv_ln_g, rwkv_ln_b, sconv_w, nsa_norm_q, nsa_norm_k, nsa_cmp_pe, nsa_cmp_w1, nsa_cmp_b1, nsa_cmp_w2, nsa_cmp_b2, out_norm, w_out):
    params = dict(norm_ffn=norm_ffn, ffn_w_gate=ffn_w_gate, ffn_w_up=ffn_w_up, ffn_w_down=ffn_w_down,
                  norm_mix=norm_mix, w_in=w_in, lru_conv_w=lru_conv_w, lru_conv_b=lru_conv_b, lru_w_r=lru_w_r,
                  lru_b_r=lru_b_r, lru_w_i=lru_w_i, lru_b_i=lru_b_i, lru_lambda=lru_lambda, rwkv_mu=rwkv_mu,
                  rwkv_w0=rwkv_w0, rwkv_w2=rwkv_w2, rwkv_a0=rwkv_a0, rwkv_a2=rwkv_a2, rwkv_g2=rwkv_g2,
                  rwkv_k_k=rwkv_k_k, rwkv_k_a=rwkv_k_a, rwkv_r_k=rwkv_r_k, rwkv_ln_g=rwkv_ln_g,
                  rwkv_ln_b=rwkv_ln_b, sconv_w=sconv_w, nsa_norm_q=nsa_norm_q, nsa_norm_k=nsa_norm_k,
                  nsa_cmp_pe=nsa_cmp_pe, nsa_cmp_w1=nsa_cmp_w1, nsa_cmp_b1=nsa_cmp_b1, nsa_cmp_w2=nsa_cmp_w2,
                  nsa_cmp_b2=nsa_cmp_b2, out_norm=out_norm, w_out=w_out)
    depth = norm_mix.shape[0]
    bp, tp, d = x_prompt.shape
    bs, ts, _ = x_sample.shape
    n_pages = page_table.shape[1]
    past_len = n_pages * PAGE_SIZE
    assert d == D_MODEL and ts == 1 and tp % 512 == 0 and tp >= WINDOW + 128 and past_len >= WINDOW
    assert bs % SUBLANES == 0 or bs < SUBLANES

    n_grp_p = tp // CMP_STRIDE
    n_sel_p = -(-tp // SEL_BLOCK)
    nsp_p = _round_up(n_sel_p, LANES)
    n_grp_s = past_len // CMP_STRIDE
    n_sel_s = -(-(past_len + 1) // SEL_BLOCK)
    nsp_s = _round_up(n_sel_s, LANES)
    cos_p, sin_p = _rope_tables(jnp.arange(tp))
    cos_s, sin_s = _rope_tables(jnp.full((bs,), past_len))
    cos_cp, sin_cp = _rope_tables(jnp.arange(n_grp_p) * CMP_STRIDE + CMP_LEN - 1)
    cos_cs, sin_cs = _rope_tables(jnp.arange(n_grp_s) * CMP_STRIDE + CMP_LEN - 1)
    expand_p = ((jnp.arange(tp)[:, None] // SEL_BLOCK) == jnp.arange(nsp_p)[None, :]).astype(BF16)
    pages_per_step = math.gcd(n_pages, 32)
    consts = dict(
        j2=_seg_ones(2), j4=_seg_ones(N_HEADS), cos_p=cos_p, sin_p=sin_p, cos_s=cos_s, sin_s=sin_s,
        cos_cp=cos_cp, sin_cp=sin_cp, cos_cs=cos_cs, sin_cs=sin_cs,
        ovl_t_p=_overlap(n_grp_p, nsp_p, n_sel_p).T,
        expand_p=expand_p,
        ovl_s=_overlap(n_grp_s, nsp_s, n_sel_s),
        pages_per_step=pages_per_step,
    )

    yp = x_prompt.reshape(bp * tp, d)
    ys = x_sample.reshape(bs, d)
    sp_all, ss_all = [], []
    for l in range(depth):
        lw = _layer_weights({name: arr[l] for name, arr in params.items()})
        yp, sp = _prompt_layer(lw, yp, bp, tp, consts)
        ys, ss = _sample_layer(lw, l, ys, (state_lru_h[l], state_lru_conv[l], state_rwkv_S[l], state_rwkv_shift[l],
                                           state_sconv[l], state_nsa_win[l]),
                               (cache_nsa_cmp, cache_nsa_slc), page_table, consts)
        sp_all.append(sp)
        ss_all.append(ss)
    outs = [yp.reshape(bp, tp, d), ys.reshape(bs, ts, d)]
    for i in range(8):
        outs.append(jnp.stack([s[i] for s in sp_all]))
        outs.append(jnp.stack([s[i] for s in ss_all]))
    return tuple(outs)
```

```python
import functools
import math

import numpy as np
import jax
import jax.numpy as jnp
from jax import lax
from jax.experimental import pallas as pl
from jax.experimental.pallas import tpu as pltpu

F32 = jnp.float32
BF16 = jnp.bfloat16

D_MODEL = 1024
GROUP_W = 256
N_HEADS = 4
HEAD_D = 64
LRU_C = 8.0
RWKV_LN_EPS = 64e-5
CMP_LEN = 32
CMP_STRIDE = 16
SEL_BLOCK = 64
N_SELECT = 16
WINDOW = 512
FORCE_BONUS = 1e4
NEG_INF = -1e30
ROPE_THETA = 10000.0
EPS = 1e-6
PAGE_SIZE = 128
PROJ_PAD = 3072
ATT_SCALE = HEAD_D ** -0.5
RWKV_CHAINS = 8

SUBLANES = 8
LANES = 128
VMEM_LIMIT = 56 * 1024 * 1024

COL_XA, COL_GA, COL_BG, COL_CG, COL_XIN, COL_Q = 4, 5, 6, 7, 8, 9
COL_CMP, COL_SLC, COL_WIN, COL_GL = 20, 21, 22, 23


def _params(*sem):
    return pltpu.CompilerParams(dimension_semantics=sem, vmem_limit_bytes=VMEM_LIMIT)


def _const_spec(shape):
    nd = len(shape)
    return pl.BlockSpec(shape, lambda *_: (0,) * nd)


def _dot(a, b):
    return jnp.dot(a, b, preferred_element_type=F32)


def _dot_nt(a, b):
    return lax.dot_general(a, b, (((1,), (1,)), ((), ())), preferred_element_type=F32)


def _dot_tn(a, b):
    return lax.dot_general(a, b, (((0,), (0,)), ((), ())), preferred_element_type=F32)


def _split3(x):
    h1 = x.astype(BF16)
    r1 = x - h1.astype(F32)
    h2 = r1.astype(BF16)
    h3 = (r1 - h2.astype(F32)).astype(BF16)
    return h1, h2, h3


def _dot_exact_rhs(x, m_bf16):
    h1, h2, h3 = _split3(x)
    return _dot(h1, m_bf16) + _dot(h2, m_bf16) + _dot(h3, m_bf16)


def _gelu_tanh(x):
    return x * (0.5 * (1.0 + jnp.tanh(math.sqrt(2.0 / math.pi) * (x + 0.044715 * (x * x * x)))))


def _sigmoid(x):
    return 1.0 / (1.0 + jnp.exp(-x))


def _softplus(x):
    return jnp.maximum(x, 0.0) + jnp.log1p(jnp.exp(-jnp.abs(x)))


def _rmsnorm_rows(x, g):
    ms = jnp.mean(x * x, axis=-1, keepdims=True)
    return x * lax.rsqrt(ms + EPS) * g


def _masked_softmax_rows(s, mask):
    s = jnp.where(mask, s, NEG_INF)
    m = jnp.max(s, axis=-1, keepdims=True)
    e = jnp.exp(s - m)
    norm = 1.0 / jnp.maximum(jnp.sum(e, axis=-1, keepdims=True), 1e-30)
    return e * jnp.where(m > 0.5 * NEG_INF, norm, 0.0)


def _rope_lanes(x, cos, sin_signed, lane):
    swapped = jnp.where((lane & (HEAD_D - 1)) < HEAD_D // 2,
                        pltpu.roll(x, LANES - HEAD_D // 2, 1), pltpu.roll(x, HEAD_D // 2, 1))
    return x * cos + swapped * sin_signed


def _ffn_body(*refs, f_chunk, n_mix):
    x_ref = refs[0]
    y_refs = refs[1:1 + n_mix]
    if n_mix:
        gm_ref, wo_ref = refs[1 + n_mix:3 + n_mix]
    g_ref, wg_ref, wu_ref, wd_ref, o_ref, act_ref = refs[1 + n_mix + (2 if n_mix else 0):]
    x = x_ref[...]
    for gi, y_ref in enumerate(y_refs):
        yn = _rmsnorm_rows(y_ref[...], gm_ref[gi:gi + 1, :]).astype(BF16)
        x = x + _dot(yn, wo_ref[gi * GROUP_W:(gi + 1) * GROUP_W, :])
    h = _rmsnorm_rows(x, g_ref[...]).astype(BF16)
    d_ff = wg_ref.shape[1]
    for c in range(d_ff // f_chunk):
        sl = slice(c * f_chunk, (c + 1) * f_chunk)
        gate = _dot(h, wg_ref[:, sl])
        up = _dot(h, wu_ref[:, sl])
        act_ref[:, sl] = (gate * _sigmoid(gate) * up).astype(BF16)
    o_ref[...] = x + 0.5 * _dot(act_ref[...], wd_ref[...])


def _ffn(x, g, ffn_w, layer, which, tm, mix=None):
    m, d = x.shape
    wg, wu, wd = ffn_w
    d_ff = wg.shape[3]
    once = pl.Buffered(1)

    def wspec(a, b):
        return pl.BlockSpec((None, None, a, b), lambda i: (layer, which, 0, 0), pipeline_mode=once)

    mix_specs, mix_args = [], []
    if mix is not None:
        yspec = pl.BlockSpec((tm, GROUP_W), lambda i: (i, 0))
        mix_specs = [yspec] * 4 + [_const_spec((N_HEADS, GROUP_W)),
                                   pl.BlockSpec((None, N_HEADS * GROUP_W, d), lambda i: (layer, 0, 0),
                                                pipeline_mode=once)]
        mix_args = list(mix)
    return pl.pallas_call(
        functools.partial(_ffn_body, f_chunk=256, n_mix=4 if mix is not None else 0),
        grid=(m // tm,),
        in_specs=[pl.BlockSpec((tm, d), lambda i: (i, 0))] + mix_specs
        + [_const_spec((1, d)), wspec(d, d_ff), wspec(d, d_ff), wspec(d_ff, d)],
        out_specs=pl.BlockSpec((tm, d), lambda i: (i, 0)),
        out_shape=jax.ShapeDtypeStruct((m, d), F32),
        scratch_shapes=[pltpu.VMEM((tm, d_ff), BF16)],
        compiler_params=_params("parallel"),
        name="ffn_mix" if mix is not None else "ffn",
    )(x, *mix_args, g, wg, wu, wd)


def _proj_body(x_ref, g_ref, w_ref, o_ref, *, n_chunk):
    h = _rmsnorm_rows(x_ref[...], g_ref[...]).astype(BF16)
    for c in range(w_ref.shape[1] // n_chunk):
        sl = slice(c * n_chunk, (c + 1) * n_chunk)
        o_ref[:, sl] = _dot(h, w_ref[:, sl])


def _proj(x, g, w, tm):
    m, d = x.shape
    n = w.shape[1]
    return pl.pallas_call(
        functools.partial(_proj_body, n_chunk=512),
        grid=(m // tm,),
        in_specs=[pl.BlockSpec((tm, d), lambda i: (i, 0)), _const_spec((1, d)), _const_spec((d, n))],
        out_specs=pl.BlockSpec((tm, n), lambda i: (i, 0)),
        out_shape=jax.ShapeDtypeStruct((m, n), F32),
        compiler_params=_params("parallel"),
        name="proj",
    )(x, g, w)


def _lru_sconv_body(xa_ref, ga_ref, bg_ref, cg_ref, xin_ref, cbuf_ref, h0_ref, sbuf_ref,
                    cw_ref, cb_ref, wri_ref, bri_ref, lam_ref, sw_ref,
                    ya_ref, yc_ref, hlast_ref, ulast_ref,
                    extx_ref, extu_ref, h_ref, *, tt, pos0):
    t = pl.program_id(1)
    nt = pl.num_programs(1)

    @pl.when(t == 0)
    def _():
        extx_ref[0:SUBLANES, :] = cbuf_ref[...]
        extu_ref[0:SUBLANES, :] = sbuf_ref[...]
        h_ref[...] = h0_ref[...]

    x = xa_ref[...]
    extx_ref[SUBLANES:SUBLANES + tt, :] = x
    cw = cw_ref[...]
    xc = extx_ref[pl.ds(SUBLANES - 3, tt), :] * cw[0:1]
    xc = xc + extx_ref[pl.ds(SUBLANES - 2, tt), :] * cw[1:2]
    xc = xc + extx_ref[pl.ds(SUBLANES - 1, tt), :] * cw[2:3]
    xc = xc + x * cw[3:4]
    xc = xc + cb_ref[...]
    extx_ref[0:SUBLANES, :] = x[tt - SUBLANES:tt]

    gates = _dot(xc.astype(BF16), wri_ref[...]) + bri_ref[...]
    r = _sigmoid(gates[:, :GROUP_W])
    ig = _sigmoid(gates[:, GROUP_W:])
    log_a = -LRU_C * r * _softplus(-lam_ref[...])
    a = jnp.exp(log_a)
    rows = lax.broadcasted_iota(jnp.int32, (tt, GROUP_W), 0)
    mult = jnp.where(rows + (pos0 + t * tt) == 0, 1.0, jnp.sqrt(1.0 - jnp.exp(2.0 * log_a)))
    b = mult * ig * xc

    s = 1
    while s < tt:
        keep = rows >= s
        a_sh = jnp.where(keep, pltpu.roll(a, s, 0), 1.0)
        b_sh = jnp.where(keep, pltpu.roll(b, s, 0), 0.0)
        b = a * b_sh + b
        a = a * a_sh
        s *= 2
    h = a * h_ref[...] + b
    h_ref[...] = h[tt - 1:tt]
    ya_ref[...] = _gelu_tanh(ga_ref[...]) * h

    u = cg_ref[...] * xin_ref[...]
    extu_ref[SUBLANES:SUBLANES + tt, :] = u
    sw = sw_ref[...]
    yv = extu_ref[pl.ds(SUBLANES - 2, tt), :] * sw[0:1]
    yv = yv + extu_ref[pl.ds(SUBLANES - 1, tt), :] * sw[1:2]
    yv = yv + u * sw[2:3]
    yc_ref[...] = bg_ref[...] * yv
    extu_ref[0:SUBLANES, :] = u[tt - SUBLANES:tt]

    @pl.when(t == nt - 1)
    def _():
        hlast_ref[...] = h[tt - SUBLANES:tt]
        ulast_ref[...] = u[tt - SUBLANES:tt]


def _lru_sconv(proj3, cbuf8, h0, sbuf8, cw, cb, wri, bri, lam, sw, tt, pos0):
    nb, tp, _ = proj3.shape
    w = GROUP_W

    def col(c):
        return pl.BlockSpec((None, tt, w), lambda b, t, c=c: (b, t, c))

    state8 = pl.BlockSpec((None, SUBLANES, w), lambda b, t: (b, 0, 0))
    return pl.pallas_call(
        functools.partial(_lru_sconv_body, tt=tt, pos0=pos0),
        grid=(nb, tp // tt),
        in_specs=[col(COL_XA), col(COL_GA), col(COL_BG), col(COL_CG), col(COL_XIN),
                  state8, pl.BlockSpec((None, 1, w), lambda b, t: (b, 0, 0)), state8,
                  _const_spec((4, w)), _const_spec((1, w)), _const_spec((w, 2 * w)), _const_spec((1, 2 * w)),
                  _const_spec((1, w)), _const_spec((3, w))],
        out_specs=[pl.BlockSpec((None, tt, w), lambda b, t: (b, t, 0)),
                   pl.BlockSpec((None, tt, w), lambda b, t: (b, t, 0)), state8, state8],
        out_shape=[jax.ShapeDtypeStruct((nb, tp, w), F32), jax.ShapeDtypeStruct((nb, tp, w), F32),
                   jax.ShapeDtypeStruct((nb, SUBLANES, w), F32), jax.ShapeDtypeStruct((nb, SUBLANES, w), F32)],
        scratch_shapes=[pltpu.VMEM((tt + SUBLANES, w), F32), pltpu.VMEM((tt + SUBLANES, w), F32),
                        pltpu.VMEM((1, w), F32)],
        compiler_params=_params("parallel", "arbitrary"),
        name="lru_sconv",
    )(proj3, proj3, proj3, proj3, proj3, cbuf8, h0, sbuf8, cw, cb, wri, bri, lam, sw)


def _rwkv_tokens(t, p_ref, mu_ref, w0_ref, w2_ref, a0_ref, a2_ref, g2_ref, kk_ref, ka_ref, rk_ref, j4,
                 ext_ref, w_buf, r_buf, k_buf, v_buf, a_buf, b_buf, g_buf, bon_buf, *, tt, chunk, t_valid):
    w = GROUP_W
    p = p_ref[...]
    ext_ref[SUBLANES:SUBLANES + tt, :] = p
    m = p + (ext_ref[pl.ds(SUBLANES - 1, tt), :] - p) * mu_ref[...]
    ext_ref[0:SUBLANES, :] = p[tt - SUBLANES:tt]
    r = m[:, 0:w]
    k = m[:, w:2 * w]
    v = m[:, 2 * w:3 * w]
    wa = m[:, 3 * w:3 * w + LANES]
    gl = m[:, 3 * w + LANES:4 * w]
    wlin = w0_ref[...] + _dot(jnp.tanh(wa).astype(BF16), w2_ref[...])
    log_decay = -jnp.exp(-_softplus(-wlin) - 0.5)
    ag = _sigmoid(a0_ref[...] + _dot(wa.astype(BF16), a2_ref[...]))
    g_buf[...] = _dot(_sigmoid(gl).astype(BF16), g2_ref[...])
    kk = k * kk_ref[...]
    kk = kk * lax.rsqrt(jnp.maximum(_dot_exact_rhs(kk * kk, j4), 1e-24))
    kf = k * (1.0 + (ag - 1.0) * ka_ref[...])
    rows = lax.broadcasted_iota(jnp.int32, (tt, w), 0)
    if t_valid % tt != 0:
        live = rows + t * tt < t_valid
        log_decay = jnp.where(live, log_decay, 0.0)
        kk = jnp.where(live, kk, 0.0)
        kf = jnp.where(live, kf, 0.0)
        v = jnp.where(live, v, 0.0)
    bon_buf[...] = _dot_exact_rhs(r * kf * rk_ref[...], j4) * v

    rin = rows & (chunk - 1)
    cl = log_decay
    s = 1
    while s < chunk:
        cl = cl + jnp.where(rin >= s, pltpu.roll(cl, s, 0), 0.0)
        s *= 2
    e_neg = jnp.exp(-cl)
    w_buf[...] = cl
    r_buf[...] = r * jnp.exp(cl)
    a_buf[...] = -kk * jnp.exp(cl - log_decay)
    b_buf[...] = kk * ag * e_neg
    k_buf[...] = kf * e_neg
    v_buf[...] = v


def _rwkv2_body(p_ref, shift_ref, sin_ref, mu_ref, w0_ref, w2_ref, a0_ref, a2_ref, g2_ref, kk_ref, ka_ref,
                rk_ref, lng_ref, lnb_ref, j4_ref, y_ref, sout_ref,
                ext_ref, s_ref, w_buf, r_buf, k_buf, v_buf, a_buf, b_buf, g_buf, bon_buf, y_buf,
                *, n_seq, tt, chunk, t_valid):
    t = pl.program_id(1)
    w = GROUP_W
    nh = N_HEADS
    nc = tt // chunk
    j4 = j4_ref[...]

    heads = [slice(h * HEAD_D, (h + 1) * HEAD_D) for h in range(nh)]

    @pl.when(t == 0)
    def _():
        for i in range(n_seq):
            ext_ref[i, 0:SUBLANES, :] = shift_ref[i]
            s_ref[i] = jnp.zeros((w, w), F32)
            for h, hs in enumerate(heads):
                s_ref[i, hs, hs] = sin_ref[i, h]

    for i in range(n_seq):
        _rwkv_tokens(t, p_ref.at[i], mu_ref, w0_ref, w2_ref, a0_ref, a2_ref, g2_ref, kk_ref, ka_ref, rk_ref, j4,
                     ext_ref.at[i], w_buf.at[i], r_buf.at[i], k_buf.at[i], v_buf.at[i], a_buf.at[i], b_buf.at[i],
                     g_buf.at[i], bon_buf.at[i], tt=tt, chunk=chunk, t_valid=t_valid)

    sl = nh * chunk
    hm_rows = lax.broadcasted_iota(jnp.int32, (sl, w), 0) // chunk
    hm_cols = lax.broadcasted_iota(jnp.int32, (sl, w), 1) // HEAD_D
    head_mask = (hm_rows == hm_cols).astype(F32)
    ri = lax.broadcasted_iota(jnp.int32, (sl, sl), 0)
    ci = lax.broadcasted_iota(jnp.int32, (sl, sl), 1)
    same = (ri // chunk) == (ci // chunk)
    strict = (same & ((ri & (chunk - 1)) > (ci & (chunk - 1)))).astype(F32)
    incl = (same & ((ri & (chunk - 1)) >= (ci & (chunk - 1)))).astype(F32)
    eye = (ri == ci).astype(F32)

    def stacked(buf, i, c):
        return jnp.concatenate([buf[i, pl.ds(c * chunk, chunk), :]] * nh, axis=0) * head_mask

    chains = [(i, c) for c in range(nc) for i in range(n_seq)]
    a_s = [stacked(a_buf, i, c).astype(BF16) for i, c in chains]
    r_s = [stacked(r_buf, i, c).astype(BF16) for i, c in chains]
    b_s = [stacked(b_buf, i, c).astype(BF16) for i, c in chains]
    k_s = [stacked(k_buf, i, c).astype(BF16) for i, c in chains]
    v_s = [stacked(v_buf, i, c).astype(BF16) for i, c in chains]
    n_mat = [_dot_nt(a, b) * strict for a, b in zip(a_s, b_s)]
    m_mat = [(_dot_nt(a, k) * strict).astype(BF16) for a, k in zip(a_s, k_s)]
    p_mat = [(_dot_nt(r, b) * incl).astype(BF16) for r, b in zip(r_s, b_s)]
    q_mat = [(_dot_nt(r, k) * incl).astype(BF16) for r, k in zip(r_s, k_s)]
    t_mat = [eye + n for n in n_mat]
    x = n_mat
    step = 2
    while step < chunk:
        xb = [xi.astype(BF16) for xi in x]
        x = [_dot(b, b) for b in xb]
        t_mat = [tm + _dot(tm.astype(BF16), xi.astype(BF16)) for tm, xi in zip(t_mat, x)]
        step *= 2
    t_b = [tm.astype(BF16) for tm in t_mat]
    w_eff = [_dot(tb, a).astype(BF16) for tb, a in zip(t_b, a_s)]
    mv = [_dot(mm, v).astype(BF16) for mm, v in zip(m_mat, v_s)]
    z = [_dot(tb, x_) for tb, x_ in zip(t_b, mv)]
    qv = [_dot(qm, v) for qm, v in zip(q_mat, v_s)]

    for n, (i, c) in enumerate(chains):
        s0 = s_ref[i]
        s0b = s0.astype(BF16)
        ub = (_dot_nt(w_eff[n], s0b) + z[n]).astype(BF16)
        y_s = _dot_nt(r_s[n], s0b) + _dot(p_mat[n], ub) + qv[n]
        yc = y_s[0:chunk]
        for hh in range(1, nh):
            yc = yc + y_s[hh * chunk:(hh + 1) * chunk]
        y_buf[i, pl.ds(c * chunk, chunk), :] = yc
        c_last = jnp.exp(w_buf[i, pl.ds((c + 1) * chunk - 1, 1), :])
        s_ref[i] = (s0 + _dot_tn(ub, b_s[n]) + _dot_tn(v_s[n], k_s[n])) * c_last

    inv_hd = 1.0 / HEAD_D
    for i in range(n_seq):
        y = y_buf[i]
        mean = _dot_exact_rhs(y, j4) * inv_hd
        yc = y - mean
        var = _dot_exact_rhs(yc * yc, j4) * inv_hd
        yn = yc * lax.rsqrt(var + RWKV_LN_EPS) * lng_ref[...] + lnb_ref[...]
        y_ref[i] = (yn + bon_buf[i]) * g_buf[i]
        for h, hs in enumerate(heads):
            sout_ref[i, h] = s_ref[i, hs, hs]


def _rwkv2(proj3, shift8, s_heads, mu, w0, w2p, a0, a2p, g2, kk, ka, rk, lng, lnb, j4, tt, chunk, t_valid, n_seq):
    nb, tp, _ = proj3.shape
    w = GROUP_W
    pw = 4 * w
    vec = _const_spec((1, w))
    state_spec = pl.BlockSpec((n_seq, N_HEADS, HEAD_D, HEAD_D), lambda b, t: (b, 0, 0, 0))
    return pl.pallas_call(
        functools.partial(_rwkv2_body, n_seq=n_seq, tt=tt, chunk=chunk, t_valid=t_valid),
        grid=(nb // n_seq, tp // tt),
        in_specs=[pl.BlockSpec((n_seq, tt, pw), lambda b, t: (b, t, 0)),
                  pl.BlockSpec((n_seq, SUBLANES, pw), lambda b, t: (b, 0, 0)), state_spec,
                  _const_spec((1, pw)), vec, _const_spec((LANES, w)), vec, _const_spec((LANES, w)),
                  _const_spec((LANES, w)), vec, vec, vec, vec, vec, _const_spec((w, w))],
        out_specs=[pl.BlockSpec((n_seq, tt, w), lambda b, t: (b, t, 0)), state_spec],
        out_shape=[jax.ShapeDtypeStruct((nb, tp, w), F32), jax.ShapeDtypeStruct(s_heads.shape, F32)],
        scratch_shapes=[pltpu.VMEM((n_seq, tt + SUBLANES, pw), F32), pltpu.VMEM((n_seq, w, w), F32)]
        + [pltpu.VMEM((n_seq, tt, w), F32)] * 9,
        compiler_params=_params("parallel", "arbitrary"),
        name="rwkv7",
    )(proj3, shift8, s_heads, mu, w0, w2p, a0, a2p, g2, kk, ka, rk, lng, lnb, j4)


def _nsa_prep_body(q_ref, cmp_ref, slc_ref, win_ref, gl_ref, cos_ref, sin_ref, gq_ref, gk_ref, j2_ref,
                   qo_ref, cmpo_ref, slco_ref, wino_ref, gate_ref, *bf_refs):
    cos = cos_ref[...]
    sin = sin_ref[...]
    j2 = j2_ref[...]
    lane = lax.broadcasted_iota(jnp.int32, cos.shape, 1)
    inv_hd = 1.0 / HEAD_D
    gq = gq_ref[...]
    q = q_ref[...]
    halves = []
    for c in range(2):
        x = q[:, c * LANES:(c + 1) * LANES]
        ms = _dot_exact_rhs(x * x, j2) * inv_hd
        halves.append(_rope_lanes(x * lax.rsqrt(ms + EPS) * gq, cos, sin, lane))
    qo_ref[...] = jnp.concatenate(halves, axis=1)
    cmpo_ref[...] = cmp_ref[...]
    gk = gk_ref[...]
    is_key = lane < HEAD_D
    for src, dst, row in ((slc_ref, slco_ref, 0), (win_ref, wino_ref, 1)):
        x = src[...]
        ms = _dot_exact_rhs(x * x, j2) * inv_hd
        roped = _rope_lanes(x * lax.rsqrt(ms + EPS) * gk[row:row + 1], cos, sin, lane)
        kv = jnp.where(is_key, roped, x)
        dst[...] = kv
        if bf_refs:
            bf_refs[row][...] = kv.astype(BF16)
            bf_refs[2 + row][...] = kv.T.astype(BF16)
    gate_ref[...] = _sigmoid(gl_ref[...])


def _nsa_prep(proj, cos, sin, gq, gk2, j2, tm, rows_per_table, with_bf16):
    m = proj.shape[0]
    nt_tab = rows_per_table // tm
    o128t = pl.BlockSpec((LANES, tm), lambda i: (0, i))
    bf_specs = [pl.BlockSpec((tm, LANES), lambda i: (i, 0))] * 2 + [o128t] * 2 if with_bf16 else []
    bf_shapes = ([jax.ShapeDtypeStruct((m, LANES), BF16)] * 2 + [jax.ShapeDtypeStruct((LANES, m), BF16)] * 2
                 if with_bf16 else [])

    def col(c, wdt):
        return pl.BlockSpec((tm, wdt), lambda i, c=c: (i, c))

    tab = pl.BlockSpec((tm, LANES), lambda i: (i % nt_tab, 0))
    o128 = pl.BlockSpec((tm, LANES), lambda i: (i, 0))
    return pl.pallas_call(
        _nsa_prep_body,
        grid=(m // tm,),
        in_specs=[col(COL_Q, GROUP_W), col(COL_CMP, LANES), col(COL_SLC, LANES), col(COL_WIN, LANES),
                  col(COL_GL, LANES), tab, tab, _const_spec((1, LANES)), _const_spec((2, LANES)),
                  _const_spec((LANES, LANES))],
        out_specs=[pl.BlockSpec((tm, GROUP_W), lambda i: (i, 0)), o128, o128, o128, o128] + bf_specs,
        out_shape=[jax.ShapeDtypeStruct((m, GROUP_W), F32)] + [jax.ShapeDtypeStruct((m, LANES), F32)] * 4
        + bf_shapes,
        compiler_params=_params("parallel"),
        name="nsa_prep",
    )(proj, proj, proj, proj, proj, cos, sin, gq, gk2, j2)


def _compress_tail(hid_lo, hid_hi_next, b1_ref, w2_ref, b2_ref, gk_ref, cos_ref, sin_ref, j2_ref):
    hidden = _gelu_tanh(hid_lo + hid_hi_next + b1_ref[...])
    kv = _dot(hidden.astype(BF16), w2_ref[...]) + b2_ref[...]
    lane = lax.broadcasted_iota(jnp.int32, kv.shape, 1)
    ms = _dot_exact_rhs(kv * kv, j2_ref[...]) * (1.0 / HEAD_D)
    roped = _rope_lanes(kv * lax.rsqrt(ms + EPS) * gk_ref[...], cos_ref[...], sin_ref[...], lane)
    return jnp.where(lane < HEAD_D, roped, kv)


def _compress_hidden(read_rows, pelo_ref, pehi_ref, wlo_ref, whi_ref):
    pair = 2 * LANES
    lo = hi = None
    for p in range(CMP_STRIDE // 2):
        x = jnp.concatenate([read_rows(2 * p), read_rows(2 * p + 1)], axis=1)
        cs = slice(p * pair, (p + 1) * pair)
        d_lo = _dot((x + pelo_ref[:, cs]).astype(BF16), wlo_ref[cs, :])
        d_hi = _dot((x + pehi_ref[:, cs]).astype(BF16), whi_ref[cs, :])
        lo = d_lo if lo is None else lo + d_lo
        hi = d_hi if hi is None else hi + d_hi
    return lo, hi


def _compress_body(x_ref, pelo_ref, pehi_ref, wlo_ref, whi_ref, b1_ref, w2_ref, b2_ref, gk_ref,
                   cos_ref, sin_ref, j2_ref, o_ref, ot_ref):
    n_grp = x_ref.shape[0] // CMP_STRIDE
    lo, hi = _compress_hidden(lambda j: x_ref[pl.ds(j, n_grp, stride=CMP_STRIDE), :],
                              pelo_ref, pehi_ref, wlo_ref, whi_ref)
    hi_next = pltpu.roll(hi, n_grp - 1, 0)
    kv = _compress_tail(lo, hi_next, b1_ref, w2_ref, b2_ref, gk_ref, cos_ref, sin_ref, j2_ref)
    o_ref[...] = kv.astype(BF16)
    ot_ref[...] = kv.T.astype(BF16)


def _compress(rows, cw, cos, sin, j2):
    nb, t, _ = rows.shape
    n_grp = t // CMP_STRIDE
    wid = CMP_STRIDE * LANES
    return pl.pallas_call(
        _compress_body,
        grid=(nb,),
        in_specs=[pl.BlockSpec((None, t, LANES), lambda b: (b, 0, 0)),
                  _const_spec((1, wid)), _const_spec((1, wid)),
                  _const_spec((wid, GROUP_W)), _const_spec((wid, GROUP_W)), _const_spec((1, GROUP_W)),
                  _const_spec((GROUP_W, LANES)), _const_spec((1, LANES)), _const_spec((1, LANES)),
                  _const_spec((n_grp, LANES)), _const_spec((n_grp, LANES)), _const_spec((LANES, LANES))],
        out_specs=[pl.BlockSpec((None, n_grp, LANES), lambda b: (b, 0, 0)),
                   pl.BlockSpec((LANES, n_grp), lambda b: (0, b))],
        out_shape=[jax.ShapeDtypeStruct((nb, n_grp, LANES), BF16),
                   jax.ShapeDtypeStruct((LANES, nb * n_grp), BF16)],
        compiler_params=_params("parallel"),
        name="nsa_compress",
    )(rows, cw["pe_lo"], cw["pe_hi"], cw["w_lo"], cw["w_hi"], cw["b1"], cw["w2"], cw["b2"], cw["gk"], cos, sin, j2)


def _stack_heads(q, lane):
    parts = []
    for h in range(N_HEADS):
        blk = q[:, (h // 2) * LANES:(h // 2 + 1) * LANES]
        if h % 2 == 1:
            blk = pltpu.roll(blk, HEAD_D, 1)
        parts.append(jnp.where(lane < HEAD_D, blk, 0.0))
    return jnp.concatenate(parts, axis=0)


def _unstack_heads(parts, lane):
    b01 = jnp.where(lane < HEAD_D, pltpu.roll(parts[0], HEAD_D, 1), parts[1])
    b23 = jnp.where(lane < HEAD_D, pltpu.roll(parts[2], HEAD_D, 1), parts[3])
    return jnp.concatenate([b01, b23], axis=1)


def _softmax_cols(st, mask):
    st = jnp.where(mask, st, NEG_INF)
    m = jnp.max(st, axis=0, keepdims=True)
    e = jnp.exp(st - m)
    norm = 1.0 / jnp.maximum(jnp.sum(e, axis=0, keepdims=True), 1e-30)
    return e * jnp.where(m > 0.5 * NEG_INF, norm, 0.0)


def _nsa_attn_t_body(q_ref, gate_ref, kvc_ref, kvct_ref, slc_ref, slct_ref, win_ref, wint_ref, ovl_ref, exp_ref,
                     o_ref, acc_ref, st_ref, e_ref, *, tq, tk, n_cmp, n_sel, n_top, win_span):
    i = pl.program_id(1)
    start = i * tq
    cols = N_HEADS * tq
    lane = lax.broadcasted_iota(jnp.int32, (tq, LANES), 1)
    q4 = _stack_heads(q_ref[...] * ATT_SCALE, lane).astype(BF16)
    tlane = start + (lax.broadcasted_iota(jnp.int32, (1, cols), 1) & (tq - 1))

    kvc = kvc_ref[...]
    ng = kvc.shape[0]
    ncol = lax.broadcasted_iota(jnp.int32, (ng, 1), 0)
    cend = jnp.where(ncol < n_cmp, ncol * CMP_STRIDE + (CMP_LEN - 1), jnp.iinfo(jnp.int32).max)
    pt = _softmax_cols(_dot_nt(kvc, q4), cend <= tlane)
    o_cmp = _dot(kvct_ref[...], pt.astype(BF16))
    psum = pt[:, 0:tq]
    for h in range(1, N_HEADS):
        psum = psum + pt[:, h * tq:(h + 1) * tq]
    ph = psum.astype(BF16)
    pl_ = (psum - ph.astype(F32)).astype(BF16)
    ovl = ovl_ref[...]
    imp_t = _dot(ovl, ph) + _dot(ovl, pl_)

    start0 = pl.multiple_of(jnp.maximum(start + tq - win_span, 0), tq)
    rel = tlane - (start0 + lax.broadcasted_iota(jnp.int32, (win_span, 1), 0))
    pw = _softmax_cols(_dot_nt(win_ref[pl.ds(start0, win_span), :], q4), (rel >= 0) & (rel < WINDOW))
    o_win = _dot(wint_ref[:, pl.ds(start0, win_span)], pw.astype(BF16))

    nsp = imp_t.shape[0]
    jblk = lax.broadcasted_iota(jnp.int32, (nsp, tq), 0)
    tcol = start + lax.broadcasted_iota(jnp.int32, (nsp, tq), 1)
    cur = tcol // SEL_BLOCK
    forced = (jblk == 0) | (jblk == cur) | (jblk == cur - 1)
    valid = (jblk * SEL_BLOCK <= tcol) & (jblk < n_sel)
    score = jnp.where(valid, imp_t + jnp.where(forced, FORCE_BONUS, 0.0), -1.0)
    taken = -3.0
    jf = jblk.astype(F32)
    left = score
    for _ in range(n_top):
        best = jnp.max(left, axis=0, keepdims=True)
        first = jnp.min(jnp.where(left == best, jf, float(nsp)), axis=0, keepdims=True)
        left = jnp.where(jf == first, taken, left)
    bias_t = jnp.where((left == taken) & (score >= 0.0), 0.0, NEG_INF)
    bias = bias_t.T.astype(BF16)
    lhs = jnp.concatenate([q4, jnp.concatenate([bias] * N_HEADS, axis=0)], axis=1)

    acc_ref[...] = jnp.zeros((LANES, cols), F32)
    n_kt = (start + tq + tk - 1) // tk

    def stage_scores(kt, causal):
        off = pl.multiple_of(kt * tk, tk)
        k_aug = jnp.concatenate([slc_ref[pl.ds(off, tk), :], exp_ref[pl.ds(off, tk), :]], axis=1)
        st = _dot_nt(k_aug, lhs)
        if causal:
            kpos = off + lax.broadcasted_iota(jnp.int32, (tk, 1), 0)
            st = jnp.where(kpos <= tlane, st, NEG_INF)
        return st, jnp.max(st.reshape(tk // SUBLANES, SUBLANES, cols), axis=0)

    def stage_exp(st, mt, m_prev, l_prev):
        m_new = jnp.maximum(m_prev, jnp.max(mt, axis=0, keepdims=True))
        m_sub = jnp.where(m_new > 0.5 * NEG_INF, m_new, 0.0)
        e = jnp.exp(st - m_sub[0:1])
        alpha = jnp.exp(m_prev - m_new)
        l_new = alpha * l_prev + jnp.sum(e.reshape(tk // SUBLANES, SUBLANES, cols), axis=0)
        return e.astype(BF16), alpha, m_new, l_new

    def stage_values(kt, e, alpha):
        off = pl.multiple_of(jnp.maximum(kt, 0) * tk, tk)
        acc_ref[...] = alpha[0:1] * acc_ref[...] + _dot(slct_ref[:, pl.ds(off, tk)], e)

    def trip(i, carry, causal):
        mt, alpha, m_run, l_run = carry
        stage_values(i - 2, e_ref[...], alpha)
        e, alpha, m_run, l_run = stage_exp(st_ref[...], mt, m_run, l_run)
        e_ref[...] = e
        st, mt = stage_scores(i, causal)
        st_ref[...] = st
        return mt, alpha, m_run, l_run

    st_ref[...] = jnp.full((tk, cols), NEG_INF, F32)
    e_ref[...] = jnp.zeros((tk, cols), BF16)
    neg8 = st_ref[0:SUBLANES, :]
    zero8 = acc_ref[0:SUBLANES, :]
    carry = (neg8, zero8 + 1.0, neg8, zero8)
    carry = lax.fori_loop(0, n_kt - 1, functools.partial(trip, causal=False), carry)
    mt, alpha, m_run, l_run = trip(n_kt - 1, carry, True)
    stage_values(n_kt - 2, e_ref[...], alpha)
    e, alpha, _, l_part = stage_exp(st_ref[...], mt, m_run, l_run)
    stage_values(n_kt - 1, e, alpha)
    o_sel = acc_ref[...] / jnp.maximum(jnp.sum(l_part, axis=0, keepdims=True), 1e-30)

    gt = gate_ref[...].T
    outs = []
    for h in range(N_HEADS):
        cs = slice(h * tq, (h + 1) * tq)
        outs.append(gt[3 * h:3 * h + 1] * o_cmp[HEAD_D:, cs] + gt[3 * h + 1:3 * h + 2] * o_sel[HEAD_D:, cs]
                    + gt[3 * h + 2:3 * h + 3] * o_win[HEAD_D:, cs])
    o_ref[...] = jnp.concatenate(outs, axis=0).T


def _nsa_attn_t(q, gates, kvc, kvc_t, slc, slc_t, win, win_t, ovl_t, expand, tq, tk, n_cmp, n_sel):
    nb, t, _ = q.shape
    ng = kvc.shape[1]
    nsp = ovl_t.shape[0]
    cols = N_HEADS * tq
    win_span = WINDOW + tq
    per_b = lambda b, i: (b, 0, 0)
    per_b_t = lambda b, i: (0, b)
    return pl.pallas_call(
        functools.partial(_nsa_attn_t_body, tq=tq, tk=tk, n_cmp=n_cmp, n_sel=n_sel,
                          n_top=min(N_SELECT, n_sel), win_span=win_span),
        grid=(nb, t // tq),
        in_specs=[pl.BlockSpec((None, tq, GROUP_W), lambda b, i: (b, i, 0)),
                  pl.BlockSpec((None, tq, LANES), lambda b, i: (b, i, 0)),
                  pl.BlockSpec((None, ng, LANES), per_b), pl.BlockSpec((LANES, ng), per_b_t),
                  pl.BlockSpec((None, t, LANES), per_b), pl.BlockSpec((LANES, t), per_b_t),
                  pl.BlockSpec((None, t, LANES), per_b), pl.BlockSpec((LANES, t), per_b_t),
                  _const_spec((nsp, ng)), _const_spec((t, nsp))],
        out_specs=pl.BlockSpec((None, tq, GROUP_W), lambda b, i: (b, i, 0)),
        out_shape=jax.ShapeDtypeStruct((nb, t, GROUP_W), F32),
        scratch_shapes=[pltpu.VMEM((LANES, cols), F32), pltpu.VMEM((tk, cols), F32), pltpu.VMEM((tk, cols), BF16)],
        compiler_params=_params("parallel", "arbitrary"),
        name="nsa_attn",
    )(q, gates, kvc, kvc_t, slc, slc_t, win, win_t, ovl_t, expand)


def _samp_cmp_body(*refs, pages_per_step, n_grp, n_cmp, n_sel, n_top, t_pos):
    pt_ref = refs[0]
    page_refs = refs[1:1 + pages_per_step]
    (pelo_ref, pehi_ref, wlo_ref, whi_ref, b1_ref, w2_ref, b2_ref, gk_ref, cos_ref, sin_ref, j2_ref,
     q4_ref, ovl_ref, ocmp_ref, idx_ref, lo_ref, hi_ref) = refs[1 + pages_per_step:]
    del pt_ref
    s = pl.program_id(1)
    ns = pl.num_programs(1)
    grp_per_page = PAGE_SIZE // CMP_STRIDE
    rows_step = pages_per_step * grp_per_page

    @pl.when(s == 0)
    def _():
        hi_ref[n_grp:n_grp + SUBLANES, :] = jnp.zeros((SUBLANES, GROUP_W), F32)

    def read_rows(j):
        return jnp.concatenate([pg[pl.ds(j, grp_per_page, stride=CMP_STRIDE), :] for pg in page_refs], axis=0)

    lo, hi = _compress_hidden(read_rows, pelo_ref, pehi_ref, wlo_ref, whi_ref)
    off = pl.multiple_of(s * rows_step, rows_step)
    lo_ref[pl.ds(off, rows_step), :] = lo
    hi_ref[pl.ds(off, rows_step), :] = hi

    @pl.when(s == ns - 1)
    def _():
        kvc = _compress_tail(lo_ref[...], hi_ref[pl.ds(1, n_grp), :], b1_ref, w2_ref, b2_ref, gk_ref,
                             cos_ref, sin_ref, j2_ref).astype(BF16)
        q4 = q4_ref[...].astype(BF16)
        sc = _dot_nt(q4, kvc) * ATT_SCALE
        nidx = lax.broadcasted_iota(jnp.int32, sc.shape, 1)
        p = _masked_softmax_rows(sc, (nidx * CMP_STRIDE + (CMP_LEN - 1) <= t_pos) & (nidx < n_cmp))
        ocmp_ref[...] = _dot(p.astype(BF16), kvc)
        hrow = lax.broadcasted_iota(jnp.int32, p.shape, 0)
        psum = jnp.sum(jnp.where(hrow < N_HEADS, p, 0.0), axis=0, keepdims=True)
        psum8 = jnp.broadcast_to(psum, p.shape)
        ph = psum8.astype(BF16)
        pl_ = (psum8 - ph.astype(F32)).astype(BF16)
        ovl = ovl_ref[...]
        imp = (_dot(ph, ovl) + _dot(pl_, ovl))[0:1]
        nsp = imp.shape[1]
        jrow = lax.broadcasted_iota(jnp.int32, (1, nsp), 1)
        cur = t_pos // SEL_BLOCK
        forced = (jrow == 0) | (jrow == cur) | (jrow == cur - 1)
        valid = (jrow * SEL_BLOCK <= t_pos) & (jrow < n_sel)
        score = jnp.where(valid, imp + jnp.where(forced, FORCE_BONUS, 0.0), -1.0)
        score = jnp.where(jrow < n_sel, score, -2.0)
        s_row = jnp.broadcast_to(score, (nsp, nsp))
        s_col = s_row.T
        ii = lax.broadcasted_iota(jnp.int32, (nsp, nsp), 0)
        jj = lax.broadcasted_iota(jnp.int32, (nsp, nsp), 1)
        beats = (s_col > s_row) | ((s_col == s_row) & (ii < jj))
        rank = jnp.sum(jnp.where(beats, 1.0, 0.0), axis=0, keepdims=True)
        chosen = (rank < n_top) & (score >= 0.0)
        slot = lax.broadcasted_iota(jnp.int32, (N_SELECT, nsp), 0).astype(F32)
        hit = (jnp.broadcast_to(rank, (N_SELECT, nsp)) == slot) & jnp.broadcast_to(chosen, (N_SELECT, nsp))
        jcol = lax.broadcasted_iota(jnp.int32, (N_SELECT, nsp), 1).astype(F32)
        blk = jnp.sum(jnp.where(hit, jcol, 0.0), axis=1, keepdims=True)
        cnt = jnp.sum(jnp.where(hit, 1.0, 0.0), axis=1, keepdims=True)
        blk = jnp.where(cnt > 0.5, blk, -1.0)
        idx_ref[...] = jnp.broadcast_to(blk, (N_SELECT, LANES)).astype(jnp.int32)


def _samp_cmp(cache, layer, page_table, cw, cos, sin, j2, q4, ovl, n_cmp, n_sel, t_pos, pages_per_step):
    nb, n_pages = page_table.shape
    grp_per_page = PAGE_SIZE // CMP_STRIDE
    wid = CMP_STRIDE * LANES
    n_grp = n_pages * grp_per_page
    nsp = ovl.shape[1]

    def page_spec(j):
        return pl.BlockSpec((None, None, PAGE_SIZE, LANES),
                            lambda b, s, pt, j=j: (layer, pt[b, s * pages_per_step + j], 0, 0))

    def cst(shape):
        nd = len(shape)
        return pl.BlockSpec(shape, lambda b, s, pt: (0,) * nd)

    grid_spec = pltpu.PrefetchScalarGridSpec(
        num_scalar_prefetch=1,
        grid=(nb, n_pages // pages_per_step),
        in_specs=[page_spec(j) for j in range(pages_per_step)] + [
            cst((1, wid)), cst((1, wid)), cst((wid, GROUP_W)), cst((wid, GROUP_W)), cst((1, GROUP_W)),
            cst((GROUP_W, LANES)), cst((1, LANES)), cst((1, LANES)),
            cst((n_grp, LANES)), cst((n_grp, LANES)), cst((LANES, LANES)),
            pl.BlockSpec((None, SUBLANES, LANES), lambda b, s, pt: (b, 0, 0)), cst((n_grp, nsp))],
        out_specs=[pl.BlockSpec((None, SUBLANES, LANES), lambda b, s, pt: (b, 0, 0)),
                   pl.BlockSpec((None, N_SELECT, LANES), lambda b, s, pt: (b, 0, 0))],
        scratch_shapes=[pltpu.VMEM((n_grp, GROUP_W), F32), pltpu.VMEM((n_grp + SUBLANES, GROUP_W), F32)],
    )
    return pl.pallas_call(
        functools.partial(_samp_cmp_body, pages_per_step=pages_per_step, n_grp=n_grp, n_cmp=n_cmp,
                          n_sel=n_sel, n_top=min(N_SELECT, n_sel), t_pos=t_pos),
        grid_spec=grid_spec,
        out_shape=[jax.ShapeDtypeStruct((nb, SUBLANES, LANES), F32),
                   jax.ShapeDtypeStruct((nb, N_SELECT, LANES), jnp.int32)],
        compiler_params=_params("parallel", "arbitrary"),
        name="nsa_decode_compress",
    )(page_table, *([cache] * pages_per_step), cw["pe_lo"], cw["pe_hi"], cw["w_lo"], cw["w_hi"], cw["b1"],
      cw["w2"], cw["b2"], cw["gk"], cos, sin, j2, q4, ovl)


def _samp_sel_body(pt_ref, idx_ref, *refs, n_past_blk, t_pos, win_pos0):
    del pt_ref
    blk_refs = refs[:N_SELECT]
    q4_ref, nslc_ref, win_ref, nwin_ref, gate_ref, ocmp_ref, o_ref = refs[N_SELECT:]
    b = pl.program_id(0)
    q4 = q4_ref[...].astype(BF16)

    n_keys = N_SELECT * SEL_BLOCK
    lane_k = lax.broadcasted_iota(jnp.int32, (1, n_keys), 1)
    jvec = jnp.zeros((1, n_keys), jnp.int32)
    parts = []
    for k in range(N_SELECT):
        j = idx_ref[b, k]
        blk = blk_refs[k][...]
        parts.append(jnp.where(j == n_past_blk, nslc_ref[...], blk[0:SUBLANES]))
        parts.append(blk[SUBLANES:])
        jvec = jnp.where(lane_k // SEL_BLOCK == k, j, jvec)
    kv = jnp.concatenate(parts, axis=0).astype(BF16)
    kpos = jvec * SEL_BLOCK + (lane_k & (SEL_BLOCK - 1))
    p_sel = _masked_softmax_rows(_dot_nt(q4, kv) * ATT_SCALE, (kpos <= t_pos) & (jvec >= 0))
    o_sel = _dot(p_sel.astype(BF16), kv)

    wb = win_ref[...].astype(BF16)
    nw = nwin_ref[...].astype(BF16)
    s_w = _dot_nt(q4, wb) * ATT_SCALE
    s_n = _dot_nt(q4, nw) * ATT_SCALE
    wpos = win_pos0 + lax.broadcasted_iota(jnp.int32, s_w.shape, 1)
    rel = t_pos - wpos
    m_w = (rel >= 0) & (rel < WINDOW) & (wpos >= win_pos0)
    m_n = lax.broadcasted_iota(jnp.int32, s_n.shape, 1) == 0
    s_w = jnp.where(m_w, s_w, NEG_INF)
    s_n = jnp.where(m_n, s_n, NEG_INF)
    mx = jnp.maximum(jnp.max(s_w, axis=-1, keepdims=True), jnp.max(s_n, axis=-1, keepdims=True))
    e_w = jnp.exp(s_w - mx) * m_w.astype(F32)
    e_n = jnp.exp(s_n - mx) * m_n.astype(F32)
    den = jnp.sum(e_w, axis=-1, keepdims=True) + jnp.sum(e_n, axis=-1, keepdims=True)
    o_win = (_dot(e_w.astype(BF16), wb) + _dot(e_n.astype(BF16), nw)) / jnp.maximum(den, 1e-30)
    o_cmp = ocmp_ref[...]
    g = gate_ref[...]
    lane = lax.broadcasted_iota(jnp.int32, (1, LANES), 1)
    outs = []
    for h in range(N_HEADS):
        outs.append(g[0:1, 3 * h:3 * h + 1] * o_cmp[h:h + 1] + g[0:1, 3 * h + 1:3 * h + 2] * o_sel[h:h + 1]
                    + g[0:1, 3 * h + 2:3 * h + 3] * o_win[h:h + 1])
    o_ref[...] = jnp.broadcast_to(_unstack_heads(outs, lane), o_ref.shape)


def _samp_sel(cache_blk, layer, page_table, top_idx, q4, nslc8, win, nwin8, gates8, ocmp, t_pos, win_pos0):
    nb, n_pages = page_table.shape
    blk_per_page = PAGE_SIZE // SEL_BLOCK
    n_past_blk = n_pages * blk_per_page
    wlen = win.shape[2]

    def blk_spec(k):
        def blk_map(b, pt, idx):
            j = jnp.clip(idx[b, k], 0, n_past_blk - 1)
            return (layer, pt[b, j // blk_per_page], j % blk_per_page, 0, 0)
        return pl.BlockSpec((None, None, None, SEL_BLOCK, LANES), blk_map)

    row8 = pl.BlockSpec((None, SUBLANES, LANES), lambda b, pt, idx: (b, 0, 0))
    grid_spec = pltpu.PrefetchScalarGridSpec(
        num_scalar_prefetch=2,
        grid=(nb,),
        in_specs=[blk_spec(k) for k in range(N_SELECT)] + [
            row8, row8, pl.BlockSpec((None, None, wlen, LANES), lambda b, pt, idx: (layer, b, 0, 0)),
            row8, row8, row8],
        out_specs=pl.BlockSpec((None, SUBLANES, GROUP_W), lambda b, pt, idx: (b, 0, 0)),
    )
    return pl.pallas_call(
        functools.partial(_samp_sel_body, n_past_blk=n_past_blk, t_pos=t_pos, win_pos0=win_pos0),
        grid_spec=grid_spec,
        out_shape=jax.ShapeDtypeStruct((nb, SUBLANES, GROUP_W), F32),
        compiler_params=_params("parallel"),
        name="nsa_decode_select",
    )(page_table, top_idx, *([cache_blk] * N_SELECT), q4, nslc8, win, nwin8, gates8, ocmp)


def _block_diag(blocks):
    n, a, b = blocks.shape
    eye = jnp.eye(n, dtype=blocks.dtype)
    return jnp.einsum("nab,nm->namb", blocks, eye).reshape(n * a, n * b)


def _rope_tables(pos):
    half = HEAD_D // 2
    inv = ROPE_THETA ** (-jnp.arange(half, dtype=F32) / half)
    ang = pos.astype(F32)[:, None] * inv[None, :]
    cos, sin = jnp.cos(ang), jnp.sin(ang)
    cos128 = jnp.concatenate([cos, cos, cos, cos], axis=1)
    sin128 = jnp.concatenate([-sin, sin, -sin, sin], axis=1)
    return cos128, sin128


def _seg_ones(n_seg):
    return _block_diag(jnp.ones((n_seg, HEAD_D, HEAD_D), F32)).astype(BF16)


def _layer_weights(lp):
    w_in = lp["w_in"]
    a_w, b_w, c_w = 2 * GROUP_W, 4 * GROUP_W, 3 * GROUP_W
    w_perm = jnp.concatenate(
        [w_in[:, a_w:a_w + b_w], w_in[:, :a_w], w_in[:, a_w + b_w:a_w + b_w + c_w], w_in[:, a_w + b_w + c_w:],
         jnp.zeros((w_in.shape[0], PROJ_PAD - w_in.shape[1]), F32)], axis=1).astype(BF16)
    lora = HEAD_D
    w1 = lp["nsa_cmp_w1"].reshape(2, 2, CMP_STRIDE, HEAD_D, 2 * HEAD_D)
    pe = lp["nsa_cmp_pe"].reshape(2, 2, CMP_STRIDE, HEAD_D)

    def cmp_half(hf):
        wk = jnp.zeros((CMP_STRIDE, 2 * HEAD_D, 4 * HEAD_D), F32)
        wk = wk.at[:, :HEAD_D, :2 * HEAD_D].set(w1[0, hf]).at[:, HEAD_D:, 2 * HEAD_D:].set(w1[1, hf])
        pk = jnp.concatenate([pe[0, hf], pe[1, hf]], axis=1)
        return wk.reshape(CMP_STRIDE * 2 * HEAD_D, 4 * HEAD_D).astype(BF16), pk.reshape(1, CMP_STRIDE * 2 * HEAD_D)

    w_lo, pe_lo = cmp_half(0)
    w_hi, pe_hi = cmp_half(1)
    ones64 = jnp.ones((HEAD_D,), F32)
    cw = dict(
        w_lo=w_lo, w_hi=w_hi, pe_lo=pe_lo, pe_hi=pe_hi,
        b1=lp["nsa_cmp_b1"].reshape(1, 4 * HEAD_D),
        w2=_block_diag(lp["nsa_cmp_w2"]).astype(BF16),
        b2=lp["nsa_cmp_b2"].reshape(1, 2 * HEAD_D),
        gk=jnp.concatenate([lp["nsa_norm_k"][0], ones64]).reshape(1, LANES),
    )
    zeros_lora = jnp.zeros((lora, GROUP_W), F32)
    return dict(
        norm_ffn=lp["norm_ffn"].reshape(2, 1, D_MODEL),
        norm_mix=lp["norm_mix"].reshape(1, D_MODEL), w_in=w_perm,
        lru_cw=lp["lru_conv_w"], lru_cb=lp["lru_conv_b"].reshape(1, GROUP_W),
        lru_wri=jnp.concatenate([_block_diag(lp["lru_w_r"]), _block_diag(lp["lru_w_i"])], axis=1).astype(BF16),
        lru_bri=jnp.concatenate([lp["lru_b_r"], lp["lru_b_i"]]).reshape(1, 2 * GROUP_W),
        lru_lam=lp["lru_lambda"].reshape(1, GROUP_W), sconv_w=lp["sconv_w"],
        mu=lp["rwkv_mu"].reshape(1, 4 * GROUP_W), w0=lp["rwkv_w0"].reshape(1, GROUP_W),
        w2p=jnp.concatenate([lp["rwkv_w2"], zeros_lora], axis=0).astype(BF16),
        a0=lp["rwkv_a0"].reshape(1, GROUP_W),
        a2p=jnp.concatenate([zeros_lora, lp["rwkv_a2"]], axis=0).astype(BF16),
        g2=lp["rwkv_g2"].astype(BF16),
        k_k=lp["rwkv_k_k"].reshape(1, GROUP_W), k_a=lp["rwkv_k_a"].reshape(1, GROUP_W),
        r_k=lp["rwkv_r_k"].reshape(1, GROUP_W),
        ln_g=lp["rwkv_ln_g"].reshape(1, GROUP_W), ln_b=lp["rwkv_ln_b"].reshape(1, GROUP_W),
        gq=jnp.tile(lp["nsa_norm_q"], 2).reshape(1, LANES),
        gk2=jnp.stack([jnp.concatenate([lp["nsa_norm_k"][1], ones64]),
                       jnp.concatenate([lp["nsa_norm_k"][2], ones64])]),
        cw=cw,
        out_norm=lp["out_norm"].reshape(N_HEADS, GROUP_W),
    )


def _overlap(n_grp, n_sel_pad, n_sel):
    cs = jnp.arange(n_grp)[:, None] * CMP_STRIDE
    js = jnp.arange(n_sel_pad)[None, :] * SEL_BLOCK
    ov = (cs < js + SEL_BLOCK) & (cs + CMP_LEN > js) & (jnp.arange(n_sel_pad)[None, :] < n_sel)
    return ov.astype(BF16)


def _pad_rows_front(x, rows):
    return jnp.pad(x, ((0, 0), (rows - x.shape[1], 0), (0, 0)))


def _round_up(x, m):
    return (x + m - 1) // m * m


def _mixers_recurrent(lw, proj3, cbuf8, h0, sbuf8, shift8, s_bd, j4, tt, chunk, pos0, t_valid):
    ya, yc, hlast, ulast = _lru_sconv(proj3, cbuf8, h0, sbuf8, lw["lru_cw"], lw["lru_cb"], lw["lru_wri"],
                                      lw["lru_bri"], lw["lru_lam"], lw["sconv_w"], tt, pos0)
    yb, s_out = _rwkv2(proj3, shift8, s_bd, lw["mu"], lw["w0"], lw["w2p"], lw["a0"], lw["a2p"], lw["g2"],
                      lw["k_k"], lw["k_a"], lw["r_k"], lw["ln_g"], lw["ln_b"], j4, tt, chunk, t_valid,
                      n_seq=math.gcd(proj3.shape[0], max(1, RWKV_CHAINS * chunk // tt)))
    return ya, yb, yc, hlast, ulast, s_out


def _prompt_layer(lw, layer, x, nb, t, consts):
    m = nb * t
    tm = 512
    j2, j4 = consts["j2"], consts["j4"]
    ffn_w = consts["ffn_w"]
    x1 = _ffn(x, lw["norm_ffn"][0], ffn_w, layer, 0, tm)
    proj = _proj(x1, lw["norm_mix"], lw["w_in"], tm)
    proj3 = proj.reshape(nb, t, PROJ_PAD)
    zeros8 = jnp.zeros((nb, SUBLANES, GROUP_W), F32)
    ya, yb, yc, hlast, ulast, s_out = _mixers_recurrent(
        lw, proj3, zeros8, jnp.zeros((nb, 1, GROUP_W), F32), zeros8,
        jnp.zeros((nb, SUBLANES, 4 * GROUP_W), F32), jnp.zeros((nb, N_HEADS, HEAD_D, HEAD_D), F32), j4,
        tt=256, chunk=HEAD_D, pos0=0, t_valid=t)
    q_r, ncmp, nslc, nwin, gates, slc_bf, win_bf, slc_t, win_t = _nsa_prep(
        proj, consts["cos_p"], consts["sin_p"], lw["gq"], lw["gk2"], j2, tm, rows_per_table=t, with_bf16=True)
    kvc, kvc_t = _compress(ncmp.reshape(nb, t, LANES), lw["cw"], consts["cos_cp"], consts["sin_cp"], j2)
    n_cmp = (t - CMP_LEN) // CMP_STRIDE + 1
    n_sel = -(-t // SEL_BLOCK)
    yd = _nsa_attn_t(q_r.reshape(nb, t, GROUP_W), gates.reshape(nb, t, LANES), kvc, kvc_t,
                     slc_bf.reshape(nb, t, LANES), slc_t, win_bf.reshape(nb, t, LANES), win_t,
                     consts["ovl_t_p"], consts["expand_p"], tq=256, tk=512, n_cmp=n_cmp, n_sel=n_sel)
    x3 = _ffn(x1, lw["norm_ffn"][1], ffn_w, layer, 1, tm,
              mix=(ya.reshape(m, GROUP_W), yb.reshape(m, GROUP_W), yc.reshape(m, GROUP_W), yd.reshape(m, GROUP_W),
                   lw["out_norm"], consts["w_out"]))
    xa = proj3[:, :, COL_XA * GROUP_W:(COL_XA + 1) * GROUP_W]
    wlen = min(WINDOW, t)
    states = (hlast[:, SUBLANES - 1], xa[:, t - 3:], s_out, proj3[:, t - 1, :4 * GROUP_W],
              ulast[:, SUBLANES - 2:], nwin.reshape(nb, t, LANES)[:, t - wlen:],
              ncmp.reshape(nb, t // PAGE_SIZE, PAGE_SIZE, LANES), nslc.reshape(nb, t // PAGE_SIZE, PAGE_SIZE, LANES))
    return x3, states


def _sample_layer(lw, layer, x, st, caches, page_table, consts):
    nb = x.shape[0]
    n_pages = page_table.shape[1]
    past_len = n_pages * PAGE_SIZE
    j2, j4 = consts["j2"], consts["j4"]
    lru_h, lru_conv, rwkv_s, rwkv_shift, sconv, win = st
    cache_cmp, cache_slc = caches
    ffn_w = consts["ffn_w"]
    x1 = _ffn(x, lw["norm_ffn"][0], ffn_w, layer, 0, nb)
    proj = _proj(x1, lw["norm_mix"], lw["w_in"], nb)
    proj3 = jnp.pad(proj[:, None, :], ((0, 0), (0, SUBLANES - 1), (0, 0)))
    ya, yb, yc, hlast, ulast, s_out = _mixers_recurrent(
        lw, proj3, _pad_rows_front(lru_conv, SUBLANES), lru_h[:, None, :], _pad_rows_front(sconv, SUBLANES),
        _pad_rows_front(rwkv_shift[:, None, :], SUBLANES), rwkv_s, j4,
        tt=SUBLANES, chunk=SUBLANES, pos0=past_len, t_valid=1)
    q_r, ncmp, nslc, nwin, gates = _nsa_prep(proj, consts["cos_s"], consts["sin_s"], lw["gq"], lw["gk2"], j2,
                                             nb, rows_per_table=nb, with_bf16=False)

    def row8(a):
        return jnp.pad(a[:, None, :], ((0, 0), (0, SUBLANES - 1), (0, 0)))

    q4 = jnp.pad(q_r.reshape(nb, N_HEADS, HEAD_D), ((0, 0), (0, SUBLANES - N_HEADS), (0, LANES - HEAD_D)))
    grp_per_page = PAGE_SIZE // CMP_STRIDE
    depth, n_phys = cache_cmp.shape[:2]
    n_cmp = (past_len + 1 - CMP_LEN) // CMP_STRIDE + 1
    n_sel = -(-(past_len + 1) // SEL_BLOCK)
    ocmp, top_idx = _samp_cmp(cache_cmp, layer, page_table, lw["cw"], consts["cos_cs"], consts["sin_cs"], j2, q4,
                              consts["ovl_s"], n_cmp, n_sel, t_pos=past_len,
                              pages_per_step=consts["pages_per_step"])
    wlen = win.shape[1]
    yd8 = _samp_sel(cache_slc.reshape(depth, n_phys, PAGE_SIZE // SEL_BLOCK, SEL_BLOCK, LANES), layer, page_table,
                    top_idx[:, :, 0], q4, row8(nslc), consts["win_all"], row8(nwin), row8(gates), ocmp,
                    t_pos=past_len, win_pos0=past_len - wlen)
    x3 = _ffn(x1, lw["norm_ffn"][1], ffn_w, layer, 1, nb,
              mix=(ya[:, 0], yb[:, 0], yc[:, 0], yd8[:, 0], lw["out_norm"], consts["w_out"]))
    xa = proj[:, COL_XA * GROUP_W:(COL_XA + 1) * GROUP_W]
    new_win = jnp.concatenate([win, nwin[:, None, :]], axis=1)
    states = (hlast[:, 0], jnp.concatenate([lru_conv[:, 1:], xa[:, None, :]], axis=1), s_out,
              proj[:, :4 * GROUP_W], jnp.concatenate([sconv[:, 1:], ulast[:, 0:1]], axis=1),
              new_win[:, -min(WINDOW, wlen + 1):], ncmp[:, None, :], nslc[:, None, :])
    return x3, states


def kernel(x_prompt, x_sample, state_lru_h, state_lru_conv, state_rwkv_S, state_rwkv_shift, state_sconv, state_nsa_win, cache_nsa_cmp, cache_nsa_slc, page_table, norm_ffn, ffn_w_gate, ffn_w_up, ffn_w_down, norm_mix, w_in, lru_conv_w, lru_conv_b, lru_w_r, lru_b_r, lru_w_i, lru_b_i, lru_lambda, rwkv_mu, rwkv_w0, rwkv_w2, rwkv_a0, rwkv_a2, rwkv_g2, rwkv_k_k, rwkv_k_a, rwkv_r_k, rwkv_ln_g, rwkv_ln_b, sconv_w, nsa_norm_q, nsa_norm_k, nsa_cmp_pe, nsa_cmp_w1, nsa_cmp_b1, nsa_cmp_w2, nsa_cmp_b2, out_norm, w_out):
    params = dict(norm_ffn=norm_ffn, norm_mix=norm_mix, w_in=w_in, lru_conv_w=lru_conv_w, lru_conv_b=lru_conv_b, lru_w_r=lru_w_r,
                  lru_b_r=lru_b_r, lru_w_i=lru_w_i, lru_b_i=lru_b_i, lru_lambda=lru_lambda, rwkv_mu=rwkv_mu,
                  rwkv_w0=rwkv_w0, rwkv_w2=rwkv_w2, rwkv_a0=rwkv_a0, rwkv_a2=rwkv_a2, rwkv_g2=rwkv_g2,
                  rwkv_k_k=rwkv_k_k, rwkv_k_a=rwkv_k_a, rwkv_r_k=rwkv_r_k, rwkv_ln_g=rwkv_ln_g,
                  rwkv_ln_b=rwkv_ln_b, sconv_w=sconv_w, nsa_norm_q=nsa_norm_q, nsa_norm_k=nsa_norm_k,
                  nsa_cmp_pe=nsa_cmp_pe, nsa_cmp_w1=nsa_cmp_w1, nsa_cmp_b1=nsa_cmp_b1, nsa_cmp_w2=nsa_cmp_w2,
                  nsa_cmp_b2=nsa_cmp_b2, out_norm=out_norm)
    depth = norm_mix.shape[0]
    bp, tp, d = x_prompt.shape
    bs, ts, _ = x_sample.shape
    n_pages = page_table.shape[1]
    past_len = n_pages * PAGE_SIZE
    assert d == D_MODEL and ts == 1 and tp % 512 == 0 and tp >= WINDOW + 256 and past_len >= WINDOW
    assert bs % SUBLANES == 0 or bs < SUBLANES

    n_grp_p = tp // CMP_STRIDE
    n_sel_p = -(-tp // SEL_BLOCK)
    nsp_p = _round_up(n_sel_p, LANES)
    n_grp_s = past_len // CMP_STRIDE
    n_sel_s = -(-(past_len + 1) // SEL_BLOCK)
    nsp_s = _round_up(n_sel_s, LANES)
    cos_p, sin_p = _rope_tables(jnp.arange(tp))
    cos_s, sin_s = _rope_tables(jnp.full((bs,), past_len))
    cos_cp, sin_cp = _rope_tables(jnp.arange(n_grp_p) * CMP_STRIDE + CMP_LEN - 1)
    cos_cs, sin_cs = _rope_tables(jnp.arange(n_grp_s) * CMP_STRIDE + CMP_LEN - 1)
    expand_p = ((jnp.arange(tp)[:, None] // SEL_BLOCK) == jnp.arange(nsp_p)[None, :]).astype(BF16)
    pages_per_step = math.gcd(n_pages, 32)
    consts = dict(
        j2=_seg_ones(2), j4=_seg_ones(N_HEADS), cos_p=cos_p, sin_p=sin_p, cos_s=cos_s, sin_s=sin_s,
        cos_cp=cos_cp, sin_cp=sin_cp, cos_cs=cos_cs, sin_cs=sin_cs,
        ovl_t_p=_overlap(n_grp_p, nsp_p, n_sel_p).T,
        expand_p=expand_p,
        ovl_s=_overlap(n_grp_s, nsp_s, n_sel_s),
        pages_per_step=pages_per_step,
        ffn_w=(ffn_w_gate.astype(BF16), ffn_w_up.astype(BF16), ffn_w_down.astype(BF16)),
        w_out=w_out.astype(BF16),
        win_all=state_nsa_win,
    )

    yp = x_prompt.reshape(bp * tp, d)
    ys = x_sample.reshape(bs, d)
    sp_all, ss_all = [], []
    for l in range(depth):
        lw = _layer_weights({name: arr[l] for name, arr in params.items()})
        yp, sp = _prompt_layer(lw, l, yp, bp, tp, consts)
        ys, ss = _sample_layer(lw, l, ys, (state_lru_h[l], state_lru_conv[l], state_rwkv_S[l], state_rwkv_shift[l],
                                           state_sconv[l], state_nsa_win[l]),
                               (cache_nsa_cmp, cache_nsa_slc), page_table, consts)
        sp_all.append(sp)
        ss_all.append(ss)
    outs = [yp.reshape(bp, tp, d), ys.reshape(bs, ts, d)]
    for i in range(8):
        outs.append(jnp.stack([s[i] for s in sp_all]))
        outs.append(jnp.stack([s[i] for s in ss_all]))
    return tuple(outs)
```

```python
import functools
import math

import numpy as np
import jax
import jax.numpy as jnp
from jax import lax
from jax.experimental import pallas as pl
from jax.experimental.pallas import tpu as pltpu

F32 = jnp.float32
BF16 = jnp.bfloat16

D_MODEL = 1024
GROUP_W = 256
N_HEADS = 4
HEAD_D = 64
LRU_C = 8.0
RWKV_LN_EPS = 64e-5
CMP_LEN = 32
CMP_STRIDE = 16
SEL_BLOCK = 64
N_SELECT = 16
WINDOW = 512
FORCE_BONUS = 1e4
NEG_INF = -1e30
ROPE_THETA = 10000.0
EPS = 1e-6
PAGE_SIZE = 128
PROJ_PAD = 3072
PROJ_KEEP = 2304
ATT_SCALE = HEAD_D ** -0.5
RWKV_CHAINS = 8

SUBLANES = 8
LANES = 128
VMEM_LIMIT = 56 * 1024 * 1024

COL_XA, COL_GA, COL_BG, COL_CG, COL_XIN = 4, 5, 6, 7, 8


def _params(*sem):
    return pltpu.CompilerParams(dimension_semantics=sem, vmem_limit_bytes=VMEM_LIMIT)


def _const_spec(shape):
    nd = len(shape)
    return pl.BlockSpec(shape, lambda *_: (0,) * nd)


def _dot(a, b):
    return jnp.dot(a, b, preferred_element_type=F32)


def _dot_nt(a, b):
    return lax.dot_general(a, b, (((1,), (1,)), ((), ())), preferred_element_type=F32)


def _dot_tn(a, b):
    return lax.dot_general(a, b, (((0,), (0,)), ((), ())), preferred_element_type=F32)


def _split3(x):
    h1 = x.astype(BF16)
    r1 = x - h1.astype(F32)
    h2 = r1.astype(BF16)
    h3 = (r1 - h2.astype(F32)).astype(BF16)
    return h1, h2, h3


def _dot_exact_rhs(x, m_bf16):
    h1, h2, h3 = _split3(x)
    return _dot(h1, m_bf16) + _dot(h2, m_bf16) + _dot(h3, m_bf16)


def _gelu_tanh(x):
    return x * (0.5 * (1.0 + jnp.tanh(math.sqrt(2.0 / math.pi) * (x + 0.044715 * (x * x * x)))))


def _sigmoid(x):
    return 1.0 / (1.0 + jnp.exp(-x))


def _softplus(x):
    return jnp.maximum(x, 0.0) + jnp.log1p(jnp.exp(-jnp.abs(x)))


def _rmsnorm_rows(x, g):
    ms = jnp.mean(x * x, axis=-1, keepdims=True)
    return x * lax.rsqrt(ms + EPS) * g


def _masked_softmax_rows(s, mask):
    s = jnp.where(mask, s, NEG_INF)
    m = jnp.max(s, axis=-1, keepdims=True)
    e = jnp.exp(s - m)
    norm = 1.0 / jnp.maximum(jnp.sum(e, axis=-1, keepdims=True), 1e-30)
    return e * jnp.where(m > 0.5 * NEG_INF, norm, 0.0)


def _rope_lanes(x, cos, sin_signed, lane):
    swapped = jnp.where((lane & (HEAD_D - 1)) < HEAD_D // 2,
                        pltpu.roll(x, LANES - HEAD_D // 2, 1), pltpu.roll(x, HEAD_D // 2, 1))
    return x * cos + swapped * sin_signed


def _ffn_body(*refs, f_chunk, n_mix):
    x_ref = refs[0]
    y_refs = refs[1:1 + n_mix]
    if n_mix:
        gm_ref, wo_ref = refs[1 + n_mix:3 + n_mix]
    g_ref, wg_ref, wu_ref, wd_ref, o_ref, act_ref = refs[1 + n_mix + (2 if n_mix else 0):]
    x = x_ref[...]
    for gi, y_ref in enumerate(y_refs):
        yn = _rmsnorm_rows(y_ref[...], gm_ref[gi:gi + 1, :]).astype(BF16)
        x = x + _dot(yn, wo_ref[gi * GROUP_W:(gi + 1) * GROUP_W, :])
    h = _rmsnorm_rows(x, g_ref[...]).astype(BF16)
    d_ff = wg_ref.shape[1]
    for c in range(d_ff // f_chunk):
        sl = slice(c * f_chunk, (c + 1) * f_chunk)
        gate = _dot(h, wg_ref[:, sl])
        up = _dot(h, wu_ref[:, sl])
        act_ref[:, sl] = (gate * _sigmoid(gate) * up).astype(BF16)
    o_ref[...] = x + 0.5 * _dot(act_ref[...], wd_ref[...])


def _ffn(x, g, ffn_w, layer, which, tm, mix=None):
    m, d = x.shape
    wg, wu, wd = ffn_w
    d_ff = wg.shape[3]
    once = pl.Buffered(1)

    def wspec(a, b):
        return pl.BlockSpec((None, None, a, b), lambda i: (layer, which, 0, 0), pipeline_mode=once)

    mix_specs, mix_args = [], []
    if mix is not None:
        yspec = pl.BlockSpec((tm, GROUP_W), lambda i: (i, 0))
        mix_specs = [yspec] * 4 + [_const_spec((N_HEADS, GROUP_W)),
                                   pl.BlockSpec((None, N_HEADS * GROUP_W, d), lambda i: (layer, 0, 0),
                                                pipeline_mode=once)]
        mix_args = list(mix)
    return pl.pallas_call(
        functools.partial(_ffn_body, f_chunk=256, n_mix=4 if mix is not None else 0),
        grid=(m // tm,),
        in_specs=[pl.BlockSpec((tm, d), lambda i: (i, 0))] + mix_specs
        + [_const_spec((1, d)), wspec(d, d_ff), wspec(d, d_ff), wspec(d_ff, d)],
        out_specs=pl.BlockSpec((tm, d), lambda i: (i, 0)),
        out_shape=jax.ShapeDtypeStruct((m, d), F32),
        scratch_shapes=[pltpu.VMEM((tm, d_ff), BF16)],
        compiler_params=_params("parallel"),
        name="ffn_mix" if mix is not None else "ffn",
    )(x, *mix_args, g, wg, wu, wd)


def _lru_sconv_body(xa_ref, ga_ref, bg_ref, cg_ref, xin_ref, cbuf_ref, h0_ref, sbuf_ref,
                    cw_ref, cb_ref, wri_ref, bri_ref, lam_ref, sw_ref,
                    ya_ref, yc_ref, hlast_ref, ulast_ref,
                    extx_ref, extu_ref, h_ref, *, tt, pos0):
    t = pl.program_id(1)
    nt = pl.num_programs(1)

    @pl.when(t == 0)
    def _():
        extx_ref[0:SUBLANES, :] = cbuf_ref[...]
        extu_ref[0:SUBLANES, :] = sbuf_ref[...]
        h_ref[...] = h0_ref[...]

    x = xa_ref[...]
    extx_ref[SUBLANES:SUBLANES + tt, :] = x
    cw = cw_ref[...]
    xc = extx_ref[pl.ds(SUBLANES - 3, tt), :] * cw[0:1]
    xc = xc + extx_ref[pl.ds(SUBLANES - 2, tt), :] * cw[1:2]
    xc = xc + extx_ref[pl.ds(SUBLANES - 1, tt), :] * cw[2:3]
    xc = xc + x * cw[3:4]
    xc = xc + cb_ref[...]
    extx_ref[0:SUBLANES, :] = x[tt - SUBLANES:tt]

    gates = _dot(xc.astype(BF16), wri_ref[...]) + bri_ref[...]
    r = _sigmoid(gates[:, :GROUP_W])
    ig = _sigmoid(gates[:, GROUP_W:])
    log_a = -LRU_C * r * _softplus(-lam_ref[...])
    a = jnp.exp(log_a)
    rows = lax.broadcasted_iota(jnp.int32, (tt, GROUP_W), 0)
    mult = jnp.where(rows + (pos0 + t * tt) == 0, 1.0, jnp.sqrt(1.0 - jnp.exp(2.0 * log_a)))
    b = mult * ig * xc

    s = 1
    while s < tt:
        keep = rows >= s
        a_sh = jnp.where(keep, pltpu.roll(a, s, 0), 1.0)
        b_sh = jnp.where(keep, pltpu.roll(b, s, 0), 0.0)
        b = a * b_sh + b
        a = a * a_sh
        s *= 2
    h = a * h_ref[...] + b
    h_ref[...] = h[tt - 1:tt]
    ya_ref[...] = _gelu_tanh(ga_ref[...]) * h

    u = cg_ref[...] * xin_ref[...]
    extu_ref[SUBLANES:SUBLANES + tt, :] = u
    sw = sw_ref[...]
    yv = extu_ref[pl.ds(SUBLANES - 2, tt), :] * sw[0:1]
    yv = yv + extu_ref[pl.ds(SUBLANES - 1, tt), :] * sw[1:2]
    yv = yv + u * sw[2:3]
    yc_ref[...] = bg_ref[...] * yv
    extu_ref[0:SUBLANES, :] = u[tt - SUBLANES:tt]

    @pl.when(t == nt - 1)
    def _():
        hlast_ref[...] = h[tt - SUBLANES:tt]
        ulast_ref[...] = u[tt - SUBLANES:tt]


def _lru_sconv(proj3, cbuf8, h0, sbuf8, cw, cb, wri, bri, lam, sw, tt, pos0):
    nb, tp, _ = proj3.shape
    w = GROUP_W

    def col(c):
        return pl.BlockSpec((None, tt, w), lambda b, t, c=c: (b, t, c))

    state8 = pl.BlockSpec((None, SUBLANES, w), lambda b, t: (b, 0, 0))
    return pl.pallas_call(
        functools.partial(_lru_sconv_body, tt=tt, pos0=pos0),
        grid=(nb, tp // tt),
        in_specs=[col(COL_XA), col(COL_GA), col(COL_BG), col(COL_CG), col(COL_XIN),
                  state8, pl.BlockSpec((None, 1, w), lambda b, t: (b, 0, 0)), state8,
                  _const_spec((4, w)), _const_spec((1, w)), _const_spec((w, 2 * w)), _const_spec((1, 2 * w)),
                  _const_spec((1, w)), _const_spec((3, w))],
        out_specs=[pl.BlockSpec((None, tt, w), lambda b, t: (b, t, 0)),
                   pl.BlockSpec((None, tt, w), lambda b, t: (b, t, 0)), state8, state8],
        out_shape=[jax.ShapeDtypeStruct((nb, tp, w), F32), jax.ShapeDtypeStruct((nb, tp, w), F32),
                   jax.ShapeDtypeStruct((nb, SUBLANES, w), F32), jax.ShapeDtypeStruct((nb, SUBLANES, w), F32)],
        scratch_shapes=[pltpu.VMEM((tt + SUBLANES, w), F32), pltpu.VMEM((tt + SUBLANES, w), F32),
                        pltpu.VMEM((1, w), F32)],
        compiler_params=_params("parallel", "arbitrary"),
        name="lru_sconv",
    )(proj3, proj3, proj3, proj3, proj3, cbuf8, h0, sbuf8, cw, cb, wri, bri, lam, sw)


def _rwkv_tokens(t, p_ref, mu_ref, w0_ref, w2_ref, a0_ref, a2_ref, g2_ref, kk_ref, ka_ref, rk_ref, j4,
                 ext_ref, w_buf, r_buf, k_buf, v_buf, a_buf, b_buf, g_buf, bon_buf, *, tt, chunk, t_valid):
    w = GROUP_W
    p = p_ref[...]
    ext_ref[SUBLANES:SUBLANES + tt, :] = p
    m = p + (ext_ref[pl.ds(SUBLANES - 1, tt), :] - p) * mu_ref[...]
    ext_ref[0:SUBLANES, :] = p[tt - SUBLANES:tt]
    r = m[:, 0:w]
    k = m[:, w:2 * w]
    v = m[:, 2 * w:3 * w]
    wa = m[:, 3 * w:3 * w + LANES]
    gl = m[:, 3 * w + LANES:4 * w]
    wlin = w0_ref[...] + _dot(jnp.tanh(wa).astype(BF16), w2_ref[...])
    log_decay = -jnp.exp(-_softplus(-wlin) - 0.5)
    ag = _sigmoid(a0_ref[...] + _dot(wa.astype(BF16), a2_ref[...]))
    g_buf[...] = _dot(_sigmoid(gl).astype(BF16), g2_ref[...])
    kk = k * kk_ref[...]
    kk = kk * lax.rsqrt(jnp.maximum(_dot_exact_rhs(kk * kk, j4), 1e-24))
    kf = k * (1.0 + (ag - 1.0) * ka_ref[...])
    rows = lax.broadcasted_iota(jnp.int32, (tt, w), 0)
    if t_valid % tt != 0:
        live = rows + t * tt < t_valid
        log_decay = jnp.where(live, log_decay, 0.0)
        kk = jnp.where(live, kk, 0.0)
        kf = jnp.where(live, kf, 0.0)
        v = jnp.where(live, v, 0.0)
    bon_buf[...] = _dot_exact_rhs(r * kf * rk_ref[...], j4) * v

    rin = rows & (chunk - 1)
    cl = log_decay
    s = 1
    while s < chunk:
        cl = cl + jnp.where(rin >= s, pltpu.roll(cl, s, 0), 0.0)
        s *= 2
    e_neg = jnp.exp(-cl)
    w_buf[...] = cl
    r_buf[...] = r * jnp.exp(cl)
    a_buf[...] = -kk * jnp.exp(cl - log_decay)
    b_buf[...] = kk * ag * e_neg
    k_buf[...] = kf * e_neg
    v_buf[...] = v


def _rwkv2_body(p_ref, shift_ref, sin_ref, mu_ref, w0_ref, w2_ref, a0_ref, a2_ref, g2_ref, kk_ref, ka_ref,
                rk_ref, lng_ref, lnb_ref, j4_ref, y_ref, sout_ref,
                ext_ref, s_ref, w_buf, r_buf, k_buf, v_buf, a_buf, b_buf, g_buf, bon_buf, y_buf,
                *, n_seq, tt, chunk, t_valid):
    t = pl.program_id(1)
    w = GROUP_W
    nh = N_HEADS
    nc = tt // chunk
    j4 = j4_ref[...]

    heads = [slice(h * HEAD_D, (h + 1) * HEAD_D) for h in range(nh)]

    @pl.when(t == 0)
    def _():
        for i in range(n_seq):
            ext_ref[i, 0:SUBLANES, :] = shift_ref[i]
            s_ref[i] = jnp.zeros((w, w), F32)
            for hs in heads:
                s_ref[i, hs, hs] = sin_ref[i, hs, :]

    for i in range(n_seq):
        _rwkv_tokens(t, p_ref.at[i], mu_ref, w0_ref, w2_ref, a0_ref, a2_ref, g2_ref, kk_ref, ka_ref, rk_ref, j4,
                     ext_ref.at[i], w_buf.at[i], r_buf.at[i], k_buf.at[i], v_buf.at[i], a_buf.at[i], b_buf.at[i],
                     g_buf.at[i], bon_buf.at[i], tt=tt, chunk=chunk, t_valid=t_valid)

    sl = nh * chunk
    hm_rows = lax.broadcasted_iota(jnp.int32, (sl, w), 0) // chunk
    hm_cols = lax.broadcasted_iota(jnp.int32, (sl, w), 1) // HEAD_D
    head_mask = (hm_rows == hm_cols).astype(F32)
    ri = lax.broadcasted_iota(jnp.int32, (sl, sl), 0)
    ci = lax.broadcasted_iota(jnp.int32, (sl, sl), 1)
    same = (ri // chunk) == (ci // chunk)
    strict = (same & ((ri & (chunk - 1)) > (ci & (chunk - 1)))).astype(F32)
    incl = (same & ((ri & (chunk - 1)) >= (ci & (chunk - 1)))).astype(F32)
    eye = (ri == ci).astype(F32)

    def stacked(buf, i, c):
        return jnp.concatenate([buf[i, pl.ds(c * chunk, chunk), :]] * nh, axis=0) * head_mask

    chains = [(i, c) for c in range(nc) for i in range(n_seq)]
    a_s = [stacked(a_buf, i, c).astype(BF16) for i, c in chains]
    r_s = [stacked(r_buf, i, c).astype(BF16) for i, c in chains]
    b_s = [stacked(b_buf, i, c).astype(BF16) for i, c in chains]
    k_s = [stacked(k_buf, i, c).astype(BF16) for i, c in chains]
    v_s = [stacked(v_buf, i, c).astype(BF16) for i, c in chains]
    n_mat = [_dot_nt(a, b) * strict for a, b in zip(a_s, b_s)]
    m_mat = [(_dot_nt(a, k) * strict).astype(BF16) for a, k in zip(a_s, k_s)]
    p_mat = [(_dot_nt(r, b) * incl).astype(BF16) for r, b in zip(r_s, b_s)]
    q_mat = [(_dot_nt(r, k) * incl).astype(BF16) for r, k in zip(r_s, k_s)]
    t_mat = [eye + n for n in n_mat]
    x = n_mat
    step = 2
    while step < chunk:
        xb = [xi.astype(BF16) for xi in x]
        x = [_dot(b, b) for b in xb]
        t_mat = [tm + _dot(tm.astype(BF16), xi.astype(BF16)) for tm, xi in zip(t_mat, x)]
        step *= 2
    t_b = [tm.astype(BF16) for tm in t_mat]
    w_eff = [_dot(tb, a).astype(BF16) for tb, a in zip(t_b, a_s)]
    mv = [_dot(mm, v).astype(BF16) for mm, v in zip(m_mat, v_s)]
    z = [_dot(tb, x_) for tb, x_ in zip(t_b, mv)]
    qv = [_dot(qm, v) for qm, v in zip(q_mat, v_s)]

    for n, (i, c) in enumerate(chains):
        s0 = s_ref[i]
        s0b = s0.astype(BF16)
        ub = (_dot_nt(w_eff[n], s0b) + z[n]).astype(BF16)
        y_s = _dot_nt(r_s[n], s0b) + _dot(p_mat[n], ub) + qv[n]
        yc = y_s[0:chunk]
        for hh in range(1, nh):
            yc = yc + y_s[hh * chunk:(hh + 1) * chunk]
        y_buf[i, pl.ds(c * chunk, chunk), :] = yc
        c_last = jnp.exp(w_buf[i, pl.ds((c + 1) * chunk - 1, 1), :])
        s_ref[i] = (s0 + _dot_tn(ub, b_s[n]) + _dot_tn(v_s[n], k_s[n])) * c_last

    inv_hd = 1.0 / HEAD_D
    for i in range(n_seq):
        y = y_buf[i]
        mean = _dot_exact_rhs(y, j4) * inv_hd
        yc = y - mean
        var = _dot_exact_rhs(yc * yc, j4) * inv_hd
        yn = yc * lax.rsqrt(var + RWKV_LN_EPS) * lng_ref[...] + lnb_ref[...]
        y_ref[i] = (yn + bon_buf[i]) * g_buf[i]
        for hs in heads:
            sout_ref[i, hs, :] = s_ref[i, hs, hs]


def _rwkv2(proj3, shift8, s_heads, mu, w0, w2p, a0, a2p, g2, kk, ka, rk, lng, lnb, j4, tt, chunk, t_valid, n_seq):
    nb, tp, _ = proj3.shape
    w = GROUP_W
    pw = 4 * w
    vec = _const_spec((1, w))
    s_rows = s_heads.reshape(nb, w, HEAD_D)
    state_spec = pl.BlockSpec((n_seq, w, HEAD_D), lambda b, t: (b, 0, 0))
    yb, s_out = pl.pallas_call(
        functools.partial(_rwkv2_body, n_seq=n_seq, tt=tt, chunk=chunk, t_valid=t_valid),
        grid=(nb // n_seq, tp // tt),
        in_specs=[pl.BlockSpec((n_seq, tt, pw), lambda b, t: (b, t, 0)),
                  pl.BlockSpec((n_seq, SUBLANES, pw), lambda b, t: (b, 0, 0)), state_spec,
                  _const_spec((1, pw)), vec, _const_spec((LANES, w)), vec, _const_spec((LANES, w)),
                  _const_spec((LANES, w)), vec, vec, vec, vec, vec, _const_spec((w, w))],
        out_specs=[pl.BlockSpec((n_seq, tt, w), lambda b, t: (b, t, 0)), state_spec],
        out_shape=[jax.ShapeDtypeStruct((nb, tp, w), F32), jax.ShapeDtypeStruct(s_rows.shape, F32)],
        scratch_shapes=[pltpu.VMEM((n_seq, tt + SUBLANES, pw), F32), pltpu.VMEM((n_seq, w, w), F32)]
        + [pltpu.VMEM((n_seq, tt, w), F32)] * 9,
        compiler_params=_params("parallel", "arbitrary"),
        name="rwkv7",
    )(proj3, shift8, s_rows, mu, w0, w2p, a0, a2p, g2, kk, ka, rk, lng, lnb, j4)
    return yb, s_out.reshape(s_heads.shape)


def _proj_prep_body(x_ref, g_ref, w_ref, cos_ref, sin_ref, gq_ref, gk_ref, j2_ref,
                    o_ref, qo_ref, cmpo_ref, slco_ref, wino_ref, gate_ref, *bf_refs, n_chunk):
    h =_rmsnorm_rows(x_ref[...], g_ref[...]).astype(BF16)
    n_keep = o_ref.shape[1]
    for c in range(n_keep // n_chunk):
        sl = slice(c * n_chunk, (c + 1) * n_chunk)
        o_ref[:, sl] = _dot(h, w_ref[:, sl])
    nsa = _dot(h, w_ref[:, n_keep:])

    cos = cos_ref[...]
    sin = sin_ref[...]
    j2 = j2_ref[...]
    lane = lax.broadcasted_iota(jnp.int32, cos.shape, 1)
    inv_hd = 1.0 / HEAD_D
    gq = gq_ref[...]
    halves = []
    for c in range(2):
        x = nsa[:, c * LANES:(c + 1) * LANES]
        ms = _dot_exact_rhs(x * x, j2) * inv_hd
        halves.append(_rope_lanes(x * lax.rsqrt(ms + EPS) * gq, cos, sin, lane))
    qo_ref[...] = jnp.concatenate(halves, axis=1)
    cmpo_ref[...] = nsa[:, 2 * LANES:3 * LANES]
    gk = gk_ref[...]
    is_key = lane < HEAD_D
    for dst, row in ((slco_ref, 0), (wino_ref, 1)):
        x = nsa[:, (3 + row) * LANES:(4 + row) * LANES]
        ms = _dot_exact_rhs(x * x, j2) * inv_hd
        roped = _rope_lanes(x * lax.rsqrt(ms + EPS) * gk[row:row + 1], cos, sin, lane)
        kv = jnp.where(is_key, roped, x)
        dst[...] = kv
        if bf_refs:
            bf_refs[row][...] = kv.astype(BF16)
            bf_refs[2 + row][...] = kv.T.astype(BF16)
    gate_ref[...] = _sigmoid(nsa[:, 5 * LANES:6 * LANES])


def _proj_prep(x, g, w, cos, sin, gq, gk2, j2, tm, rows_per_table, with_bf16):
    m, d = x.shape
    n = w.shape[1]
    nt_tab = rows_per_table // tm
    o128t = pl.BlockSpec((LANES, tm), lambda i: (0, i))
    bf_specs = [pl.BlockSpec((tm, LANES), lambda i: (i, 0))] * 2 + [o128t] * 2 if with_bf16 else []
    bf_shapes = ([jax.ShapeDtypeStruct((m, LANES), BF16)] * 2 + [jax.ShapeDtypeStruct((LANES, m), BF16)] * 2
                 if with_bf16 else [])
    tab = pl.BlockSpec((tm, LANES), lambda i: (i % nt_tab, 0))
    o128 = pl.BlockSpec((tm, LANES), lambda i: (i, 0))
    return pl.pallas_call(
        functools.partial(_proj_prep_body, n_chunk=256),
        grid=(m // tm,),
        in_specs=[pl.BlockSpec((tm, d), lambda i: (i, 0)), _const_spec((1, d)),
                  pl.BlockSpec((d, n), lambda i: (0, 0), pipeline_mode=pl.Buffered(1)),
                  tab, tab, _const_spec((1, LANES)), _const_spec((2, LANES)), _const_spec((LANES, LANES))],
        out_specs=[pl.BlockSpec((tm, PROJ_KEEP), lambda i: (i, 0)),
                   pl.BlockSpec((tm, GROUP_W), lambda i: (i, 0)), o128, o128, o128, o128] + bf_specs,
        out_shape=[jax.ShapeDtypeStruct((m, PROJ_KEEP), F32), jax.ShapeDtypeStruct((m, GROUP_W), F32)]
        + [jax.ShapeDtypeStruct((m, LANES), F32)] * 4 + bf_shapes,
        compiler_params=_params("parallel"),
        name="proj_prep",
    )(x, g, w, cos, sin, gq, gk2, j2)


def _compress_tail(hid_lo, hid_hi_next, b1_ref, w2_ref, b2_ref, gk_ref, cos_ref, sin_ref, j2_ref):
    hidden = _gelu_tanh(hid_lo + hid_hi_next + b1_ref[...])
    kv = _dot(hidden.astype(BF16), w2_ref[...]) + b2_ref[...]
    lane = lax.broadcasted_iota(jnp.int32, kv.shape, 1)
    ms = _dot_exact_rhs(kv * kv, j2_ref[...]) * (1.0 / HEAD_D)
    roped = _rope_lanes(kv * lax.rsqrt(ms + EPS) * gk_ref[...], cos_ref[...], sin_ref[...], lane)
    return jnp.where(lane < HEAD_D, roped, kv)


def _compress_hidden(read_rows, pelo_ref, pehi_ref, wlo_ref, whi_ref):
    pair = 2 * LANES
    lo = hi = None
    for p in range(CMP_STRIDE // 2):
        x = jnp.concatenate([read_rows(2 * p), read_rows(2 * p + 1)], axis=1)
        cs = slice(p * pair, (p + 1) * pair)
        d_lo = _dot((x + pelo_ref[:, cs]).astype(BF16), wlo_ref[cs, :])
        d_hi = _dot((x + pehi_ref[:, cs]).astype(BF16), whi_ref[cs, :])
        lo = d_lo if lo is None else lo + d_lo
        hi = d_hi if hi is None else hi + d_hi
    return lo, hi


def _compress_body(x_ref, pelo_ref, pehi_ref, wlo_ref, whi_ref, b1_ref, w2_ref, b2_ref, gk_ref,
                   cos_ref, sin_ref, j2_ref, o_ref, ot_ref):
    n_grp = x_ref.shape[0] // CMP_STRIDE
    lo, hi = _compress_hidden(lambda j: x_ref[pl.ds(j, n_grp, stride=CMP_STRIDE), :],
                              pelo_ref, pehi_ref, wlo_ref, whi_ref)
    hi_next = pltpu.roll(hi, n_grp - 1, 0)
    kv = _compress_tail(lo, hi_next, b1_ref, w2_ref, b2_ref, gk_ref, cos_ref, sin_ref, j2_ref)
    o_ref[...] = kv.astype(BF16)
    ot_ref[...] = kv.T.astype(BF16)


def _compress(rows, cw, cos, sin, j2):
    nb, t, _ = rows.shape
    n_grp = t // CMP_STRIDE
    wid = CMP_STRIDE * LANES
    return pl.pallas_call(
        _compress_body,
        grid=(nb,),
        in_specs=[pl.BlockSpec((None, t, LANES), lambda b: (b, 0, 0)),
                  _const_spec((1, wid)), _const_spec((1, wid)),
                  _const_spec((wid, GROUP_W)), _const_spec((wid, GROUP_W)), _const_spec((1, GROUP_W)),
                  _const_spec((GROUP_W, LANES)), _const_spec((1, LANES)), _const_spec((1, LANES)),
                  _const_spec((n_grp, LANES)), _const_spec((n_grp, LANES)), _const_spec((LANES, LANES))],
        out_specs=[pl.BlockSpec((None, n_grp, LANES), lambda b: (b, 0, 0)),
                   pl.BlockSpec((LANES, n_grp), lambda b: (0, b))],
        out_shape=[jax.ShapeDtypeStruct((nb, n_grp, LANES), BF16),
                   jax.ShapeDtypeStruct((LANES, nb * n_grp), BF16)],
        compiler_params=_params("parallel"),
        name="nsa_compress",
    )(rows, cw["pe_lo"], cw["pe_hi"], cw["w_lo"], cw["w_hi"], cw["b1"], cw["w2"], cw["b2"], cw["gk"], cos, sin, j2)


def _stack_heads(q, lane):
    parts = []
    for h in range(N_HEADS):
        blk = q[:, (h // 2) * LANES:(h // 2 + 1) * LANES]
        if h % 2 == 1:
            blk = pltpu.roll(blk, HEAD_D, 1)
        parts.append(jnp.where(lane < HEAD_D, blk, 0.0))
    return jnp.concatenate(parts, axis=0)


def _unstack_heads(parts, lane):
    b01 = jnp.where(lane < HEAD_D, pltpu.roll(parts[0], HEAD_D, 1), parts[1])
    b23 = jnp.where(lane < HEAD_D, pltpu.roll(parts[2], HEAD_D, 1), parts[3])
    return jnp.concatenate([b01, b23], axis=1)


def _softmax_cols(st, mask):
    st = jnp.where(mask, st, NEG_INF)
    m = jnp.max(st, axis=0, keepdims=True)
    e = jnp.exp(st - m)
    norm = 1.0 / jnp.maximum(jnp.sum(e, axis=0, keepdims=True), 1e-30)
    return e * jnp.where(m > 0.5 * NEG_INF, norm, 0.0)


def _nsa_attn_t_body(q_ref, gate_ref, kvc_ref, kvct_ref, slc_ref, slct_ref, win_ref, wint_ref, ovl_ref, exp_ref,
                     o_ref, acc_ref, st_ref, e_ref, *, tq, tk, n_cmp, n_sel, n_top, win_span):
    i = pl.program_id(1)
    start = i * tq
    cols = N_HEADS * tq
    lane = lax.broadcasted_iota(jnp.int32, (tq, LANES), 1)
    q4 = _stack_heads(q_ref[...] * ATT_SCALE, lane).astype(BF16)
    tlane = start + (lax.broadcasted_iota(jnp.int32, (1, cols), 1) & (tq - 1))

    kvc = kvc_ref[...]
    ng = kvc.shape[0]
    ncol = lax.broadcasted_iota(jnp.int32, (ng, 1), 0)
    cend = jnp.where(ncol < n_cmp, ncol * CMP_STRIDE + (CMP_LEN - 1), jnp.iinfo(jnp.int32).max)
    pt = _softmax_cols(_dot_nt(kvc, q4), cend <= tlane)
    o_cmp = _dot(kvct_ref[...], pt.astype(BF16))
    psum = pt[:, 0:tq]
    for h in range(1, N_HEADS):
        psum = psum + pt[:, h * tq:(h + 1) * tq]
    ph = psum.astype(BF16)
    pl_ = (psum - ph.astype(F32)).astype(BF16)
    ovl = ovl_ref[...]
    imp_t = _dot(ovl, ph) + _dot(ovl, pl_)

    start0 = pl.multiple_of(jnp.maximum(start + tq - win_span, 0), tq)
    rel = tlane - (start0 + lax.broadcasted_iota(jnp.int32, (win_span, 1), 0))
    pw = _softmax_cols(_dot_nt(win_ref[pl.ds(start0, win_span), :], q4), (rel >= 0) & (rel < WINDOW))
    o_win = _dot(wint_ref[:, pl.ds(start0, win_span)], pw.astype(BF16))

    nsp = imp_t.shape[0]
    jblk = lax.broadcasted_iota(jnp.int32, (nsp, tq), 0)
    tcol = start + lax.broadcasted_iota(jnp.int32, (nsp, tq), 1)
    cur = tcol // SEL_BLOCK
    forced = (jblk == 0) | (jblk == cur) | (jblk == cur - 1)
    valid = (jblk * SEL_BLOCK <= tcol) & (jblk < n_sel)
    score = jnp.where(valid, imp_t + jnp.where(forced, FORCE_BONUS, 0.0), -1.0)
    taken = -3.0
    jf = jblk.astype(F32)
    left = score
    for _ in range(n_top):
        best = jnp.max(left, axis=0, keepdims=True)
        first = jnp.min(jnp.where(left == best, jf, float(nsp)), axis=0, keepdims=True)
        left = jnp.where(jf == first, taken, left)
    bias_t = jnp.where((left == taken) & (score >= 0.0), 0.0, NEG_INF)
    bias = bias_t.T.astype(BF16)
    lhs = jnp.concatenate([q4, jnp.concatenate([bias] * N_HEADS, axis=0)], axis=1)

    acc_ref[...] = jnp.zeros((LANES, cols), F32)
    n_kt = (start + tq + tk - 1) // tk

    def stage_scores(kt, causal):
        off = pl.multiple_of(kt * tk, tk)
        k_aug = jnp.concatenate([slc_ref[pl.ds(off, tk), :], exp_ref[pl.ds(off, tk), :]], axis=1)
        st = _dot_nt(k_aug, lhs)
        if causal:
            kpos = off + lax.broadcasted_iota(jnp.int32, (tk, 1), 0)
            st = jnp.where(kpos <= tlane, st, NEG_INF)
        return st, jnp.max(st.reshape(tk // SUBLANES, SUBLANES, cols), axis=0)

    def stage_exp(st, mt, m_prev, l_prev):
        m_new = jnp.maximum(m_prev, jnp.max(mt, axis=0, keepdims=True))
        m_sub = jnp.where(m_new > 0.5 * NEG_INF, m_new, 0.0)
        e = jnp.exp(st - m_sub[0:1])
        alpha = jnp.exp(m_prev - m_new)
        l_new = alpha * l_prev + jnp.sum(e.reshape(tk // SUBLANES, SUBLANES, cols), axis=0)
        return e.astype(BF16), alpha, m_new, l_new

    def stage_values(kt, e, alpha):
        off = pl.multiple_of(jnp.maximum(kt, 0) * tk, tk)
        acc_ref[...] = alpha[0:1] * acc_ref[...] + _dot(slct_ref[:, pl.ds(off, tk)], e)

    def trip(i, carry, causal):
        mt, alpha, m_run, l_run = carry
        stage_values(i - 2, e_ref[...], alpha)
        e, alpha, m_run, l_run = stage_exp(st_ref[...], mt, m_run, l_run)
        e_ref[...] = e
        st, mt = stage_scores(i, causal)
        st_ref[...] = st
        return mt, alpha, m_run, l_run

    st_ref[...] = jnp.full((tk, cols), NEG_INF, F32)
    e_ref[...] = jnp.zeros((tk, cols), BF16)
    neg8 = st_ref[0:SUBLANES, :]
    zero8 = acc_ref[0:SUBLANES, :]
    carry = (neg8, zero8 + 1.0, neg8, zero8)
    carry = lax.fori_loop(0, n_kt - 1, functools.partial(trip, causal=False), carry)
    mt, alpha, m_run, l_run = trip(n_kt - 1, carry, True)
    stage_values(n_kt - 2, e_ref[...], alpha)
    e, alpha, _, l_part = stage_exp(st_ref[...], mt, m_run, l_run)
    stage_values(n_kt - 1, e, alpha)
    o_sel = acc_ref[...] / jnp.maximum(jnp.sum(l_part, axis=0, keepdims=True), 1e-30)

    gt = gate_ref[...].T
    outs = []
    for h in range(N_HEADS):
        cs = slice(h * tq, (h + 1) * tq)
        outs.append(gt[3 * h:3 * h + 1] * o_cmp[HEAD_D:, cs] + gt[3 * h + 1:3 * h + 2] * o_sel[HEAD_D:, cs]
                    + gt[3 * h + 2:3 * h + 3] * o_win[HEAD_D:, cs])
    o_ref[...] = jnp.concatenate(outs, axis=0).T


def _nsa_attn_t(q, gates, kvc, kvc_t, slc, slc_t, win, win_t, ovl_t, expand, tq, tk, n_cmp, n_sel):
    nb, t, _ = q.shape
    ng = kvc.shape[1]
    nsp = ovl_t.shape[0]
    cols = N_HEADS * tq
    win_span = WINDOW + tq
    per_b = lambda b, i: (b, 0, 0)
    per_b_t = lambda b, i: (0, b)
    return pl.pallas_call(
        functools.partial(_nsa_attn_t_body, tq=tq, tk=tk, n_cmp=n_cmp, n_sel=n_sel,
                          n_top=min(N_SELECT, n_sel), win_span=win_span),
        grid=(nb, t // tq),
        in_specs=[pl.BlockSpec((None, tq, GROUP_W), lambda b, i: (b, i, 0)),
                  pl.BlockSpec((None, tq, LANES), lambda b, i: (b, i, 0)),
                  pl.BlockSpec((None, ng, LANES), per_b), pl.BlockSpec((LANES, ng), per_b_t),
                  pl.BlockSpec((None, t, LANES), per_b), pl.BlockSpec((LANES, t), per_b_t),
                  pl.BlockSpec((None, t, LANES), per_b), pl.BlockSpec((LANES, t), per_b_t),
                  _const_spec((nsp, ng)), _const_spec((t, nsp))],
        out_specs=pl.BlockSpec((None, tq, GROUP_W), lambda b, i: (b, i, 0)),
        out_shape=jax.ShapeDtypeStruct((nb, t, GROUP_W), F32),
        scratch_shapes=[pltpu.VMEM((LANES, cols), F32), pltpu.VMEM((tk, cols), F32), pltpu.VMEM((tk, cols), BF16)],
        compiler_params=_params("parallel", "arbitrary"),
        name="nsa_attn",
    )(q, gates, kvc, kvc_t, slc, slc_t, win, win_t, ovl_t, expand)


def _samp_cmp_body(*refs, pages_per_step, n_grp, n_cmp, n_sel, n_top, t_pos):
    pt_ref = refs[0]
    page_refs = refs[1:1 + pages_per_step]
    (pelo_ref, pehi_ref, wlo_ref, whi_ref, b1_ref, w2_ref, b2_ref, gk_ref, cos_ref, sin_ref, j2_ref,
     q4_ref, ovl_ref, ocmp_ref, idx_ref, lo_ref, hi_ref) = refs[1 + pages_per_step:]
    del pt_ref
    s = pl.program_id(1)
    ns = pl.num_programs(1)
    grp_per_page = PAGE_SIZE // CMP_STRIDE
    rows_step = pages_per_step * grp_per_page

    @pl.when(s == 0)
    def _():
        hi_ref[n_grp:n_grp + SUBLANES, :] = jnp.zeros((SUBLANES, GROUP_W), F32)

    def read_rows(j):
        return jnp.concatenate([pg[pl.ds(j, grp_per_page, stride=CMP_STRIDE), :] for pg in page_refs], axis=0)

    lo, hi = _compress_hidden(read_rows, pelo_ref, pehi_ref, wlo_ref, whi_ref)
    off = pl.multiple_of(s * rows_step, rows_step)
    lo_ref[pl.ds(off, rows_step), :] = lo
    hi_ref[pl.ds(off, rows_step), :] = hi

    @pl.when(s == ns - 1)
    def _():
        kvc = _compress_tail(lo_ref[...], hi_ref[pl.ds(1, n_grp), :], b1_ref, w2_ref, b2_ref, gk_ref,
                             cos_ref, sin_ref, j2_ref).astype(BF16)
        q4 = q4_ref[...].astype(BF16)
        sc = _dot_nt(q4, kvc) * ATT_SCALE
        nidx = lax.broadcasted_iota(jnp.int32, sc.shape, 1)
        p = _masked_softmax_rows(sc, (nidx * CMP_STRIDE + (CMP_LEN - 1) <= t_pos) & (nidx < n_cmp))
        ocmp_ref[...] = _dot(p.astype(BF16), kvc)
        hrow = lax.broadcasted_iota(jnp.int32, p.shape, 0)
        psum = jnp.sum(jnp.where(hrow < N_HEADS, p, 0.0), axis=0, keepdims=True)
        psum8 = jnp.broadcast_to(psum, p.shape)
        ph = psum8.astype(BF16)
        pl_ = (psum8 - ph.astype(F32)).astype(BF16)
        ovl = ovl_ref[...]
        imp = (_dot(ph, ovl) + _dot(pl_, ovl))[0:1]
        nsp = imp.shape[1]
        jrow = lax.broadcasted_iota(jnp.int32, (1, nsp), 1)
        cur = t_pos // SEL_BLOCK
        forced = (jrow == 0) | (jrow == cur) | (jrow == cur - 1)
        valid = (jrow * SEL_BLOCK <= t_pos) & (jrow < n_sel)
        score = jnp.where(valid, imp + jnp.where(forced, FORCE_BONUS, 0.0), -1.0)
        score = jnp.where(jrow < n_sel, score, -2.0)
        s_row = jnp.broadcast_to(score, (nsp, nsp))
        s_col = s_row.T
        ii = lax.broadcasted_iota(jnp.int32, (nsp, nsp), 0)
        jj = lax.broadcasted_iota(jnp.int32, (nsp, nsp), 1)
        beats = (s_col > s_row) | ((s_col == s_row) & (ii < jj))
        rank = jnp.sum(jnp.where(beats, 1.0, 0.0), axis=0, keepdims=True)
        chosen = (rank < n_top) & (score >= 0.0)
        slot = lax.broadcasted_iota(jnp.int32, (N_SELECT, nsp), 0).astype(F32)
        hit = (jnp.broadcast_to(rank, (N_SELECT, nsp)) == slot) & jnp.broadcast_to(chosen, (N_SELECT, nsp))
        jcol = lax.broadcasted_iota(jnp.int32, (N_SELECT, nsp), 1).astype(F32)
        blk = jnp.sum(jnp.where(hit, jcol, 0.0), axis=1, keepdims=True)
        cnt = jnp.sum(jnp.where(hit, 1.0, 0.0), axis=1, keepdims=True)
        blk = jnp.where(cnt > 0.5, blk, -1.0)
        idx_ref[...] = jnp.broadcast_to(blk, (N_SELECT, LANES)).astype(jnp.int32)


def _samp_cmp(cache, layer, page_table, cw, cos, sin, j2, q4, ovl, n_cmp, n_sel, t_pos, pages_per_step):
    nb, n_pages = page_table.shape
    grp_per_page = PAGE_SIZE // CMP_STRIDE
    wid = CMP_STRIDE * LANES
    n_grp = n_pages * grp_per_page
    nsp = ovl.shape[1]

    def page_spec(j):
        return pl.BlockSpec((None, None, PAGE_SIZE, LANES),
                            lambda b, s, pt, j=j: (layer, pt[b, s * pages_per_step + j], 0, 0))

    def cst(shape):
        nd = len(shape)
        return pl.BlockSpec(shape, lambda b, s, pt: (0,) * nd)

    grid_spec = pltpu.PrefetchScalarGridSpec(
        num_scalar_prefetch=1,
        grid=(nb, n_pages // pages_per_step),
        in_specs=[page_spec(j) for j in range(pages_per_step)] + [
            cst((1, wid)), cst((1, wid)), cst((wid, GROUP_W)), cst((wid, GROUP_W)), cst((1, GROUP_W)),
            cst((GROUP_W, LANES)), cst((1, LANES)), cst((1, LANES)),
            cst((n_grp, LANES)), cst((n_grp, LANES)), cst((LANES, LANES)),
            pl.BlockSpec((None, SUBLANES, LANES), lambda b, s, pt: (b, 0, 0)), cst((n_grp, nsp))],
        out_specs=[pl.BlockSpec((None, SUBLANES, LANES), lambda b, s, pt: (b, 0, 0)),
                   pl.BlockSpec((None, N_SELECT, LANES), lambda b, s, pt: (b, 0, 0))],
        scratch_shapes=[pltpu.VMEM((n_grp, GROUP_W), F32), pltpu.VMEM((n_grp + SUBLANES, GROUP_W), F32)],
    )
    return pl.pallas_call(
        functools.partial(_samp_cmp_body, pages_per_step=pages_per_step, n_grp=n_grp, n_cmp=n_cmp,
                          n_sel=n_sel, n_top=min(N_SELECT, n_sel), t_pos=t_pos),
        grid_spec=grid_spec,
        out_shape=[jax.ShapeDtypeStruct((nb, SUBLANES, LANES), F32),
                   jax.ShapeDtypeStruct((nb, N_SELECT, LANES), jnp.int32)],
        compiler_params=_params("parallel", "arbitrary"),
        name="nsa_decode_compress",
    )(page_table, *([cache] * pages_per_step), cw["pe_lo"], cw["pe_hi"], cw["w_lo"], cw["w_hi"], cw["b1"],
      cw["w2"], cw["b2"], cw["gk"], cos, sin, j2, q4, ovl)


def _samp_sel_body(pt_ref, idx_ref, *refs, n_past_blk, t_pos, win_pos0):
    del pt_ref
    blk_refs = refs[:N_SELECT]
    q4_ref, nslc_ref, win_ref, nwin_ref, gate_ref, ocmp_ref, o_ref = refs[N_SELECT:]
    b = pl.program_id(0)
    q4 = q4_ref[...].astype(BF16)

    n_keys = N_SELECT * SEL_BLOCK
    lane_k = lax.broadcasted_iota(jnp.int32, (1, n_keys), 1)
    jvec = jnp.zeros((1, n_keys), jnp.int32)
    parts = []
    for k in range(N_SELECT):
        j = idx_ref[b, k]
        blk = blk_refs[k][...]
        parts.append(jnp.where(j == n_past_blk, nslc_ref[...], blk[0:SUBLANES]))
        parts.append(blk[SUBLANES:])
        jvec = jnp.where(lane_k // SEL_BLOCK == k, j, jvec)
    kv = jnp.concatenate(parts, axis=0).astype(BF16)
    kpos = jvec * SEL_BLOCK + (lane_k & (SEL_BLOCK - 1))
    p_sel = _masked_softmax_rows(_dot_nt(q4, kv) * ATT_SCALE, (kpos <= t_pos) & (jvec >= 0))
    o_sel = _dot(p_sel.astype(BF16), kv)

    wb = win_ref[...].astype(BF16)
    nw = nwin_ref[...].astype(BF16)
    s_w = _dot_nt(q4, wb) * ATT_SCALE
    s_n = _dot_nt(q4, nw) * ATT_SCALE
    wpos = win_pos0 + lax.broadcasted_iota(jnp.int32, s_w.shape, 1)
    rel = t_pos - wpos
    m_w = (rel >= 0) & (rel < WINDOW) & (wpos >= win_pos0)
    m_n = lax.broadcasted_iota(jnp.int32, s_n.shape, 1) == 0
    s_w = jnp.where(m_w, s_w, NEG_INF)
    s_n = jnp.where(m_n, s_n, NEG_INF)
    mx = jnp.maximum(jnp.max(s_w, axis=-1, keepdims=True), jnp.max(s_n, axis=-1, keepdims=True))
    e_w = jnp.exp(s_w - mx) * m_w.astype(F32)
    e_n = jnp.exp(s_n - mx) * m_n.astype(F32)
    den = jnp.sum(e_w, axis=-1, keepdims=True) + jnp.sum(e_n, axis=-1, keepdims=True)
    o_win = (_dot(e_w.astype(BF16), wb) + _dot(e_n.astype(BF16), nw)) / jnp.maximum(den, 1e-30)
    o_cmp = ocmp_ref[...]
    g = gate_ref[...]
    lane = lax.broadcasted_iota(jnp.int32, (1, LANES), 1)
    outs = []
    for h in range(N_HEADS):
        outs.append(g[0:1, 3 * h:3 * h + 1] * o_cmp[h:h + 1] + g[0:1, 3 * h + 1:3 * h + 2] * o_sel[h:h + 1]
                    + g[0:1, 3 * h + 2:3 * h + 3] * o_win[h:h + 1])
    o_ref[...] = jnp.broadcast_to(_unstack_heads(outs, lane), o_ref.shape)


def _samp_sel(cache_blk, layer, page_table, top_idx, q4, nslc8, win, nwin8, gates8, ocmp, t_pos, win_pos0):
    nb, n_pages = page_table.shape
    blk_per_page = PAGE_SIZE // SEL_BLOCK
    n_past_blk = n_pages * blk_per_page
    wlen = win.shape[2]

    def blk_spec(k):
        def blk_map(b, pt, idx):
            j = jnp.clip(idx[b, k], 0, n_past_blk - 1)
            return (layer, pt[b, j // blk_per_page], j % blk_per_page, 0, 0)
        return pl.BlockSpec((None, None, None, SEL_BLOCK, LANES), blk_map)

    row8 = pl.BlockSpec((None, SUBLANES, LANES), lambda b, pt, idx: (b, 0, 0))
    grid_spec = pltpu.PrefetchScalarGridSpec(
        num_scalar_prefetch=2,
        grid=(nb,),
        in_specs=[blk_spec(k) for k in range(N_SELECT)] + [
            row8, row8, pl.BlockSpec((None, None, wlen, LANES), lambda b, pt, idx: (layer, b, 0, 0)),
            row8, row8, row8],
        out_specs=pl.BlockSpec((None, SUBLANES, GROUP_W), lambda b, pt, idx: (b, 0, 0)),
    )
    return pl.pallas_call(
        functools.partial(_samp_sel_body, n_past_blk=n_past_blk, t_pos=t_pos, win_pos0=win_pos0),
        grid_spec=grid_spec,
        out_shape=jax.ShapeDtypeStruct((nb, SUBLANES, GROUP_W), F32),
        compiler_params=_params("parallel"),
        name="nsa_decode_select",
    )(page_table, top_idx, *([cache_blk] * N_SELECT), q4, nslc8, win, nwin8, gates8, ocmp)


def _block_diag(blocks):
    n, a, b = blocks.shape
    eye = jnp.eye(n, dtype=blocks.dtype)
    return jnp.einsum("nab,nm->namb", blocks, eye).reshape(n * a, n * b)


def _rope_tables(pos):
    half = HEAD_D // 2
    inv = ROPE_THETA ** (-jnp.arange(half, dtype=F32) / half)
    ang = pos.astype(F32)[:, None] * inv[None, :]
    cos, sin = jnp.cos(ang), jnp.sin(ang)
    cos128 = jnp.concatenate([cos, cos, cos, cos], axis=1)
    sin128 = jnp.concatenate([-sin, sin, -sin, sin], axis=1)
    return cos128, sin128


def _seg_ones(n_seg):
    return _block_diag(jnp.ones((n_seg, HEAD_D, HEAD_D), F32)).astype(BF16)


def _layer_weights(lp):
    w_in = lp["w_in"]
    a_w, b_w, c_w = 2 * GROUP_W, 4 * GROUP_W, 3 * GROUP_W
    w_perm = jnp.concatenate(
        [w_in[:, a_w:a_w + b_w], w_in[:, :a_w], w_in[:, a_w + b_w:a_w + b_w + c_w], w_in[:, a_w + b_w + c_w:],
         jnp.zeros((w_in.shape[0], PROJ_PAD - w_in.shape[1]), F32)], axis=1).astype(BF16)
    lora = HEAD_D
    w1 = lp["nsa_cmp_w1"].reshape(2, 2, CMP_STRIDE, HEAD_D, 2 * HEAD_D)
    pe = lp["nsa_cmp_pe"].reshape(2, 2, CMP_STRIDE, HEAD_D)

    def cmp_half(hf):
        wk = jnp.zeros((CMP_STRIDE, 2 * HEAD_D, 4 * HEAD_D), F32)
        wk = wk.at[:, :HEAD_D, :2 * HEAD_D].set(w1[0, hf]).at[:, HEAD_D:, 2 * HEAD_D:].set(w1[1, hf])
        pk = jnp.concatenate([pe[0, hf], pe[1, hf]], axis=1)
        return wk.reshape(CMP_STRIDE * 2 * HEAD_D, 4 * HEAD_D).astype(BF16), pk.reshape(1, CMP_STRIDE * 2 * HEAD_D)

    w_lo, pe_lo = cmp_half(0)
    w_hi, pe_hi = cmp_half(1)
    ones64 = jnp.ones((HEAD_D,), F32)
    cw = dict(
        w_lo=w_lo, w_hi=w_hi, pe_lo=pe_lo, pe_hi=pe_hi,
        b1=lp["nsa_cmp_b1"].reshape(1, 4 * HEAD_D),
        w2=_block_diag(lp["nsa_cmp_w2"]).astype(BF16),
        b2=lp["nsa_cmp_b2"].reshape(1, 2 * HEAD_D),
        gk=jnp.concatenate([lp["nsa_norm_k"][0], ones64]).reshape(1, LANES),
    )
    zeros_lora = jnp.zeros((lora, GROUP_W), F32)
    return dict(
        norm_ffn=lp["norm_ffn"].reshape(2, 1, D_MODEL),
        norm_mix=lp["norm_mix"].reshape(1, D_MODEL), w_in=w_perm,
        lru_cw=lp["lru_conv_w"], lru_cb=lp["lru_conv_b"].reshape(1, GROUP_W),
        lru_wri=jnp.concatenate([_block_diag(lp["lru_w_r"]), _block_diag(lp["lru_w_i"])], axis=1).astype(BF16),
        lru_bri=jnp.concatenate([lp["lru_b_r"], lp["lru_b_i"]]).reshape(1, 2 * GROUP_W),
        lru_lam=lp["lru_lambda"].reshape(1, GROUP_W), sconv_w=lp["sconv_w"],
        mu=lp["rwkv_mu"].reshape(1, 4 * GROUP_W), w0=lp["rwkv_w0"].reshape(1, GROUP_W),
        w2p=jnp.concatenate([lp["rwkv_w2"], zeros_lora], axis=0).astype(BF16),
        a0=lp["rwkv_a0"].reshape(1, GROUP_W),
        a2p=jnp.concatenate([zeros_lora, lp["rwkv_a2"]], axis=0).astype(BF16),
        g2=lp["rwkv_g2"].astype(BF16),
        k_k=lp["rwkv_k_k"].reshape(1, GROUP_W), k_a=lp["rwkv_k_a"].reshape(1, GROUP_W),
        r_k=lp["rwkv_r_k"].reshape(1, GROUP_W),
        ln_g=lp["rwkv_ln_g"].reshape(1, GROUP_W), ln_b=lp["rwkv_ln_b"].reshape(1, GROUP_W),
        gq=jnp.tile(lp["nsa_norm_q"], 2).reshape(1, LANES),
        gk2=jnp.stack([jnp.concatenate([lp["nsa_norm_k"][1], ones64]),
                       jnp.concatenate([lp["nsa_norm_k"][2], ones64])]),
        cw=cw,
        out_norm=lp["out_norm"].reshape(N_HEADS, GROUP_W),
    )


def _overlap(n_grp, n_sel_pad, n_sel):
    cs = jnp.arange(n_grp)[:, None] * CMP_STRIDE
    js = jnp.arange(n_sel_pad)[None, :] * SEL_BLOCK
    ov = (cs < js + SEL_BLOCK) & (cs + CMP_LEN > js) & (jnp.arange(n_sel_pad)[None, :] < n_sel)
    return ov.astype(BF16)


def _pad_rows_front(x, rows):
    return jnp.pad(x, ((0, 0), (rows - x.shape[1], 0), (0, 0)))


def _round_up(x, m):
    return (x + m - 1) // m * m


def _mixers_recurrent(lw, proj3, cbuf8, h0, sbuf8, shift8, s_bd, j4, tt, chunk, pos0, t_valid):
    ya, yc, hlast, ulast = _lru_sconv(proj3, cbuf8, h0, sbuf8, lw["lru_cw"], lw["lru_cb"], lw["lru_wri"],
                                      lw["lru_bri"], lw["lru_lam"], lw["sconv_w"], tt, pos0)
    yb, s_out = _rwkv2(proj3, shift8, s_bd, lw["mu"], lw["w0"], lw["w2p"], lw["a0"], lw["a2p"], lw["g2"],
                      lw["k_k"], lw["k_a"], lw["r_k"], lw["ln_g"], lw["ln_b"], j4, tt, chunk, t_valid,
                      n_seq=math.gcd(proj3.shape[0], max(1, RWKV_CHAINS * chunk // tt)))
    return ya, yb, yc, hlast, ulast, s_out


def _prompt_layer(lw, layer, x, nb, t, consts):
    m = nb * t
    tm = 512
    j2, j4 = consts["j2"], consts["j4"]
    ffn_w = consts["ffn_w"]
    x1 = _ffn(x, lw["norm_ffn"][0], ffn_w, layer, 0, tm)
    proj, q_r, ncmp, nslc, nwin, gates, slc_bf, win_bf, slc_t, win_t = _proj_prep(
        x1, lw["norm_mix"], lw["w_in"], consts["cos_p"], consts["sin_p"], lw["gq"], lw["gk2"], j2, tm,
        rows_per_table=t, with_bf16=True)
    proj3 = proj.reshape(nb, t, PROJ_KEEP)
    zeros8 = jnp.zeros((nb, SUBLANES, GROUP_W), F32)
    ya, yb, yc, hlast, ulast, s_out = _mixers_recurrent(
        lw, proj3, zeros8, jnp.zeros((nb, 1, GROUP_W), F32), zeros8,
        jnp.zeros((nb, SUBLANES, 4 * GROUP_W), F32), jnp.zeros((nb, N_HEADS, HEAD_D, HEAD_D), F32), j4,
        tt=256, chunk=HEAD_D, pos0=0, t_valid=t)
    kvc, kvc_t = _compress(ncmp.reshape(nb, t, LANES), lw["cw"], consts["cos_cp"], consts["sin_cp"], j2)
    n_cmp = (t - CMP_LEN) // CMP_STRIDE + 1
    n_sel = -(-t // SEL_BLOCK)
    yd = _nsa_attn_t(q_r.reshape(nb, t, GROUP_W), gates.reshape(nb, t, LANES), kvc, kvc_t,
                     slc_bf.reshape(nb, t, LANES), slc_t, win_bf.reshape(nb, t, LANES), win_t,
                     consts["ovl_t_p"], consts["expand_p"], tq=256, tk=512, n_cmp=n_cmp, n_sel=n_sel)
    x3 = _ffn(x1, lw["norm_ffn"][1], ffn_w, layer, 1, tm,
              mix=(ya.reshape(m, GROUP_W), yb.reshape(m, GROUP_W), yc.reshape(m, GROUP_W), yd.reshape(m, GROUP_W),
                   lw["out_norm"], consts["w_out"]))
    xa = proj3[:, :, COL_XA * GROUP_W:(COL_XA + 1) * GROUP_W]
    wlen = min(WINDOW, t)
    states = (hlast[:, SUBLANES - 1], xa[:, t - 3:], s_out, proj3[:, t - 1, :4 * GROUP_W],
              ulast[:, SUBLANES - 2:], nwin.reshape(nb, t, LANES)[:, t - wlen:],
              ncmp.reshape(nb, t // PAGE_SIZE, PAGE_SIZE, LANES), nslc.reshape(nb, t // PAGE_SIZE, PAGE_SIZE, LANES))
    return x3, states


def _sample_layer(lw, layer, x, st, caches, page_table, consts):
    nb = x.shape[0]
    n_pages = page_table.shape[1]
    past_len = n_pages * PAGE_SIZE
    j2, j4 = consts["j2"], consts["j4"]
    lru_h, lru_conv, rwkv_s, rwkv_shift, sconv, win = st
    cache_cmp, cache_slc = caches
    ffn_w = consts["ffn_w"]
    x1 = _ffn(x, lw["norm_ffn"][0], ffn_w, layer, 0, nb)
    proj, q_r, ncmp, nslc, nwin, gates = _proj_prep(
        x1, lw["norm_mix"], lw["w_in"], consts["cos_s"], consts["sin_s"], lw["gq"], lw["gk2"], j2, nb,
        rows_per_table=nb, with_bf16=False)
    proj3 = jnp.pad(proj[:, None, :], ((0, 0), (0, SUBLANES - 1), (0, 0)))
    ya, yb, yc, hlast, ulast, s_out = _mixers_recurrent(
        lw, proj3, _pad_rows_front(lru_conv, SUBLANES), lru_h[:, None, :], _pad_rows_front(sconv, SUBLANES),
        _pad_rows_front(rwkv_shift[:, None, :], SUBLANES), rwkv_s, j4,
        tt=SUBLANES, chunk=SUBLANES, pos0=past_len, t_valid=1)

    def row8(a):
        return jnp.pad(a[:, None, :], ((0, 0), (0, SUBLANES - 1), (0, 0)))

    q4 = jnp.pad(q_r.reshape(nb, N_HEADS, HEAD_D), ((0, 0), (0, SUBLANES - N_HEADS), (0, LANES - HEAD_D)))
    grp_per_page = PAGE_SIZE // CMP_STRIDE
    depth, n_phys = cache_cmp.shape[:2]
    n_cmp = (past_len + 1 - CMP_LEN) // CMP_STRIDE + 1
    n_sel = -(-(past_len + 1) // SEL_BLOCK)
    ocmp, top_idx = _samp_cmp(cache_cmp, layer, page_table, lw["cw"], consts["cos_cs"], consts["sin_cs"], j2, q4,
                              consts["ovl_s"], n_cmp, n_sel, t_pos=past_len,
                              pages_per_step=consts["pages_per_step"])
    wlen = win.shape[1]
    yd8 = _samp_sel(cache_slc.reshape(depth, n_phys, PAGE_SIZE // SEL_BLOCK, SEL_BLOCK, LANES), layer, page_table,
                    top_idx[:, :, 0], q4, row8(nslc), consts["win_all"], row8(nwin), row8(gates), ocmp,
                    t_pos=past_len, win_pos0=past_len - wlen)
    x3 = _ffn(x1, lw["norm_ffn"][1], ffn_w, layer, 1, nb,
              mix=(ya[:, 0], yb[:, 0], yc[:, 0], yd8[:, 0], lw["out_norm"], consts["w_out"]))
    xa = proj[:, COL_XA * GROUP_W:(COL_XA + 1) * GROUP_W]
    new_win = jnp.concatenate([win, nwin[:, None, :]], axis=1)
    states = (hlast[:, 0], jnp.concatenate([lru_conv[:, 1:], xa[:, None, :]], axis=1), s_out,
              proj[:, :4 * GROUP_W], jnp.concatenate([sconv[:, 1:], ulast[:, 0:1]], axis=1),
              new_win[:, -min(WINDOW, wlen + 1):], ncmp[:, None, :], nslc[:, None, :])
    return x3, states


def kernel(x_prompt, x_sample, state_lru_h, state_lru_conv, state_rwkv_S, state_rwkv_shift, state_sconv, state_nsa_win, cache_nsa_cmp, cache_nsa_slc, page_table, norm_ffn, ffn_w_gate, ffn_w_up, ffn_w_down, norm_mix, w_in, lru_conv_w, lru_conv_b, lru_w_r, lru_b_r, lru_w_i, lru_b_i, lru_lambda, rwkv_mu, rwkv_w0, rwkv_w2, rwkv_a0, rwkv_a2, rwkv_g2, rwkv_k_k, rwkv_k_a, rwkv_r_k, rwkv_ln_g, rwkv_ln_b, sconv_w, nsa_norm_q, nsa_norm_k, nsa_cmp_pe, nsa_cmp_w1, nsa_cmp_b1, nsa_cmp_w2, nsa_cmp_b2, out_norm, w_out):
    params = dict(norm_ffn=norm_ffn, norm_mix=norm_mix, w_in=w_in, lru_conv_w=lru_conv_w, lru_conv_b=lru_conv_b, lru_w_r=lru_w_r,
                  lru_b_r=lru_b_r, lru_w_i=lru_w_i, lru_b_i=lru_b_i, lru_lambda=lru_lambda, rwkv_mu=rwkv_mu,
                  rwkv_w0=rwkv_w0, rwkv_w2=rwkv_w2, rwkv_a0=rwkv_a0, rwkv_a2=rwkv_a2, rwkv_g2=rwkv_g2,
                  rwkv_k_k=rwkv_k_k, rwkv_k_a=rwkv_k_a, rwkv_r_k=rwkv_r_k, rwkv_ln_g=rwkv_ln_g,
                  rwkv_ln_b=rwkv_ln_b, sconv_w=sconv_w, nsa_norm_q=nsa_norm_q, nsa_norm_k=nsa_norm_k,
                  nsa_cmp_pe=nsa_cmp_pe, nsa_cmp_w1=nsa_cmp_w1, nsa_cmp_b1=nsa_cmp_b1, nsa_cmp_w2=nsa_cmp_w2,
                  nsa_cmp_b2=nsa_cmp_b2, out_norm=out_norm)
    depth = norm_mix.shape[0]
    bp, tp, d = x_prompt.shape
    bs, ts, _ = x_sample.shape
    n_pages = page_table.shape[1]
    past_len = n_pages * PAGE_SIZE
    assert d == D_MODEL and ts == 1 and tp % 512 == 0 and tp >= WINDOW + 256 and past_len >= WINDOW
    assert bs % SUBLANES == 0 or bs < SUBLANES

    n_grp_p = tp // CMP_STRIDE
    n_sel_p = -(-tp // SEL_BLOCK)
    nsp_p = _round_up(n_sel_p, LANES)
    n_grp_s = past_len // CMP_STRIDE
    n_sel_s = -(-(past_len + 1) // SEL_BLOCK)
    nsp_s = _round_up(n_sel_s, LANES)
    cos_p, sin_p = _rope_tables(jnp.arange(tp))
    cos_s, sin_s = _rope_tables(jnp.full((bs,), past_len))
    cos_cp, sin_cp = _rope_tables(jnp.arange(n_grp_p) * CMP_STRIDE + CMP_LEN - 1)
    cos_cs, sin_cs = _rope_tables(jnp.arange(n_grp_s) * CMP_STRIDE + CMP_LEN - 1)
    expand_p = ((jnp.arange(tp)[:, None] // SEL_BLOCK) == jnp.arange(nsp_p)[None, :]).astype(BF16)
    pages_per_step = math.gcd(n_pages, 64)
    consts = dict(
        j2=_seg_ones(2), j4=_seg_ones(N_HEADS), cos_p=cos_p, sin_p=sin_p, cos_s=cos_s, sin_s=sin_s,
        cos_cp=cos_cp, sin_cp=sin_cp, cos_cs=cos_cs, sin_cs=sin_cs,
        ovl_t_p=_overlap(n_grp_p, nsp_p, n_sel_p).T,
        expand_p=expand_p,
        ovl_s=_overlap(n_grp_s, nsp_s, n_sel_s),
        pages_per_step=pages_per_step,
        ffn_w=(ffn_w_gate.astype(BF16), ffn_w_up.astype(BF16), ffn_w_down.astype(BF16)),
        w_out=w_out.astype(BF16),
        win_all=state_nsa_win,
    )

    yp = x_prompt.reshape(bp * tp, d)
    ys = x_sample.reshape(bs, d)
    sp_all, ss_all = [], []
    for l in range(depth):
        lw = _layer_weights({name: arr[l] for name, arr in params.items()})
        yp, sp = _prompt_layer(lw, l, yp, bp, tp, consts)
        ys, ss = _sample_layer(lw, l, ys, (state_lru_h[l], state_lru_conv[l], state_rwkv_S[l], state_rwkv_shift[l],
                                           state_sconv[l], state_nsa_win[l]),
                               (cache_nsa_cmp, cache_nsa_slc), page_table, consts)
        sp_all.append(sp)
        ss_all.append(ss)
    outs = [yp.reshape(bp, tp, d), ys.reshape(bs, ts, d)]
    for i in range(8):
        outs.append(jnp.stack([s[i] for s in sp_all]))
        outs.append(jnp.stack([s[i] for s in ss_all]))
    return tuple(outs)
```

```python
import functools
import math

import numpy as np
import jax
import jax.numpy as jnp
from jax import lax
from jax.experimental import pallas as pl
from jax.experimental.pallas import tpu as pltpu

F32 = jnp.float32
BF16 = jnp.bfloat16

D_MODEL = 1024
GROUP_W = 256
N_HEADS = 4
HEAD_D = 64
LRU_C = 8.0
RWKV_LN_EPS = 64e-5
CMP_LEN = 32
CMP_STRIDE = 16
SEL_BLOCK = 64
N_SELECT = 16
WINDOW = 512
FORCE_BONUS = 1e4
NEG_INF = -1e30
ROPE_THETA = 10000.0
EPS = 1e-6
PAGE_SIZE = 128
PROJ_PAD = 3072
PROJ_KEEP = 2304
ATT_SCALE = HEAD_D ** -0.5
RWKV_CHAINS = 8

SUBLANES = 8
LANES = 128
VMEM_LIMIT = 56 * 1024 * 1024

COL_XA, COL_GA, COL_BG, COL_CG, COL_XIN = 4, 5, 6, 7, 8


def _params(*sem):
    return pltpu.CompilerParams(dimension_semantics=sem, vmem_limit_bytes=VMEM_LIMIT)


def _const_spec(shape):
    nd = len(shape)
    return pl.BlockSpec(shape, lambda *_: (0,) * nd)


def _dot(a, b):
    return jnp.dot(a, b, preferred_element_type=F32)


def _dot_nt(a, b):
    return lax.dot_general(a, b, (((1,), (1,)), ((), ())), preferred_element_type=F32)


def _dot_tn(a, b):
    return lax.dot_general(a, b, (((0,), (0,)), ((), ())), preferred_element_type=F32)


def _split3(x):
    h1 = x.astype(BF16)
    r1 = x - h1.astype(F32)
    h2 = r1.astype(BF16)
    h3 = (r1 - h2.astype(F32)).astype(BF16)
    return h1, h2, h3


def _dot_exact_rhs(x, m_bf16):
    h1, h2, h3 = _split3(x)
    return _dot(h1, m_bf16) + _dot(h2, m_bf16) + _dot(h3, m_bf16)


def _gelu_tanh(x):
    return x * (0.5 * (1.0 + jnp.tanh(math.sqrt(2.0 / math.pi) * (x + 0.044715 * (x * x * x)))))


def _sigmoid(x):
    return 1.0 / (1.0 + jnp.exp(-x))


def _softplus(x):
    return jnp.maximum(x, 0.0) + jnp.log1p(jnp.exp(-jnp.abs(x)))


def _rmsnorm_rows(x, g):
    ms = jnp.mean(x * x, axis=-1, keepdims=True)
    return x * lax.rsqrt(ms + EPS) * g


def _masked_softmax_rows(s, mask):
    s = jnp.where(mask, s, NEG_INF)
    m = jnp.max(s, axis=-1, keepdims=True)
    e = jnp.exp(s - m)
    norm = 1.0 / jnp.maximum(jnp.sum(e, axis=-1, keepdims=True), 1e-30)
    return e * jnp.where(m > 0.5 * NEG_INF, norm, 0.0)


def _rope_lanes(x, cos, sin_signed, lane):
    swapped = jnp.where((lane & (HEAD_D - 1)) < HEAD_D // 2,
                        pltpu.roll(x, LANES - HEAD_D // 2, 1), pltpu.roll(x, HEAD_D // 2, 1))
    return x * cos + swapped * sin_signed


def _ffn_body(*refs, f_chunk, n_mix):
    x_ref = refs[0]
    y_refs = refs[1:1 + n_mix]
    if n_mix:
        gm_ref, wo_ref = refs[1 + n_mix:3 + n_mix]
    g_ref, wg_ref, wu_ref, wd_ref, o_ref, act_ref = refs[1 + n_mix + (2 if n_mix else 0):]
    x = x_ref[...]
    for gi, y_ref in enumerate(y_refs):
        yn = _rmsnorm_rows(y_ref[...], gm_ref[gi:gi + 1, :]).astype(BF16)
        x = x + _dot(yn, wo_ref[gi * GROUP_W:(gi + 1) * GROUP_W, :])
    h = _rmsnorm_rows(x, g_ref[...]).astype(BF16)
    d_ff = wg_ref.shape[1]
    for c in range(d_ff // f_chunk):
        sl = slice(c * f_chunk, (c + 1) * f_chunk)
        gate = _dot(h, wg_ref[:, sl])
        up = _dot(h, wu_ref[:, sl])
        act_ref[:, sl] = (gate * _sigmoid(gate) * up).astype(BF16)
    o_ref[...] = x + 0.5 * _dot(act_ref[...], wd_ref[...])


def _ffn(x, g, ffn_w, layer, which, tm, mix=None):
    m, d = x.shape
    wg, wu, wd = ffn_w
    d_ff = wg.shape[3]
    once = pl.Buffered(1)

    def wspec(a, b):
        return pl.BlockSpec((None, None, a, b), lambda i: (layer, which, 0, 0), pipeline_mode=once)

    mix_specs, mix_args = [], []
    if mix is not None:
        yspec = pl.BlockSpec((tm, GROUP_W), lambda i: (i, 0))
        mix_specs = [yspec] * 4 + [_const_spec((N_HEADS, GROUP_W)),
                                   pl.BlockSpec((None, N_HEADS * GROUP_W, d), lambda i: (layer, 0, 0),
                                                pipeline_mode=once)]
        mix_args = list(mix)
    return pl.pallas_call(
        functools.partial(_ffn_body, f_chunk=256, n_mix=4 if mix is not None else 0),
        grid=(m // tm,),
        in_specs=[pl.BlockSpec((tm, d), lambda i: (i, 0))] + mix_specs
        + [_const_spec((1, d)), wspec(d, d_ff), wspec(d, d_ff), wspec(d_ff, d)],
        out_specs=pl.BlockSpec((tm, d), lambda i: (i, 0)),
        out_shape=jax.ShapeDtypeStruct((m, d), F32),
        scratch_shapes=[pltpu.VMEM((tm, d_ff), BF16)],
        compiler_params=_params("parallel"),
        name="ffn_mix" if mix is not None else "ffn",
    )(x, *mix_args, g, wg, wu, wd)


def _lru_sconv_body(xa_ref, ga_ref, bg_ref, cg_ref, xin_ref, cbuf_ref, h0_ref, sbuf_ref,
                    cw_ref, cb_ref, wri_ref, bri_ref, lam_ref, sw_ref,
                    ya_ref, yc_ref, hlast_ref, ulast_ref,
                    extx_ref, extu_ref, h_ref, *, tt, pos0):
    t = pl.program_id(1)
    nt = pl.num_programs(1)

    @pl.when(t == 0)
    def _():
        extx_ref[0:SUBLANES, :] = cbuf_ref[...]
        extu_ref[0:SUBLANES, :] = sbuf_ref[...]
        h_ref[...] = h0_ref[...]

    x = xa_ref[...]
    extx_ref[SUBLANES:SUBLANES + tt, :] = x
    cw = cw_ref[...]
    xc = extx_ref[pl.ds(SUBLANES - 3, tt), :] * cw[0:1]
    xc = xc + extx_ref[pl.ds(SUBLANES - 2, tt), :] * cw[1:2]
    xc = xc + extx_ref[pl.ds(SUBLANES - 1, tt), :] * cw[2:3]
    xc = xc + x * cw[3:4]
    xc = xc + cb_ref[...]
    extx_ref[0:SUBLANES, :] = x[tt - SUBLANES:tt]

    gates = _dot(xc.astype(BF16), wri_ref[...]) + bri_ref[...]
    r = _sigmoid(gates[:, :GROUP_W])
    ig = _sigmoid(gates[:, GROUP_W:])
    log_a = -LRU_C * r * _softplus(-lam_ref[...])
    a = jnp.exp(log_a)
    rows = lax.broadcasted_iota(jnp.int32, (tt, GROUP_W), 0)
    mult = jnp.where(rows + (pos0 + t * tt) == 0, 1.0, jnp.sqrt(1.0 - jnp.exp(2.0 * log_a)))
    b = mult * ig * xc

    s = 1
    while s < tt:
        keep = rows >= s
        a_sh = jnp.where(keep, pltpu.roll(a, s, 0), 1.0)
        b_sh = jnp.where(keep, pltpu.roll(b, s, 0), 0.0)
        b = a * b_sh + b
        a = a * a_sh
        s *= 2
    h = a * h_ref[...] + b
    h_ref[...] = h[tt - 1:tt]
    ya_ref[...] = _gelu_tanh(ga_ref[...]) * h

    u = cg_ref[...] * xin_ref[...]
    extu_ref[SUBLANES:SUBLANES + tt, :] = u
    sw = sw_ref[...]
    yv = extu_ref[pl.ds(SUBLANES - 2, tt), :] * sw[0:1]
    yv = yv + extu_ref[pl.ds(SUBLANES - 1, tt), :] * sw[1:2]
    yv = yv + u * sw[2:3]
    yc_ref[...] = bg_ref[...] * yv
    extu_ref[0:SUBLANES, :] = u[tt - SUBLANES:tt]

    @pl.when(t == nt - 1)
    def _():
        hlast_ref[...] = h[tt - SUBLANES:tt]
        ulast_ref[...] = u[tt - SUBLANES:tt]


def _lru_sconv(proj3, cbuf8, h0, sbuf8, cw, cb, wri, bri, lam, sw, tt, pos0):
    nb, tp, _ = proj3.shape
    w = GROUP_W

    def col(c):
        return pl.BlockSpec((None, tt, w), lambda b, t, c=c: (b, t, c))

    state8 = pl.BlockSpec((None, SUBLANES, w), lambda b, t: (b, 0, 0))
    return pl.pallas_call(
        functools.partial(_lru_sconv_body, tt=tt, pos0=pos0),
        grid=(nb, tp // tt),
        in_specs=[col(COL_XA), col(COL_GA), col(COL_BG), col(COL_CG), col(COL_XIN),
                  state8, pl.BlockSpec((None, 1, w), lambda b, t: (b, 0, 0)), state8,
                  _const_spec((4, w)), _const_spec((1, w)), _const_spec((w, 2 * w)), _const_spec((1, 2 * w)),
                  _const_spec((1, w)), _const_spec((3, w))],
        out_specs=[pl.BlockSpec((None, tt, w), lambda b, t: (b, t, 0)),
                   pl.BlockSpec((None, tt, w), lambda b, t: (b, t, 0)), state8, state8],
        out_shape=[jax.ShapeDtypeStruct((nb, tp, w), F32), jax.ShapeDtypeStruct((nb, tp, w), F32),
                   jax.ShapeDtypeStruct((nb, SUBLANES, w), F32), jax.ShapeDtypeStruct((nb, SUBLANES, w), F32)],
        scratch_shapes=[pltpu.VMEM((tt + SUBLANES, w), F32), pltpu.VMEM((tt + SUBLANES, w), F32),
                        pltpu.VMEM((1, w), F32)],
        compiler_params=_params("parallel", "arbitrary"),
        name="lru_sconv",
    )(proj3, proj3, proj3, proj3, proj3, cbuf8, h0, sbuf8, cw, cb, wri, bri, lam, sw)


def _rwkv_tokens(t, p_ref, mu_ref, w0_ref, w2_ref, a0_ref, a2_ref, g2_ref, kk_ref, ka_ref, rk_ref, j4,
                 ext_ref, w_buf, r_buf, k_buf, v_buf, a_buf, b_buf, g_buf, bon_buf, *, tt, chunk, t_valid):
    w = GROUP_W
    p = p_ref[...]
    ext_ref[SUBLANES:SUBLANES + tt, :] = p
    m = p + (ext_ref[pl.ds(SUBLANES - 1, tt), :] - p) * mu_ref[...]
    ext_ref[0:SUBLANES, :] = p[tt - SUBLANES:tt]
    r = m[:, 0:w]
    k = m[:, w:2 * w]
    v = m[:, 2 * w:3 * w]
    wa = m[:, 3 * w:3 * w + LANES]
    gl = m[:, 3 * w + LANES:4 * w]
    wlin = w0_ref[...] + _dot(jnp.tanh(wa).astype(BF16), w2_ref[...])
    log_decay = -jnp.exp(-_softplus(-wlin) - 0.5)
    ag = _sigmoid(a0_ref[...] + _dot(wa.astype(BF16), a2_ref[...]))
    g_buf[...] = _dot(_sigmoid(gl).astype(BF16), g2_ref[...])
    kk = k * kk_ref[...]
    kk = kk * lax.rsqrt(jnp.maximum(_dot_exact_rhs(kk * kk, j4), 1e-24))
    kf = k * (1.0 + (ag - 1.0) * ka_ref[...])
    rows = lax.broadcasted_iota(jnp.int32, (tt, w), 0)
    if t_valid % tt != 0:
        live = rows + t * tt < t_valid
        log_decay = jnp.where(live, log_decay, 0.0)
        kk = jnp.where(live, kk, 0.0)
        kf = jnp.where(live, kf, 0.0)
        v = jnp.where(live, v, 0.0)
    bon_buf[...] = _dot_exact_rhs(r * kf * rk_ref[...], j4) * v

    rin = rows & (chunk - 1)
    cl = log_decay
    s = 1
    while s < chunk:
        cl = cl + jnp.where(rin >= s, pltpu.roll(cl, s, 0), 0.0)
        s *= 2
    e_neg = jnp.exp(-cl)
    w_buf[...] = cl
    r_buf[...] = r * jnp.exp(cl)
    a_buf[...] = -kk * jnp.exp(cl - log_decay)
    b_buf[...] = kk * ag * e_neg
    k_buf[...] = kf * e_neg
    v_buf[...] = v


def _rwkv2_body(p_ref, shift_ref, sin_ref, mu_ref, w0_ref, w2_ref, a0_ref, a2_ref, g2_ref, kk_ref, ka_ref,
                rk_ref, lng_ref, lnb_ref, j4_ref, y_ref, sout_ref,
                ext_ref, s_ref, w_buf, r_buf, k_buf, v_buf, a_buf, b_buf, g_buf, bon_buf, y_buf,
                *, n_seq, tt, chunk, t_valid):
    t = pl.program_id(1)
    w = GROUP_W
    nh = N_HEADS
    nc = tt // chunk
    j4 = j4_ref[...]

    heads = [slice(h * HEAD_D, (h + 1) * HEAD_D) for h in range(nh)]

    @pl.when(t == 0)
    def _():
        for i in range(n_seq):
            ext_ref[i, 0:SUBLANES, :] = shift_ref[i]
            s_ref[i] = jnp.zeros((w, w), F32)
            for hs in heads:
                s_ref[i, hs, hs] = sin_ref[i, hs, :]

    for i in range(n_seq):
        _rwkv_tokens(t, p_ref.at[i], mu_ref, w0_ref, w2_ref, a0_ref, a2_ref, g2_ref, kk_ref, ka_ref, rk_ref, j4,
                     ext_ref.at[i], w_buf.at[i], r_buf.at[i], k_buf.at[i], v_buf.at[i], a_buf.at[i], b_buf.at[i],
                     g_buf.at[i], bon_buf.at[i], tt=tt, chunk=chunk, t_valid=t_valid)

    sl = nh * chunk
    hm_rows = lax.broadcasted_iota(jnp.int32, (sl, w), 0) // chunk
    hm_cols = lax.broadcasted_iota(jnp.int32, (sl, w), 1) // HEAD_D
    head_mask = (hm_rows == hm_cols).astype(F32)
    ri = lax.broadcasted_iota(jnp.int32, (sl, sl), 0)
    ci = lax.broadcasted_iota(jnp.int32, (sl, sl), 1)
    same = (ri // chunk) == (ci // chunk)
    strict = (same & ((ri & (chunk - 1)) > (ci & (chunk - 1)))).astype(F32)
    incl = (same & ((ri & (chunk - 1)) >= (ci & (chunk - 1)))).astype(F32)
    eye = (ri == ci).astype(F32)

    def stacked(buf, i, c):
        return jnp.concatenate([buf[i, pl.ds(c * chunk, chunk), :]] * nh, axis=0) * head_mask

    chains = [(i, c) for c in range(nc) for i in range(n_seq)]
    a_s = [stacked(a_buf, i, c).astype(BF16) for i, c in chains]
    r_s = [stacked(r_buf, i, c).astype(BF16) for i, c in chains]
    b_s = [stacked(b_buf, i, c).astype(BF16) for i, c in chains]
    k_s = [stacked(k_buf, i, c).astype(BF16) for i, c in chains]
    v_s = [stacked(v_buf, i, c).astype(BF16) for i, c in chains]
    n_mat = [_dot_nt(a, b) * strict for a, b in zip(a_s, b_s)]
    m_mat = [(_dot_nt(a, k) * strict).astype(BF16) for a, k in zip(a_s, k_s)]
    p_mat = [(_dot_nt(r, b) * incl).astype(BF16) for r, b in zip(r_s, b_s)]
    q_mat = [(_dot_nt(r, k) * incl).astype(BF16) for r, k in zip(r_s, k_s)]
    t_mat = [eye + n for n in n_mat]
    x = n_mat
    step = 2
    while step < chunk:
        xb = [xi.astype(BF16) for xi in x]
        x = [_dot(b, b) for b in xb]
        t_mat = [tm + _dot(tm.astype(BF16), xi.astype(BF16)) for tm, xi in zip(t_mat, x)]
        step *= 2
    t_b = [tm.astype(BF16) for tm in t_mat]
    w_eff = [_dot(tb, a).astype(BF16) for tb, a in zip(t_b, a_s)]
    mv = [_dot(mm, v).astype(BF16) for mm, v in zip(m_mat, v_s)]
    z = [_dot(tb, x_) for tb, x_ in zip(t_b, mv)]
    qv = [_dot(qm, v) for qm, v in zip(q_mat, v_s)]

    for n, (i, c) in enumerate(chains):
        s0 = s_ref[i]
        s0b = s0.astype(BF16)
        ub = (_dot_nt(w_eff[n], s0b) + z[n]).astype(BF16)
        y_s = _dot_nt(r_s[n], s0b) + _dot(p_mat[n], ub) + qv[n]
        yc = y_s[0:chunk]
        for hh in range(1, nh):
            yc = yc + y_s[hh * chunk:(hh + 1) * chunk]
        y_buf[i, pl.ds(c * chunk, chunk), :] = yc
        c_last = jnp.exp(w_buf[i, pl.ds((c + 1) * chunk - 1, 1), :])
        s_ref[i] = (s0 + _dot_tn(ub, b_s[n]) + _dot_tn(v_s[n], k_s[n])) * c_last

    inv_hd = 1.0 / HEAD_D
    for i in range(n_seq):
        y = y_buf[i]
        mean = _dot_exact_rhs(y, j4) * inv_hd
        yc = y - mean
        var = _dot_exact_rhs(yc * yc, j4) * inv_hd
        yn = yc * lax.rsqrt(var + RWKV_LN_EPS) * lng_ref[...] + lnb_ref[...]
        y_ref[i] = (yn + bon_buf[i]) * g_buf[i]
        for hs in heads:
            sout_ref[i, hs, :] = s_ref[i, hs, hs]


def _rwkv2(proj3, shift8, s_heads, mu, w0, w2p, a0, a2p, g2, kk, ka, rk, lng, lnb, j4, tt, chunk, t_valid, n_seq):
    nb, tp, _ = proj3.shape
    w = GROUP_W
    pw = 4 * w
    vec = _const_spec((1, w))
    s_rows = s_heads.reshape(nb, w, HEAD_D)
    state_spec = pl.BlockSpec((n_seq, w, HEAD_D), lambda b, t: (b, 0, 0))
    yb, s_out = pl.pallas_call(
        functools.partial(_rwkv2_body, n_seq=n_seq, tt=tt, chunk=chunk, t_valid=t_valid),
        grid=(nb // n_seq, tp // tt),
        in_specs=[pl.BlockSpec((n_seq, tt, pw), lambda b, t: (b, t, 0)),
                  pl.BlockSpec((n_seq, SUBLANES, pw), lambda b, t: (b, 0, 0)), state_spec,
                  _const_spec((1, pw)), vec, _const_spec((LANES, w)), vec, _const_spec((LANES, w)),
                  _const_spec((LANES, w)), vec, vec, vec, vec, vec, _const_spec((w, w))],
        out_specs=[pl.BlockSpec((n_seq, tt, w), lambda b, t: (b, t, 0)), state_spec],
        out_shape=[jax.ShapeDtypeStruct((nb, tp, w), F32), jax.ShapeDtypeStruct(s_rows.shape, F32)],
        scratch_shapes=[pltpu.VMEM((n_seq, tt + SUBLANES, pw), F32), pltpu.VMEM((n_seq, w, w), F32)]
        + [pltpu.VMEM((n_seq, tt, w), F32)] * 9,
        compiler_params=_params("parallel", "arbitrary"),
        name="rwkv7",
    )(proj3, shift8, s_rows, mu, w0, w2p, a0, a2p, g2, kk, ka, rk, lng, lnb, j4)
    return yb, s_out.reshape(s_heads.shape)


def _proj_prep_body(x_ref, g_ref, w_ref, cos_ref, sin_ref, gq_ref, gk_ref, j2_ref,
                    o_ref, qo_ref, cmpo_ref, slco_ref, wino_ref, gate_ref, *bf_refs, n_chunk):
    h =_rmsnorm_rows(x_ref[...], g_ref[...]).astype(BF16)
    n_keep = o_ref.shape[1]
    for c in range(n_keep // n_chunk):
        sl = slice(c * n_chunk, (c + 1) * n_chunk)
        o_ref[:, sl] = _dot(h, w_ref[:, sl])
    nsa = _dot(h, w_ref[:, n_keep:])

    cos = cos_ref[...]
    sin = sin_ref[...]
    j2 = j2_ref[...]
    lane = lax.broadcasted_iota(jnp.int32, cos.shape, 1)
    inv_hd = 1.0 / HEAD_D
    gq = gq_ref[...]
    halves = []
    for c in range(2):
        x = nsa[:, c * LANES:(c + 1) * LANES]
        ms = _dot_exact_rhs(x * x, j2) * inv_hd
        halves.append(_rope_lanes(x * lax.rsqrt(ms + EPS) * gq, cos, sin, lane))
    qo_ref[...] = jnp.concatenate(halves, axis=1)
    cmpo_ref[...] = nsa[:, 2 * LANES:3 * LANES]
    gk = gk_ref[...]
    is_key = lane < HEAD_D
    for dst, row in ((slco_ref, 0), (wino_ref, 1)):
        x = nsa[:, (3 + row) * LANES:(4 + row) * LANES]
        ms = _dot_exact_rhs(x * x, j2) * inv_hd
        roped = _rope_lanes(x * lax.rsqrt(ms + EPS) * gk[row:row + 1], cos, sin, lane)
        kv = jnp.where(is_key, roped, x)
        dst[...] = kv
        if bf_refs:
            bf_refs[row][...] = kv.astype(BF16)
            kv_t = kv.T
            if row == 0:
                kv_t = jnp.where(lax.broadcasted_iota(jnp.int32, kv_t.shape, 0) == 0, 1.0, kv_t)
            bf_refs[2 + row][...] = kv_t.astype(BF16)
    gate_ref[...] = _sigmoid(nsa[:, 5 * LANES:6 * LANES])


def _proj_prep(x, g, w, layer, cos, sin, gq, gk2, j2, tm, rows_per_table, with_bf16):
    m, d = x.shape
    n = w.shape[2]
    nt_tab = rows_per_table // tm
    o128t = pl.BlockSpec((LANES, tm), lambda i: (0, i))
    bf_specs = [pl.BlockSpec((tm, LANES), lambda i: (i, 0))] * 2 + [o128t] * 2 if with_bf16 else []
    bf_shapes = ([jax.ShapeDtypeStruct((m, LANES), BF16)] * 2 + [jax.ShapeDtypeStruct((LANES, m), BF16)] * 2
                 if with_bf16 else [])
    tab = pl.BlockSpec((tm, LANES), lambda i: (i % nt_tab, 0))
    o128 = pl.BlockSpec((tm, LANES), lambda i: (i, 0))
    return pl.pallas_call(
        functools.partial(_proj_prep_body, n_chunk=256),
        grid=(m // tm,),
        in_specs=[pl.BlockSpec((tm, d), lambda i: (i, 0)), _const_spec((1, d)),
                  pl.BlockSpec((None, d, n), lambda i: (layer, 0, 0), pipeline_mode=pl.Buffered(1)),
                  tab, tab, _const_spec((1, LANES)), _const_spec((2, LANES)), _const_spec((LANES, LANES))],
        out_specs=[pl.BlockSpec((tm, PROJ_KEEP), lambda i: (i, 0)),
                   pl.BlockSpec((tm, GROUP_W), lambda i: (i, 0)), o128, o128, o128, o128] + bf_specs,
        out_shape=[jax.ShapeDtypeStruct((m, PROJ_KEEP), F32), jax.ShapeDtypeStruct((m, GROUP_W), F32)]
        + [jax.ShapeDtypeStruct((m, LANES), F32)] * 4 + bf_shapes,
        compiler_params=_params("parallel"),
        name="proj_prep",
    )(x, g, w, cos, sin, gq, gk2, j2)


def _compress_tail(hid_lo, hid_hi_next, b1_ref, w2_ref, b2_ref, gk_ref, cos_ref, sin_ref, j2_ref):
    hidden = _gelu_tanh(hid_lo + hid_hi_next + b1_ref[...])
    kv = _dot(hidden.astype(BF16), w2_ref[...]) + b2_ref[...]
    lane = lax.broadcasted_iota(jnp.int32, kv.shape, 1)
    ms = _dot_exact_rhs(kv * kv, j2_ref[...]) * (1.0 / HEAD_D)
    roped = _rope_lanes(kv * lax.rsqrt(ms + EPS) * gk_ref[...], cos_ref[...], sin_ref[...], lane)
    return jnp.where(lane < HEAD_D, roped, kv)


def _compress_hidden(read_rows, pelo_ref, pehi_ref, wlo_ref, whi_ref):
    pair = 2 * LANES
    lo = hi = None
    for p in range(CMP_STRIDE // 2):
        x = jnp.concatenate([read_rows(2 * p), read_rows(2 * p + 1)], axis=1)
        cs = slice(p * pair, (p + 1) * pair)
        d_lo = _dot((x + pelo_ref[:, cs]).astype(BF16), wlo_ref[cs, :])
        d_hi = _dot((x + pehi_ref[:, cs]).astype(BF16), whi_ref[cs, :])
        lo = d_lo if lo is None else lo + d_lo
        hi = d_hi if hi is None else hi + d_hi
    return lo, hi


def _compress_body(x_ref, pelo_ref, pehi_ref, wlo_ref, whi_ref, b1_ref, w2_ref, b2_ref, gk_ref,
                   cos_ref, sin_ref, j2_ref, o_ref, ot_ref):
    n_grp = x_ref.shape[0] // CMP_STRIDE
    lo, hi = _compress_hidden(lambda j: x_ref[pl.ds(j, n_grp, stride=CMP_STRIDE), :],
                              pelo_ref, pehi_ref, wlo_ref, whi_ref)
    hi_next = pltpu.roll(hi, n_grp - 1, 0)
    kv = _compress_tail(lo, hi_next, b1_ref, w2_ref, b2_ref, gk_ref, cos_ref, sin_ref, j2_ref)
    o_ref[...] = kv.astype(BF16)
    ot_ref[...] = kv.T.astype(BF16)


def _compress(rows, cw, cos, sin, j2):
    nb, t, _ = rows.shape
    n_grp = t // CMP_STRIDE
    wid = CMP_STRIDE * LANES
    return pl.pallas_call(
        _compress_body,
        grid=(nb,),
        in_specs=[pl.BlockSpec((None, t, LANES), lambda b: (b, 0, 0)),
                  _const_spec((1, wid)), _const_spec((1, wid)),
                  _const_spec((wid, GROUP_W)), _const_spec((wid, GROUP_W)), _const_spec((1, GROUP_W)),
                  _const_spec((GROUP_W, LANES)), _const_spec((1, LANES)), _const_spec((1, LANES)),
                  _const_spec((n_grp, LANES)), _const_spec((n_grp, LANES)), _const_spec((LANES, LANES))],
        out_specs=[pl.BlockSpec((None, n_grp, LANES), lambda b: (b, 0, 0)),
                   pl.BlockSpec((LANES, n_grp), lambda b: (0, b))],
        out_shape=[jax.ShapeDtypeStruct((nb, n_grp, LANES), BF16),
                   jax.ShapeDtypeStruct((LANES, nb * n_grp), BF16)],
        compiler_params=_params("parallel"),
        name="nsa_compress",
    )(rows, cw["pe_lo"], cw["pe_hi"], cw["w_lo"], cw["w_hi"], cw["b1"], cw["w2"], cw["b2"], cw["gk"], cos, sin, j2)


def _stack_heads(q, lane):
    parts = []
    for h in range(N_HEADS):
        blk = q[:, (h // 2) * LANES:(h // 2 + 1) * LANES]
        if h % 2 == 1:
            blk = pltpu.roll(blk, HEAD_D, 1)
        parts.append(jnp.where(lane < HEAD_D, blk, 0.0))
    return jnp.concatenate(parts, axis=0)


def _unstack_heads(parts, lane):
    b01 = jnp.where(lane < HEAD_D, pltpu.roll(parts[0], HEAD_D, 1), parts[1])
    b23 = jnp.where(lane < HEAD_D, pltpu.roll(parts[2], HEAD_D, 1), parts[3])
    return jnp.concatenate([b01, b23], axis=1)


def _softmax_cols(st, mask):
    st = jnp.where(mask, st, NEG_INF)
    m = jnp.max(st, axis=0, keepdims=True)
    e = jnp.exp(st - m)
    norm = 1.0 / jnp.maximum(jnp.sum(e, axis=0, keepdims=True), 1e-30)
    return e * jnp.where(m > 0.5 * NEG_INF, norm, 0.0)


def _nsa_attn_t_body(q_ref, gate_ref, kvc_ref, kvct_ref, slc_ref, slct_ref, win_ref, wint_ref, ovl_ref, exp_ref,
                     o_ref, acc_ref, st_ref, e_ref, *, tq, tk, n_cmp, n_sel, n_top, win_span):
    i = pl.program_id(1)
    start = i * tq
    cols = N_HEADS * tq
    lane = lax.broadcasted_iota(jnp.int32, (tq, LANES), 1)
    q4 = _stack_heads(q_ref[...] * ATT_SCALE, lane).astype(BF16)
    tlane = start + (lax.broadcasted_iota(jnp.int32, (1, cols), 1) & (tq - 1))

    kvc = kvc_ref[...]
    ng = kvc.shape[0]
    ncol = lax.broadcasted_iota(jnp.int32, (ng, 1), 0)
    cend = jnp.where(ncol < n_cmp, ncol * CMP_STRIDE + (CMP_LEN - 1), jnp.iinfo(jnp.int32).max)
    pt = _softmax_cols(_dot_nt(kvc, q4), cend <= tlane)
    o_cmp = _dot(kvct_ref[...], pt.astype(BF16))
    psum = pt[:, 0:tq]
    for h in range(1, N_HEADS):
        psum = psum + pt[:, h * tq:(h + 1) * tq]
    ph = psum.astype(BF16)
    pl_ = (psum - ph.astype(F32)).astype(BF16)
    ovl = ovl_ref[...]
    imp_t = _dot(ovl, ph) + _dot(ovl, pl_)

    start0 = pl.multiple_of(jnp.maximum(start + tq - win_span, 0), tq)
    rel = tlane - (start0 + lax.broadcasted_iota(jnp.int32, (win_span, 1), 0))
    pw = _softmax_cols(_dot_nt(win_ref[pl.ds(start0, win_span), :], q4), (rel >= 0) & (rel < WINDOW))
    o_win = _dot(wint_ref[:, pl.ds(start0, win_span)], pw.astype(BF16))

    nsp = imp_t.shape[0]
    jblk = lax.broadcasted_iota(jnp.int32, (nsp, tq), 0)
    tcol = start + lax.broadcasted_iota(jnp.int32, (nsp, tq), 1)
    cur = tcol // SEL_BLOCK
    forced = (jblk == 0) | (jblk == cur) | (jblk == cur - 1)
    valid = (jblk * SEL_BLOCK <= tcol) & (jblk < n_sel)
    score = jnp.where(valid, imp_t + jnp.where(forced, FORCE_BONUS, 0.0), -1.0)
    taken = -3.0
    jf = jblk.astype(F32)
    left = score
    for _ in range(n_top):
        best = jnp.max(left, axis=0, keepdims=True)
        first = jnp.min(jnp.where(left == best, jf, float(nsp)), axis=0, keepdims=True)
        left = jnp.where(jf == first, taken, left)
    bias_t = jnp.where((left == taken) & (score >= 0.0), 0.0, NEG_INF)
    bias = bias_t.T.astype(BF16)
    lhs = jnp.concatenate([q4, jnp.concatenate([bias] * N_HEADS, axis=0)], axis=1)

    acc_ref[...] = jnp.zeros((LANES, cols), F32)
    n_kt = (start + tq + tk - 1) // tk

    def stage_scores(kt, causal):
        off = pl.multiple_of(kt * tk, tk)
        k_aug = jnp.concatenate([slc_ref[pl.ds(off, tk), :], exp_ref[pl.ds(off, tk), :]], axis=1)
        st = _dot_nt(k_aug, lhs)
        if causal:
            kpos = off + lax.broadcasted_iota(jnp.int32, (tk, 1), 0)
            st = jnp.where(kpos <= tlane, st, NEG_INF)
        return st, jnp.max(st.reshape(tk // SUBLANES, SUBLANES, cols), axis=0)

    def stage_exp(st, mt, m_prev):
        m_new = jnp.maximum(m_prev, jnp.max(mt, axis=0, keepdims=True))
        m_sub = jnp.where(m_new > 0.5 * NEG_INF, m_new, 0.0)
        e = jnp.exp(st - m_sub[0:1])
        return e.astype(BF16), jnp.exp(m_prev - m_new), m_new

    def stage_values(kt, e, alpha):
        off = pl.multiple_of(jnp.maximum(kt, 0) * tk, tk)
        acc_ref[...] = alpha[0:1] * acc_ref[...] + _dot(slct_ref[:, pl.ds(off, tk)], e)

    def trip(i, carry, causal):
        mt, alpha, m_run = carry
        stage_values(i - 2, e_ref[...], alpha)
        e, alpha, m_run = stage_exp(st_ref[...], mt, m_run)
        e_ref[...] = e
        st, mt = stage_scores(i, causal)
        st_ref[...] = st
        return mt, alpha, m_run

    st_ref[...] = jnp.full((tk, cols), NEG_INF, F32)
    e_ref[...] = jnp.zeros((tk, cols), BF16)
    neg8 = st_ref[0:SUBLANES, :]
    carry = (neg8, acc_ref[0:SUBLANES, :] + 1.0, neg8)
    carry = lax.fori_loop(0, n_kt - 1, functools.partial(trip, causal=False), carry)
    mt, alpha, m_run = trip(n_kt - 1, carry, True)
    stage_values(n_kt - 2, e_ref[...], alpha)
    e, alpha, _ = stage_exp(st_ref[...], mt, m_run)
    stage_values(n_kt - 1, e, alpha)
    acc = acc_ref[...]
    o_sel = acc / jnp.maximum(acc[0:1], 1e-30)

    gt = gate_ref[...].T
    outs = []
    for h in range(N_HEADS):
        cs = slice(h * tq, (h + 1) * tq)
        outs.append(gt[3 * h:3 * h + 1] * o_cmp[HEAD_D:, cs] + gt[3 * h + 1:3 * h + 2] * o_sel[HEAD_D:, cs]
                    + gt[3 * h + 2:3 * h + 3] * o_win[HEAD_D:, cs])
    o_ref[...] = jnp.concatenate(outs, axis=0).T


def _nsa_attn_t(q, gates, kvc, kvc_t, slc, slc_t, win, win_t, ovl_t, expand, tq, tk, n_cmp, n_sel):
    nb, t, _ = q.shape
    ng = kvc.shape[1]
    nsp = ovl_t.shape[0]
    cols = N_HEADS * tq
    win_span = WINDOW + tq
    per_b = lambda b, i: (b, 0, 0)
    per_b_t = lambda b, i: (0, b)
    return pl.pallas_call(
        functools.partial(_nsa_attn_t_body, tq=tq, tk=tk, n_cmp=n_cmp, n_sel=n_sel,
                          n_top=min(N_SELECT, n_sel), win_span=win_span),
        grid=(nb, t // tq),
        in_specs=[pl.BlockSpec((None, tq, GROUP_W), lambda b, i: (b, i, 0)),
                  pl.BlockSpec((None, tq, LANES), lambda b, i: (b, i, 0)),
                  pl.BlockSpec((None, ng, LANES), per_b), pl.BlockSpec((LANES, ng), per_b_t),
                  pl.BlockSpec((None, t, LANES), per_b), pl.BlockSpec((LANES, t), per_b_t),
                  pl.BlockSpec((None, t, LANES), per_b), pl.BlockSpec((LANES, t), per_b_t),
                  _const_spec((nsp, ng)), _const_spec((t, nsp))],
        out_specs=pl.BlockSpec((None, tq, GROUP_W), lambda b, i: (b, i, 0)),
        out_shape=jax.ShapeDtypeStruct((nb, t, GROUP_W), F32),
        scratch_shapes=[pltpu.VMEM((LANES, cols), F32), pltpu.VMEM((tk, cols), F32), pltpu.VMEM((tk, cols), BF16)],
        compiler_params=_params("parallel", "arbitrary"),
        name="nsa_attn",
    )(q, gates, kvc, kvc_t, slc, slc_t, win, win_t, ovl_t, expand)


def _samp_cmp_body(*refs, pages_per_step, n_grp, n_cmp, n_sel, n_top, t_pos):
    pt_ref = refs[0]
    page_refs = refs[1:1 + pages_per_step]
    (pelo_ref, pehi_ref, wlo_ref, whi_ref, b1_ref, w2_ref, b2_ref, gk_ref, cos_ref, sin_ref, j2_ref,
     q4_ref, ovl_ref, ocmp_ref, idx_ref, lo_ref, hi_ref) = refs[1 + pages_per_step:]
    del pt_ref
    s = pl.program_id(1)
    ns = pl.num_programs(1)
    grp_per_page = PAGE_SIZE // CMP_STRIDE
    rows_step = pages_per_step * grp_per_page

    @pl.when(s == 0)
    def _():
        hi_ref[n_grp:n_grp + SUBLANES, :] = jnp.zeros((SUBLANES, GROUP_W), F32)

    def read_rows(j):
        return jnp.concatenate([pg[pl.ds(j, grp_per_page, stride=CMP_STRIDE), :] for pg in page_refs], axis=0)

    lo, hi = _compress_hidden(read_rows, pelo_ref, pehi_ref, wlo_ref, whi_ref)
    off = pl.multiple_of(s * rows_step, rows_step)
    lo_ref[pl.ds(off, rows_step), :] = lo
    hi_ref[pl.ds(off, rows_step), :] = hi

    @pl.when(s == ns - 1)
    def _():
        kvc = _compress_tail(lo_ref[...], hi_ref[pl.ds(1, n_grp), :], b1_ref, w2_ref, b2_ref, gk_ref,
                             cos_ref, sin_ref, j2_ref).astype(BF16)
        q4 = q4_ref[...].astype(BF16)
        sc = _dot_nt(q4, kvc) * ATT_SCALE
        nidx = lax.broadcasted_iota(jnp.int32, sc.shape, 1)
        p = _masked_softmax_rows(sc, (nidx * CMP_STRIDE + (CMP_LEN - 1) <= t_pos) & (nidx < n_cmp))
        ocmp_ref[...] = _dot(p.astype(BF16), kvc)
        hrow = lax.broadcasted_iota(jnp.int32, p.shape, 0)
        psum = jnp.sum(jnp.where(hrow < N_HEADS, p, 0.0), axis=0, keepdims=True)
        psum8 = jnp.broadcast_to(psum, p.shape)
        ph = psum8.astype(BF16)
        pl_ = (psum8 - ph.astype(F32)).astype(BF16)
        ovl = ovl_ref[...]
        imp = (_dot(ph, ovl) + _dot(pl_, ovl))[0:1]
        nsp = imp.shape[1]
        jrow = lax.broadcasted_iota(jnp.int32, (1, nsp), 1)
        cur = t_pos // SEL_BLOCK
        forced = (jrow == 0) | (jrow == cur) | (jrow == cur - 1)
        valid = (jrow * SEL_BLOCK <= t_pos) & (jrow < n_sel)
        score = jnp.where(valid, imp + jnp.where(forced, FORCE_BONUS, 0.0), -1.0)
        score = jnp.where(jrow < n_sel, score, -2.0)
        s_row = jnp.broadcast_to(score, (nsp, nsp))
        s_col = s_row.T
        ii = lax.broadcasted_iota(jnp.int32, (nsp, nsp), 0)
        jj = lax.broadcasted_iota(jnp.int32, (nsp, nsp), 1)
        beats = (s_col > s_row) | ((s_col == s_row) & (ii < jj))
        rank = jnp.sum(jnp.where(beats, 1.0, 0.0), axis=0, keepdims=True)
        chosen = (rank < n_top) & (score >= 0.0)
        slot = lax.broadcasted_iota(jnp.int32, (N_SELECT, nsp), 0).astype(F32)
        hit = (jnp.broadcast_to(rank, (N_SELECT, nsp)) == slot) & jnp.broadcast_to(chosen, (N_SELECT, nsp))
        jcol = lax.broadcasted_iota(jnp.int32, (N_SELECT, nsp), 1).astype(F32)
        blk = jnp.sum(jnp.where(hit, jcol, 0.0), axis=1, keepdims=True)
        cnt = jnp.sum(jnp.where(hit, 1.0, 0.0), axis=1, keepdims=True)
        blk = jnp.where(cnt > 0.5, blk, -1.0)
        idx_ref[...] = jnp.broadcast_to(blk, (N_SELECT, LANES)).astype(jnp.int32)


def _samp_cmp(cache, layer, page_table, cw, cos, sin, j2, q4, ovl, n_cmp, n_sel, t_pos, pages_per_step):
    nb, n_pages = page_table.shape
    grp_per_page = PAGE_SIZE // CMP_STRIDE
    wid = CMP_STRIDE * LANES
    n_grp = n_pages * grp_per_page
    nsp = ovl.shape[1]

    def page_spec(j):
        return pl.BlockSpec((None, None, PAGE_SIZE, LANES),
                            lambda b, s, pt, j=j: (layer, pt[b, s * pages_per_step + j], 0, 0))

    def cst(shape):
        nd = len(shape)
        return pl.BlockSpec(shape, lambda b, s, pt: (0,) * nd)

    grid_spec = pltpu.PrefetchScalarGridSpec(
        num_scalar_prefetch=1,
        grid=(nb, n_pages // pages_per_step),
        in_specs=[page_spec(j) for j in range(pages_per_step)] + [
            cst((1, wid)), cst((1, wid)), cst((wid, GROUP_W)), cst((wid, GROUP_W)), cst((1, GROUP_W)),
            cst((GROUP_W, LANES)), cst((1, LANES)), cst((1, LANES)),
            cst((n_grp, LANES)), cst((n_grp, LANES)), cst((LANES, LANES)),
            pl.BlockSpec((None, SUBLANES, LANES), lambda b, s, pt: (b, 0, 0)), cst((n_grp, nsp))],
        out_specs=[pl.BlockSpec((None, SUBLANES, LANES), lambda b, s, pt: (b, 0, 0)),
                   pl.BlockSpec((None, N_SELECT, LANES), lambda b, s, pt: (b, 0, 0))],
        scratch_shapes=[pltpu.VMEM((n_grp, GROUP_W), F32), pltpu.VMEM((n_grp + SUBLANES, GROUP_W), F32)],
    )
    return pl.pallas_call(
        functools.partial(_samp_cmp_body, pages_per_step=pages_per_step, n_grp=n_grp, n_cmp=n_cmp,
                          n_sel=n_sel, n_top=min(N_SELECT, n_sel), t_pos=t_pos),
        grid_spec=grid_spec,
        out_shape=[jax.ShapeDtypeStruct((nb, SUBLANES, LANES), F32),
                   jax.ShapeDtypeStruct((nb, N_SELECT, LANES), jnp.int32)],
        compiler_params=_params("parallel", "arbitrary"),
        name="nsa_decode_compress",
    )(page_table, *([cache] * pages_per_step), cw["pe_lo"], cw["pe_hi"], cw["w_lo"], cw["w_hi"], cw["b1"],
      cw["w2"], cw["b2"], cw["gk"], cos, sin, j2, q4, ovl)


def _samp_sel_body(pt_ref, idx_ref, *refs, n_past_blk, t_pos, win_pos0):
    del pt_ref
    blk_refs = refs[:N_SELECT]
    q4_ref, nslc_ref, win_ref, nwin_ref, gate_ref, ocmp_ref, o_ref = refs[N_SELECT:]
    b = pl.program_id(0)
    q4 = q4_ref[...].astype(BF16)

    n_keys = N_SELECT * SEL_BLOCK
    lane_k = lax.broadcasted_iota(jnp.int32, (1, n_keys), 1)
    jvec = jnp.zeros((1, n_keys), jnp.int32)
    parts = []
    for k in range(N_SELECT):
        j = idx_ref[b, k]
        blk = blk_refs[k][...]
        parts.append(jnp.where(j == n_past_blk, nslc_ref[...], blk[0:SUBLANES]))
        parts.append(blk[SUBLANES:])
        jvec = jnp.where(lane_k // SEL_BLOCK == k, j, jvec)
    kv = jnp.concatenate(parts, axis=0).astype(BF16)
    kpos = jvec * SEL_BLOCK + (lane_k & (SEL_BLOCK - 1))
    p_sel = _masked_softmax_rows(_dot_nt(q4, kv) * ATT_SCALE, (kpos <= t_pos) & (jvec >= 0))
    o_sel = _dot(p_sel.astype(BF16), kv)

    wb = win_ref[...].astype(BF16)
    nw = nwin_ref[...].astype(BF16)
    s_w = _dot_nt(q4, wb) * ATT_SCALE
    s_n = _dot_nt(q4, nw) * ATT_SCALE
    wpos = win_pos0 + lax.broadcasted_iota(jnp.int32, s_w.shape, 1)
    rel = t_pos - wpos
    m_w = (rel >= 0) & (rel < WINDOW) & (wpos >= win_pos0)
    m_n = lax.broadcasted_iota(jnp.int32, s_n.shape, 1) == 0
    s_w = jnp.where(m_w, s_w, NEG_INF)
    s_n = jnp.where(m_n, s_n, NEG_INF)
    mx = jnp.maximum(jnp.max(s_w, axis=-1, keepdims=True), jnp.max(s_n, axis=-1, keepdims=True))
    e_w = jnp.exp(s_w - mx) * m_w.astype(F32)
    e_n = jnp.exp(s_n - mx) * m_n.astype(F32)
    den = jnp.sum(e_w, axis=-1, keepdims=True) + jnp.sum(e_n, axis=-1, keepdims=True)
    o_win = (_dot(e_w.astype(BF16), wb) + _dot(e_n.astype(BF16), nw)) / jnp.maximum(den, 1e-30)
    o_cmp = ocmp_ref[...]
    g = gate_ref[...]
    lane = lax.broadcasted_iota(jnp.int32, (1, LANES), 1)
    outs = []
    for h in range(N_HEADS):
        outs.append(g[0:1, 3 * h:3 * h + 1] * o_cmp[h:h + 1] + g[0:1, 3 * h + 1:3 * h + 2] * o_sel[h:h + 1]
                    + g[0:1, 3 * h + 2:3 * h + 3] * o_win[h:h + 1])
    o_ref[...] = jnp.broadcast_to(_unstack_heads(outs, lane), o_ref.shape)


def _samp_sel(cache_blk, layer, page_table, top_idx, q4, nslc8, win, nwin8, gates8, ocmp, t_pos, win_pos0):
    nb, n_pages = page_table.shape
    blk_per_page = PAGE_SIZE // SEL_BLOCK
    n_past_blk = n_pages * blk_per_page
    wlen = win.shape[2]

    def blk_spec(k):
        def blk_map(b, pt, idx):
            j = jnp.clip(idx[b, k], 0, n_past_blk - 1)
            return (layer, pt[b, j // blk_per_page], j % blk_per_page, 0, 0)
        return pl.BlockSpec((None, None, None, SEL_BLOCK, LANES), blk_map)

    row8 = pl.BlockSpec((None, SUBLANES, LANES), lambda b, pt, idx: (b, 0, 0))
    grid_spec = pltpu.PrefetchScalarGridSpec(
        num_scalar_prefetch=2,
        grid=(nb,),
        in_specs=[blk_spec(k) for k in range(N_SELECT)] + [
            row8, row8, pl.BlockSpec((None, None, wlen, LANES), lambda b, pt, idx: (layer, b, 0, 0)),
            row8, row8, row8],
        out_specs=pl.BlockSpec((None, SUBLANES, GROUP_W), lambda b, pt, idx: (b, 0, 0)),
    )
    return pl.pallas_call(
        functools.partial(_samp_sel_body, n_past_blk=n_past_blk, t_pos=t_pos, win_pos0=win_pos0),
        grid_spec=grid_spec,
        out_shape=jax.ShapeDtypeStruct((nb, SUBLANES, GROUP_W), F32),
        compiler_params=_params("parallel"),
        name="nsa_decode_select",
    )(page_table, top_idx, *([cache_blk] * N_SELECT), q4, nslc8, win, nwin8, gates8, ocmp)


def _block_diag(blocks):
    n, a, b = blocks.shape
    eye = jnp.eye(n, dtype=blocks.dtype)
    return jnp.einsum("nab,nm->namb", blocks, eye).reshape(n * a, n * b)


def _rope_tables(pos):
    half = HEAD_D // 2
    inv = ROPE_THETA ** (-jnp.arange(half, dtype=F32) / half)
    ang = pos.astype(F32)[:, None] * inv[None, :]
    cos, sin = jnp.cos(ang), jnp.sin(ang)
    cos128 = jnp.concatenate([cos, cos, cos, cos], axis=1)
    sin128 = jnp.concatenate([-sin, sin, -sin, sin], axis=1)
    return cos128, sin128


def _seg_ones(n_seg):
    return _block_diag(jnp.ones((n_seg, HEAD_D, HEAD_D), F32)).astype(BF16)


def _w_in_stack(w_in):
    a_w, b_w, c_w = 2 * GROUP_W, 4 * GROUP_W, 3 * GROUP_W
    return jnp.concatenate(
        [w_in[..., a_w:a_w + b_w], w_in[..., :a_w], w_in[..., a_w + b_w:a_w + b_w + c_w], w_in[..., a_w + b_w + c_w:],
         jnp.zeros(w_in.shape[:-1] + (PROJ_PAD - w_in.shape[-1],), F32)], axis=-1).astype(BF16)


def _layer_weights(lp):
    lora = HEAD_D
    w1 = lp["nsa_cmp_w1"].reshape(2, 2, CMP_STRIDE, HEAD_D, 2 * HEAD_D)
    pe = lp["nsa_cmp_pe"].reshape(2, 2, CMP_STRIDE, HEAD_D)

    def cmp_half(hf):
        wk = jnp.zeros((CMP_STRIDE, 2 * HEAD_D, 4 * HEAD_D), F32)
        wk = wk.at[:, :HEAD_D, :2 * HEAD_D].set(w1[0, hf]).at[:, HEAD_D:, 2 * HEAD_D:].set(w1[1, hf])
        pk = jnp.concatenate([pe[0, hf], pe[1, hf]], axis=1)
        return wk.reshape(CMP_STRIDE * 2 * HEAD_D, 4 * HEAD_D).astype(BF16), pk.reshape(1, CMP_STRIDE * 2 * HEAD_D)

    w_lo, pe_lo = cmp_half(0)
    w_hi, pe_hi = cmp_half(1)
    ones64 = jnp.ones((HEAD_D,), F32)
    cw = dict(
        w_lo=w_lo, w_hi=w_hi, pe_lo=pe_lo, pe_hi=pe_hi,
        b1=lp["nsa_cmp_b1"].reshape(1, 4 * HEAD_D),
        w2=_block_diag(lp["nsa_cmp_w2"]).astype(BF16),
        b2=lp["nsa_cmp_b2"].reshape(1, 2 * HEAD_D),
        gk=jnp.concatenate([lp["nsa_norm_k"][0], ones64]).reshape(1, LANES),
    )
    zeros_lora = jnp.zeros((lora, GROUP_W), F32)
    return dict(
        norm_ffn=lp["norm_ffn"].reshape(2, 1, D_MODEL),
        norm_mix=lp["norm_mix"].reshape(1, D_MODEL),
        lru_cw=lp["lru_conv_w"], lru_cb=lp["lru_conv_b"].reshape(1, GROUP_W),
        lru_wri=jnp.concatenate([_block_diag(lp["lru_w_r"]), _block_diag(lp["lru_w_i"])], axis=1).astype(BF16),
        lru_bri=jnp.concatenate([lp["lru_b_r"], lp["lru_b_i"]]).reshape(1, 2 * GROUP_W),
        lru_lam=lp["lru_lambda"].reshape(1, GROUP_W), sconv_w=lp["sconv_w"],
        mu=lp["rwkv_mu"].reshape(1, 4 * GROUP_W), w0=lp["rwkv_w0"].reshape(1, GROUP_W),
        w2p=jnp.concatenate([lp["rwkv_w2"], zeros_lora], axis=0).astype(BF16),
        a0=lp["rwkv_a0"].reshape(1, GROUP_W),
        a2p=jnp.concatenate([zeros_lora, lp["rwkv_a2"]], axis=0).astype(BF16),
        g2=lp["rwkv_g2"].astype(BF16),
        k_k=lp["rwkv_k_k"].reshape(1, GROUP_W), k_a=lp["rwkv_k_a"].reshape(1, GROUP_W),
        r_k=lp["rwkv_r_k"].reshape(1, GROUP_W),
        ln_g=lp["rwkv_ln_g"].reshape(1, GROUP_W), ln_b=lp["rwkv_ln_b"].reshape(1, GROUP_W),
        gq=jnp.tile(lp["nsa_norm_q"], 2).reshape(1, LANES),
        gk2=jnp.stack([jnp.concatenate([lp["nsa_norm_k"][1], ones64]),
                       jnp.concatenate([lp["nsa_norm_k"][2], ones64])]),
        cw=cw,
        out_norm=lp["out_norm"].reshape(N_HEADS, GROUP_W),
    )


def _overlap(n_grp, n_sel_pad, n_sel):
    cs = jnp.arange(n_grp)[:, None] * CMP_STRIDE
    js = jnp.arange(n_sel_pad)[None, :] * SEL_BLOCK
    ov = (cs < js + SEL_BLOCK) & (cs + CMP_LEN > js) & (jnp.arange(n_sel_pad)[None, :] < n_sel)
    return ov.astype(BF16)


def _pad_rows_front(x, rows):
    return jnp.pad(x, ((0, 0), (rows - x.shape[1], 0), (0, 0)))


def _round_up(x, m):
    return (x + m - 1) // m * m


def _mixers_recurrent(lw, proj3, cbuf8, h0, sbuf8, shift8, s_bd, j4, tt, chunk, pos0, t_valid):
    ya, yc, hlast, ulast = _lru_sconv(proj3, cbuf8, h0, sbuf8, lw["lru_cw"], lw["lru_cb"], lw["lru_wri"],
                                      lw["lru_bri"], lw["lru_lam"], lw["sconv_w"], tt, pos0)
    yb, s_out = _rwkv2(proj3, shift8, s_bd, lw["mu"], lw["w0"], lw["w2p"], lw["a0"], lw["a2p"], lw["g2"],
                      lw["k_k"], lw["k_a"], lw["r_k"], lw["ln_g"], lw["ln_b"], j4, tt, chunk, t_valid,
                      n_seq=math.gcd(proj3.shape[0], max(1, RWKV_CHAINS * chunk // tt)))
    return ya, yb, yc, hlast, ulast, s_out


def _prompt_layer(lw, layer, x, nb, t, consts):
    m = nb * t
    tm = 512
    j2, j4 = consts["j2"], consts["j4"]
    ffn_w = consts["ffn_w"]
    x1 = _ffn(x, lw["norm_ffn"][0], ffn_w, layer, 0, tm)
    proj, q_r, ncmp, nslc, nwin, gates, slc_bf, win_bf, slc_t, win_t = _proj_prep(
        x1, lw["norm_mix"], consts["w_in"], layer, consts["cos_p"], consts["sin_p"], lw["gq"], lw["gk2"], j2, tm,
        rows_per_table=t, with_bf16=True)
    proj3 = proj.reshape(nb, t, PROJ_KEEP)
    zeros8 = jnp.zeros((nb, SUBLANES, GROUP_W), F32)
    ya, yb, yc, hlast, ulast, s_out = _mixers_recurrent(
        lw, proj3, zeros8, jnp.zeros((nb, 1, GROUP_W), F32), zeros8,
        jnp.zeros((nb, SUBLANES, 4 * GROUP_W), F32), jnp.zeros((nb, N_HEADS, HEAD_D, HEAD_D), F32), j4,
        tt=256, chunk=HEAD_D, pos0=0, t_valid=t)
    kvc, kvc_t = _compress(ncmp.reshape(nb, t, LANES), lw["cw"], consts["cos_cp"], consts["sin_cp"], j2)
    n_cmp = (t - CMP_LEN) // CMP_STRIDE + 1
    n_sel = -(-t // SEL_BLOCK)
    yd = _nsa_attn_t(q_r.reshape(nb, t, GROUP_W), gates.reshape(nb, t, LANES), kvc, kvc_t,
                     slc_bf.reshape(nb, t, LANES), slc_t, win_bf.reshape(nb, t, LANES), win_t,
                     consts["ovl_t_p"], consts["expand_p"], tq=256, tk=512, n_cmp=n_cmp, n_sel=n_sel)
    x3 = _ffn(x1, lw["norm_ffn"][1], ffn_w, layer, 1, tm,
              mix=(ya.reshape(m, GROUP_W), yb.reshape(m, GROUP_W), yc.reshape(m, GROUP_W), yd.reshape(m, GROUP_W),
                   lw["out_norm"], consts["w_out"]))
    xa = proj3[:, :, COL_XA * GROUP_W:(COL_XA + 1) * GROUP_W]
    wlen = min(WINDOW, t)
    states = (hlast[:, SUBLANES - 1], xa[:, t - 3:], s_out, proj3[:, t - 1, :4 * GROUP_W],
              ulast[:, SUBLANES - 2:], nwin.reshape(nb, t, LANES)[:, t - wlen:],
              ncmp.reshape(nb, t // PAGE_SIZE, PAGE_SIZE, LANES), nslc.reshape(nb, t // PAGE_SIZE, PAGE_SIZE, LANES))
    return x3, states


def _sample_layer(lw, layer, x, st, caches, page_table, consts):
    nb = x.shape[0]
    n_pages = page_table.shape[1]
    past_len = n_pages * PAGE_SIZE
    j2, j4 = consts["j2"], consts["j4"]
    lru_h, lru_conv, rwkv_s, rwkv_shift, sconv, win = st
    cache_cmp, cache_slc = caches
    ffn_w = consts["ffn_w"]
    x1 = _ffn(x, lw["norm_ffn"][0], ffn_w, layer, 0, nb)
    proj, q_r, ncmp, nslc, nwin, gates = _proj_prep(
        x1, lw["norm_mix"], consts["w_in"], layer, consts["cos_s"], consts["sin_s"], lw["gq"], lw["gk2"], j2, nb,
        rows_per_table=nb, with_bf16=False)
    proj3 = jnp.pad(proj[:, None, :], ((0, 0), (0, SUBLANES - 1), (0, 0)))
    ya, yb, yc, hlast, ulast, s_out = _mixers_recurrent(
        lw, proj3, _pad_rows_front(lru_conv, SUBLANES), lru_h[:, None, :], _pad_rows_front(sconv, SUBLANES),
        _pad_rows_front(rwkv_shift[:, None, :], SUBLANES), rwkv_s, j4,
        tt=SUBLANES, chunk=SUBLANES, pos0=past_len, t_valid=1)

    def row8(a):
        return jnp.pad(a[:, None, :], ((0, 0), (0, SUBLANES - 1), (0, 0)))

    q4 = jnp.pad(q_r.reshape(nb, N_HEADS, HEAD_D), ((0, 0), (0, SUBLANES - N_HEADS), (0, LANES - HEAD_D)))
    grp_per_page = PAGE_SIZE // CMP_STRIDE
    depth, n_phys = cache_cmp.shape[:2]
    n_cmp = (past_len + 1 - CMP_LEN) // CMP_STRIDE + 1
    n_sel = -(-(past_len + 1) // SEL_BLOCK)
    ocmp, top_idx = _samp_cmp(cache_cmp, layer, page_table, lw["cw"], consts["cos_cs"], consts["sin_cs"], j2, q4,
                              consts["ovl_s"], n_cmp, n_sel, t_pos=past_len,
                              pages_per_step=consts["pages_per_step"])
    wlen = win.shape[1]
    yd8 = _samp_sel(cache_slc.reshape(depth, n_phys, PAGE_SIZE // SEL_BLOCK, SEL_BLOCK, LANES), layer, page_table,
                    top_idx[:, :, 0], q4, row8(nslc), consts["win_all"], row8(nwin), row8(gates), ocmp,
                    t_pos=past_len, win_pos0=past_len - wlen)
    x3 = _ffn(x1, lw["norm_ffn"][1], ffn_w, layer, 1, nb,
              mix=(ya[:, 0], yb[:, 0], yc[:, 0], yd8[:, 0], lw["out_norm"], consts["w_out"]))
    xa = proj[:, COL_XA * GROUP_W:(COL_XA + 1) * GROUP_W]
    new_win = jnp.concatenate([win, nwin[:, None, :]], axis=1)
    states = (hlast[:, 0], jnp.concatenate([lru_conv[:, 1:], xa[:, None, :]], axis=1), s_out,
              proj[:, :4 * GROUP_W], jnp.concatenate([sconv[:, 1:], ulast[:, 0:1]], axis=1),
              new_win[:, -min(WINDOW, wlen + 1):], ncmp[:, None, :], nslc[:, None, :])
    return x3, states


def kernel(x_prompt, x_sample, state_lru_h, state_lru_conv, state_rwkv_S, state_rwkv_shift, state_sconv, state_nsa_win, cache_nsa_cmp, cache_nsa_slc, page_table, norm_ffn, ffn_w_gate, ffn_w_up, ffn_w_down, norm_mix, w_in, lru_conv_w, lru_conv_b, lru_w_r, lru_b_r, lru_w_i, lru_b_i, lru_lambda, rwkv_mu, rwkv_w0, rwkv_w2, rwkv_a0, rwkv_a2, rwkv_g2, rwkv_k_k, rwkv_k_a, rwkv_r_k, rwkv_ln_g, rwkv_ln_b, sconv_w, nsa_norm_q, nsa_norm_k, nsa_cmp_pe, nsa_cmp_w1, nsa_cmp_b1, nsa_cmp_w2, nsa_cmp_b2, out_norm, w_out):
    params = dict(norm_ffn=norm_ffn, norm_mix=norm_mix, lru_conv_w=lru_conv_w, lru_conv_b=lru_conv_b, lru_w_r=lru_w_r,
                  lru_b_r=lru_b_r, lru_w_i=lru_w_i, lru_b_i=lru_b_i, lru_lambda=lru_lambda, rwkv_mu=rwkv_mu,
                  rwkv_w0=rwkv_w0, rwkv_w2=rwkv_w2, rwkv_a0=rwkv_a0, rwkv_a2=rwkv_a2, rwkv_g2=rwkv_g2,
                  rwkv_k_k=rwkv_k_k, rwkv_k_a=rwkv_k_a, rwkv_r_k=rwkv_r_k, rwkv_ln_g=rwkv_ln_g,
                  rwkv_ln_b=rwkv_ln_b, sconv_w=sconv_w, nsa_norm_q=nsa_norm_q, nsa_norm_k=nsa_norm_k,
                  nsa_cmp_pe=nsa_cmp_pe, nsa_cmp_w1=nsa_cmp_w1, nsa_cmp_b1=nsa_cmp_b1, nsa_cmp_w2=nsa_cmp_w2,
                  nsa_cmp_b2=nsa_cmp_b2, out_norm=out_norm)
    depth = norm_mix.shape[0]
    bp, tp, d = x_prompt.shape
    bs, ts, _ = x_sample.shape
    n_pages = page_table.shape[1]
    past_len = n_pages * PAGE_SIZE
    assert d == D_MODEL and ts == 1 and tp % 512 == 0 and tp >= WINDOW + 256 and past_len >= WINDOW
    assert bs % SUBLANES == 0 or bs < SUBLANES

    n_grp_p = tp // CMP_STRIDE
    n_sel_p = -(-tp // SEL_BLOCK)
    nsp_p = _round_up(n_sel_p, LANES)
    n_grp_s = past_len // CMP_STRIDE
    n_sel_s = -(-(past_len + 1) // SEL_BLOCK)
    nsp_s = _round_up(n_sel_s, LANES)
    cos_p, sin_p = _rope_tables(jnp.arange(tp))
    cos_s, sin_s = _rope_tables(jnp.full((bs,), past_len))
    cos_cp, sin_cp = _rope_tables(jnp.arange(n_grp_p) * CMP_STRIDE + CMP_LEN - 1)
    cos_cs, sin_cs = _rope_tables(jnp.arange(n_grp_s) * CMP_STRIDE + CMP_LEN - 1)
    expand_p = ((jnp.arange(tp)[:, None] // SEL_BLOCK) == jnp.arange(nsp_p)[None, :]).astype(BF16)
    pages_per_step = math.gcd(n_pages, 64)
    consts = dict(
        j2=_seg_ones(2), j4=_seg_ones(N_HEADS), cos_p=cos_p, sin_p=sin_p, cos_s=cos_s, sin_s=sin_s,
        cos_cp=cos_cp, sin_cp=sin_cp, cos_cs=cos_cs, sin_cs=sin_cs,
        ovl_t_p=_overlap(n_grp_p, nsp_p, n_sel_p).T,
        expand_p=expand_p,
        ovl_s=_overlap(n_grp_s, nsp_s, n_sel_s),
        pages_per_step=pages_per_step,
        ffn_w=(ffn_w_gate.astype(BF16), ffn_w_up.astype(BF16), ffn_w_down.astype(BF16)),
        w_out=w_out.astype(BF16),
        w_in=_w_in_stack(w_in),
        win_all=state_nsa_win,
    )

    yp = x_prompt.reshape(bp * tp, d)
    ys = x_sample.reshape(bs, d)
    sp_all, ss_all = [], []
    for l in range(depth):
        lw = _layer_weights({name: arr[l] for name, arr in params.items()})
        yp, sp = _prompt_layer(lw, l, yp, bp, tp, consts)
        ys, ss = _sample_layer(lw, l, ys, (state_lru_h[l], state_lru_conv[l], state_rwkv_S[l], state_rwkv_shift[l],
                                           state_sconv[l], state_nsa_win[l]),
                               (cache_nsa_cmp, cache_nsa_slc), page_table, consts)
        sp_all.append(sp)
        ss_all.append(ss)
    outs = [yp.reshape(bp, tp, d), ys.reshape(bs, ts, d)]
    for i in range(8):
        outs.append(jnp.stack([s[i] for s in sp_all]))
        outs.append(jnp.stack([s[i] for s in ss_all]))
    return tuple(outs)
```

```python
import functools
import math

import jax
import jax.numpy as jnp
from jax import lax
from jax.experimental import pallas as pl
from jax.experimental.pallas import tpu as pltpu

F32 = jnp.float32
BF16 = jnp.bfloat16

D_MODEL = 1024
GROUP_W = 256
N_HEADS = 4
HEAD_D = 64
LRU_C = 8.0
RWKV_LN_EPS = 64e-5
CMP_LEN = 32
CMP_STRIDE = 16
SEL_BLOCK = 64
N_SELECT = 16
WINDOW = 512
FORCE_BONUS = 1e4
NEG_INF = -1e30
ROPE_THETA = 10000.0
EPS = 1e-6
PAGE_SIZE = 128
PROJ_PAD = 3072
PROJ_KEEP = 2304
ATT_SCALE = HEAD_D ** -0.5
RWKV_CHAINS = 8

SUBLANES = 8
LANES = 128
VMEM_LIMIT = 56 * 1024 * 1024

COL_XA, COL_GA, COL_BG, COL_CG, COL_XIN = 4, 5, 6, 7, 8


def _params(*sem):
    return pltpu.CompilerParams(dimension_semantics=sem, vmem_limit_bytes=VMEM_LIMIT)


def _const_spec(shape):
    nd = len(shape)
    return pl.BlockSpec(shape, lambda *_: (0,) * nd)


def _dot(a, b):
    return jnp.dot(a, b, preferred_element_type=F32)


def _dot_nt(a, b):
    return lax.dot_general(a, b, (((1,), (1,)), ((), ())), preferred_element_type=F32)


def _dot_tn(a, b):
    return lax.dot_general(a, b, (((0,), (0,)), ((), ())), preferred_element_type=F32)


def _split3(x):
    h1 = x.astype(BF16)
    r1 = x - h1.astype(F32)
    h2 = r1.astype(BF16)
    h3 = (r1 - h2.astype(F32)).astype(BF16)
    return h1, h2, h3


def _dot_exact_rhs(x, m_bf16):
    h1, h2, h3 = _split3(x)
    return _dot(h1, m_bf16) + _dot(h2, m_bf16) + _dot(h3, m_bf16)


def _gelu_tanh(x):
    return x * (0.5 * (1.0 + jnp.tanh(math.sqrt(2.0 / math.pi) * (x + 0.044715 * (x * x * x)))))


def _sigmoid(x):
    return 1.0 / (1.0 + jnp.exp(-x))


def _softplus(x):
    return jnp.maximum(x, 0.0) + jnp.log1p(jnp.exp(-jnp.abs(x)))


def _rmsnorm_rows(x, g):
    ms = jnp.mean(x * x, axis=-1, keepdims=True)
    return x * lax.rsqrt(ms + EPS) * g


def _masked_softmax_rows(s, mask):
    s = jnp.where(mask, s, NEG_INF)
    m = jnp.max(s, axis=-1, keepdims=True)
    e = jnp.exp(s - m)
    norm = 1.0 / jnp.maximum(jnp.sum(e, axis=-1, keepdims=True), 1e-30)
    return e * jnp.where(m > 0.5 * NEG_INF, norm, 0.0)


def _rope_lanes(x, cos, sin_signed, lane):
    swapped = jnp.where((lane & (HEAD_D - 1)) < HEAD_D // 2,
                        pltpu.roll(x, LANES - HEAD_D // 2, 1), pltpu.roll(x, HEAD_D // 2, 1))
    return x * cos + swapped * sin_signed


def _ffn_body(*refs, f_chunk, n_mix):
    x_ref = refs[0]
    y_refs = refs[1:1 + n_mix]
    if n_mix:
        gm_ref, wo_ref = refs[1 + n_mix:3 + n_mix]
    g_ref, wg_ref, wu_ref, wd_ref, o_ref, act_ref = refs[1 + n_mix + (2 if n_mix else 0):]
    x = x_ref[...]
    for gi, y_ref in enumerate(y_refs):
        yn = _rmsnorm_rows(y_ref[...], gm_ref[gi:gi + 1, :]).astype(BF16)
        x = x + _dot(yn, wo_ref[gi * GROUP_W:(gi + 1) * GROUP_W, :])
    h = _rmsnorm_rows(x, g_ref[...]).astype(BF16)
    d_ff = wg_ref.shape[1]
    for c in range(d_ff // f_chunk):
        sl = slice(c * f_chunk, (c + 1) * f_chunk)
        gate = _dot(h, wg_ref[:, sl])
        up = _dot(h, wu_ref[:, sl])
        act_ref[:, sl] = (gate * _sigmoid(gate) * up).astype(BF16)
    o_ref[...] = x + 0.5 * _dot(act_ref[...], wd_ref[...])


def _ffn(x, g, ffn_w, layer, which, tm, mix=None):
    m, d = x.shape
    wg, wu, wd = ffn_w
    d_ff = wg.shape[3]
    once = pl.Buffered(1)

    def wspec(a, b):
        return pl.BlockSpec((None, None, a, b), lambda i: (layer, which, 0, 0), pipeline_mode=once)

    mix_specs, mix_args = [], []
    if mix is not None:
        yspec = pl.BlockSpec((tm, GROUP_W), lambda i: (i, 0))
        mix_specs = [yspec] * 4 + [_const_spec((N_HEADS, GROUP_W)),
                                   pl.BlockSpec((None, N_HEADS * GROUP_W, d), lambda i: (layer, 0, 0),
                                                pipeline_mode=once)]
        mix_args = list(mix)
    return pl.pallas_call(
        functools.partial(_ffn_body, f_chunk=256, n_mix=4 if mix is not None else 0),
        grid=(m // tm,),
        in_specs=[pl.BlockSpec((tm, d), lambda i: (i, 0))] + mix_specs
        + [_const_spec((1, d)), wspec(d, d_ff), wspec(d, d_ff), wspec(d_ff, d)],
        out_specs=pl.BlockSpec((tm, d), lambda i: (i, 0)),
        out_shape=jax.ShapeDtypeStruct((m, d), F32),
        scratch_shapes=[pltpu.VMEM((tm, d_ff), BF16)],
        compiler_params=_params("parallel"),
        name="ffn_mix" if mix is not None else "ffn",
    )(x, *mix_args, g, wg, wu, wd)


def _lru_sconv_body(xa_ref, ga_ref, bg_ref, cg_ref, xin_ref, cbuf_ref, h0_ref, sbuf_ref,
                    cw_ref, cb_ref, wri_ref, bri_ref, lam_ref, sw_ref,
                    ya_ref, yc_ref, hlast_ref, ulast_ref,
                    extx_ref, extu_ref, h_ref, *, tt, pos0):
    t = pl.program_id(1)
    nt = pl.num_programs(1)

    @pl.when(t == 0)
    def _():
        extx_ref[0:SUBLANES, :] = cbuf_ref[...]
        extu_ref[0:SUBLANES, :] = sbuf_ref[...]
        h_ref[...] = h0_ref[...]

    x = xa_ref[...]
    extx_ref[SUBLANES:SUBLANES + tt, :] = x
    cw = cw_ref[...]
    xc = extx_ref[pl.ds(SUBLANES - 3, tt), :] * cw[0:1]
    xc = xc + extx_ref[pl.ds(SUBLANES - 2, tt), :] * cw[1:2]
    xc = xc + extx_ref[pl.ds(SUBLANES - 1, tt), :] * cw[2:3]
    xc = xc + x * cw[3:4]
    xc = xc + cb_ref[...]
    extx_ref[0:SUBLANES, :] = x[tt - SUBLANES:tt]

    gates = _dot(xc.astype(BF16), wri_ref[...]) + bri_ref[...]
    r = _sigmoid(gates[:, :GROUP_W])
    ig = _sigmoid(gates[:, GROUP_W:])
    log_a = -LRU_C * r * _softplus(-lam_ref[...])
    a = jnp.exp(log_a)
    rows = lax.broadcasted_iota(jnp.int32, (tt, GROUP_W), 0)
    mult = jnp.where(rows + (pos0 + t * tt) == 0, 1.0, jnp.sqrt(1.0 - jnp.exp(2.0 * log_a)))
    b = mult * ig * xc

    s = 1
    while s < tt:
        keep = rows >= s
        a_sh = jnp.where(keep, pltpu.roll(a, s, 0), 1.0)
        b_sh = jnp.where(keep, pltpu.roll(b, s, 0), 0.0)
        b = a * b_sh + b
        a = a * a_sh
        s *= 2
    h = a * h_ref[...] + b
    h_ref[...] = h[tt - 1:tt]
    ya_ref[...] = _gelu_tanh(ga_ref[...]) * h

    u = cg_ref[...] * xin_ref[...]
    extu_ref[SUBLANES:SUBLANES + tt, :] = u
    sw = sw_ref[...]
    yv = extu_ref[pl.ds(SUBLANES - 2, tt), :] * sw[0:1]
    yv = yv + extu_ref[pl.ds(SUBLANES - 1, tt), :] * sw[1:2]
    yv = yv + u * sw[2:3]
    yc_ref[...] = bg_ref[...] * yv
    extu_ref[0:SUBLANES, :] = u[tt - SUBLANES:tt]

    @pl.when(t == nt - 1)
    def _():
        hlast_ref[...] = h[tt - SUBLANES:tt]
        ulast_ref[...] = u[tt - SUBLANES:tt]


def _lru_sconv(proj3, cbuf8, h0, sbuf8, cw, cb, wri, bri, lam, sw, tt, pos0):
    nb, tp, _ = proj3.shape
    w = GROUP_W

    def col(c):
        return pl.BlockSpec((None, tt, w), lambda b, t, c=c: (b, t, c))

    state8 = pl.BlockSpec((None, SUBLANES, w), lambda b, t: (b, 0, 0))
    return pl.pallas_call(
        functools.partial(_lru_sconv_body, tt=tt, pos0=pos0),
        grid=(nb, tp // tt),
        in_specs=[col(COL_XA), col(COL_GA), col(COL_BG), col(COL_CG), col(COL_XIN),
                  state8, pl.BlockSpec((None, 1, w), lambda b, t: (b, 0, 0)), state8,
                  _const_spec((4, w)), _const_spec((1, w)), _const_spec((w, 2 * w)), _const_spec((1, 2 * w)),
                  _const_spec((1, w)), _const_spec((3, w))],
        out_specs=[pl.BlockSpec((None, tt, w), lambda b, t: (b, t, 0)),
                   pl.BlockSpec((None, tt, w), lambda b, t: (b, t, 0)), state8, state8],
        out_shape=[jax.ShapeDtypeStruct((nb, tp, w), F32), jax.ShapeDtypeStruct((nb, tp, w), F32),
                   jax.ShapeDtypeStruct((nb, SUBLANES, w), F32), jax.ShapeDtypeStruct((nb, SUBLANES, w), F32)],
        scratch_shapes=[pltpu.VMEM((tt + SUBLANES, w), F32), pltpu.VMEM((tt + SUBLANES, w), F32),
                        pltpu.VMEM((1, w), F32)],
        compiler_params=_params("parallel", "arbitrary"),
        name="lru_sconv",
    )(proj3, proj3, proj3, proj3, proj3, cbuf8, h0, sbuf8, cw, cb, wri, bri, lam, sw)


def _rwkv_tokens(t, p_ref, mu_ref, w0_ref, w2_ref, a0_ref, a2_ref, g2_ref, kk_ref, ka_ref, rk_ref, j4,
                 ext_ref, w_buf, r_buf, k_buf, v_buf, a_buf, b_buf, g_buf, bon_buf, *, tt, chunk, t_valid):
    w = GROUP_W
    p = p_ref[...]
    ext_ref[SUBLANES:SUBLANES + tt, :] = p
    m = p + (ext_ref[pl.ds(SUBLANES - 1, tt), :] - p) * mu_ref[...]
    ext_ref[0:SUBLANES, :] = p[tt - SUBLANES:tt]
    r = m[:, 0:w]
    k = m[:, w:2 * w]
    v = m[:, 2 * w:3 * w]
    wa = m[:, 3 * w:3 * w + LANES]
    gl = m[:, 3 * w + LANES:4 * w]
    wlin = w0_ref[...] + _dot(jnp.tanh(wa).astype(BF16), w2_ref[...])
    log_decay = -jnp.exp(-_softplus(-wlin) - 0.5)
    ag = _sigmoid(a0_ref[...] + _dot(wa.astype(BF16), a2_ref[...]))
    g_buf[...] = _dot(_sigmoid(gl).astype(BF16), g2_ref[...])
    kk = k * kk_ref[...]
    kk = kk * lax.rsqrt(jnp.maximum(_dot_exact_rhs(kk * kk, j4), 1e-24))
    kf = k * (1.0 + (ag - 1.0) * ka_ref[...])
    rows = lax.broadcasted_iota(jnp.int32, (tt, w), 0)
    if t_valid % tt != 0:
        live = rows + t * tt < t_valid
        log_decay = jnp.where(live, log_decay, 0.0)
        kk = jnp.where(live, kk, 0.0)
        kf = jnp.where(live, kf, 0.0)
        v = jnp.where(live, v, 0.0)
    bon_buf[...] = _dot_exact_rhs(r * kf * rk_ref[...], j4) * v

    rin = rows & (chunk - 1)
    cl = log_decay
    s = 1
    while s < chunk:
        cl = cl + jnp.where(rin >= s, pltpu.roll(cl, s, 0), 0.0)
        s *= 2
    e_neg = jnp.exp(-cl)
    w_buf[...] = cl
    r_buf[...] = r * jnp.exp(cl)
    a_buf[...] = -kk * jnp.exp(cl - log_decay)
    b_buf[...] = kk * ag * e_neg
    k_buf[...] = kf * e_neg
    v_buf[...] = v


def _rwkv_body(p_ref, shift_ref, sin_ref, mu_ref, w0_ref, w2_ref, a0_ref, a2_ref, g2_ref, kk_ref, ka_ref,
                rk_ref, lng_ref, lnb_ref, j4_ref, y_ref, sout_ref,
                ext_ref, s_ref, w_buf, r_buf, k_buf, v_buf, a_buf, b_buf, g_buf, bon_buf, y_buf,
                *, n_seq, tt, chunk, t_valid):
    t = pl.program_id(1)
    w = GROUP_W
    nh = N_HEADS
    nc = tt // chunk
    j4 = j4_ref[...]

    heads = [slice(h * HEAD_D, (h + 1) * HEAD_D) for h in range(nh)]

    @pl.when(t == 0)
    def _():
        for i in range(n_seq):
            ext_ref[i, 0:SUBLANES, :] = shift_ref[i]
            s_ref[i] = jnp.zeros((w, w), F32)
            for hs in heads:
                s_ref[i, hs, hs] = sin_ref[i, hs, :]

    for i in range(n_seq):
        _rwkv_tokens(t, p_ref.at[i], mu_ref, w0_ref, w2_ref, a0_ref, a2_ref, g2_ref, kk_ref, ka_ref, rk_ref, j4,
                     ext_ref.at[i], w_buf.at[i], r_buf.at[i], k_buf.at[i], v_buf.at[i], a_buf.at[i], b_buf.at[i],
                     g_buf.at[i], bon_buf.at[i], tt=tt, chunk=chunk, t_valid=t_valid)

    sl = nh * chunk
    hm_rows = lax.broadcasted_iota(jnp.int32, (sl, w), 0) // chunk
    hm_cols = lax.broadcasted_iota(jnp.int32, (sl, w), 1) // HEAD_D
    head_mask = (hm_rows == hm_cols).astype(F32)
    ri = lax.broadcasted_iota(jnp.int32, (sl, sl), 0)
    ci = lax.broadcasted_iota(jnp.int32, (sl, sl), 1)
    same = (ri // chunk) == (ci // chunk)
    strict = (same & ((ri & (chunk - 1)) > (ci & (chunk - 1)))).astype(F32)
    incl = (same & ((ri & (chunk - 1)) >= (ci & (chunk - 1)))).astype(F32)
    eye = (ri == ci).astype(F32)

    def stacked(buf, i, c):
        return jnp.concatenate([buf[i, pl.ds(c * chunk, chunk), :]] * nh, axis=0) * head_mask

    chains = [(i, c) for c in range(nc) for i in range(n_seq)]
    a_s = [stacked(a_buf, i, c).astype(BF16) for i, c in chains]
    r_s = [stacked(r_buf, i, c).astype(BF16) for i, c in chains]
    b_s = [stacked(b_buf, i, c).astype(BF16) for i, c in chains]
    k_s = [stacked(k_buf, i, c).astype(BF16) for i, c in chains]
    v_s = [stacked(v_buf, i, c).astype(BF16) for i, c in chains]
    n_mat = [_dot_nt(a, b) * strict for a, b in zip(a_s, b_s)]
    m_mat = [(_dot_nt(a, k) * strict).astype(BF16) for a, k in zip(a_s, k_s)]
    p_mat = [(_dot_nt(r, b) * incl).astype(BF16) for r, b in zip(r_s, b_s)]
    q_mat = [(_dot_nt(r, k) * incl).astype(BF16) for r, k in zip(r_s, k_s)]
    t_mat = [eye + n for n in n_mat]
    x = n_mat
    step = 2
    while step < chunk:
        xb = [xi.astype(BF16) for xi in x]
        x = [_dot(b, b) for b in xb]
        t_mat = [tm + _dot(tm.astype(BF16), xi.astype(BF16)) for tm, xi in zip(t_mat, x)]
        step *= 2
    t_b = [tm.astype(BF16) for tm in t_mat]
    w_eff = [_dot(tb, a).astype(BF16) for tb, a in zip(t_b, a_s)]
    mv = [_dot(mm, v).astype(BF16) for mm, v in zip(m_mat, v_s)]
    z = [_dot(tb, x_) for tb, x_ in zip(t_b, mv)]
    qv = [_dot(qm, v) for qm, v in zip(q_mat, v_s)]

    for n, (i, c) in enumerate(chains):
        s0 = s_ref[i]
        s0b = s0.astype(BF16)
        ub = (_dot_nt(w_eff[n], s0b) + z[n]).astype(BF16)
        y_s = _dot_nt(r_s[n], s0b) + _dot(p_mat[n], ub) + qv[n]
        yc = y_s[0:chunk]
        for hh in range(1, nh):
            yc = yc + y_s[hh * chunk:(hh + 1) * chunk]
        y_buf[i, pl.ds(c * chunk, chunk), :] = yc
        c_last = jnp.exp(w_buf[i, pl.ds((c + 1) * chunk - 1, 1), :])
        s_ref[i] = (s0 + _dot_tn(ub, b_s[n]) + _dot_tn(v_s[n], k_s[n])) * c_last

    inv_hd = 1.0 / HEAD_D
    for i in range(n_seq):
        y = y_buf[i]
        mean = _dot_exact_rhs(y, j4) * inv_hd
        yc = y - mean
        var = _dot_exact_rhs(yc * yc, j4) * inv_hd
        yn = yc * lax.rsqrt(var + RWKV_LN_EPS) * lng_ref[...] + lnb_ref[...]
        y_ref[i] = (yn + bon_buf[i]) * g_buf[i]
        for hs in heads:
            sout_ref[i, hs, :] = s_ref[i, hs, hs]


def _rwkv(proj3, shift8, s_heads, mu, w0, w2p, a0, a2p, g2, kk, ka, rk, lng, lnb, j4, tt, chunk, t_valid, n_seq):
    nb, tp, _ = proj3.shape
    w = GROUP_W
    pw = 4 * w
    vec = _const_spec((1, w))
    s_rows = s_heads.reshape(nb, w, HEAD_D)
    state_spec = pl.BlockSpec((n_seq, w, HEAD_D), lambda b, t: (b, 0, 0))
    yb, s_out = pl.pallas_call(
        functools.partial(_rwkv_body, n_seq=n_seq, tt=tt, chunk=chunk, t_valid=t_valid),
        grid=(nb // n_seq, tp // tt),
        in_specs=[pl.BlockSpec((n_seq, tt, pw), lambda b, t: (b, t, 0)),
                  pl.BlockSpec((n_seq, SUBLANES, pw), lambda b, t: (b, 0, 0)), state_spec,
                  _const_spec((1, pw)), vec, _const_spec((LANES, w)), vec, _const_spec((LANES, w)),
                  _const_spec((LANES, w)), vec, vec, vec, vec, vec, _const_spec((w, w))],
        out_specs=[pl.BlockSpec((n_seq, tt, w), lambda b, t: (b, t, 0)), state_spec],
        out_shape=[jax.ShapeDtypeStruct((nb, tp, w), F32), jax.ShapeDtypeStruct(s_rows.shape, F32)],
        scratch_shapes=[pltpu.VMEM((n_seq, tt + SUBLANES, pw), F32), pltpu.VMEM((n_seq, w, w), F32)]
        + [pltpu.VMEM((n_seq, tt, w), F32)] * 9,
        compiler_params=_params("parallel", "arbitrary"),
        name="rwkv7",
    )(proj3, shift8, s_rows, mu, w0, w2p, a0, a2p, g2, kk, ka, rk, lng, lnb, j4)
    return yb, s_out.reshape(s_heads.shape)


def _proj_prep_body(x_ref, g_ref, w_ref, cos_ref, sin_ref, gq_ref, gk_ref, j2_ref,
                    o_ref, qo_ref, cmpo_ref, slco_ref, wino_ref, gate_ref, *bf_refs, n_chunk):
    h =_rmsnorm_rows(x_ref[...], g_ref[...]).astype(BF16)
    n_keep = o_ref.shape[1]
    for c in range(n_keep // n_chunk):
        sl = slice(c * n_chunk, (c + 1) * n_chunk)
        o_ref[:, sl] = _dot(h, w_ref[:, sl])
    nsa = _dot(h, w_ref[:, n_keep:])

    cos = cos_ref[...]
    sin = sin_ref[...]
    j2 = j2_ref[...]
    lane = lax.broadcasted_iota(jnp.int32, cos.shape, 1)
    inv_hd = 1.0 / HEAD_D
    gq = gq_ref[...]
    halves = []
    for c in range(2):
        x = nsa[:, c * LANES:(c + 1) * LANES]
        ms = _dot_exact_rhs(x * x, j2) * inv_hd
        halves.append(_rope_lanes(x * lax.rsqrt(ms + EPS) * gq, cos, sin, lane))
    qo_ref[...] = jnp.concatenate(halves, axis=1)
    cmpo_ref[...] = nsa[:, 2 * LANES:3 * LANES]
    gk = gk_ref[...]
    is_key = lane < HEAD_D
    for dst, row in ((slco_ref, 0), (wino_ref, 1)):
        x = nsa[:, (3 + row) * LANES:(4 + row) * LANES]
        ms = _dot_exact_rhs(x * x, j2) * inv_hd
        roped = _rope_lanes(x * lax.rsqrt(ms + EPS) * gk[row:row + 1], cos, sin, lane)
        kv = jnp.where(is_key, roped, x)
        dst[...] = kv
        if bf_refs:
            bf_refs[row][...] = kv.astype(BF16)
            kv_t = jnp.where(lax.broadcasted_iota(jnp.int32, (LANES, kv.shape[0]), 0) == 0, 1.0, kv.T)
            bf_refs[2 + row][...] = kv_t.astype(BF16)
    gate_ref[...] = _sigmoid(nsa[:, 5 * LANES:6 * LANES])


def _proj_prep(x, g, w, layer, cos, sin, gq, gk2, j2, tm, rows_per_table, with_bf16):
    m, d = x.shape
    n = w.shape[2]
    nt_tab = rows_per_table // tm
    o128t = pl.BlockSpec((LANES, tm), lambda i: (0, i))
    bf_specs = [pl.BlockSpec((tm, LANES), lambda i: (i, 0))] * 2 + [o128t] * 2 if with_bf16 else []
    bf_shapes = ([jax.ShapeDtypeStruct((m, LANES), BF16)] * 2 + [jax.ShapeDtypeStruct((LANES, m), BF16)] * 2
                 if with_bf16 else [])
    tab = pl.BlockSpec((tm, LANES), lambda i: (i % nt_tab, 0))
    o128 = pl.BlockSpec((tm, LANES), lambda i: (i, 0))
    return pl.pallas_call(
        functools.partial(_proj_prep_body, n_chunk=256),
        grid=(m // tm,),
        in_specs=[pl.BlockSpec((tm, d), lambda i: (i, 0)), _const_spec((1, d)),
                  pl.BlockSpec((None, d, n), lambda i: (layer, 0, 0), pipeline_mode=pl.Buffered(1)),
                  tab, tab, _const_spec((1, LANES)), _const_spec((2, LANES)), _const_spec((LANES, LANES))],
        out_specs=[pl.BlockSpec((tm, PROJ_KEEP), lambda i: (i, 0)),
                   pl.BlockSpec((tm, GROUP_W), lambda i: (i, 0)), o128, o128, o128, o128] + bf_specs,
        out_shape=[jax.ShapeDtypeStruct((m, PROJ_KEEP), F32), jax.ShapeDtypeStruct((m, GROUP_W), F32)]
        + [jax.ShapeDtypeStruct((m, LANES), F32)] * 4 + bf_shapes,
        compiler_params=_params("parallel"),
        name="proj_prep",
    )(x, g, w, cos, sin, gq, gk2, j2)


def _compress_tail(hid_lo, hid_hi_next, b1_ref, w2_ref, b2_ref, gk_ref, cos_ref, sin_ref, j2_ref):
    hidden = _gelu_tanh(hid_lo + hid_hi_next + b1_ref[...])
    kv = _dot(hidden.astype(BF16), w2_ref[...]) + b2_ref[...]
    lane = lax.broadcasted_iota(jnp.int32, kv.shape, 1)
    ms = _dot_exact_rhs(kv * kv, j2_ref[...]) * (1.0 / HEAD_D)
    roped = _rope_lanes(kv * lax.rsqrt(ms + EPS) * gk_ref[...], cos_ref[...], sin_ref[...], lane)
    return jnp.where(lane < HEAD_D, roped, kv)


def _compress_hidden(read_rows, pelo_ref, pehi_ref, wlo_ref, whi_ref):
    pair = 2 * LANES
    lo = hi = None
    for p in range(CMP_STRIDE // 2):
        x = jnp.concatenate([read_rows(2 * p), read_rows(2 * p + 1)], axis=1)
        cs = slice(p * pair, (p + 1) * pair)
        d_lo = _dot((x + pelo_ref[:, cs]).astype(BF16), wlo_ref[cs, :])
        d_hi = _dot((x + pehi_ref[:, cs]).astype(BF16), whi_ref[cs, :])
        lo = d_lo if lo is None else lo + d_lo
        hi = d_hi if hi is None else hi + d_hi
    return lo, hi


def _compress_body(x_ref, pelo_ref, pehi_ref, wlo_ref, whi_ref, b1_ref, w2_ref, b2_ref, gk_ref,
                   cos_ref, sin_ref, j2_ref, o_ref, ot_ref):
    n_grp = x_ref.shape[0] // CMP_STRIDE
    lo, hi = _compress_hidden(lambda j: x_ref[pl.ds(j, n_grp, stride=CMP_STRIDE), :],
                              pelo_ref, pehi_ref, wlo_ref, whi_ref)
    hi_next = pltpu.roll(hi, n_grp - 1, 0)
    kv = _compress_tail(lo, hi_next, b1_ref, w2_ref, b2_ref, gk_ref, cos_ref, sin_ref, j2_ref)
    o_ref[...] = kv.astype(BF16)
    ot_ref[...] = kv.T.astype(BF16)


def _compress(rows, cw, cos, sin, j2):
    nb, t, _ = rows.shape
    n_grp = t // CMP_STRIDE
    wid = CMP_STRIDE * LANES
    return pl.pallas_call(
        _compress_body,
        grid=(nb,),
        in_specs=[pl.BlockSpec((None, t, LANES), lambda b: (b, 0, 0)),
                  _const_spec((1, wid)), _const_spec((1, wid)),
                  _const_spec((wid, GROUP_W)), _const_spec((wid, GROUP_W)), _const_spec((1, GROUP_W)),
                  _const_spec((GROUP_W, LANES)), _const_spec((1, LANES)), _const_spec((1, LANES)),
                  _const_spec((n_grp, LANES)), _const_spec((n_grp, LANES)), _const_spec((LANES, LANES))],
        out_specs=[pl.BlockSpec((None, n_grp, LANES), lambda b: (b, 0, 0)),
                   pl.BlockSpec((LANES, n_grp), lambda b: (0, b))],
        out_shape=[jax.ShapeDtypeStruct((nb, n_grp, LANES), BF16),
                   jax.ShapeDtypeStruct((LANES, nb * n_grp), BF16)],
        compiler_params=_params("parallel"),
        name="nsa_compress",
    )(rows, cw["pe_lo"], cw["pe_hi"], cw["w_lo"], cw["w_hi"], cw["b1"], cw["w2"], cw["b2"], cw["gk"], cos, sin, j2)


def _stack_heads(q, lane):
    parts = []
    for h in range(N_HEADS):
        blk = q[:, (h // 2) * LANES:(h // 2 + 1) * LANES]
        if h % 2 == 1:
            blk = pltpu.roll(blk, HEAD_D, 1)
        parts.append(jnp.where(lane < HEAD_D, blk, 0.0))
    return jnp.concatenate(parts, axis=0)


def _unstack_heads(parts, lane):
    b01 = jnp.where(lane < HEAD_D, pltpu.roll(parts[0], HEAD_D, 1), parts[1])
    b23 = jnp.where(lane < HEAD_D, pltpu.roll(parts[2], HEAD_D, 1), parts[3])
    return jnp.concatenate([b01, b23], axis=1)


def _softmax_cols(st, mask):
    st = jnp.where(mask, st, NEG_INF)
    m = jnp.max(st, axis=0, keepdims=True)
    e = jnp.exp(st - m)
    norm = 1.0 / jnp.maximum(jnp.sum(e, axis=0, keepdims=True), 1e-30)
    return e * jnp.where(m > 0.5 * NEG_INF, norm, 0.0)


def _nsa_attn_t_body(q_ref, gate_ref, kvc_ref, kvct_ref, slc_ref, slct_ref, win_ref, wint_ref, ovl_ref, exp_ref,
                     o_ref, acc_ref, st_ref, e_ref, *, tq, tk, n_cmp, n_sel, n_top, win_span):
    i = pl.program_id(1)
    start = i * tq
    cols = N_HEADS * tq
    lane = lax.broadcasted_iota(jnp.int32, (tq, LANES), 1)
    q4 = _stack_heads(q_ref[...] * ATT_SCALE, lane).astype(BF16)
    tlane = start + (lax.broadcasted_iota(jnp.int32, (1, cols), 1) & (tq - 1))

    kvc = kvc_ref[...]
    ng = kvc.shape[0]
    ncol = lax.broadcasted_iota(jnp.int32, (ng, 1), 0)
    cend = jnp.where(ncol < n_cmp, ncol * CMP_STRIDE + (CMP_LEN - 1), jnp.iinfo(jnp.int32).max)
    pt = _softmax_cols(_dot_nt(kvc, q4), cend <= tlane)
    o_cmp = _dot(kvct_ref[...], pt.astype(BF16))
    psum = pt[:, 0:tq]
    for h in range(1, N_HEADS):
        psum = psum + pt[:, h * tq:(h + 1) * tq]
    ph = psum.astype(BF16)
    pl_ = (psum - ph.astype(F32)).astype(BF16)
    ovl = ovl_ref[...]
    imp_t = _dot(ovl, ph) + _dot(ovl, pl_)

    start0 = pl.multiple_of(jnp.maximum(start + tq - win_span, 0), tq)
    rel = tlane - (start0 + lax.broadcasted_iota(jnp.int32, (win_span, 1), 0))
    sw = jnp.where((rel >= 0) & (rel < WINDOW), _dot_nt(win_ref[pl.ds(start0, win_span), :], q4), NEG_INF)
    mw = jnp.max(sw, axis=0, keepdims=True)
    o_win = _dot(wint_ref[:, pl.ds(start0, win_span)], jnp.exp(sw - mw).astype(BF16))
    o_win = o_win * jnp.where(mw > 0.5 * NEG_INF, 1.0 / jnp.maximum(o_win[0:1], 1e-30), 0.0)

    nsp = imp_t.shape[0]
    jblk = lax.broadcasted_iota(jnp.int32, (nsp, tq), 0)
    tcol = start + lax.broadcasted_iota(jnp.int32, (nsp, tq), 1)
    cur = tcol // SEL_BLOCK
    forced = (jblk == 0) | (jblk == cur) | (jblk == cur - 1)
    valid = (jblk * SEL_BLOCK <= tcol) & (jblk < n_sel)
    score = jnp.where(valid, imp_t + jnp.where(forced, FORCE_BONUS, 0.0), -1.0)
    taken = -3.0
    jf = jblk.astype(F32)
    left = score
    for _ in range(n_top):
        best = jnp.max(left, axis=0, keepdims=True)
        first = jnp.min(jnp.where(left == best, jf, float(nsp)), axis=0, keepdims=True)
        left = jnp.where(jf == first, taken, left)
    bias_t = jnp.where((left == taken) & (score >= 0.0), 0.0, NEG_INF)
    bias = bias_t.T.astype(BF16)
    lhs = jnp.concatenate([q4, jnp.concatenate([bias] * N_HEADS, axis=0)], axis=1)

    acc_ref[...] = jnp.zeros((LANES, cols), F32)
    n_kt = (start + tq + tk - 1) // tk

    def stage_scores(kt, causal):
        off = pl.multiple_of(kt * tk, tk)
        k_aug = jnp.concatenate([slc_ref[pl.ds(off, tk), :], exp_ref[pl.ds(off, tk), :]], axis=1)
        st = _dot_nt(k_aug, lhs)
        if causal:
            kpos = off + lax.broadcasted_iota(jnp.int32, (tk, 1), 0)
            st = jnp.where(kpos <= tlane, st, NEG_INF)
        return st, jnp.max(st.reshape(tk // SUBLANES, SUBLANES, cols), axis=0)

    def stage_exp(st, mt, m_prev):
        m_new = jnp.maximum(m_prev, jnp.max(mt, axis=0, keepdims=True))
        m_sub = jnp.where(m_new > 0.5 * NEG_INF, m_new, 0.0)
        e = jnp.exp(st - m_sub[0:1])
        return e.astype(BF16), jnp.exp(m_prev - m_new), m_new

    def stage_values(kt, e, alpha):
        off = pl.multiple_of(jnp.maximum(kt, 0) * tk, tk)
        acc_ref[...] = alpha[0:1] * acc_ref[...] + _dot(slct_ref[:, pl.ds(off, tk)], e)

    def trip(i, carry, causal):
        mt, alpha, m_run = carry
        stage_values(i - 2, e_ref[...], alpha)
        e, alpha, m_run = stage_exp(st_ref[...], mt, m_run)
        e_ref[...] = e
        st, mt = stage_scores(i, causal)
        st_ref[...] = st
        return mt, alpha, m_run

    st_ref[...] = jnp.full((tk, cols), NEG_INF, F32)
    e_ref[...] = jnp.zeros((tk, cols), BF16)
    neg8 = st_ref[0:SUBLANES, :]
    carry = (neg8, acc_ref[0:SUBLANES, :] + 1.0, neg8)
    carry = lax.fori_loop(0, n_kt - 1, functools.partial(trip, causal=False), carry)
    mt, alpha, m_run = trip(n_kt - 1, carry, True)
    stage_values(n_kt - 2, e_ref[...], alpha)
    e, alpha, _ = stage_exp(st_ref[...], mt, m_run)
    stage_values(n_kt - 1, e, alpha)
    acc = acc_ref[...]
    o_sel = acc / jnp.maximum(acc[0:1], 1e-30)

    gt = gate_ref[...].T
    outs = []
    for h in range(N_HEADS):
        cs = slice(h * tq, (h + 1) * tq)
        outs.append(gt[3 * h:3 * h + 1] * o_cmp[HEAD_D:, cs] + gt[3 * h + 1:3 * h + 2] * o_sel[HEAD_D:, cs]
                    + gt[3 * h + 2:3 * h + 3] * o_win[HEAD_D:, cs])
    o_ref[...] = jnp.concatenate(outs, axis=0).T


def _nsa_attn_t(q, gates, kvc, kvc_t, slc, slc_t, win, win_t, ovl_t, expand, tq, tk, n_cmp, n_sel):
    nb, t, _ = q.shape
    ng = kvc.shape[1]
    nsp = ovl_t.shape[0]
    cols = N_HEADS * tq
    win_span = WINDOW + tq
    per_b = lambda b, i: (b, 0, 0)
    per_b_t = lambda b, i: (0, b)
    return pl.pallas_call(
        functools.partial(_nsa_attn_t_body, tq=tq, tk=tk, n_cmp=n_cmp, n_sel=n_sel,
                          n_top=min(N_SELECT, n_sel), win_span=win_span),
        grid=(nb, t // tq),
        in_specs=[pl.BlockSpec((None, tq, GROUP_W), lambda b, i: (b, i, 0)),
                  pl.BlockSpec((None, tq, LANES), lambda b, i: (b, i, 0)),
                  pl.BlockSpec((None, ng, LANES), per_b), pl.BlockSpec((LANES, ng), per_b_t),
                  pl.BlockSpec((None, t, LANES), per_b), pl.BlockSpec((LANES, t), per_b_t),
                  pl.BlockSpec((None, t, LANES), per_b), pl.BlockSpec((LANES, t), per_b_t),
                  _const_spec((nsp, ng)), _const_spec((t, nsp))],
        out_specs=pl.BlockSpec((None, tq, GROUP_W), lambda b, i: (b, i, 0)),
        out_shape=jax.ShapeDtypeStruct((nb, t, GROUP_W), F32),
        scratch_shapes=[pltpu.VMEM((LANES, cols), F32), pltpu.VMEM((tk, cols), F32), pltpu.VMEM((tk, cols), BF16)],
        compiler_params=_params("parallel", "arbitrary"),
        name="nsa_attn",
    )(q, gates, kvc, kvc_t, slc, slc_t, win, win_t, ovl_t, expand)


def _samp_cmp_body(*refs, pages_per_step, n_grp, n_cmp, n_sel, n_top, t_pos):
    pt_ref = refs[0]
    page_refs = refs[1:1 + pages_per_step]
    (pelo_ref, pehi_ref, wlo_ref, whi_ref, b1_ref, w2_ref, b2_ref, gk_ref, cos_ref, sin_ref, j2_ref,
     q4_ref, ovl_ref, ocmp_ref, idx_ref, lo_ref, hi_ref) = refs[1 + pages_per_step:]
    del pt_ref
    s = pl.program_id(1)
    ns = pl.num_programs(1)
    grp_per_page = PAGE_SIZE // CMP_STRIDE
    rows_step = pages_per_step * grp_per_page

    @pl.when(s == 0)
    def _():
        hi_ref[n_grp:n_grp + SUBLANES, :] = jnp.zeros((SUBLANES, GROUP_W), F32)

    def read_rows(j):
        return jnp.concatenate([pg[pl.ds(j, grp_per_page, stride=CMP_STRIDE), :] for pg in page_refs], axis=0)

    lo, hi = _compress_hidden(read_rows, pelo_ref, pehi_ref, wlo_ref, whi_ref)
    off = pl.multiple_of(s * rows_step, rows_step)
    lo_ref[pl.ds(off, rows_step), :] = lo
    hi_ref[pl.ds(off, rows_step), :] = hi

    @pl.when(s == ns - 1)
    def _():
        kvc = _compress_tail(lo_ref[...], hi_ref[pl.ds(1, n_grp), :], b1_ref, w2_ref, b2_ref, gk_ref,
                             cos_ref, sin_ref, j2_ref).astype(BF16)
        q4 = q4_ref[...].astype(BF16)
        sc = _dot_nt(q4, kvc) * ATT_SCALE
        nidx = lax.broadcasted_iota(jnp.int32, sc.shape, 1)
        p = _masked_softmax_rows(sc, (nidx * CMP_STRIDE + (CMP_LEN - 1) <= t_pos) & (nidx < n_cmp))
        ocmp_ref[...] = _dot(p.astype(BF16), kvc)
        hrow = lax.broadcasted_iota(jnp.int32, p.shape, 0)
        psum = jnp.sum(jnp.where(hrow < N_HEADS, p, 0.0), axis=0, keepdims=True)
        psum8 = jnp.broadcast_to(psum, p.shape)
        ph = psum8.astype(BF16)
        pl_ = (psum8 - ph.astype(F32)).astype(BF16)
        ovl = ovl_ref[...]
        imp = (_dot(ph, ovl) + _dot(pl_, ovl))[0:1]
        nsp = imp.shape[1]
        jrow = lax.broadcasted_iota(jnp.int32, (1, nsp), 1)
        cur = t_pos // SEL_BLOCK
        forced = (jrow == 0) | (jrow == cur) | (jrow == cur - 1)
        valid = (jrow * SEL_BLOCK <= t_pos) & (jrow < n_sel)
        score = jnp.where(valid, imp + jnp.where(forced, FORCE_BONUS, 0.0), -1.0)
        score = jnp.where(jrow < n_sel, score, -2.0)
        s_row = jnp.broadcast_to(score, (nsp, nsp))
        s_col = s_row.T
        ii = lax.broadcasted_iota(jnp.int32, (nsp, nsp), 0)
        jj = lax.broadcasted_iota(jnp.int32, (nsp, nsp), 1)
        beats = (s_col > s_row) | ((s_col == s_row) & (ii < jj))
        rank = jnp.sum(jnp.where(beats, 1.0, 0.0), axis=0, keepdims=True)
        chosen = (rank < n_top) & (score >= 0.0)
        slot = lax.broadcasted_iota(jnp.int32, (N_SELECT, nsp), 0).astype(F32)
        hit = (jnp.broadcast_to(rank, (N_SELECT, nsp)) == slot) & jnp.broadcast_to(chosen, (N_SELECT, nsp))
        jcol = lax.broadcasted_iota(jnp.int32, (N_SELECT, nsp), 1).astype(F32)
        blk = jnp.sum(jnp.where(hit, jcol, 0.0), axis=1, keepdims=True)
        cnt = jnp.sum(jnp.where(hit, 1.0, 0.0), axis=1, keepdims=True)
        blk = jnp.where(cnt > 0.5, blk, -1.0)
        idx_ref[...] = jnp.broadcast_to(blk, (N_SELECT, LANES)).astype(jnp.int32)


def _samp_cmp(cache, layer, page_table, cw, cos, sin, j2, q4, ovl, n_cmp, n_sel, t_pos, pages_per_step):
    nb, n_pages = page_table.shape
    grp_per_page = PAGE_SIZE // CMP_STRIDE
    wid = CMP_STRIDE * LANES
    n_grp = n_pages * grp_per_page
    nsp = ovl.shape[1]

    def page_spec(j):
        return pl.BlockSpec((None, None, PAGE_SIZE, LANES),
                            lambda b, s, pt, j=j: (layer, pt[b, s * pages_per_step + j], 0, 0))

    def cst(shape):
        nd = len(shape)
        return pl.BlockSpec(shape, lambda b, s, pt: (0,) * nd)

    grid_spec = pltpu.PrefetchScalarGridSpec(
        num_scalar_prefetch=1,
        grid=(nb, n_pages // pages_per_step),
        in_specs=[page_spec(j) for j in range(pages_per_step)] + [
            cst((1, wid)), cst((1, wid)), cst((wid, GROUP_W)), cst((wid, GROUP_W)), cst((1, GROUP_W)),
            cst((GROUP_W, LANES)), cst((1, LANES)), cst((1, LANES)),
            cst((n_grp, LANES)), cst((n_grp, LANES)), cst((LANES, LANES)),
            pl.BlockSpec((None, SUBLANES, LANES), lambda b, s, pt: (b, 0, 0)), cst((n_grp, nsp))],
        out_specs=[pl.BlockSpec((None, SUBLANES, LANES), lambda b, s, pt: (b, 0, 0)),
                   pl.BlockSpec((None, N_SELECT, LANES), lambda b, s, pt: (b, 0, 0))],
        scratch_shapes=[pltpu.VMEM((n_grp, GROUP_W), F32), pltpu.VMEM((n_grp + SUBLANES, GROUP_W), F32)],
    )
    return pl.pallas_call(
        functools.partial(_samp_cmp_body, pages_per_step=pages_per_step, n_grp=n_grp, n_cmp=n_cmp,
                          n_sel=n_sel, n_top=min(N_SELECT, n_sel), t_pos=t_pos),
        grid_spec=grid_spec,
        out_shape=[jax.ShapeDtypeStruct((nb, SUBLANES, LANES), F32),
                   jax.ShapeDtypeStruct((nb, N_SELECT, LANES), jnp.int32)],
        compiler_params=_params("parallel", "arbitrary"),
        name="nsa_decode_compress",
    )(page_table, *([cache] * pages_per_step), cw["pe_lo"], cw["pe_hi"], cw["w_lo"], cw["w_hi"], cw["b1"],
      cw["w2"], cw["b2"], cw["gk"], cos, sin, j2, q4, ovl)


def _samp_sel_body(pt_ref, idx_ref, *refs, n_past_blk, t_pos, win_pos0):
    del pt_ref
    blk_refs = refs[:N_SELECT]
    q4_ref, nslc_ref, win_ref, nwin_ref, gate_ref, ocmp_ref, o_ref = refs[N_SELECT:]
    b = pl.program_id(0)
    q4 = q4_ref[...].astype(BF16)

    n_keys = N_SELECT * SEL_BLOCK
    lane_k = lax.broadcasted_iota(jnp.int32, (1, n_keys), 1)
    jvec = jnp.zeros((1, n_keys), jnp.int32)
    parts = []
    for k in range(N_SELECT):
        j = idx_ref[b, k]
        blk = blk_refs[k][...]
        parts.append(jnp.where(j == n_past_blk, nslc_ref[...], blk[0:SUBLANES]))
        parts.append(blk[SUBLANES:])
        jvec = jnp.where(lane_k // SEL_BLOCK == k, j, jvec)
    kv = jnp.concatenate(parts, axis=0).astype(BF16)
    kpos = jvec * SEL_BLOCK + (lane_k & (SEL_BLOCK - 1))
    p_sel = _masked_softmax_rows(_dot_nt(q4, kv) * ATT_SCALE, (kpos <= t_pos) & (jvec >= 0))
    o_sel = _dot(p_sel.astype(BF16), kv)

    wb = win_ref[...].astype(BF16)
    nw = nwin_ref[...].astype(BF16)
    s_w = _dot_nt(q4, wb) * ATT_SCALE
    s_n = _dot_nt(q4, nw) * ATT_SCALE
    wpos = win_pos0 + lax.broadcasted_iota(jnp.int32, s_w.shape, 1)
    rel = t_pos - wpos
    m_w = (rel >= 0) & (rel < WINDOW) & (wpos >= win_pos0)
    m_n = lax.broadcasted_iota(jnp.int32, s_n.shape, 1) == 0
    s_w = jnp.where(m_w, s_w, NEG_INF)
    s_n = jnp.where(m_n, s_n, NEG_INF)
    mx = jnp.maximum(jnp.max(s_w, axis=-1, keepdims=True), jnp.max(s_n, axis=-1, keepdims=True))
    e_w = jnp.exp(s_w - mx) * m_w.astype(F32)
    e_n = jnp.exp(s_n - mx) * m_n.astype(F32)
    den = jnp.sum(e_w, axis=-1, keepdims=True) + jnp.sum(e_n, axis=-1, keepdims=True)
    o_win = (_dot(e_w.astype(BF16), wb) + _dot(e_n.astype(BF16), nw)) / jnp.maximum(den, 1e-30)
    o_cmp = ocmp_ref[...]
    g = gate_ref[...]
    lane = lax.broadcasted_iota(jnp.int32, (1, LANES), 1)
    outs = []
    for h in range(N_HEADS):
        outs.append(g[0:1, 3 * h:3 * h + 1] * o_cmp[h:h + 1] + g[0:1, 3 * h + 1:3 * h + 2] * o_sel[h:h + 1]
                    + g[0:1, 3 * h + 2:3 * h + 3] * o_win[h:h + 1])
    o_ref[...] = jnp.broadcast_to(_unstack_heads(outs, lane), o_ref.shape)


def _samp_sel(cache_blk, layer, page_table, top_idx, q4, nslc8, win, nwin8, gates8, ocmp, t_pos, win_pos0):
    nb, n_pages = page_table.shape
    blk_per_page = PAGE_SIZE // SEL_BLOCK
    n_past_blk = n_pages * blk_per_page
    wlen = win.shape[2]

    def blk_spec(k):
        def blk_map(b, pt, idx):
            j = jnp.clip(idx[b, k], 0, n_past_blk - 1)
            return (layer, pt[b, j // blk_per_page], j % blk_per_page, 0, 0)
        return pl.BlockSpec((None, None, None, SEL_BLOCK, LANES), blk_map)

    row8 = pl.BlockSpec((None, SUBLANES, LANES), lambda b, pt, idx: (b, 0, 0))
    grid_spec = pltpu.PrefetchScalarGridSpec(
        num_scalar_prefetch=2,
        grid=(nb,),
        in_specs=[blk_spec(k) for k in range(N_SELECT)] + [
            row8, row8, pl.BlockSpec((None, None, wlen, LANES), lambda b, pt, idx: (layer, b, 0, 0)),
            row8, row8, row8],
        out_specs=pl.BlockSpec((None, SUBLANES, GROUP_W), lambda b, pt, idx: (b, 0, 0)),
    )
    return pl.pallas_call(
        functools.partial(_samp_sel_body, n_past_blk=n_past_blk, t_pos=t_pos, win_pos0=win_pos0),
        grid_spec=grid_spec,
        out_shape=jax.ShapeDtypeStruct((nb, SUBLANES, GROUP_W), F32),
        compiler_params=_params("parallel"),
        name="nsa_decode_select",
    )(page_table, top_idx, *([cache_blk] * N_SELECT), q4, nslc8, win, nwin8, gates8, ocmp)


def _block_diag(blocks):
    n, a, b = blocks.shape
    eye = jnp.eye(n, dtype=blocks.dtype)
    return jnp.einsum("nab,nm->namb", blocks, eye).reshape(n * a, n * b)


def _rope_tables(pos):
    half = HEAD_D // 2
    inv = ROPE_THETA ** (-jnp.arange(half, dtype=F32) / half)
    ang = pos.astype(F32)[:, None] * inv[None, :]
    cos, sin = jnp.cos(ang), jnp.sin(ang)
    cos128 = jnp.concatenate([cos, cos, cos, cos], axis=1)
    sin128 = jnp.concatenate([-sin, sin, -sin, sin], axis=1)
    return cos128, sin128


def _seg_ones(n_seg):
    return _block_diag(jnp.ones((n_seg, HEAD_D, HEAD_D), F32)).astype(BF16)


def _w_in_stack(w_in):
    a_w, b_w, c_w = 2 * GROUP_W, 4 * GROUP_W, 3 * GROUP_W
    return jnp.concatenate(
        [w_in[..., a_w:a_w + b_w], w_in[..., :a_w], w_in[..., a_w + b_w:a_w + b_w + c_w], w_in[..., a_w + b_w + c_w:],
         jnp.zeros(w_in.shape[:-1] + (PROJ_PAD - w_in.shape[-1],), F32)], axis=-1).astype(BF16)


def _layer_weights(lp):
    lora = HEAD_D
    w1 = lp["nsa_cmp_w1"].reshape(2, 2, CMP_STRIDE, HEAD_D, 2 * HEAD_D)
    pe = lp["nsa_cmp_pe"].reshape(2, 2, CMP_STRIDE, HEAD_D)

    def cmp_half(hf):
        wk = jnp.zeros((CMP_STRIDE, 2 * HEAD_D, 4 * HEAD_D), F32)
        wk = wk.at[:, :HEAD_D, :2 * HEAD_D].set(w1[0, hf]).at[:, HEAD_D:, 2 * HEAD_D:].set(w1[1, hf])
        pk = jnp.concatenate([pe[0, hf], pe[1, hf]], axis=1)
        return wk.reshape(CMP_STRIDE * 2 * HEAD_D, 4 * HEAD_D).astype(BF16), pk.reshape(1, CMP_STRIDE * 2 * HEAD_D)

    w_lo, pe_lo = cmp_half(0)
    w_hi, pe_hi = cmp_half(1)
    ones64 = jnp.ones((HEAD_D,), F32)
    cw = dict(
        w_lo=w_lo, w_hi=w_hi, pe_lo=pe_lo, pe_hi=pe_hi,
        b1=lp["nsa_cmp_b1"].reshape(1, 4 * HEAD_D),
        w2=_block_diag(lp["nsa_cmp_w2"]).astype(BF16),
        b2=lp["nsa_cmp_b2"].reshape(1, 2 * HEAD_D),
        gk=jnp.concatenate([lp["nsa_norm_k"][0], ones64]).reshape(1, LANES),
    )
    zeros_lora = jnp.zeros((lora, GROUP_W), F32)
    return dict(
        norm_ffn=lp["norm_ffn"].reshape(2, 1, D_MODEL),
        norm_mix=lp["norm_mix"].reshape(1, D_MODEL),
        lru_cw=lp["lru_conv_w"], lru_cb=lp["lru_conv_b"].reshape(1, GROUP_W),
        lru_wri=jnp.concatenate([_block_diag(lp["lru_w_r"]), _block_diag(lp["lru_w_i"])], axis=1).astype(BF16),
        lru_bri=jnp.concatenate([lp["lru_b_r"], lp["lru_b_i"]]).reshape(1, 2 * GROUP_W),
        lru_lam=lp["lru_lambda"].reshape(1, GROUP_W), sconv_w=lp["sconv_w"],
        mu=lp["rwkv_mu"].reshape(1, 4 * GROUP_W), w0=lp["rwkv_w0"].reshape(1, GROUP_W),
        w2p=jnp.concatenate([lp["rwkv_w2"], zeros_lora], axis=0).astype(BF16),
        a0=lp["rwkv_a0"].reshape(1, GROUP_W),
        a2p=jnp.concatenate([zeros_lora, lp["rwkv_a2"]], axis=0).astype(BF16),
        g2=lp["rwkv_g2"].astype(BF16),
        k_k=lp["rwkv_k_k"].reshape(1, GROUP_W), k_a=lp["rwkv_k_a"].reshape(1, GROUP_W),
        r_k=lp["rwkv_r_k"].reshape(1, GROUP_W),
        ln_g=lp["rwkv_ln_g"].reshape(1, GROUP_W), ln_b=lp["rwkv_ln_b"].reshape(1, GROUP_W),
        gq=jnp.tile(lp["nsa_norm_q"], 2).reshape(1, LANES),
        gk2=jnp.stack([jnp.concatenate([lp["nsa_norm_k"][1], ones64]),
                       jnp.concatenate([lp["nsa_norm_k"][2], ones64])]),
        cw=cw,
        out_norm=lp["out_norm"].reshape(N_HEADS, GROUP_W),
    )


def _overlap(n_grp, n_sel_pad, n_sel):
    cs = jnp.arange(n_grp)[:, None] * CMP_STRIDE
    js = jnp.arange(n_sel_pad)[None, :] * SEL_BLOCK
    ov = (cs < js + SEL_BLOCK) & (cs + CMP_LEN > js) & (jnp.arange(n_sel_pad)[None, :] < n_sel)
    return ov.astype(BF16)


def _pad_rows_front(x, rows):
    return jnp.pad(x, ((0, 0), (rows - x.shape[1], 0), (0, 0)))


def _round_up(x, m):
    return (x + m - 1) // m * m


def _mixers_recurrent(lw, proj3, cbuf8, h0, sbuf8, shift8, s_heads, j4, tt, chunk, pos0, t_valid):
    ya, yc, hlast, ulast = _lru_sconv(proj3, cbuf8, h0, sbuf8, lw["lru_cw"], lw["lru_cb"], lw["lru_wri"],
                                      lw["lru_bri"], lw["lru_lam"], lw["sconv_w"], tt, pos0)
    yb, s_out = _rwkv(proj3, shift8, s_heads, lw["mu"], lw["w0"], lw["w2p"], lw["a0"], lw["a2p"], lw["g2"],
                      lw["k_k"], lw["k_a"], lw["r_k"], lw["ln_g"], lw["ln_b"], j4, tt, chunk, t_valid,
                      n_seq=math.gcd(proj3.shape[0], max(1, RWKV_CHAINS * chunk // tt)))
    return ya, yb, yc, hlast, ulast, s_out


def _prompt_layer(lw, layer, x, nb, t, consts):
    m = nb * t
    tm = 512
    j2, j4 = consts["j2"], consts["j4"]
    ffn_w = consts["ffn_w"]
    x1 = _ffn(x, lw["norm_ffn"][0], ffn_w, layer, 0, tm)
    proj, q_r, ncmp, nslc, nwin, gates, slc_bf, win_bf, slc_t, win_t = _proj_prep(
        x1, lw["norm_mix"], consts["w_in"], layer, consts["cos_p"], consts["sin_p"], lw["gq"], lw["gk2"], j2, tm,
        rows_per_table=t, with_bf16=True)
    proj3 = proj.reshape(nb, t, PROJ_KEEP)
    zeros8 = jnp.zeros((nb, SUBLANES, GROUP_W), F32)
    ya, yb, yc, hlast, ulast, s_out = _mixers_recurrent(
        lw, proj3, zeros8, jnp.zeros((nb, 1, GROUP_W), F32), zeros8,
        jnp.zeros((nb, SUBLANES, 4 * GROUP_W), F32), jnp.zeros((nb, N_HEADS, HEAD_D, HEAD_D), F32), j4,
        tt=256, chunk=HEAD_D, pos0=0, t_valid=t)
    kvc, kvc_t = _compress(ncmp.reshape(nb, t, LANES), lw["cw"], consts["cos_cp"], consts["sin_cp"], j2)
    n_cmp = (t - CMP_LEN) // CMP_STRIDE + 1
    n_sel = -(-t // SEL_BLOCK)
    yd = _nsa_attn_t(q_r.reshape(nb, t, GROUP_W), gates.reshape(nb, t, LANES), kvc, kvc_t,
                     slc_bf.reshape(nb, t, LANES), slc_t, win_bf.reshape(nb, t, LANES), win_t,
                     consts["ovl_t_p"], consts["expand_p"], tq=256, tk=512, n_cmp=n_cmp, n_sel=n_sel)
    x3 = _ffn(x1, lw["norm_ffn"][1], ffn_w, layer, 1, tm,
              mix=(ya.reshape(m, GROUP_W), yb.reshape(m, GROUP_W), yc.reshape(m, GROUP_W), yd.reshape(m, GROUP_W),
                   lw["out_norm"], consts["w_out"]))
    xa = proj3[:, :, COL_XA * GROUP_W:(COL_XA + 1) * GROUP_W]
    wlen = min(WINDOW, t)
    states = (hlast[:, SUBLANES - 1], xa[:, t - 3:], s_out, proj3[:, t - 1, :4 * GROUP_W],
              ulast[:, SUBLANES - 2:], nwin.reshape(nb, t, LANES)[:, t - wlen:],
              ncmp.reshape(nb, t // PAGE_SIZE, PAGE_SIZE, LANES), nslc.reshape(nb, t // PAGE_SIZE, PAGE_SIZE, LANES))
    return x3, states


def _sample_layer(lw, layer, x, st, caches, page_table, consts):
    nb = x.shape[0]
    n_pages = page_table.shape[1]
    past_len = n_pages * PAGE_SIZE
    j2, j4 = consts["j2"], consts["j4"]
    lru_h, lru_conv, rwkv_s, rwkv_shift, sconv, win = st
    cache_cmp, cache_slc = caches
    ffn_w = consts["ffn_w"]
    x1 = _ffn(x, lw["norm_ffn"][0], ffn_w, layer, 0, nb)
    proj, q_r, ncmp, nslc, nwin, gates = _proj_prep(
        x1, lw["norm_mix"], consts["w_in"], layer, consts["cos_s"], consts["sin_s"], lw["gq"], lw["gk2"], j2, nb,
        rows_per_table=nb, with_bf16=False)
    proj3 = jnp.pad(proj[:, None, :], ((0, 0), (0, SUBLANES - 1), (0, 0)))
    ya, yb, yc, hlast, ulast, s_out = _mixers_recurrent(
        lw, proj3, _pad_rows_front(lru_conv, SUBLANES), lru_h[:, None, :], _pad_rows_front(sconv, SUBLANES),
        _pad_rows_front(rwkv_shift[:, None, :], SUBLANES), rwkv_s, j4,
        tt=SUBLANES, chunk=SUBLANES, pos0=past_len, t_valid=1)

    def row8(a):
        return jnp.pad(a[:, None, :], ((0, 0), (0, SUBLANES - 1), (0, 0)))

    q4 = jnp.pad(q_r.reshape(nb, N_HEADS, HEAD_D), ((0, 0), (0, SUBLANES - N_HEADS), (0, LANES - HEAD_D)))
    depth, n_phys = cache_cmp.shape[:2]
    n_cmp = (past_len + 1 - CMP_LEN) // CMP_STRIDE + 1
    n_sel = -(-(past_len + 1) // SEL_BLOCK)
    ocmp, top_idx = _samp_cmp(cache_cmp, layer, page_table, lw["cw"], consts["cos_cs"], consts["sin_cs"], j2, q4,
                              consts["ovl_s"], n_cmp, n_sel, t_pos=past_len,
                              pages_per_step=consts["pages_per_step"])
    wlen = win.shape[1]
    yd8 = _samp_sel(cache_slc.reshape(depth, n_phys, PAGE_SIZE // SEL_BLOCK, SEL_BLOCK, LANES), layer, page_table,
                    top_idx[:, :, 0], q4, row8(nslc), consts["win_all"], row8(nwin), row8(gates), ocmp,
                    t_pos=past_len, win_pos0=past_len - wlen)
    x3 = _ffn(x1, lw["norm_ffn"][1], ffn_w, layer, 1, nb,
              mix=(ya[:, 0], yb[:, 0], yc[:, 0], yd8[:, 0], lw["out_norm"], consts["w_out"]))
    xa = proj[:, COL_XA * GROUP_W:(COL_XA + 1) * GROUP_W]
    new_win = jnp.concatenate([win, nwin[:, None, :]], axis=1)
    states = (hlast[:, 0], jnp.concatenate([lru_conv[:, 1:], xa[:, None, :]], axis=1), s_out,
              proj[:, :4 * GROUP_W], jnp.concatenate([sconv[:, 1:], ulast[:, 0:1]], axis=1),
              new_win[:, -min(WINDOW, wlen + 1):], ncmp[:, None, :], nslc[:, None, :])
    return x3, states


def kernel(x_prompt, x_sample, state_lru_h, state_lru_conv, state_rwkv_S, state_rwkv_shift, state_sconv, state_nsa_win, cache_nsa_cmp, cache_nsa_slc, page_table, norm_ffn, ffn_w_gate, ffn_w_up, ffn_w_down, norm_mix, w_in, lru_conv_w, lru_conv_b, lru_w_r, lru_b_r, lru_w_i, lru_b_i, lru_lambda, rwkv_mu, rwkv_w0, rwkv_w2, rwkv_a0, rwkv_a2, rwkv_g2, rwkv_k_k, rwkv_k_a, rwkv_r_k, rwkv_ln_g, rwkv_ln_b, sconv_w, nsa_norm_q, nsa_norm_k, nsa_cmp_pe, nsa_cmp_w1, nsa_cmp_b1, nsa_cmp_w2, nsa_cmp_b2, out_norm, w_out):
    params = dict(norm_ffn=norm_ffn, norm_mix=norm_mix, lru_conv_w=lru_conv_w, lru_conv_b=lru_conv_b, lru_w_r=lru_w_r,
                  lru_b_r=lru_b_r, lru_w_i=lru_w_i, lru_b_i=lru_b_i, lru_lambda=lru_lambda, rwkv_mu=rwkv_mu,
                  rwkv_w0=rwkv_w0, rwkv_w2=rwkv_w2, rwkv_a0=rwkv_a0, rwkv_a2=rwkv_a2, rwkv_g2=rwkv_g2,
                  rwkv_k_k=rwkv_k_k, rwkv_k_a=rwkv_k_a, rwkv_r_k=rwkv_r_k, rwkv_ln_g=rwkv_ln_g,
                  rwkv_ln_b=rwkv_ln_b, sconv_w=sconv_w, nsa_norm_q=nsa_norm_q, nsa_norm_k=nsa_norm_k,
                  nsa_cmp_pe=nsa_cmp_pe, nsa_cmp_w1=nsa_cmp_w1, nsa_cmp_b1=nsa_cmp_b1, nsa_cmp_w2=nsa_cmp_w2,
                  nsa_cmp_b2=nsa_cmp_b2, out_norm=out_norm)
    depth = norm_mix.shape[0]
    bp, tp, d = x_prompt.shape
    bs, ts, _ = x_sample.shape
    n_pages = page_table.shape[1]
    past_len = n_pages * PAGE_SIZE
    assert d == D_MODEL and ts == 1 and tp % 512 == 0 and tp >= WINDOW + 256 and past_len >= WINDOW
    assert bs % SUBLANES == 0 or bs < SUBLANES

    n_grp_p = tp // CMP_STRIDE
    n_sel_p = -(-tp // SEL_BLOCK)
    nsp_p = _round_up(n_sel_p, LANES)
    n_grp_s = past_len // CMP_STRIDE
    n_sel_s = -(-(past_len + 1) // SEL_BLOCK)
    nsp_s = _round_up(n_sel_s, LANES)
    cos_p, sin_p = _rope_tables(jnp.arange(tp))
    cos_s, sin_s = _rope_tables(jnp.full((bs,), past_len))
    cos_cp, sin_cp = _rope_tables(jnp.arange(n_grp_p) * CMP_STRIDE + CMP_LEN - 1)
    cos_cs, sin_cs = _rope_tables(jnp.arange(n_grp_s) * CMP_STRIDE + CMP_LEN - 1)
    expand_p = ((jnp.arange(tp)[:, None] // SEL_BLOCK) == jnp.arange(nsp_p)[None, :]).astype(BF16)
    pages_per_step = math.gcd(n_pages, 64)
    consts = dict(
        j2=_seg_ones(2), j4=_seg_ones(N_HEADS), cos_p=cos_p, sin_p=sin_p, cos_s=cos_s, sin_s=sin_s,
        cos_cp=cos_cp, sin_cp=sin_cp, cos_cs=cos_cs, sin_cs=sin_cs,
        ovl_t_p=_overlap(n_grp_p, nsp_p, n_sel_p).T,
        expand_p=expand_p,
        ovl_s=_overlap(n_grp_s, nsp_s, n_sel_s),
        pages_per_step=pages_per_step,
        ffn_w=(ffn_w_gate.astype(BF16), ffn_w_up.astype(BF16), ffn_w_down.astype(BF16)),
        w_out=w_out.astype(BF16),
        w_in=_w_in_stack(w_in),
        win_all=state_nsa_win,
    )

    yp = x_prompt.reshape(bp * tp, d)
    ys = x_sample.reshape(bs, d)
    sp_all, ss_all = [], []
    for l in range(depth):
        lw = _layer_weights({name: arr[l] for name, arr in params.items()})
        yp, sp = _prompt_layer(lw, l, yp, bp, tp, consts)
        ys, ss = _sample_layer(lw, l, ys, (state_lru_h[l], state_lru_conv[l], state_rwkv_S[l], state_rwkv_shift[l],
                                           state_sconv[l], state_nsa_win[l]),
                               (cache_nsa_cmp, cache_nsa_slc), page_table, consts)
        sp_all.append(sp)
        ss_all.append(ss)
    outs = [yp.reshape(bp, tp, d), ys.reshape(bs, ts, d)]
    for i in range(8):
        outs.append(jnp.stack([s[i] for s in sp_all]))
        outs.append(jnp.stack([s[i] for s in ss_all]))
    return tuple(outs)
```

```python
import functools
import math

import jax
import jax.numpy as jnp
from jax import lax
from jax.experimental import pallas as pl
from jax.experimental.pallas import tpu as pltpu

F32 = jnp.float32
BF16 = jnp.bfloat16

D_MODEL = 1024
GROUP_W = 256
N_HEADS = 4
HEAD_D = 64
LRU_C = 8.0
RWKV_LN_EPS = 64e-5
CMP_LEN = 32
CMP_STRIDE = 16
SEL_BLOCK = 64
N_SELECT = 16
WINDOW = 512
FORCE_BONUS = 1e4
NEG_INF = -1e30
ROPE_THETA = 10000.0
EPS = 1e-6
PAGE_SIZE = 128
PROJ_PAD = 3072
PROJ_KEEP = 2304
ATT_SCALE = HEAD_D ** -0.5
RWKV_CHAINS = 8

SUBLANES = 8
LANES = 128
VMEM_LIMIT = 56 * 1024 * 1024

COL_XA, COL_GA, COL_BG, COL_CG, COL_XIN = 4, 5, 6, 7, 8


def _params(*sem):
    return pltpu.CompilerParams(dimension_semantics=sem, vmem_limit_bytes=VMEM_LIMIT)


def _const_spec(shape):
    nd = len(shape)
    return pl.BlockSpec(shape, lambda *_: (0,) * nd)


def _dot(a, b):
    return jnp.dot(a, b, preferred_element_type=F32)


def _dot_nt(a, b):
    return lax.dot_general(a, b, (((1,), (1,)), ((), ())), preferred_element_type=F32)


def _dot_tn(a, b):
    return lax.dot_general(a, b, (((0,), (0,)), ((), ())), preferred_element_type=F32)


def _split3(x):
    h1 = x.astype(BF16)
    r1 = x - h1.astype(F32)
    h2 = r1.astype(BF16)
    h3 = (r1 - h2.astype(F32)).astype(BF16)
    return h1, h2, h3


def _dot_exact_rhs(x, m_bf16):
    h1, h2, h3 = _split3(x)
    return _dot(h1, m_bf16) + _dot(h2, m_bf16) + _dot(h3, m_bf16)


def _gelu_tanh(x):
    return x * (0.5 * (1.0 + jnp.tanh(math.sqrt(2.0 / math.pi) * (x + 0.044715 * (x * x * x)))))


def _sigmoid(x):
    return 1.0 / (1.0 + jnp.exp(-x))


def _softplus(x):
    return jnp.maximum(x, 0.0) + jnp.log1p(jnp.exp(-jnp.abs(x)))


def _rmsnorm_rows(x, g):
    ms = jnp.mean(x * x, axis=-1, keepdims=True)
    return x * lax.rsqrt(ms + EPS) * g


def _masked_softmax_rows(s, mask):
    s = jnp.where(mask, s, NEG_INF)
    m = jnp.max(s, axis=-1, keepdims=True)
    e = jnp.exp(s - m)
    norm = 1.0 / jnp.maximum(jnp.sum(e, axis=-1, keepdims=True), 1e-30)
    return e * jnp.where(m > 0.5 * NEG_INF, norm, 0.0)


def _rope_lanes(x, cos, sin_signed, lane):
    swapped = jnp.where((lane & (HEAD_D - 1)) < HEAD_D // 2,
                        pltpu.roll(x, LANES - HEAD_D // 2, 1), pltpu.roll(x, HEAD_D // 2, 1))
    return x * cos + swapped * sin_signed


def _ffn_body(*refs, f_chunk, n_mix):
    x_ref = refs[0]
    y_refs = refs[1:1 + n_mix]
    if n_mix:
        gm_ref, wo_ref = refs[1 + n_mix:3 + n_mix]
    g_ref, wg_ref, wu_ref, wd_ref, o_ref, act_ref = refs[1 + n_mix + (2 if n_mix else 0):]
    x = x_ref[...]
    for gi, y_ref in enumerate(y_refs):
        yn = _rmsnorm_rows(y_ref[...], gm_ref[gi:gi + 1, :]).astype(BF16)
        x = x + _dot(yn, wo_ref[gi * GROUP_W:(gi + 1) * GROUP_W, :])
    h = _rmsnorm_rows(x, g_ref[...]).astype(BF16)
    d_ff = wg_ref.shape[1]
    for c in range(d_ff // f_chunk):
        sl = slice(c * f_chunk, (c + 1) * f_chunk)
        gate = _dot(h, wg_ref[:, sl])
        up = _dot(h, wu_ref[:, sl])
        act_ref[:, sl] = (gate * _sigmoid(gate) * up).astype(BF16)
    o_ref[...] = x + 0.5 * _dot(act_ref[...], wd_ref[...])


def _ffn(x, g, ffn_w, layer, which, tm, mix=None):
    m, d = x.shape
    wg, wu, wd = ffn_w
    d_ff = wg.shape[3]
    once = pl.Buffered(1)

    def wspec(a, b):
        return pl.BlockSpec((None, None, a, b), lambda i: (layer, which, 0, 0), pipeline_mode=once)

    mix_specs, mix_args = [], []
    if mix is not None:
        yspec = pl.BlockSpec((tm, GROUP_W), lambda i: (i, 0))
        mix_specs = [yspec] * 4 + [_const_spec((N_HEADS, GROUP_W)),
                                   pl.BlockSpec((None, N_HEADS * GROUP_W, d), lambda i: (layer, 0, 0),
                                                pipeline_mode=once)]
        mix_args = list(mix)
    return pl.pallas_call(
        functools.partial(_ffn_body, f_chunk=256, n_mix=4 if mix is not None else 0),
        grid=(m // tm,),
        in_specs=[pl.BlockSpec((tm, d), lambda i: (i, 0))] + mix_specs
        + [_const_spec((1, d)), wspec(d, d_ff), wspec(d, d_ff), wspec(d_ff, d)],
        out_specs=pl.BlockSpec((tm, d), lambda i: (i, 0)),
        out_shape=jax.ShapeDtypeStruct((m, d), F32),
        scratch_shapes=[pltpu.VMEM((tm, d_ff), BF16)],
        compiler_params=_params("parallel"),
        name="ffn_mix" if mix is not None else "ffn",
    )(x, *mix_args, g, wg, wu, wd)


def _lru_sconv_body(xa_ref, ga_ref, bg_ref, cg_ref, xin_ref, cbuf_ref, h0_ref, sbuf_ref,
                    cw_ref, cb_ref, wri_ref, bri_ref, lam_ref, sw_ref,
                    ya_ref, yc_ref, hlast_ref, ulast_ref,
                    extx_ref, extu_ref, h_ref, *, tt, pos0):
    t = pl.program_id(1)
    nt = pl.num_programs(1)

    @pl.when(t == 0)
    def _():
        extx_ref[0:SUBLANES, :] = cbuf_ref[...]
        extu_ref[0:SUBLANES, :] = sbuf_ref[...]
        h_ref[...] = h0_ref[...]

    x = xa_ref[...]
    extx_ref[SUBLANES:SUBLANES + tt, :] = x
    cw = cw_ref[...]
    xc = extx_ref[pl.ds(SUBLANES - 3, tt), :] * cw[0:1]
    xc = xc + extx_ref[pl.ds(SUBLANES - 2, tt), :] * cw[1:2]
    xc = xc + extx_ref[pl.ds(SUBLANES - 1, tt), :] * cw[2:3]
    xc = xc + x * cw[3:4]
    xc = xc + cb_ref[...]
    extx_ref[0:SUBLANES, :] = x[tt - SUBLANES:tt]

    gates = _dot(xc.astype(BF16), wri_ref[...]) + bri_ref[...]
    r = _sigmoid(gates[:, :GROUP_W])
    ig = _sigmoid(gates[:, GROUP_W:])
    log_a = -LRU_C * r * _softplus(-lam_ref[...])
    a = jnp.exp(log_a)
    rows = lax.broadcasted_iota(jnp.int32, (tt, GROUP_W), 0)
    mult = jnp.where(rows + (pos0 + t * tt) == 0, 1.0, jnp.sqrt(1.0 - jnp.exp(2.0 * log_a)))
    b = mult * ig * xc

    s = 1
    while s < tt:
        keep = rows >= s
        a_sh = jnp.where(keep, pltpu.roll(a, s, 0), 1.0)
        b_sh = jnp.where(keep, pltpu.roll(b, s, 0), 0.0)
        b = a * b_sh + b
        a = a * a_sh
        s *= 2
    h = a * h_ref[...] + b
    h_ref[...] = h[tt - 1:tt]
    ya_ref[...] = _gelu_tanh(ga_ref[...]) * h

    u = cg_ref[...] * xin_ref[...]
    extu_ref[SUBLANES:SUBLANES + tt, :] = u
    sw = sw_ref[...]
    yv = extu_ref[pl.ds(SUBLANES - 2, tt), :] * sw[0:1]
    yv = yv + extu_ref[pl.ds(SUBLANES - 1, tt), :] * sw[1:2]
    yv = yv + u * sw[2:3]
    yc_ref[...] = bg_ref[...] * yv
    extu_ref[0:SUBLANES, :] = u[tt - SUBLANES:tt]

    @pl.when(t == nt - 1)
    def _():
        hlast_ref[...] = h[tt - SUBLANES:tt]
        ulast_ref[...] = u[tt - SUBLANES:tt]


def _lru_sconv(proj3, cbuf8, h0, sbuf8, cw, cb, wri, bri, lam, sw, tt, pos0):
    nb, tp, _ = proj3.shape
    w = GROUP_W

    def col(c):
        return pl.BlockSpec((None, tt, w), lambda b, t, c=c: (b, t, c))

    state8 = pl.BlockSpec((None, SUBLANES, w), lambda b, t: (b, 0, 0))
    return pl.pallas_call(
        functools.partial(_lru_sconv_body, tt=tt, pos0=pos0),
        grid=(nb, tp // tt),
        in_specs=[col(COL_XA), col(COL_GA), col(COL_BG), col(COL_CG), col(COL_XIN),
                  state8, pl.BlockSpec((None, 1, w), lambda b, t: (b, 0, 0)), state8,
                  _const_spec((4, w)), _const_spec((1, w)), _const_spec((w, 2 * w)), _const_spec((1, 2 * w)),
                  _const_spec((1, w)), _const_spec((3, w))],
        out_specs=[pl.BlockSpec((None, tt, w), lambda b, t: (b, t, 0)),
                   pl.BlockSpec((None, tt, w), lambda b, t: (b, t, 0)), state8, state8],
        out_shape=[jax.ShapeDtypeStruct((nb, tp, w), F32), jax.ShapeDtypeStruct((nb, tp, w), F32),
                   jax.ShapeDtypeStruct((nb, SUBLANES, w), F32), jax.ShapeDtypeStruct((nb, SUBLANES, w), F32)],
        scratch_shapes=[pltpu.VMEM((tt + SUBLANES, w), F32), pltpu.VMEM((tt + SUBLANES, w), F32),
                        pltpu.VMEM((1, w), F32)],
        compiler_params=_params("parallel", "arbitrary"),
        name="lru_sconv",
    )(proj3, proj3, proj3, proj3, proj3, cbuf8, h0, sbuf8, cw, cb, wri, bri, lam, sw)


def _rwkv_tokens(t, p_ref, mu_ref, w0_ref, w2_ref, a0_ref, a2_ref, g2_ref, kk_ref, ka_ref, rk_ref, j4,
                 ext_ref, w_buf, r_buf, k_buf, v_buf, a_buf, b_buf, g_buf, bon_buf, *, tt, chunk, t_valid):
    w = GROUP_W
    p = p_ref[...]
    ext_ref[SUBLANES:SUBLANES + tt, :] = p
    m = p + (ext_ref[pl.ds(SUBLANES - 1, tt), :] - p) * mu_ref[...]
    ext_ref[0:SUBLANES, :] = p[tt - SUBLANES:tt]
    r = m[:, 0:w]
    k = m[:, w:2 * w]
    v = m[:, 2 * w:3 * w]
    wa = m[:, 3 * w:3 * w + LANES]
    gl = m[:, 3 * w + LANES:4 * w]
    wlin = w0_ref[...] + _dot(jnp.tanh(wa).astype(BF16), w2_ref[...])
    log_decay = -jnp.exp(-_softplus(-wlin) - 0.5)
    ag = _sigmoid(a0_ref[...] + _dot(wa.astype(BF16), a2_ref[...]))
    g_buf[...] = _dot(_sigmoid(gl).astype(BF16), g2_ref[...])
    kk = k * kk_ref[...]
    kk = kk * lax.rsqrt(jnp.maximum(_dot_exact_rhs(kk * kk, j4), 1e-24))
    kf = k * (1.0 + (ag - 1.0) * ka_ref[...])
    rows = lax.broadcasted_iota(jnp.int32, (tt, w), 0)
    if t_valid % tt != 0:
        live = rows + t * tt < t_valid
        log_decay = jnp.where(live, log_decay, 0.0)
        kk = jnp.where(live, kk, 0.0)
        kf = jnp.where(live, kf, 0.0)
        v = jnp.where(live, v, 0.0)
    bon_buf[...] = _dot_exact_rhs(r * kf * rk_ref[...], j4) * v

    rin = rows & (chunk - 1)
    cl = log_decay
    s = 1
    while s < chunk:
        cl = cl + jnp.where(rin >= s, pltpu.roll(cl, s, 0), 0.0)
        s *= 2
    e_neg = jnp.exp(-cl)
    w_buf[...] = cl
    r_buf[...] = r * jnp.exp(cl)
    a_buf[...] = -kk * jnp.exp(cl - log_decay)
    b_buf[...] = kk * ag * e_neg
    k_buf[...] = kf * e_neg
    v_buf[...] = v


def _rwkv_body(p_ref, shift_ref, sin_ref, mu_ref, w0_ref, w2_ref, a0_ref, a2_ref, g2_ref, kk_ref, ka_ref,
                rk_ref, lng_ref, lnb_ref, j4_ref, y_ref, sout_ref,
                ext_ref, s_ref, w_buf, r_buf, k_buf, v_buf, a_buf, b_buf, g_buf, bon_buf, y_buf,
                *, n_seq, tt, chunk, t_valid):
    t = pl.program_id(1)
    w = GROUP_W
    nh = N_HEADS
    nc = tt // chunk
    j4 = j4_ref[...]

    heads = [slice(h * HEAD_D, (h + 1) * HEAD_D) for h in range(nh)]

    @pl.when(t == 0)
    def _():
        for i in range(n_seq):
            ext_ref[i, 0:SUBLANES, :] = shift_ref[i]
            s_ref[i] = jnp.zeros((w, w), F32)
            for hs in heads:
                s_ref[i, hs, hs] = sin_ref[i, hs, :]

    for i in range(n_seq):
        _rwkv_tokens(t, p_ref.at[i], mu_ref, w0_ref, w2_ref, a0_ref, a2_ref, g2_ref, kk_ref, ka_ref, rk_ref, j4,
                     ext_ref.at[i], w_buf.at[i], r_buf.at[i], k_buf.at[i], v_buf.at[i], a_buf.at[i], b_buf.at[i],
                     g_buf.at[i], bon_buf.at[i], tt=tt, chunk=chunk, t_valid=t_valid)

    sl = nh * chunk
    hm_rows = lax.broadcasted_iota(jnp.int32, (sl, w), 0) // chunk
    hm_cols = lax.broadcasted_iota(jnp.int32, (sl, w), 1) // HEAD_D
    head_mask = (hm_rows == hm_cols).astype(F32)
    ri = lax.broadcasted_iota(jnp.int32, (chunk, chunk), 0)
    ci = lax.broadcasted_iota(jnp.int32, (chunk, chunk), 1)
    strict = (ri > ci).astype(F32)
    incl = (ri >= ci).astype(F32)
    eye = (ri == ci).astype(F32)

    def stacked(buf, i, c):
        return jnp.concatenate([buf[i, pl.ds(c * chunk, chunk), :]] * nh, axis=0) * head_mask

    chains = [(i, c) for c in range(nc) for i in range(n_seq)]
    hb = [slice(h * chunk, (h + 1) * chunk) for h in range(nh)]

    def heads_of(x):
        return [x[hs] for hs in hb]

    def per_head(mats, rhs):
        return jnp.concatenate([_dot(m_, x_) for m_, x_ in zip(mats, heads_of(rhs))], axis=0)

    a_s = [stacked(a_buf, i, c).astype(BF16) for i, c in chains]
    r_s = [stacked(r_buf, i, c).astype(BF16) for i, c in chains]
    b_s = [stacked(b_buf, i, c).astype(BF16) for i, c in chains]
    k_s = [stacked(k_buf, i, c).astype(BF16) for i, c in chains]
    v_s = [stacked(v_buf, i, c).astype(BF16) for i, c in chains]
    n_c = [[_dot_nt(x, y) * strict for x, y in zip(heads_of(a), heads_of(b))] for a, b in zip(a_s, b_s)]
    m_c = [[(_dot_nt(x, y) * strict).astype(BF16) for x, y in zip(heads_of(a), heads_of(k))]
           for a, k in zip(a_s, k_s)]
    p_c = [[(_dot_nt(x, y) * incl).astype(BF16) for x, y in zip(heads_of(r), heads_of(b))]
           for r, b in zip(r_s, b_s)]
    q_c = [[(_dot_nt(x, y) * incl).astype(BF16) for x, y in zip(heads_of(r), heads_of(k))]
           for r, k in zip(r_s, k_s)]
    x = [xi for n in n_c for xi in n]
    t_c = [eye + xi for xi in x]
    step = 2
    while step < chunk:
        xb = [xi.astype(BF16) for xi in x]
        x = [_dot(b, b) for b in xb]
        t_c = [tm + _dot(tm.astype(BF16), xi.astype(BF16)) for tm, xi in zip(t_c, x)]
        step *= 2
    t_b = [[t_c[n * nh + h].astype(BF16) for h in range(nh)] for n in range(len(chains))]
    w_eff = [per_head(tb, a).astype(BF16) for tb, a in zip(t_b, a_s)]
    mv = [per_head(mm, v).astype(BF16) for mm, v in zip(m_c, v_s)]
    z = [per_head(tb, x_) for tb, x_ in zip(t_b, mv)]
    qv = [per_head(qm, v) for qm, v in zip(q_c, v_s)]

    for n, (i, c) in enumerate(chains):
        s0 = s_ref[i]
        s0b = s0.astype(BF16)
        ub = (_dot_nt(w_eff[n], s0b) + z[n]).astype(BF16)
        y_s = _dot_nt(r_s[n], s0b) + per_head(p_c[n], ub) + qv[n]
        yc = y_s[0:chunk]
        for hh in range(1, nh):
            yc = yc + y_s[hh * chunk:(hh + 1) * chunk]
        y_buf[i, pl.ds(c * chunk, chunk), :] = yc
        c_last = jnp.exp(w_buf[i, pl.ds((c + 1) * chunk - 1, 1), :])
        s_ref[i] = (s0 + _dot_tn(ub, b_s[n]) + _dot_tn(v_s[n], k_s[n])) * c_last

    inv_hd = 1.0 / HEAD_D
    for i in range(n_seq):
        y = y_buf[i]
        mean = _dot_exact_rhs(y, j4) * inv_hd
        yc = y - mean
        var = _dot_exact_rhs(yc * yc, j4) * inv_hd
        yn = yc * lax.rsqrt(var + RWKV_LN_EPS) * lng_ref[...] + lnb_ref[...]
        y_ref[i] = (yn + bon_buf[i]) * g_buf[i]
        for hs in heads:
            sout_ref[i, hs, :] = s_ref[i, hs, hs]


def _rwkv(proj3, shift8, s_heads, mu, w0, w2p, a0, a2p, g2, kk, ka, rk, lng, lnb, j4, tt, chunk, t_valid, n_seq):
    nb, tp, _ = proj3.shape
    w = GROUP_W
    pw = 4 * w
    vec = _const_spec((1, w))
    s_rows = s_heads.reshape(nb, w, HEAD_D)
    state_spec = pl.BlockSpec((n_seq, w, HEAD_D), lambda b, t: (b, 0, 0))
    yb, s_out = pl.pallas_call(
        functools.partial(_rwkv_body, n_seq=n_seq, tt=tt, chunk=chunk, t_valid=t_valid),
        grid=(nb // n_seq, tp // tt),
        in_specs=[pl.BlockSpec((n_seq, tt, pw), lambda b, t: (b, t, 0)),
                  pl.BlockSpec((n_seq, SUBLANES, pw), lambda b, t: (b, 0, 0)), state_spec,
                  _const_spec((1, pw)), vec, _const_spec((LANES, w)), vec, _const_spec((LANES, w)),
                  _const_spec((LANES, w)), vec, vec, vec, vec, vec, _const_spec((w, w))],
        out_specs=[pl.BlockSpec((n_seq, tt, w), lambda b, t: (b, t, 0)), state_spec],
        out_shape=[jax.ShapeDtypeStruct((nb, tp, w), F32), jax.ShapeDtypeStruct(s_rows.shape, F32)],
        scratch_shapes=[pltpu.VMEM((n_seq, tt + SUBLANES, pw), F32), pltpu.VMEM((n_seq, w, w), F32)]
        + [pltpu.VMEM((n_seq, tt, w), F32)] * 9,
        compiler_params=_params("parallel", "arbitrary"),
        name="rwkv7",
    )(proj3, shift8, s_rows, mu, w0, w2p, a0, a2p, g2, kk, ka, rk, lng, lnb, j4)
    return yb, s_out.reshape(s_heads.shape)


def _proj_prep_body(x_ref, g_ref, w_ref, cos_ref, sin_ref, gq_ref, gk_ref, j2_ref,
                    o_ref, qo_ref, cmpo_ref, slco_ref, wino_ref, gate_ref, *bf_refs, n_chunk):
    h =_rmsnorm_rows(x_ref[...], g_ref[...]).astype(BF16)
    n_keep = o_ref.shape[1]
    for c in range(n_keep // n_chunk):
        sl = slice(c * n_chunk, (c + 1) * n_chunk)
        o_ref[:, sl] = _dot(h, w_ref[:, sl])
    nsa = _dot(h, w_ref[:, n_keep:])

    cos = cos_ref[...]
    sin = sin_ref[...]
    j2 = j2_ref[...]
    lane = lax.broadcasted_iota(jnp.int32, cos.shape, 1)
    inv_hd = 1.0 / HEAD_D
    gq = gq_ref[...]
    halves = []
    for c in range(2):
        x = nsa[:, c * LANES:(c + 1) * LANES]
        ms = _dot_exact_rhs(x * x, j2) * inv_hd
        halves.append(_rope_lanes(x * lax.rsqrt(ms + EPS) * gq, cos, sin, lane))
    qo_ref[...] = jnp.concatenate(halves, axis=1)
    cmpo_ref[...] = nsa[:, 2 * LANES:3 * LANES]
    gk = gk_ref[...]
    is_key = lane < HEAD_D
    for dst, row in ((slco_ref, 0), (wino_ref, 1)):
        x = nsa[:, (3 + row) * LANES:(4 + row) * LANES]
        ms = _dot_exact_rhs(x * x, j2) * inv_hd
        roped = _rope_lanes(x * lax.rsqrt(ms + EPS) * gk[row:row + 1], cos, sin, lane)
        kv = jnp.where(is_key, roped, x)
        dst[...] = kv
        if bf_refs:
            bf_refs[row][...] = kv.astype(BF16)
            kv_t = jnp.where(lax.broadcasted_iota(jnp.int32, (LANES, kv.shape[0]), 0) == 0, 1.0, kv.T)
            bf_refs[2 + row][...] = kv_t.astype(BF16)
    gate_ref[...] = _sigmoid(nsa[:, 5 * LANES:6 * LANES])


def _proj_prep(x, g, w, layer, cos, sin, gq, gk2, j2, tm, rows_per_table, with_bf16):
    m, d = x.shape
    n = w.shape[2]
    nt_tab = rows_per_table // tm
    o128t = pl.BlockSpec((LANES, tm), lambda i: (0, i))
    bf_specs = [pl.BlockSpec((tm, LANES), lambda i: (i, 0))] * 2 + [o128t] * 2 if with_bf16 else []
    bf_shapes = ([jax.ShapeDtypeStruct((m, LANES), BF16)] * 2 + [jax.ShapeDtypeStruct((LANES, m), BF16)] * 2
                 if with_bf16 else [])
    tab = pl.BlockSpec((tm, LANES), lambda i: (i % nt_tab, 0))
    o128 = pl.BlockSpec((tm, LANES), lambda i: (i, 0))
    return pl.pallas_call(
        functools.partial(_proj_prep_body, n_chunk=256),
        grid=(m // tm,),
        in_specs=[pl.BlockSpec((tm, d), lambda i: (i, 0)), _const_spec((1, d)),
                  pl.BlockSpec((None, d, n), lambda i: (layer, 0, 0), pipeline_mode=pl.Buffered(1)),
                  tab, tab, _const_spec((1, LANES)), _const_spec((2, LANES)), _const_spec((LANES, LANES))],
        out_specs=[pl.BlockSpec((tm, PROJ_KEEP), lambda i: (i, 0)),
                   pl.BlockSpec((tm, GROUP_W), lambda i: (i, 0)), o128, o128, o128, o128] + bf_specs,
        out_shape=[jax.ShapeDtypeStruct((m, PROJ_KEEP), F32), jax.ShapeDtypeStruct((m, GROUP_W), F32)]
        + [jax.ShapeDtypeStruct((m, LANES), F32)] * 4 + bf_shapes,
        compiler_params=_params("parallel"),
        name="proj_prep",
    )(x, g, w, cos, sin, gq, gk2, j2)


def _compress_tail(hid_lo, hid_hi_next, b1_ref, w2_ref, b2_ref, gk_ref, cos_ref, sin_ref, j2_ref):
    hidden = _gelu_tanh(hid_lo + hid_hi_next + b1_ref[...])
    kv = _dot(hidden.astype(BF16), w2_ref[...]) + b2_ref[...]
    lane = lax.broadcasted_iota(jnp.int32, kv.shape, 1)
    ms = _dot_exact_rhs(kv * kv, j2_ref[...]) * (1.0 / HEAD_D)
    roped = _rope_lanes(kv * lax.rsqrt(ms + EPS) * gk_ref[...], cos_ref[...], sin_ref[...], lane)
    return jnp.where(lane < HEAD_D, roped, kv)


def _compress_hidden(read_rows, pelo_ref, pehi_ref, wlo_ref, whi_ref):
    pair = 2 * LANES
    lo = hi = None
    for p in range(CMP_STRIDE // 2):
        x = jnp.concatenate([read_rows(2 * p), read_rows(2 * p + 1)], axis=1)
        cs = slice(p * pair, (p + 1) * pair)
        d_lo = _dot((x + pelo_ref[:, cs]).astype(BF16), wlo_ref[cs, :])
        d_hi = _dot((x + pehi_ref[:, cs]).astype(BF16), whi_ref[cs, :])
        lo = d_lo if lo is None else lo + d_lo
        hi = d_hi if hi is None else hi + d_hi
    return lo, hi


def _compress_body(x_ref, pelo_ref, pehi_ref, wlo_ref, whi_ref, b1_ref, w2_ref, b2_ref, gk_ref,
                   cos_ref, sin_ref, j2_ref, o_ref, ot_ref):
    n_grp = x_ref.shape[0] // CMP_STRIDE
    lo, hi = _compress_hidden(lambda j: x_ref[pl.ds(j, n_grp, stride=CMP_STRIDE), :],
                              pelo_ref, pehi_ref, wlo_ref, whi_ref)
    hi_next = pltpu.roll(hi, n_grp - 1, 0)
    kv = _compress_tail(lo, hi_next, b1_ref, w2_ref, b2_ref, gk_ref, cos_ref, sin_ref, j2_ref)
    o_ref[...] = kv.astype(BF16)
    ot_ref[...] = kv.T.astype(BF16)


def _compress(rows, cw, cos, sin, j2):
    nb, t, _ = rows.shape
    n_grp = t // CMP_STRIDE
    wid = CMP_STRIDE * LANES
    return pl.pallas_call(
        _compress_body,
        grid=(nb,),
        in_specs=[pl.BlockSpec((None, t, LANES), lambda b: (b, 0, 0)),
                  _const_spec((1, wid)), _const_spec((1, wid)),
                  _const_spec((wid, GROUP_W)), _const_spec((wid, GROUP_W)), _const_spec((1, GROUP_W)),
                  _const_spec((GROUP_W, LANES)), _const_spec((1, LANES)), _const_spec((1, LANES)),
                  _const_spec((n_grp, LANES)), _const_spec((n_grp, LANES)), _const_spec((LANES, LANES))],
        out_specs=[pl.BlockSpec((None, n_grp, LANES), lambda b: (b, 0, 0)),
                   pl.BlockSpec((LANES, n_grp), lambda b: (0, b))],
        out_shape=[jax.ShapeDtypeStruct((nb, n_grp, LANES), BF16),
                   jax.ShapeDtypeStruct((LANES, nb * n_grp), BF16)],
        compiler_params=_params("parallel"),
        name="nsa_compress",
    )(rows, cw["pe_lo"], cw["pe_hi"], cw["w_lo"], cw["w_hi"], cw["b1"], cw["w2"], cw["b2"], cw["gk"], cos, sin, j2)


def _stack_heads(q, lane):
    parts = []
    for h in range(N_HEADS):
        blk = q[:, (h // 2) * LANES:(h // 2 + 1) * LANES]
        if h % 2 == 1:
            blk = pltpu.roll(blk, HEAD_D, 1)
        parts.append(jnp.where(lane < HEAD_D, blk, 0.0))
    return jnp.concatenate(parts, axis=0)


def _unstack_heads(parts, lane):
    b01 = jnp.where(lane < HEAD_D, pltpu.roll(parts[0], HEAD_D, 1), parts[1])
    b23 = jnp.where(lane < HEAD_D, pltpu.roll(parts[2], HEAD_D, 1), parts[3])
    return jnp.concatenate([b01, b23], axis=1)


def _softmax_cols(st, mask):
    st = jnp.where(mask, st, NEG_INF)
    m = jnp.max(st, axis=0, keepdims=True)
    e = jnp.exp(st - m)
    norm = 1.0 / jnp.maximum(jnp.sum(e, axis=0, keepdims=True), 1e-30)
    return e * jnp.where(m > 0.5 * NEG_INF, norm, 0.0)


def _nsa_attn_t_body(q_ref, gate_ref, kvc_ref, kvct_ref, slc_ref, slct_ref, win_ref, wint_ref, ovl_ref, exp_ref,
                     o_ref, acc_ref, st_ref, e_ref, *, tq, tk, n_cmp, n_sel, n_top, win_span):
    i = pl.program_id(1)
    start = i * tq
    cols = N_HEADS * tq
    lane = lax.broadcasted_iota(jnp.int32, (tq, LANES), 1)
    q4 = _stack_heads(q_ref[...] * ATT_SCALE, lane).astype(BF16)
    tlane = start + (lax.broadcasted_iota(jnp.int32, (1, cols), 1) & (tq - 1))

    kvc = kvc_ref[...]
    ng = kvc.shape[0]
    ncol = lax.broadcasted_iota(jnp.int32, (ng, 1), 0)
    cend = jnp.where(ncol < n_cmp, ncol * CMP_STRIDE + (CMP_LEN - 1), jnp.iinfo(jnp.int32).max)
    pt = _softmax_cols(_dot_nt(kvc, q4), cend <= tlane)
    o_cmp = _dot(kvct_ref[...], pt.astype(BF16))
    psum = pt[:, 0:tq]
    for h in range(1, N_HEADS):
        psum = psum + pt[:, h * tq:(h + 1) * tq]
    ph = psum.astype(BF16)
    pl_ = (psum - ph.astype(F32)).astype(BF16)
    ovl = ovl_ref[...]
    imp_t = _dot(ovl, ph) + _dot(ovl, pl_)

    start0 = pl.multiple_of(jnp.maximum(start + tq - win_span, 0), tq)
    rel = tlane - (start0 + lax.broadcasted_iota(jnp.int32, (win_span, 1), 0))
    sw = jnp.where((rel >= 0) & (rel < WINDOW), _dot_nt(win_ref[pl.ds(start0, win_span), :], q4), NEG_INF)
    mw = jnp.max(sw, axis=0, keepdims=True)
    o_win = _dot(wint_ref[:, pl.ds(start0, win_span)], jnp.exp(sw - mw).astype(BF16))
    o_win = o_win * jnp.where(mw > 0.5 * NEG_INF, 1.0 / jnp.maximum(o_win[0:1], 1e-30), 0.0)

    nsp = imp_t.shape[0]
    jblk = lax.broadcasted_iota(jnp.int32, (nsp, tq), 0)
    tcol = start + lax.broadcasted_iota(jnp.int32, (nsp, tq), 1)
    cur = tcol // SEL_BLOCK
    forced = (jblk == 0) | (jblk == cur) | (jblk == cur - 1)
    valid = (jblk * SEL_BLOCK <= tcol) & (jblk < n_sel)
    score = jnp.where(valid, imp_t + jnp.where(forced, FORCE_BONUS, 0.0), -1.0)
    taken = -3.0
    jf = jblk.astype(F32)
    left = score
    for _ in range(n_top):
        best = jnp.max(left, axis=0, keepdims=True)
        first = jnp.min(jnp.where(left == best, jf, float(nsp)), axis=0, keepdims=True)
        left = jnp.where(jf == first, taken, left)
    bias_t = jnp.where((left == taken) & (score >= 0.0), 0.0, NEG_INF)
    bias = bias_t.T.astype(BF16)
    lhs = jnp.concatenate([q4, jnp.concatenate([bias] * N_HEADS, axis=0)], axis=1)

    acc_ref[...] = jnp.zeros((LANES, cols), F32)
    n_kt = (start + tq + tk - 1) // tk

    def stage_scores(kt, causal):
        off = pl.multiple_of(kt * tk, tk)
        k_aug = jnp.concatenate([slc_ref[pl.ds(off, tk), :], exp_ref[pl.ds(off, tk), :]], axis=1)
        st = _dot_nt(k_aug, lhs)
        if causal:
            kpos = off + lax.broadcasted_iota(jnp.int32, (tk, 1), 0)
            st = jnp.where(kpos <= tlane, st, NEG_INF)
        return st, jnp.max(st.reshape(tk // SUBLANES, SUBLANES, cols), axis=0)

    def stage_exp(st, mt, m_prev):
        m_new = jnp.maximum(m_prev, jnp.max(mt, axis=0, keepdims=True))
        m_sub = jnp.where(m_new > 0.5 * NEG_INF, m_new, 0.0)
        e = jnp.exp(st - m_sub[0:1])
        return e.astype(BF16), jnp.exp(m_prev - m_new), m_new

    def stage_values(kt, e, alpha):
        off = pl.multiple_of(jnp.maximum(kt, 0) * tk, tk)
        acc_ref[...] = alpha[0:1] * acc_ref[...] + _dot(slct_ref[:, pl.ds(off, tk)], e)

    def trip(i, carry, causal):
        mt, alpha, m_run = carry
        stage_values(i - 2, e_ref[...], alpha)
        e, alpha, m_run = stage_exp(st_ref[...], mt, m_run)
        e_ref[...] = e
        st, mt = stage_scores(i, causal)
        st_ref[...] = st
        return mt, alpha, m_run

    st_ref[...] = jnp.full((tk, cols), NEG_INF, F32)
    e_ref[...] = jnp.zeros((tk, cols), BF16)
    neg8 = st_ref[0:SUBLANES, :]
    carry = (neg8, acc_ref[0:SUBLANES, :] + 1.0, neg8)
    carry = lax.fori_loop(0, n_kt - 1, functools.partial(trip, causal=False), carry)
    mt, alpha, m_run = trip(n_kt - 1, carry, True)
    stage_values(n_kt - 2, e_ref[...], alpha)
    e, alpha, _ = stage_exp(st_ref[...], mt, m_run)
    stage_values(n_kt - 1, e, alpha)
    acc = acc_ref[...]
    o_sel = acc / jnp.maximum(acc[0:1], 1e-30)

    gt = gate_ref[...].T
    outs = []
    for h in range(N_HEADS):
        cs = slice(h * tq, (h + 1) * tq)
        outs.append(gt[3 * h:3 * h + 1] * o_cmp[HEAD_D:, cs] + gt[3 * h + 1:3 * h + 2] * o_sel[HEAD_D:, cs]
                    + gt[3 * h + 2:3 * h + 3] * o_win[HEAD_D:, cs])
    o_ref[...] = jnp.concatenate(outs, axis=0).T


def _nsa_attn_t(q, gates, kvc, kvc_t, slc, slc_t, win, win_t, ovl_t, expand, tq, tk, n_cmp, n_sel):
    nb, t, _ = q.shape
    ng = kvc.shape[1]
    nsp = ovl_t.shape[0]
    cols = N_HEADS * tq
    win_span = WINDOW + tq
    per_b = lambda b, i: (b, 0, 0)
    per_b_t = lambda b, i: (0, b)
    return pl.pallas_call(
        functools.partial(_nsa_attn_t_body, tq=tq, tk=tk, n_cmp=n_cmp, n_sel=n_sel,
                          n_top=min(N_SELECT, n_sel), win_span=win_span),
        grid=(nb, t // tq),
        in_specs=[pl.BlockSpec((None, tq, GROUP_W), lambda b, i: (b, i, 0)),
                  pl.BlockSpec((None, tq, LANES), lambda b, i: (b, i, 0)),
                  pl.BlockSpec((None, ng, LANES), per_b), pl.BlockSpec((LANES, ng), per_b_t),
                  pl.BlockSpec((None, t, LANES), per_b), pl.BlockSpec((LANES, t), per_b_t),
                  pl.BlockSpec((None, t, LANES), per_b), pl.BlockSpec((LANES, t), per_b_t),
                  _const_spec((nsp, ng)), _const_spec((t, nsp))],
        out_specs=pl.BlockSpec((None, tq, GROUP_W), lambda b, i: (b, i, 0)),
        out_shape=jax.ShapeDtypeStruct((nb, t, GROUP_W), F32),
        scratch_shapes=[pltpu.VMEM((LANES, cols), F32), pltpu.VMEM((tk, cols), F32), pltpu.VMEM((tk, cols), BF16)],
        compiler_params=_params("parallel", "arbitrary"),
        name="nsa_attn",
    )(q, gates, kvc, kvc_t, slc, slc_t, win, win_t, ovl_t, expand)


def _samp_cmp_body(*refs, pages_per_step, n_grp, n_cmp, n_sel, n_top, t_pos):
    pt_ref = refs[0]
    page_refs = refs[1:1 + pages_per_step]
    (pelo_ref, pehi_ref, wlo_ref, whi_ref, b1_ref, w2_ref, b2_ref, gk_ref, cos_ref, sin_ref, j2_ref,
     q4_ref, ovl_ref, ocmp_ref, idx_ref, lo_ref, hi_ref) = refs[1 + pages_per_step:]
    del pt_ref
    s = pl.program_id(1)
    ns = pl.num_programs(1)
    grp_per_page = PAGE_SIZE // CMP_STRIDE
    rows_step = pages_per_step * grp_per_page

    @pl.when(s == 0)
    def _():
        hi_ref[n_grp:n_grp + SUBLANES, :] = jnp.zeros((SUBLANES, GROUP_W), F32)

    def read_rows(j):
        return jnp.concatenate([pg[pl.ds(j, grp_per_page, stride=CMP_STRIDE), :] for pg in page_refs], axis=0)

    lo, hi = _compress_hidden(read_rows, pelo_ref, pehi_ref, wlo_ref, whi_ref)
    off = pl.multiple_of(s * rows_step, rows_step)
    lo_ref[pl.ds(off, rows_step), :] = lo
    hi_ref[pl.ds(off, rows_step), :] = hi

    @pl.when(s == ns - 1)
    def _():
        kvc = _compress_tail(lo_ref[...], hi_ref[pl.ds(1, n_grp), :], b1_ref, w2_ref, b2_ref, gk_ref,
                             cos_ref, sin_ref, j2_ref).astype(BF16)
        q4 = q4_ref[...].astype(BF16)
        sc = _dot_nt(q4, kvc) * ATT_SCALE
        nidx = lax.broadcasted_iota(jnp.int32, sc.shape, 1)
        p = _masked_softmax_rows(sc, (nidx * CMP_STRIDE + (CMP_LEN - 1) <= t_pos) & (nidx < n_cmp))
        ocmp_ref[...] = _dot(p.astype(BF16), kvc)
        hrow = lax.broadcasted_iota(jnp.int32, p.shape, 0)
        psum = jnp.sum(jnp.where(hrow < N_HEADS, p, 0.0), axis=0, keepdims=True)
        psum8 = jnp.broadcast_to(psum, p.shape)
        ph = psum8.astype(BF16)
        pl_ = (psum8 - ph.astype(F32)).astype(BF16)
        ovl = ovl_ref[...]
        imp = (_dot(ph, ovl) + _dot(pl_, ovl))[0:1]
        nsp = imp.shape[1]
        jrow = lax.broadcasted_iota(jnp.int32, (1, nsp), 1)
        cur = t_pos // SEL_BLOCK
        forced = (jrow == 0) | (jrow == cur) | (jrow == cur - 1)
        valid = (jrow * SEL_BLOCK <= t_pos) & (jrow < n_sel)
        score = jnp.where(valid, imp + jnp.where(forced, FORCE_BONUS, 0.0), -1.0)
        score = jnp.where(jrow < n_sel, score, -2.0)
        s_row = jnp.broadcast_to(score, (nsp, nsp))
        s_col = s_row.T
        ii = lax.broadcasted_iota(jnp.int32, (nsp, nsp), 0)
        jj = lax.broadcasted_iota(jnp.int32, (nsp, nsp), 1)
        beats = (s_col > s_row) | ((s_col == s_row) & (ii < jj))
        rank = jnp.sum(jnp.where(beats, 1.0, 0.0), axis=0, keepdims=True)
        chosen = (rank < n_top) & (score >= 0.0)
        slot = lax.broadcasted_iota(jnp.int32, (N_SELECT, nsp), 0).astype(F32)
        hit = (jnp.broadcast_to(rank, (N_SELECT, nsp)) == slot) & jnp.broadcast_to(chosen, (N_SELECT, nsp))
        jcol = lax.broadcasted_iota(jnp.int32, (N_SELECT, nsp), 1).astype(F32)
        blk = jnp.sum(jnp.where(hit, jcol, 0.0), axis=1, keepdims=True)
        cnt = jnp.sum(jnp.where(hit, 1.0, 0.0), axis=1, keepdims=True)
        blk = jnp.where(cnt > 0.5, blk, -1.0)
        idx_ref[...] = jnp.broadcast_to(blk, (N_SELECT, LANES)).astype(jnp.int32)


def _samp_cmp(cache, layer, page_table, cw, cos, sin, j2, q4, ovl, n_cmp, n_sel, t_pos, pages_per_step):
    nb, n_pages = page_table.shape
    grp_per_page = PAGE_SIZE // CMP_STRIDE
    wid = CMP_STRIDE * LANES
    n_grp = n_pages * grp_per_page
    nsp = ovl.shape[1]

    def page_spec(j):
        return pl.BlockSpec((None, None, PAGE_SIZE, LANES),
                            lambda b, s, pt, j=j: (layer, pt[b, s * pages_per_step + j], 0, 0))

    def cst(shape):
        nd = len(shape)
        return pl.BlockSpec(shape, lambda b, s, pt: (0,) * nd)

    grid_spec = pltpu.PrefetchScalarGridSpec(
        num_scalar_prefetch=1,
        grid=(nb, n_pages // pages_per_step),
        in_specs=[page_spec(j) for j in range(pages_per_step)] + [
            cst((1, wid)), cst((1, wid)), cst((wid, GROUP_W)), cst((wid, GROUP_W)), cst((1, GROUP_W)),
            cst((GROUP_W, LANES)), cst((1, LANES)), cst((1, LANES)),
            cst((n_grp, LANES)), cst((n_grp, LANES)), cst((LANES, LANES)),
            pl.BlockSpec((None, SUBLANES, LANES), lambda b, s, pt: (b, 0, 0)), cst((n_grp, nsp))],
        out_specs=[pl.BlockSpec((None, SUBLANES, LANES), lambda b, s, pt: (b, 0, 0)),
                   pl.BlockSpec((None, N_SELECT, LANES), lambda b, s, pt: (b, 0, 0))],
        scratch_shapes=[pltpu.VMEM((n_grp, GROUP_W), F32), pltpu.VMEM((n_grp + SUBLANES, GROUP_W), F32)],
    )
    return pl.pallas_call(
        functools.partial(_samp_cmp_body, pages_per_step=pages_per_step, n_grp=n_grp, n_cmp=n_cmp,
                          n_sel=n_sel, n_top=min(N_SELECT, n_sel), t_pos=t_pos),
        grid_spec=grid_spec,
        out_shape=[jax.ShapeDtypeStruct((nb, SUBLANES, LANES), F32),
                   jax.ShapeDtypeStruct((nb, N_SELECT, LANES), jnp.int32)],
        compiler_params=_params("parallel", "arbitrary"),
        name="nsa_decode_compress",
    )(page_table, *([cache] * pages_per_step), cw["pe_lo"], cw["pe_hi"], cw["w_lo"], cw["w_hi"], cw["b1"],
      cw["w2"], cw["b2"], cw["gk"], cos, sin, j2, q4, ovl)


def _samp_sel_body(pt_ref, idx_ref, *refs, n_past_blk, t_pos, win_pos0):
    del pt_ref
    blk_refs = refs[:N_SELECT]
    q4_ref, nslc_ref, win_ref, nwin_ref, gate_ref, ocmp_ref, o_ref = refs[N_SELECT:]
    b = pl.program_id(0)
    q4 = q4_ref[...].astype(BF16)

    n_keys = N_SELECT * SEL_BLOCK
    lane_k = lax.broadcasted_iota(jnp.int32, (1, n_keys), 1)
    jvec = jnp.zeros((1, n_keys), jnp.int32)
    parts = []
    for k in range(N_SELECT):
        j = idx_ref[b, k]
        blk = blk_refs[k][...]
        parts.append(jnp.where(j == n_past_blk, nslc_ref[...], blk[0:SUBLANES]))
        parts.append(blk[SUBLANES:])
        jvec = jnp.where(lane_k // SEL_BLOCK == k, j, jvec)
    kv = jnp.concatenate(parts, axis=0).astype(BF16)
    kpos = jvec * SEL_BLOCK + (lane_k & (SEL_BLOCK - 1))
    p_sel = _masked_softmax_rows(_dot_nt(q4, kv) * ATT_SCALE, (kpos <= t_pos) & (jvec >= 0))
    o_sel = _dot(p_sel.astype(BF16), kv)

    wb = win_ref[...].astype(BF16)
    nw = nwin_ref[...].astype(BF16)
    s_w = _dot_nt(q4, wb) * ATT_SCALE
    s_n = _dot_nt(q4, nw) * ATT_SCALE
    wpos = win_pos0 + lax.broadcasted_iota(jnp.int32, s_w.shape, 1)
    rel = t_pos - wpos
    m_w = (rel >= 0) & (rel < WINDOW) & (wpos >= win_pos0)
    m_n = lax.broadcasted_iota(jnp.int32, s_n.shape, 1) == 0
    s_w = jnp.where(m_w, s_w, NEG_INF)
    s_n = jnp.where(m_n, s_n, NEG_INF)
    mx = jnp.maximum(jnp.max(s_w, axis=-1, keepdims=True), jnp.max(s_n, axis=-1, keepdims=True))
    e_w = jnp.exp(s_w - mx) * m_w.astype(F32)
    e_n = jnp.exp(s_n - mx) * m_n.astype(F32)
    den = jnp.sum(e_w, axis=-1, keepdims=True) + jnp.sum(e_n, axis=-1, keepdims=True)
    o_win = (_dot(e_w.astype(BF16), wb) + _dot(e_n.astype(BF16), nw)) / jnp.maximum(den, 1e-30)
    o_cmp = ocmp_ref[...]
    g = gate_ref[...]
    lane = lax.broadcasted_iota(jnp.int32, (1, LANES), 1)
    outs = []
    for h in range(N_HEADS):
        outs.append(g[0:1, 3 * h:3 * h + 1] * o_cmp[h:h + 1] + g[0:1, 3 * h + 1:3 * h + 2] * o_sel[h:h + 1]
                    + g[0:1, 3 * h + 2:3 * h + 3] * o_win[h:h + 1])
    o_ref[...] = jnp.broadcast_to(_unstack_heads(outs, lane), o_ref.shape)


def _samp_sel(cache_blk, layer, page_table, top_idx, q4, nslc8, win, nwin8, gates8, ocmp, t_pos, win_pos0):
    nb, n_pages = page_table.shape
    blk_per_page = PAGE_SIZE // SEL_BLOCK
    n_past_blk = n_pages * blk_per_page
    wlen = win.shape[2]

    def blk_spec(k):
        def blk_map(b, pt, idx):
            j = jnp.clip(idx[b, k], 0, n_past_blk - 1)
            return (layer, pt[b, j // blk_per_page], j % blk_per_page, 0, 0)
        return pl.BlockSpec((None, None, None, SEL_BLOCK, LANES), blk_map)

    row8 = pl.BlockSpec((None, SUBLANES, LANES), lambda b, pt, idx: (b, 0, 0))
    grid_spec = pltpu.PrefetchScalarGridSpec(
        num_scalar_prefetch=2,
        grid=(nb,),
        in_specs=[blk_spec(k) for k in range(N_SELECT)] + [
            row8, row8, pl.BlockSpec((None, None, wlen, LANES), lambda b, pt, idx: (layer, b, 0, 0)),
            row8, row8, row8],
        out_specs=pl.BlockSpec((None, SUBLANES, GROUP_W), lambda b, pt, idx: (b, 0, 0)),
    )
    return pl.pallas_call(
        functools.partial(_samp_sel_body, n_past_blk=n_past_blk, t_pos=t_pos, win_pos0=win_pos0),
        grid_spec=grid_spec,
        out_shape=jax.ShapeDtypeStruct((nb, SUBLANES, GROUP_W), F32),
        compiler_params=_params("parallel"),
        name="nsa_decode_select",
    )(page_table, top_idx, *([cache_blk] * N_SELECT), q4, nslc8, win, nwin8, gates8, ocmp)


def _block_diag(blocks):
    n, a, b = blocks.shape
    eye = jnp.eye(n, dtype=blocks.dtype)
    return jnp.einsum("nab,nm->namb", blocks, eye).reshape(n * a, n * b)


def _rope_tables(pos):
    half = HEAD_D // 2
    inv = ROPE_THETA ** (-jnp.arange(half, dtype=F32) / half)
    ang = pos.astype(F32)[:, None] * inv[None, :]
    cos, sin = jnp.cos(ang), jnp.sin(ang)
    cos128 = jnp.concatenate([cos, cos, cos, cos], axis=1)
    sin128 = jnp.concatenate([-sin, sin, -sin, sin], axis=1)
    return cos128, sin128


def _seg_ones(n_seg):
    return _block_diag(jnp.ones((n_seg, HEAD_D, HEAD_D), F32)).astype(BF16)


def _w_in_stack(w_in):
    a_w, b_w, c_w = 2 * GROUP_W, 4 * GROUP_W, 3 * GROUP_W
    return jnp.concatenate(
        [w_in[..., a_w:a_w + b_w], w_in[..., :a_w], w_in[..., a_w + b_w:a_w + b_w + c_w], w_in[..., a_w + b_w + c_w:],
         jnp.zeros(w_in.shape[:-1] + (PROJ_PAD - w_in.shape[-1],), F32)], axis=-1).astype(BF16)


def _layer_weights(lp):
    lora = HEAD_D
    w1 = lp["nsa_cmp_w1"].reshape(2, 2, CMP_STRIDE, HEAD_D, 2 * HEAD_D)
    pe = lp["nsa_cmp_pe"].reshape(2, 2, CMP_STRIDE, HEAD_D)

    def cmp_half(hf):
        wk = jnp.zeros((CMP_STRIDE, 2 * HEAD_D, 4 * HEAD_D), F32)
        wk = wk.at[:, :HEAD_D, :2 * HEAD_D].set(w1[0, hf]).at[:, HEAD_D:, 2 * HEAD_D:].set(w1[1, hf])
        pk = jnp.concatenate([pe[0, hf], pe[1, hf]], axis=1)
        return wk.reshape(CMP_STRIDE * 2 * HEAD_D, 4 * HEAD_D).astype(BF16), pk.reshape(1, CMP_STRIDE * 2 * HEAD_D)

    w_lo, pe_lo = cmp_half(0)
    w_hi, pe_hi = cmp_half(1)
    ones64 = jnp.ones((HEAD_D,), F32)
    cw = dict(
        w_lo=w_lo, w_hi=w_hi, pe_lo=pe_lo, pe_hi=pe_hi,
        b1=lp["nsa_cmp_b1"].reshape(1, 4 * HEAD_D),
        w2=_block_diag(lp["nsa_cmp_w2"]).astype(BF16),
        b2=lp["nsa_cmp_b2"].reshape(1, 2 * HEAD_D),
        gk=jnp.concatenate([lp["nsa_norm_k"][0], ones64]).reshape(1, LANES),
    )
    zeros_lora = jnp.zeros((lora, GROUP_W), F32)
    return dict(
        norm_ffn=lp["norm_ffn"].reshape(2, 1, D_MODEL),
        norm_mix=lp["norm_mix"].reshape(1, D_MODEL),
        lru_cw=lp["lru_conv_w"], lru_cb=lp["lru_conv_b"].reshape(1, GROUP_W),
        lru_wri=jnp.concatenate([_block_diag(lp["lru_w_r"]), _block_diag(lp["lru_w_i"])], axis=1).astype(BF16),
        lru_bri=jnp.concatenate([lp["lru_b_r"], lp["lru_b_i"]]).reshape(1, 2 * GROUP_W),
        lru_lam=lp["lru_lambda"].reshape(1, GROUP_W), sconv_w=lp["sconv_w"],
        mu=lp["rwkv_mu"].reshape(1, 4 * GROUP_W), w0=lp["rwkv_w0"].reshape(1, GROUP_W),
        w2p=jnp.concatenate([lp["rwkv_w2"], zeros_lora], axis=0).astype(BF16),
        a0=lp["rwkv_a0"].reshape(1, GROUP_W),
        a2p=jnp.concatenate([zeros_lora, lp["rwkv_a2"]], axis=0).astype(BF16),
        g2=lp["rwkv_g2"].astype(BF16),
        k_k=lp["rwkv_k_k"].reshape(1, GROUP_W), k_a=lp["rwkv_k_a"].reshape(1, GROUP_W),
        r_k=lp["rwkv_r_k"].reshape(1, GROUP_W),
        ln_g=lp["rwkv_ln_g"].reshape(1, GROUP_W), ln_b=lp["rwkv_ln_b"].reshape(1, GROUP_W),
        gq=jnp.tile(lp["nsa_norm_q"], 2).reshape(1, LANES),
        gk2=jnp.stack([jnp.concatenate([lp["nsa_norm_k"][1], ones64]),
                       jnp.concatenate([lp["nsa_norm_k"][2], ones64])]),
        cw=cw,
        out_norm=lp["out_norm"].reshape(N_HEADS, GROUP_W),
    )


def _overlap(n_grp, n_sel_pad, n_sel):
    cs = jnp.arange(n_grp)[:, None] * CMP_STRIDE
    js = jnp.arange(n_sel_pad)[None, :] * SEL_BLOCK
    ov = (cs < js + SEL_BLOCK) & (cs + CMP_LEN > js) & (jnp.arange(n_sel_pad)[None, :] < n_sel)
    return ov.astype(BF16)


def _pad_rows_front(x, rows):
    return jnp.pad(x, ((0, 0), (rows - x.shape[1], 0), (0, 0)))


def _round_up(x, m):
    return (x + m - 1) // m * m


def _mixers_recurrent(lw, proj3, cbuf8, h0, sbuf8, shift8, s_heads, j4, tt, chunk, pos0, t_valid):
    ya, yc, hlast, ulast = _lru_sconv(proj3, cbuf8, h0, sbuf8, lw["lru_cw"], lw["lru_cb"], lw["lru_wri"],
                                      lw["lru_bri"], lw["lru_lam"], lw["sconv_w"], tt, pos0)
    yb, s_out = _rwkv(proj3, shift8, s_heads, lw["mu"], lw["w0"], lw["w2p"], lw["a0"], lw["a2p"], lw["g2"],
                      lw["k_k"], lw["k_a"], lw["r_k"], lw["ln_g"], lw["ln_b"], j4, tt, chunk, t_valid,
                      n_seq=math.gcd(proj3.shape[0], max(1, RWKV_CHAINS * chunk // tt)))
    return ya, yb, yc, hlast, ulast, s_out


def _prompt_layer(lw, layer, x, nb, t, consts):
    m = nb * t
    tm = 512
    j2, j4 = consts["j2"], consts["j4"]
    ffn_w = consts["ffn_w"]
    x1 = _ffn(x, lw["norm_ffn"][0], ffn_w, layer, 0, tm)
    proj, q_r, ncmp, nslc, nwin, gates, slc_bf, win_bf, slc_t, win_t = _proj_prep(
        x1, lw["norm_mix"], consts["w_in"], layer, consts["cos_p"], consts["sin_p"], lw["gq"], lw["gk2"], j2, tm,
        rows_per_table=t, with_bf16=True)
    proj3 = proj.reshape(nb, t, PROJ_KEEP)
    zeros8 = jnp.zeros((nb, SUBLANES, GROUP_W), F32)
    ya, yb, yc, hlast, ulast, s_out = _mixers_recurrent(
        lw, proj3, zeros8, jnp.zeros((nb, 1, GROUP_W), F32), zeros8,
        jnp.zeros((nb, SUBLANES, 4 * GROUP_W), F32), jnp.zeros((nb, N_HEADS, HEAD_D, HEAD_D), F32), j4,
        tt=256, chunk=HEAD_D, pos0=0, t_valid=t)
    kvc, kvc_t = _compress(ncmp.reshape(nb, t, LANES), lw["cw"], consts["cos_cp"], consts["sin_cp"], j2)
    n_cmp = (t - CMP_LEN) // CMP_STRIDE + 1
    n_sel = -(-t // SEL_BLOCK)
    yd = _nsa_attn_t(q_r.reshape(nb, t, GROUP_W), gates.reshape(nb, t, LANES), kvc, kvc_t,
                     slc_bf.reshape(nb, t, LANES), slc_t, win_bf.reshape(nb, t, LANES), win_t,
                     consts["ovl_t_p"], consts["expand_p"], tq=256, tk=512, n_cmp=n_cmp, n_sel=n_sel)
    x3 = _ffn(x1, lw["norm_ffn"][1], ffn_w, layer, 1, tm,
              mix=(ya.reshape(m, GROUP_W), yb.reshape(m, GROUP_W), yc.reshape(m, GROUP_W), yd.reshape(m, GROUP_W),
                   lw["out_norm"], consts["w_out"]))
    xa = proj3[:, :, COL_XA * GROUP_W:(COL_XA + 1) * GROUP_W]
    wlen = min(WINDOW, t)
    states = (hlast[:, SUBLANES - 1], xa[:, t - 3:], s_out, proj3[:, t - 1, :4 * GROUP_W],
              ulast[:, SUBLANES - 2:], nwin.reshape(nb, t, LANES)[:, t - wlen:],
              ncmp.reshape(nb, t // PAGE_SIZE, PAGE_SIZE, LANES), nslc.reshape(nb, t // PAGE_SIZE, PAGE_SIZE, LANES))
    return x3, states


def _sample_layer(lw, layer, x, st, caches, page_table, consts):
    nb = x.shape[0]
    n_pages = page_table.shape[1]
    past_len = n_pages * PAGE_SIZE
    j2, j4 = consts["j2"], consts["j4"]
    lru_h, lru_conv, rwkv_s, rwkv_shift, sconv, win = st
    cache_cmp, cache_slc = caches
    ffn_w = consts["ffn_w"]
    x1 = _ffn(x, lw["norm_ffn"][0], ffn_w, layer, 0, nb)
    proj, q_r, ncmp, nslc, nwin, gates = _proj_prep(
        x1, lw["norm_mix"], consts["w_in"], layer, consts["cos_s"], consts["sin_s"], lw["gq"], lw["gk2"], j2, nb,
        rows_per_table=nb, with_bf16=False)
    proj3 = jnp.pad(proj[:, None, :], ((0, 0), (0, SUBLANES - 1), (0, 0)))
    ya, yb, yc, hlast, ulast, s_out = _mixers_recurrent(
        lw, proj3, _pad_rows_front(lru_conv, SUBLANES), lru_h[:, None, :], _pad_rows_front(sconv, SUBLANES),
        _pad_rows_front(rwkv_shift[:, None, :], SUBLANES), rwkv_s, j4,
        tt=SUBLANES, chunk=SUBLANES, pos0=past_len, t_valid=1)

    def row8(a):
        return jnp.pad(a[:, None, :], ((0, 0), (0, SUBLANES - 1), (0, 0)))

    q4 = jnp.pad(q_r.reshape(nb, N_HEADS, HEAD_D), ((0, 0), (0, SUBLANES - N_HEADS), (0, LANES - HEAD_D)))
    depth, n_phys = cache_cmp.shape[:2]
    n_cmp = (past_len + 1 - CMP_LEN) // CMP_STRIDE + 1
    n_sel = -(-(past_len + 1) // SEL_BLOCK)
    ocmp, top_idx = _samp_cmp(cache_cmp, layer, page_table, lw["cw"], consts["cos_cs"], consts["sin_cs"], j2, q4,
                              consts["ovl_s"], n_cmp, n_sel, t_pos=past_len,
                              pages_per_step=consts["pages_per_step"])
    wlen = win.shape[1]
    yd8 = _samp_sel(cache_slc.reshape(depth, n_phys, PAGE_SIZE // SEL_BLOCK, SEL_BLOCK, LANES), layer, page_table,
                    top_idx[:, :, 0], q4, row8(nslc), consts["win_all"], row8(nwin), row8(gates), ocmp,
                    t_pos=past_len, win_pos0=past_len - wlen)
    x3 = _ffn(x1, lw["norm_ffn"][1], ffn_w, layer, 1, nb,
              mix=(ya[:, 0], yb[:, 0], yc[:, 0], yd8[:, 0], lw["out_norm"], consts["w_out"]))
    xa = proj[:, COL_XA * GROUP_W:(COL_XA + 1) * GROUP_W]
    new_win = jnp.concatenate([win, nwin[:, None, :]], axis=1)
    states = (hlast[:, 0], jnp.concatenate([lru_conv[:, 1:], xa[:, None, :]], axis=1), s_out,
              proj[:, :4 * GROUP_W], jnp.concatenate([sconv[:, 1:], ulast[:, 0:1]], axis=1),
              new_win[:, -min(WINDOW, wlen + 1):], ncmp[:, None, :], nslc[:, None, :])
    return x3, states


def kernel(x_prompt, x_sample, state_lru_h, state_lru_conv, state_rwkv_S, state_rwkv_shift, state_sconv, state_nsa_win, cache_nsa_cmp, cache_nsa_slc, page_table, norm_ffn, ffn_w_gate, ffn_w_up, ffn_w_down, norm_mix, w_in, lru_conv_w, lru_conv_b, lru_w_r, lru_b_r, lru_w_i, lru_b_i, lru_lambda, rwkv_mu, rwkv_w0, rwkv_w2, rwkv_a0, rwkv_a2, rwkv_g2, rwkv_k_k, rwkv_k_a, rwkv_r_k, rwkv_ln_g, rwkv_ln_b, sconv_w, nsa_norm_q, nsa_norm_k, nsa_cmp_pe, nsa_cmp_w1, nsa_cmp_b1, nsa_cmp_w2, nsa_cmp_b2, out_norm, w_out):
    params = dict(norm_ffn=norm_ffn, norm_mix=norm_mix, lru_conv_w=lru_conv_w, lru_conv_b=lru_conv_b, lru_w_r=lru_w_r,
                  lru_b_r=lru_b_r, lru_w_i=lru_w_i, lru_b_i=lru_b_i, lru_lambda=lru_lambda, rwkv_mu=rwkv_mu,
                  rwkv_w0=rwkv_w0, rwkv_w2=rwkv_w2, rwkv_a0=rwkv_a0, rwkv_a2=rwkv_a2, rwkv_g2=rwkv_g2,
                  rwkv_k_k=rwkv_k_k, rwkv_k_a=rwkv_k_a, rwkv_r_k=rwkv_r_k, rwkv_ln_g=rwkv_ln_g,
                  rwkv_ln_b=rwkv_ln_b, sconv_w=sconv_w, nsa_norm_q=nsa_norm_q, nsa_norm_k=nsa_norm_k,
                  nsa_cmp_pe=nsa_cmp_pe, nsa_cmp_w1=nsa_cmp_w1, nsa_cmp_b1=nsa_cmp_b1, nsa_cmp_w2=nsa_cmp_w2,
                  nsa_cmp_b2=nsa_cmp_b2, out_norm=out_norm)
    depth = norm_mix.shape[0]
    bp, tp, d = x_prompt.shape
    bs, ts, _ = x_sample.shape
    n_pages = page_table.shape[1]
    past_len = n_pages * PAGE_SIZE
    assert d == D_MODEL and ts == 1 and tp % 512 == 0 and tp >= WINDOW + 256 and past_len >= WINDOW
    assert bs % SUBLANES == 0 or bs < SUBLANES

    n_grp_p = tp // CMP_STRIDE
    n_sel_p = -(-tp // SEL_BLOCK)
    nsp_p = _round_up(n_sel_p, LANES)
    n_grp_s = past_len // CMP_STRIDE
    n_sel_s = -(-(past_len + 1) // SEL_BLOCK)
    nsp_s = _round_up(n_sel_s, LANES)
    cos_p, sin_p = _rope_tables(jnp.arange(tp))
    cos_s, sin_s = _rope_tables(jnp.full((bs,), past_len))
    cos_cp, sin_cp = _rope_tables(jnp.arange(n_grp_p) * CMP_STRIDE + CMP_LEN - 1)
    cos_cs, sin_cs = _rope_tables(jnp.arange(n_grp_s) * CMP_STRIDE + CMP_LEN - 1)
    expand_p = ((jnp.arange(tp)[:, None] // SEL_BLOCK) == jnp.arange(nsp_p)[None, :]).astype(BF16)
    pages_per_step = math.gcd(n_pages, 64)
    consts = dict(
        j2=_seg_ones(2), j4=_seg_ones(N_HEADS), cos_p=cos_p, sin_p=sin_p, cos_s=cos_s, sin_s=sin_s,
        cos_cp=cos_cp, sin_cp=sin_cp, cos_cs=cos_cs, sin_cs=sin_cs,
        ovl_t_p=_overlap(n_grp_p, nsp_p, n_sel_p).T,
        expand_p=expand_p,
        ovl_s=_overlap(n_grp_s, nsp_s, n_sel_s),
        pages_per_step=pages_per_step,
        ffn_w=(ffn_w_gate.astype(BF16), ffn_w_up.astype(BF16), ffn_w_down.astype(BF16)),
        w_out=w_out.astype(BF16),
        w_in=_w_in_stack(w_in),
        win_all=state_nsa_win,
    )

    yp = x_prompt.reshape(bp * tp, d)
    ys = x_sample.reshape(bs, d)
    sp_all, ss_all = [], []
    for l in range(depth):
        lw = _layer_weights({name: arr[l] for name, arr in params.items()})
        yp, sp = _prompt_layer(lw, l, yp, bp, tp, consts)
        ys, ss = _sample_layer(lw, l, ys, (state_lru_h[l], state_lru_conv[l], state_rwkv_S[l], state_rwkv_shift[l],
                                           state_sconv[l], state_nsa_win[l]),
                               (cache_nsa_cmp, cache_nsa_slc), page_table, consts)
        sp_all.append(sp)
        ss_all.append(ss)
    outs = [yp.reshape(bp, tp, d), ys.reshape(bs, ts, d)]
    for i in range(8):
        outs.append(jnp.stack([s[i] for s in sp_all]))
        outs.append(jnp.stack([s[i] for s in ss_all]))
    return tuple(outs)
```

```python
import functools
import math

import jax
import jax.numpy as jnp
from jax import lax
from jax.experimental import pallas as pl
from jax.experimental.pallas import tpu as pltpu

F32 = jnp.float32
BF16 = jnp.bfloat16

D_MODEL = 1024
GROUP_W = 256
N_HEADS = 4
HEAD_D = 64
LRU_C = 8.0
RWKV_LN_EPS = 64e-5
CMP_LEN = 32
CMP_STRIDE = 16
SEL_BLOCK = 64
N_SELECT = 16
WINDOW = 512
FORCE_BONUS = 1e4
NEG_INF = -1e30
ROPE_THETA = 10000.0
EPS = 1e-6
PAGE_SIZE = 128
PROJ_PAD = 3072
PROJ_KEEP = 2304
ATT_SCALE = HEAD_D ** -0.5
RWKV_CHAINS = 8

SUBLANES = 8
LANES = 128
VMEM_LIMIT = 56 * 1024 * 1024

COL_XA, COL_GA, COL_BG, COL_CG, COL_XIN = 4, 5, 6, 7, 8


def _params(*sem):
    return pltpu.CompilerParams(dimension_semantics=sem, vmem_limit_bytes=VMEM_LIMIT)


def _const_spec(shape):
    nd = len(shape)
    return pl.BlockSpec(shape, lambda *_: (0,) * nd)


def _dot(a, b):
    return jnp.dot(a, b, preferred_element_type=F32)


def _dot_nt(a, b):
    return lax.dot_general(a, b, (((1,), (1,)), ((), ())), preferred_element_type=F32)


def _dot_tn(a, b):
    return lax.dot_general(a, b, (((0,), (0,)), ((), ())), preferred_element_type=F32)


def _split3(x):
    h1 = x.astype(BF16)
    r1 = x - h1.astype(F32)
    h2 = r1.astype(BF16)
    h3 = (r1 - h2.astype(F32)).astype(BF16)
    return h1, h2, h3


def _dot_exact_rhs(x, m_bf16):
    h1, h2, h3 = _split3(x)
    return _dot(h1, m_bf16) + _dot(h2, m_bf16) + _dot(h3, m_bf16)


def _gelu_tanh(x):
    return x * (0.5 * (1.0 + jnp.tanh(math.sqrt(2.0 / math.pi) * (x + 0.044715 * (x * x * x)))))


def _sigmoid(x):
    return 1.0 / (1.0 + jnp.exp(-x))


def _softplus(x):
    return jnp.maximum(x, 0.0) + jnp.log1p(jnp.exp(-jnp.abs(x)))


def _rmsnorm_rows(x, g):
    ms = jnp.mean(x * x, axis=-1, keepdims=True)
    return x * lax.rsqrt(ms + EPS) * g


def _masked_softmax_rows(s, mask):
    s = jnp.where(mask, s, NEG_INF)
    m = jnp.max(s, axis=-1, keepdims=True)
    e = jnp.exp(s - m)
    norm = 1.0 / jnp.maximum(jnp.sum(e, axis=-1, keepdims=True), 1e-30)
    return e * jnp.where(m > 0.5 * NEG_INF, norm, 0.0)


def _rope_lanes(x, cos, sin_signed, lane):
    swapped = jnp.where((lane & (HEAD_D - 1)) < HEAD_D // 2,
                        pltpu.roll(x, LANES - HEAD_D // 2, 1), pltpu.roll(x, HEAD_D // 2, 1))
    return x * cos + swapped * sin_signed


def _ffn_body(*refs, f_chunk, n_mix):
    x_ref = refs[0]
    y_refs = refs[1:1 + n_mix]
    if n_mix:
        gm_ref, wo_ref = refs[1 + n_mix:3 + n_mix]
    g_ref, wg_ref, wu_ref, wd_ref, o_ref, act_ref = refs[1 + n_mix + (2 if n_mix else 0):]
    x = x_ref[...]
    for gi, y_ref in enumerate(y_refs):
        yn = _rmsnorm_rows(y_ref[...], gm_ref[gi:gi + 1, :]).astype(BF16)
        x = x + _dot(yn, wo_ref[gi * GROUP_W:(gi + 1) * GROUP_W, :])
    h = _rmsnorm_rows(x, g_ref[...]).astype(BF16)
    d_ff = wg_ref.shape[1]
    for c in range(d_ff // f_chunk):
        sl = slice(c * f_chunk, (c + 1) * f_chunk)
        gate = _dot(h, wg_ref[:, sl])
        up = _dot(h, wu_ref[:, sl])
        act_ref[:, sl] = (gate * _sigmoid(gate) * up).astype(BF16)
    o_ref[...] = x + 0.5 * _dot(act_ref[...], wd_ref[...])


def _ffn(x, g, ffn_w, layer, which, tm, mix=None):
    m, d = x.shape
    wg, wu, wd = ffn_w
    d_ff = wg.shape[3]
    once = pl.Buffered(1)

    def wspec(a, b):
        return pl.BlockSpec((None, None, a, b), lambda i: (layer, which, 0, 0), pipeline_mode=once)

    mix_specs, mix_args = [], []
    if mix is not None:
        yspec = pl.BlockSpec((tm, GROUP_W), lambda i: (i, 0))
        mix_specs = [yspec] * 4 + [_const_spec((N_HEADS, GROUP_W)),
                                   pl.BlockSpec((None, N_HEADS * GROUP_W, d), lambda i: (layer, 0, 0),
                                                pipeline_mode=once)]
        mix_args = list(mix)
    return pl.pallas_call(
        functools.partial(_ffn_body, f_chunk=256, n_mix=4 if mix is not None else 0),
        grid=(m // tm,),
        in_specs=[pl.BlockSpec((tm, d), lambda i: (i, 0))] + mix_specs
        + [_const_spec((1, d)), wspec(d, d_ff), wspec(d, d_ff), wspec(d_ff, d)],
        out_specs=pl.BlockSpec((tm, d), lambda i: (i, 0)),
        out_shape=jax.ShapeDtypeStruct((m, d), F32),
        scratch_shapes=[pltpu.VMEM((tm, d_ff), BF16)],
        compiler_params=_params("parallel"),
        name="ffn_mix" if mix is not None else "ffn",
    )(x, *mix_args, g, wg, wu, wd)


def _lru_sconv_tile(t, xa_ref, ga_ref, bg_ref, cg_ref, xin_ref,
                    cw_ref, cb_ref, wri_ref, bri_ref, lam_ref, sw_ref,
                    ya_ref, yc_ref, hlast_ref, ulast_ref,
                    extx_ref, extu_ref, h_ref, *, tt, pos0):
    x = xa_ref[...]
    extx_ref[SUBLANES:SUBLANES + tt, :] = x
    cw = cw_ref[...]
    xc = extx_ref[pl.ds(SUBLANES - 3, tt), :] * cw[0:1]
    xc = xc + extx_ref[pl.ds(SUBLANES - 2, tt), :] * cw[1:2]
    xc = xc + extx_ref[pl.ds(SUBLANES - 1, tt), :] * cw[2:3]
    xc = xc + x * cw[3:4]
    xc = xc + cb_ref[...]
    extx_ref[0:SUBLANES, :] = x[tt - SUBLANES:tt]

    gates = _dot(xc.astype(BF16), wri_ref[...]) + bri_ref[...]
    r = _sigmoid(gates[:, :GROUP_W])
    ig = _sigmoid(gates[:, GROUP_W:])
    log_a = -LRU_C * r * _softplus(-lam_ref[...])
    a = jnp.exp(log_a)
    rows = lax.broadcasted_iota(jnp.int32, (tt, GROUP_W), 0)
    mult = jnp.where(rows + (pos0 + t * tt) == 0, 1.0, jnp.sqrt(1.0 - jnp.exp(2.0 * log_a)))
    b = mult * ig * xc

    s = 1
    while s < tt:
        keep = rows >= s
        a_sh = jnp.where(keep, pltpu.roll(a, s, 0), 1.0)
        b_sh = jnp.where(keep, pltpu.roll(b, s, 0), 0.0)
        b = a * b_sh + b
        a = a * a_sh
        s *= 2
    h = a * h_ref[...] + b
    h_ref[...] = h[tt - 1:tt]
    ya_ref[...] = _gelu_tanh(ga_ref[...]) * h

    u = cg_ref[...] * xin_ref[...]
    extu_ref[SUBLANES:SUBLANES + tt, :] = u
    sw = sw_ref[...]
    yv = extu_ref[pl.ds(SUBLANES - 2, tt), :] * sw[0:1]
    yv = yv + extu_ref[pl.ds(SUBLANES - 1, tt), :] * sw[1:2]
    yv = yv + u * sw[2:3]
    yc_ref[...] = bg_ref[...] * yv
    extu_ref[0:SUBLANES, :] = u[tt - SUBLANES:tt]
    hlast_ref[...] = h[tt - SUBLANES:tt]
    ulast_ref[...] = u[tt - SUBLANES:tt]


def _rwkv_tokens(t, p_ref, mu_ref, w0_ref, w2_ref, a0_ref, a2_ref, g2_ref, kk_ref, ka_ref, rk_ref, j4,
                 ext_ref, w_buf, r_buf, k_buf, v_buf, a_buf, b_buf, g_buf, bon_buf, *, tt, chunk, t_valid):
    w = GROUP_W
    p = p_ref[...]
    ext_ref[SUBLANES:SUBLANES + tt, :] = p
    m = p + (ext_ref[pl.ds(SUBLANES - 1, tt), :] - p) * mu_ref[...]
    ext_ref[0:SUBLANES, :] = p[tt - SUBLANES:tt]
    r = m[:, 0:w]
    k = m[:, w:2 * w]
    v = m[:, 2 * w:3 * w]
    wa = m[:, 3 * w:3 * w + LANES]
    gl = m[:, 3 * w + LANES:4 * w]
    wlin = w0_ref[...] + _dot(jnp.tanh(wa).astype(BF16), w2_ref[...])
    log_decay = -jnp.exp(-_softplus(-wlin) - 0.5)
    ag = _sigmoid(a0_ref[...] + _dot(wa.astype(BF16), a2_ref[...]))
    g_buf[...] = _dot(_sigmoid(gl).astype(BF16), g2_ref[...])
    kk = k * kk_ref[...]
    kk = kk * lax.rsqrt(jnp.maximum(_dot_exact_rhs(kk * kk, j4), 1e-24))
    kf = k * (1.0 + (ag - 1.0) * ka_ref[...])
    rows = lax.broadcasted_iota(jnp.int32, (tt, w), 0)
    if t_valid % tt != 0:
        live = rows + t * tt < t_valid
        log_decay = jnp.where(live, log_decay, 0.0)
        kk = jnp.where(live, kk, 0.0)
        kf = jnp.where(live, kf, 0.0)
        v = jnp.where(live, v, 0.0)
    bon_buf[...] = _dot_exact_rhs(r * kf * rk_ref[...], j4) * v

    rin = rows & (chunk - 1)
    cl = log_decay
    s = 1
    while s < chunk:
        cl = cl + jnp.where(rin >= s, pltpu.roll(cl, s, 0), 0.0)
        s *= 2
    e_neg = jnp.exp(-cl)
    w_buf[...] = cl
    r_buf[...] = r * jnp.exp(cl)
    a_buf[...] = -kk * jnp.exp(cl - log_decay)
    b_buf[...] = kk * ag * e_neg
    k_buf[...] = kf * e_neg
    v_buf[...] = v


def _rwkv_body(p_ref, shift_ref, sin_ref, mu_ref, w0_ref, w2_ref, a0_ref, a2_ref, g2_ref, kk_ref, ka_ref,
                rk_ref, lng_ref, lnb_ref, j4_ref, y_ref, sout_ref,
                ext_ref, s_ref, w_buf, r_buf, k_buf, v_buf, a_buf, b_buf, g_buf, bon_buf, y_buf,
                *, n_seq, tt, chunk, t_valid, side_work=None):
    t = pl.program_id(1)
    w = GROUP_W
    nh = N_HEADS
    nc = tt // chunk
    j4 = j4_ref[...]

    heads = [slice(h * HEAD_D, (h + 1) * HEAD_D) for h in range(nh)]

    @pl.when(t == 0)
    def _():
        for i in range(n_seq):
            ext_ref[i, 0:SUBLANES, :] = shift_ref[i]
            s_ref[i] = jnp.zeros((w, w), F32)
            for hs in heads:
                s_ref[i, hs, hs] = sin_ref[i, hs, :]

    for i in range(n_seq):
        _rwkv_tokens(t, p_ref.at[i], mu_ref, w0_ref, w2_ref, a0_ref, a2_ref, g2_ref, kk_ref, ka_ref, rk_ref, j4,
                     ext_ref.at[i], w_buf.at[i], r_buf.at[i], k_buf.at[i], v_buf.at[i], a_buf.at[i], b_buf.at[i],
                     g_buf.at[i], bon_buf.at[i], tt=tt, chunk=chunk, t_valid=t_valid)
    if side_work is not None:
        side_work(t)

    sl = nh * chunk
    hm_rows = lax.broadcasted_iota(jnp.int32, (sl, w), 0) // chunk
    hm_cols = lax.broadcasted_iota(jnp.int32, (sl, w), 1) // HEAD_D
    head_mask = (hm_rows == hm_cols).astype(F32)
    ri = lax.broadcasted_iota(jnp.int32, (chunk, chunk), 0)
    ci = lax.broadcasted_iota(jnp.int32, (chunk, chunk), 1)
    strict = (ri > ci).astype(F32)
    incl = (ri >= ci).astype(F32)
    eye = (ri == ci).astype(F32)

    def stacked(buf, i, c):
        return jnp.concatenate([buf[i, pl.ds(c * chunk, chunk), :]] * nh, axis=0) * head_mask

    chains = [(i, c) for c in range(nc) for i in range(n_seq)]
    hb = [slice(h * chunk, (h + 1) * chunk) for h in range(nh)]

    def heads_of(x):
        return [x[hs] for hs in hb]

    def per_head(mats, rhs):
        return jnp.concatenate([_dot(m_, x_) for m_, x_ in zip(mats, heads_of(rhs))], axis=0)

    a_s = [stacked(a_buf, i, c).astype(BF16) for i, c in chains]
    r_s = [stacked(r_buf, i, c).astype(BF16) for i, c in chains]
    b_s = [stacked(b_buf, i, c).astype(BF16) for i, c in chains]
    k_s = [stacked(k_buf, i, c).astype(BF16) for i, c in chains]
    v_s = [stacked(v_buf, i, c).astype(BF16) for i, c in chains]
    n_c = [[_dot_nt(x, y) * strict for x, y in zip(heads_of(a), heads_of(b))] for a, b in zip(a_s, b_s)]
    m_c = [[(_dot_nt(x, y) * strict).astype(BF16) for x, y in zip(heads_of(a), heads_of(k))]
           for a, k in zip(a_s, k_s)]
    p_c = [[(_dot_nt(x, y) * incl).astype(BF16) for x, y in zip(heads_of(r), heads_of(b))]
           for r, b in zip(r_s, b_s)]
    q_c = [[(_dot_nt(x, y) * incl).astype(BF16) for x, y in zip(heads_of(r), heads_of(k))]
           for r, k in zip(r_s, k_s)]
    x = [xi for n in n_c for xi in n]
    t_c = [eye + xi for xi in x]
    step = 2
    while step < chunk:
        xb = [xi.astype(BF16) for xi in x]
        x = [_dot(b, b) for b in xb]
        t_c = [tm + _dot(tm.astype(BF16), xi.astype(BF16)) for tm, xi in zip(t_c, x)]
        step *= 2
    t_b = [[t_c[n * nh + h].astype(BF16) for h in range(nh)] for n in range(len(chains))]
    w_eff = [per_head(tb, a).astype(BF16) for tb, a in zip(t_b, a_s)]
    mv = [per_head(mm, v).astype(BF16) for mm, v in zip(m_c, v_s)]
    z = [per_head(tb, x_) for tb, x_ in zip(t_b, mv)]
    qv = [per_head(qm, v) for qm, v in zip(q_c, v_s)]

    for n, (i, c) in enumerate(chains):
        s0 = s_ref[i]
        s0b = s0.astype(BF16)
        ub = (_dot_nt(w_eff[n], s0b) + z[n]).astype(BF16)
        y_s = _dot_nt(r_s[n], s0b) + per_head(p_c[n], ub) + qv[n]
        yc = y_s[0:chunk]
        for hh in range(1, nh):
            yc = yc + y_s[hh * chunk:(hh + 1) * chunk]
        y_buf[i, pl.ds(c * chunk, chunk), :] = yc
        c_last = jnp.exp(w_buf[i, pl.ds((c + 1) * chunk - 1, 1), :])
        s_ref[i] = (s0 + _dot_tn(ub, b_s[n]) + _dot_tn(v_s[n], k_s[n])) * c_last

    inv_hd = 1.0 / HEAD_D
    for i in range(n_seq):
        y = y_buf[i]
        mean = _dot_exact_rhs(y, j4) * inv_hd
        yc = y - mean
        var = _dot_exact_rhs(yc * yc, j4) * inv_hd
        yn = yc * lax.rsqrt(var + RWKV_LN_EPS) * lng_ref[...] + lnb_ref[...]
        y_ref[i] = (yn + bon_buf[i]) * g_buf[i]
        for hs in heads:
            sout_ref[i, hs, :] = s_ref[i, hs, hs]


N_RWKV_IN, N_LRU_IN, N_RWKV_SCRATCH = 15, 14, 11


def _mixers_body(*refs, n_seq, tt, chunk, t_valid, pos0):
    rw_in, lru_in = refs[:N_RWKV_IN], refs[N_RWKV_IN:N_RWKV_IN + N_LRU_IN]
    outs = refs[N_RWKV_IN + N_LRU_IN:N_RWKV_IN + N_LRU_IN + 6]
    scratch = refs[N_RWKV_IN + N_LRU_IN + 6:]
    rw_out, (ya_ref, yc_ref, hlast_ref, ulast_ref) = outs[:2], outs[2:]
    rw_scratch, (extx_ref, extu_ref, h_ref) = scratch[:N_RWKV_SCRATCH], scratch[N_RWKV_SCRATCH:]
    xa_ref, ga_ref, bg_ref, cg_ref, xin_ref, cbuf_ref, h0_ref, sbuf_ref = lru_in[:8]
    lru_w = lru_in[8:]

    @pl.when(pl.program_id(1) == 0)
    def _():
        for i in range(n_seq):
            extx_ref[i, 0:SUBLANES, :] = cbuf_ref[i]
            extu_ref[i, 0:SUBLANES, :] = sbuf_ref[i]
            h_ref[i] = h0_ref[i]

    def lru_tiles(t):
        for i in range(n_seq):
            _lru_sconv_tile(t, xa_ref.at[i], ga_ref.at[i], bg_ref.at[i], cg_ref.at[i], xin_ref.at[i], *lru_w,
                            ya_ref.at[i], yc_ref.at[i], hlast_ref.at[i], ulast_ref.at[i],
                            extx_ref.at[i], extu_ref.at[i], h_ref.at[i], tt=tt, pos0=pos0)

    _rwkv_body(*rw_in, *rw_out, *rw_scratch, n_seq=n_seq, tt=tt, chunk=chunk, t_valid=t_valid, side_work=lru_tiles)


def _mixers(proj3, shift8, s_heads, rw, cbuf8, h0, sbuf8, lru, j4, tt, chunk, t_valid, pos0, n_seq):
    nb, tp, _ = proj3.shape
    w = GROUP_W
    pw = 4 * w
    vec = _const_spec((1, w))
    s_rows = s_heads.reshape(nb, w, HEAD_D)
    state_spec = pl.BlockSpec((n_seq, w, HEAD_D), lambda b, t: (b, 0, 0))
    tile = pl.BlockSpec((n_seq, tt, w), lambda b, t: (b, t, 0))
    state8 = pl.BlockSpec((n_seq, SUBLANES, w), lambda b, t: (b, 0, 0))

    def col(c):
        return pl.BlockSpec((n_seq, tt, w), lambda b, t, c=c: (b, t, c))

    yb, s_out, ya, yc, hlast, ulast = pl.pallas_call(
        functools.partial(_mixers_body, n_seq=n_seq, tt=tt, chunk=chunk, t_valid=t_valid, pos0=pos0),
        grid=(nb // n_seq, tp // tt),
        in_specs=[pl.BlockSpec((n_seq, tt, pw), lambda b, t: (b, t, 0)),
                  pl.BlockSpec((n_seq, SUBLANES, pw), lambda b, t: (b, 0, 0)), state_spec,
                  _const_spec((1, pw)), vec, _const_spec((LANES, w)), vec, _const_spec((LANES, w)),
                  _const_spec((LANES, w)), vec, vec, vec, vec, vec, _const_spec((w, w)),
                  col(COL_XA), col(COL_GA), col(COL_BG), col(COL_CG), col(COL_XIN),
                  state8, pl.BlockSpec((n_seq, 1, w), lambda b, t: (b, 0, 0)), state8,
                  _const_spec((4, w)), vec, _const_spec((w, 2 * w)), _const_spec((1, 2 * w)), vec,
                  _const_spec((3, w))],
        out_specs=[tile, state_spec, tile, tile, state8, state8],
        out_shape=[jax.ShapeDtypeStruct((nb, tp, w), F32), jax.ShapeDtypeStruct(s_rows.shape, F32),
                   jax.ShapeDtypeStruct((nb, tp, w), F32), jax.ShapeDtypeStruct((nb, tp, w), F32),
                   jax.ShapeDtypeStruct((nb, SUBLANES, w), F32), jax.ShapeDtypeStruct((nb, SUBLANES, w), F32)],
        scratch_shapes=[pltpu.VMEM((n_seq, tt + SUBLANES, pw), F32), pltpu.VMEM((n_seq, w, w), F32)]
        + [pltpu.VMEM((n_seq, tt, w), F32)] * 9
        + [pltpu.VMEM((n_seq, tt + SUBLANES, w), F32)] * 2 + [pltpu.VMEM((n_seq, 1, w), F32)],
        compiler_params=_params("parallel", "arbitrary"),
        name="rwkv7_lru",
    )(proj3, shift8, s_rows, *rw, j4, proj3, proj3, proj3, proj3, proj3, cbuf8, h0, sbuf8, *lru)
    return ya, yb, yc, hlast, ulast, s_out.reshape(s_heads.shape)


def _proj_prep_body(x_ref, g_ref, w_ref, cos_ref, sin_ref, gq_ref, gk_ref, j2_ref,
                    o_ref, qo_ref, cmpo_ref, slco_ref, wino_ref, gate_ref, *bf_refs, n_chunk):
    h =_rmsnorm_rows(x_ref[...], g_ref[...]).astype(BF16)
    n_keep = o_ref.shape[1]
    for c in range(n_keep // n_chunk):
        sl = slice(c * n_chunk, (c + 1) * n_chunk)
        o_ref[:, sl] = _dot(h, w_ref[:, sl])
    nsa = _dot(h, w_ref[:, n_keep:])

    cos = cos_ref[...]
    sin = sin_ref[...]
    j2 = j2_ref[...]
    lane = lax.broadcasted_iota(jnp.int32, cos.shape, 1)
    inv_hd = 1.0 / HEAD_D
    gq = gq_ref[...]
    halves = []
    for c in range(2):
        x = nsa[:, c * LANES:(c + 1) * LANES]
        ms = _dot_exact_rhs(x * x, j2) * inv_hd
        halves.append(_rope_lanes(x * lax.rsqrt(ms + EPS) * gq, cos, sin, lane))
    qo_ref[...] = jnp.concatenate(halves, axis=1)
    cmpo_ref[...] = nsa[:, 2 * LANES:3 * LANES]
    gk = gk_ref[...]
    is_key = lane < HEAD_D
    for dst, row in ((slco_ref, 0), (wino_ref, 1)):
        x = nsa[:, (3 + row) * LANES:(4 + row) * LANES]
        ms = _dot_exact_rhs(x * x, j2) * inv_hd
        roped = _rope_lanes(x * lax.rsqrt(ms + EPS) * gk[row:row + 1], cos, sin, lane)
        kv = jnp.where(is_key, roped, x)
        dst[...] = kv
        if bf_refs:
            bf_refs[row][...] = kv.astype(BF16)
            kv_t = jnp.where(lax.broadcasted_iota(jnp.int32, (LANES, kv.shape[0]), 0) == 0, 1.0, kv.T)
            bf_refs[2 + row][...] = kv_t.astype(BF16)
    gate_ref[...] = _sigmoid(nsa[:, 5 * LANES:6 * LANES])


def _proj_prep(x, g, w, layer, cos, sin, gq, gk2, j2, tm, rows_per_table, with_bf16):
    m, d = x.shape
    n = w.shape[2]
    nt_tab = rows_per_table // tm
    o128t = pl.BlockSpec((LANES, tm), lambda i: (0, i))
    bf_specs = [pl.BlockSpec((tm, LANES), lambda i: (i, 0))] * 2 + [o128t] * 2 if with_bf16 else []
    bf_shapes = ([jax.ShapeDtypeStruct((m, LANES), BF16)] * 2 + [jax.ShapeDtypeStruct((LANES, m), BF16)] * 2
                 if with_bf16 else [])
    tab = pl.BlockSpec((tm, LANES), lambda i: (i % nt_tab, 0))
    o128 = pl.BlockSpec((tm, LANES), lambda i: (i, 0))
    return pl.pallas_call(
        functools.partial(_proj_prep_body, n_chunk=256),
        grid=(m // tm,),
        in_specs=[pl.BlockSpec((tm, d), lambda i: (i, 0)), _const_spec((1, d)),
                  pl.BlockSpec((None, d, n), lambda i: (layer, 0, 0), pipeline_mode=pl.Buffered(1)),
                  tab, tab, _const_spec((1, LANES)), _const_spec((2, LANES)), _const_spec((LANES, LANES))],
        out_specs=[pl.BlockSpec((tm, PROJ_KEEP), lambda i: (i, 0)),
                   pl.BlockSpec((tm, GROUP_W), lambda i: (i, 0)), o128, o128, o128, o128] + bf_specs,
        out_shape=[jax.ShapeDtypeStruct((m, PROJ_KEEP), F32), jax.ShapeDtypeStruct((m, GROUP_W), F32)]
        + [jax.ShapeDtypeStruct((m, LANES), F32)] * 4 + bf_shapes,
        compiler_params=_params("parallel"),
        name="proj_prep",
    )(x, g, w, cos, sin, gq, gk2, j2)


def _compress_tail(hid_lo, hid_hi_next, b1_ref, w2_ref, b2_ref, gk_ref, cos_ref, sin_ref, j2_ref):
    hidden = _gelu_tanh(hid_lo + hid_hi_next + b1_ref[...])
    kv = _dot(hidden.astype(BF16), w2_ref[...]) + b2_ref[...]
    lane = lax.broadcasted_iota(jnp.int32, kv.shape, 1)
    ms = _dot_exact_rhs(kv * kv, j2_ref[...]) * (1.0 / HEAD_D)
    roped = _rope_lanes(kv * lax.rsqrt(ms + EPS) * gk_ref[...], cos_ref[...], sin_ref[...], lane)
    return jnp.where(lane < HEAD_D, roped, kv)


def _compress_hidden(read_rows, pelo_ref, pehi_ref, wlo_ref, whi_ref):
    pair = 2 * LANES
    lo = hi = None
    for p in range(CMP_STRIDE // 2):
        x = jnp.concatenate([read_rows(2 * p), read_rows(2 * p + 1)], axis=1)
        cs = slice(p * pair, (p + 1) * pair)
        d_lo = _dot((x + pelo_ref[:, cs]).astype(BF16), wlo_ref[cs, :])
        d_hi = _dot((x + pehi_ref[:, cs]).astype(BF16), whi_ref[cs, :])
        lo = d_lo if lo is None else lo + d_lo
        hi = d_hi if hi is None else hi + d_hi
    return lo, hi


def _compress_body(x_ref, pelo_ref, pehi_ref, wlo_ref, whi_ref, b1_ref, w2_ref, b2_ref, gk_ref,
                   cos_ref, sin_ref, j2_ref, o_ref, ot_ref):
    n_grp = x_ref.shape[0] // CMP_STRIDE
    lo, hi = _compress_hidden(lambda j: x_ref[pl.ds(j, n_grp, stride=CMP_STRIDE), :],
                              pelo_ref, pehi_ref, wlo_ref, whi_ref)
    hi_next = pltpu.roll(hi, n_grp - 1, 0)
    kv = _compress_tail(lo, hi_next, b1_ref, w2_ref, b2_ref, gk_ref, cos_ref, sin_ref, j2_ref)
    o_ref[...] = kv.astype(BF16)
    ot_ref[...] = kv.T.astype(BF16)


def _compress(rows, cw, cos, sin, j2):
    nb, t, _ = rows.shape
    n_grp = t // CMP_STRIDE
    wid = CMP_STRIDE * LANES
    return pl.pallas_call(
        _compress_body,
        grid=(nb,),
        in_specs=[pl.BlockSpec((None, t, LANES), lambda b: (b, 0, 0)),
                  _const_spec((1, wid)), _const_spec((1, wid)),
                  _const_spec((wid, GROUP_W)), _const_spec((wid, GROUP_W)), _const_spec((1, GROUP_W)),
                  _const_spec((GROUP_W, LANES)), _const_spec((1, LANES)), _const_spec((1, LANES)),
                  _const_spec((n_grp, LANES)), _const_spec((n_grp, LANES)), _const_spec((LANES, LANES))],
        out_specs=[pl.BlockSpec((None, n_grp, LANES), lambda b: (b, 0, 0)),
                   pl.BlockSpec((LANES, n_grp), lambda b: (0, b))],
        out_shape=[jax.ShapeDtypeStruct((nb, n_grp, LANES), BF16),
                   jax.ShapeDtypeStruct((LANES, nb * n_grp), BF16)],
        compiler_params=_params("parallel"),
        name="nsa_compress",
    )(rows, cw["pe_lo"], cw["pe_hi"], cw["w_lo"], cw["w_hi"], cw["b1"], cw["w2"], cw["b2"], cw["gk"], cos, sin, j2)


def _stack_heads(q, lane):
    parts = []
    for h in range(N_HEADS):
        blk = q[:, (h // 2) * LANES:(h // 2 + 1) * LANES]
        if h % 2 == 1:
            blk = pltpu.roll(blk, HEAD_D, 1)
        parts.append(jnp.where(lane < HEAD_D, blk, 0.0))
    return jnp.concatenate(parts, axis=0)


def _unstack_heads(parts, lane):
    b01 = jnp.where(lane < HEAD_D, pltpu.roll(parts[0], HEAD_D, 1), parts[1])
    b23 = jnp.where(lane < HEAD_D, pltpu.roll(parts[2], HEAD_D, 1), parts[3])
    return jnp.concatenate([b01, b23], axis=1)


def _softmax_cols(st, mask):
    st = jnp.where(mask, st, NEG_INF)
    m = jnp.max(st, axis=0, keepdims=True)
    e = jnp.exp(st - m)
    norm = 1.0 / jnp.maximum(jnp.sum(e, axis=0, keepdims=True), 1e-30)
    return e * jnp.where(m > 0.5 * NEG_INF, norm, 0.0)


def _nsa_attn_t_body(q_ref, gate_ref, kvc_ref, kvct_ref, slc_ref, slct_ref, win_ref, wint_ref, ovl_ref, exp_ref,
                     o_ref, acc_ref, st_ref, e_ref, *, tq, tk, n_cmp, n_sel, n_top, win_span):
    i = pl.program_id(1)
    start = i * tq
    cols = N_HEADS * tq
    lane = lax.broadcasted_iota(jnp.int32, (tq, LANES), 1)
    q4 = _stack_heads(q_ref[...] * ATT_SCALE, lane).astype(BF16)
    tlane = start + (lax.broadcasted_iota(jnp.int32, (1, cols), 1) & (tq - 1))

    kvc = kvc_ref[...]
    ng = kvc.shape[0]
    ncol = lax.broadcasted_iota(jnp.int32, (ng, 1), 0)
    cend = jnp.where(ncol < n_cmp, ncol * CMP_STRIDE + (CMP_LEN - 1), jnp.iinfo(jnp.int32).max)
    pt = _softmax_cols(_dot_nt(kvc, q4), cend <= tlane)
    o_cmp = _dot(kvct_ref[...], pt.astype(BF16))
    psum = pt[:, 0:tq]
    for h in range(1, N_HEADS):
        psum = psum + pt[:, h * tq:(h + 1) * tq]
    ph = psum.astype(BF16)
    pl_ = (psum - ph.astype(F32)).astype(BF16)
    ovl = ovl_ref[...]
    imp_t = _dot(ovl, ph) + _dot(ovl, pl_)

    start0 = pl.multiple_of(jnp.maximum(start + tq - win_span, 0), tq)
    rel = tlane - (start0 + lax.broadcasted_iota(jnp.int32, (win_span, 1), 0))
    sw = jnp.where((rel >= 0) & (rel < WINDOW), _dot_nt(win_ref[pl.ds(start0, win_span), :], q4), NEG_INF)
    mw = jnp.max(sw, axis=0, keepdims=True)
    o_win = _dot(wint_ref[:, pl.ds(start0, win_span)], jnp.exp(sw - mw).astype(BF16))
    o_win = o_win * jnp.where(mw > 0.5 * NEG_INF, 1.0 / jnp.maximum(o_win[0:1], 1e-30), 0.0)

    nsp = imp_t.shape[0]
    jblk = lax.broadcasted_iota(jnp.int32, (nsp, tq), 0)
    tcol = start + lax.broadcasted_iota(jnp.int32, (nsp, tq), 1)
    cur = tcol // SEL_BLOCK
    forced = (jblk == 0) | (jblk == cur) | (jblk == cur - 1)
    valid = (jblk * SEL_BLOCK <= tcol) & (jblk < n_sel)
    score = jnp.where(valid, imp_t + jnp.where(forced, FORCE_BONUS, 0.0), -1.0)
    taken = -3.0
    jf = jblk.astype(F32)
    left = score
    for _ in range(n_top):
        best = jnp.max(left, axis=0, keepdims=True)
        first = jnp.min(jnp.where(left == best, jf, float(nsp)), axis=0, keepdims=True)
        left = jnp.where(jf == first, taken, left)
    bias_t = jnp.where((left == taken) & (score >= 0.0), 0.0, NEG_INF)
    bias = bias_t.T.astype(BF16)
    lhs = jnp.concatenate([q4, jnp.concatenate([bias] * N_HEADS, axis=0)], axis=1)

    acc_ref[...] = jnp.zeros((LANES, cols), F32)
    n_kt = (start + tq + tk - 1) // tk

    def stage_scores(kt, causal):
        off = pl.multiple_of(kt * tk, tk)
        k_aug = jnp.concatenate([slc_ref[pl.ds(off, tk), :], exp_ref[pl.ds(off, tk), :]], axis=1)
        st = _dot_nt(k_aug, lhs)
        if causal:
            kpos = off + lax.broadcasted_iota(jnp.int32, (tk, 1), 0)
            st = jnp.where(kpos <= tlane, st, NEG_INF)
        return st, jnp.max(st.reshape(tk // SUBLANES, SUBLANES, cols), axis=0)

    def stage_exp(st, mt, m_prev):
        m_new = jnp.maximum(m_prev, jnp.max(mt, axis=0, keepdims=True))
        m_sub = jnp.where(m_new > 0.5 * NEG_INF, m_new, 0.0)
        e = jnp.exp(st - m_sub[0:1])
        return e.astype(BF16), jnp.exp(m_prev - m_new), m_new

    def stage_values(kt, e, alpha):
        off = pl.multiple_of(jnp.maximum(kt, 0) * tk, tk)
        acc_ref[...] = alpha[0:1] * acc_ref[...] + _dot(slct_ref[:, pl.ds(off, tk)], e)

    def trip(i, carry, causal):
        mt, alpha, m_run = carry
        stage_values(i - 2, e_ref[...], alpha)
        e, alpha, m_run = stage_exp(st_ref[...], mt, m_run)
        e_ref[...] = e
        st, mt = stage_scores(i, causal)
        st_ref[...] = st
        return mt, alpha, m_run

    st_ref[...] = jnp.full((tk, cols), NEG_INF, F32)
    e_ref[...] = jnp.zeros((tk, cols), BF16)
    neg8 = st_ref[0:SUBLANES, :]
    carry = (neg8, acc_ref[0:SUBLANES, :] + 1.0, neg8)
    carry = lax.fori_loop(0, n_kt - 1, functools.partial(trip, causal=False), carry)
    mt, alpha, m_run = trip(n_kt - 1, carry, True)
    stage_values(n_kt - 2, e_ref[...], alpha)
    e, alpha, _ = stage_exp(st_ref[...], mt, m_run)
    stage_values(n_kt - 1, e, alpha)
    acc = acc_ref[...]
    o_sel = acc / jnp.maximum(acc[0:1], 1e-30)

    gt = gate_ref[...].T
    outs = []
    for h in range(N_HEADS):
        cs = slice(h * tq, (h + 1) * tq)
        outs.append(gt[3 * h:3 * h + 1] * o_cmp[HEAD_D:, cs] + gt[3 * h + 1:3 * h + 2] * o_sel[HEAD_D:, cs]
                    + gt[3 * h + 2:3 * h + 3] * o_win[HEAD_D:, cs])
    o_ref[...] = jnp.concatenate(outs, axis=0).T


def _nsa_attn_t(q, gates, kvc, kvc_t, slc, slc_t, win, win_t, ovl_t, expand, tq, tk, n_cmp, n_sel):
    nb, t, _ = q.shape
    ng = kvc.shape[1]
    nsp = ovl_t.shape[0]
    cols = N_HEADS * tq
    win_span = WINDOW + tq
    per_b = lambda b, i: (b, 0, 0)
    per_b_t = lambda b, i: (0, b)
    return pl.pallas_call(
        functools.partial(_nsa_attn_t_body, tq=tq, tk=tk, n_cmp=n_cmp, n_sel=n_sel,
                          n_top=min(N_SELECT, n_sel), win_span=win_span),
        grid=(nb, t // tq),
        in_specs=[pl.BlockSpec((None, tq, GROUP_W), lambda b, i: (b, i, 0)),
                  pl.BlockSpec((None, tq, LANES), lambda b, i: (b, i, 0)),
                  pl.BlockSpec((None, ng, LANES), per_b), pl.BlockSpec((LANES, ng), per_b_t),
                  pl.BlockSpec((None, t, LANES), per_b), pl.BlockSpec((LANES, t), per_b_t),
                  pl.BlockSpec((None, t, LANES), per_b), pl.BlockSpec((LANES, t), per_b_t),
                  _const_spec((nsp, ng)), _const_spec((t, nsp))],
        out_specs=pl.BlockSpec((None, tq, GROUP_W), lambda b, i: (b, i, 0)),
        out_shape=jax.ShapeDtypeStruct((nb, t, GROUP_W), F32),
        scratch_shapes=[pltpu.VMEM((LANES, cols), F32), pltpu.VMEM((tk, cols), F32), pltpu.VMEM((tk, cols), BF16)],
        compiler_params=_params("parallel", "arbitrary"),
        name="nsa_attn",
    )(q, gates, kvc, kvc_t, slc, slc_t, win, win_t, ovl_t, expand)


def _samp_cmp_body(*refs, pages_per_step, n_grp, n_cmp, n_sel, n_top, t_pos):
    pt_ref = refs[0]
    page_refs = refs[1:1 + pages_per_step]
    (pelo_ref, pehi_ref, wlo_ref, whi_ref, b1_ref, w2_ref, b2_ref, gk_ref, cos_ref, sin_ref, j2_ref,
     q4_ref, ovl_ref, ocmp_ref, idx_ref, lo_ref, hi_ref) = refs[1 + pages_per_step:]
    del pt_ref
    s = pl.program_id(1)
    ns = pl.num_programs(1)
    grp_per_page = PAGE_SIZE // CMP_STRIDE
    rows_step = pages_per_step * grp_per_page

    @pl.when(s == 0)
    def _():
        hi_ref[n_grp:n_grp + SUBLANES, :] = jnp.zeros((SUBLANES, GROUP_W), F32)

    def read_rows(j):
        return jnp.concatenate([pg[pl.ds(j, grp_per_page, stride=CMP_STRIDE), :] for pg in page_refs], axis=0)

    lo, hi = _compress_hidden(read_rows, pelo_ref, pehi_ref, wlo_ref, whi_ref)
    off = pl.multiple_of(s * rows_step, rows_step)
    lo_ref[pl.ds(off, rows_step), :] = lo
    hi_ref[pl.ds(off, rows_step), :] = hi

    @pl.when(s == ns - 1)
    def _():
        kvc = _compress_tail(lo_ref[...], hi_ref[pl.ds(1, n_grp), :], b1_ref, w2_ref, b2_ref, gk_ref,
                             cos_ref, sin_ref, j2_ref).astype(BF16)
        q4 = q4_ref[...].astype(BF16)
        sc = _dot_nt(q4, kvc) * ATT_SCALE
        nidx = lax.broadcasted_iota(jnp.int32, sc.shape, 1)
        p = _masked_softmax_rows(sc, (nidx * CMP_STRIDE + (CMP_LEN - 1) <= t_pos) & (nidx < n_cmp))
        ocmp_ref[...] = _dot(p.astype(BF16), kvc)
        hrow = lax.broadcasted_iota(jnp.int32, p.shape, 0)
        psum = jnp.sum(jnp.where(hrow < N_HEADS, p, 0.0), axis=0, keepdims=True)
        psum8 = jnp.broadcast_to(psum, p.shape)
        ph = psum8.astype(BF16)
        pl_ = (psum8 - ph.astype(F32)).astype(BF16)
        ovl = ovl_ref[...]
        imp = (_dot(ph, ovl) + _dot(pl_, ovl))[0:1]
        nsp = imp.shape[1]
        jrow = lax.broadcasted_iota(jnp.int32, (1, nsp), 1)
        cur = t_pos // SEL_BLOCK
        forced = (jrow == 0) | (jrow == cur) | (jrow == cur - 1)
        valid = (jrow * SEL_BLOCK <= t_pos) & (jrow < n_sel)
        score = jnp.where(valid, imp + jnp.where(forced, FORCE_BONUS, 0.0), -1.0)
        score = jnp.where(jrow < n_sel, score, -2.0)
        s_row = jnp.broadcast_to(score, (nsp, nsp))
        s_col = s_row.T
        ii = lax.broadcasted_iota(jnp.int32, (nsp, nsp), 0)
        jj = lax.broadcasted_iota(jnp.int32, (nsp, nsp), 1)
        beats = (s_col > s_row) | ((s_col == s_row) & (ii < jj))
        rank = jnp.sum(jnp.where(beats, 1.0, 0.0), axis=0, keepdims=True)
        chosen = (rank < n_top) & (score >= 0.0)
        slot = lax.broadcasted_iota(jnp.int32, (N_SELECT, nsp), 0).astype(F32)
        hit = (jnp.broadcast_to(rank, (N_SELECT, nsp)) == slot) & jnp.broadcast_to(chosen, (N_SELECT, nsp))
        jcol = lax.broadcasted_iota(jnp.int32, (N_SELECT, nsp), 1).astype(F32)
        blk = jnp.sum(jnp.where(hit, jcol, 0.0), axis=1, keepdims=True)
        cnt = jnp.sum(jnp.where(hit, 1.0, 0.0), axis=1, keepdims=True)
        blk = jnp.where(cnt > 0.5, blk, -1.0)
        idx_ref[...] = jnp.broadcast_to(blk, (N_SELECT, LANES)).astype(jnp.int32)


def _samp_cmp(cache, layer, page_table, cw, cos, sin, j2, q4, ovl, n_cmp, n_sel, t_pos, pages_per_step):
    nb, n_pages = page_table.shape
    grp_per_page = PAGE_SIZE // CMP_STRIDE
    wid = CMP_STRIDE * LANES
    n_grp = n_pages * grp_per_page
    nsp = ovl.shape[1]

    def page_spec(j):
        return pl.BlockSpec((None, None, PAGE_SIZE, LANES),
                            lambda b, s, pt, j=j: (layer, pt[b, s * pages_per_step + j], 0, 0))

    def cst(shape):
        nd = len(shape)
        return pl.BlockSpec(shape, lambda b, s, pt: (0,) * nd)

    grid_spec = pltpu.PrefetchScalarGridSpec(
        num_scalar_prefetch=1,
        grid=(nb, n_pages // pages_per_step),
        in_specs=[page_spec(j) for j in range(pages_per_step)] + [
            cst((1, wid)), cst((1, wid)), cst((wid, GROUP_W)), cst((wid, GROUP_W)), cst((1, GROUP_W)),
            cst((GROUP_W, LANES)), cst((1, LANES)), cst((1, LANES)),
            cst((n_grp, LANES)), cst((n_grp, LANES)), cst((LANES, LANES)),
            pl.BlockSpec((None, SUBLANES, LANES), lambda b, s, pt: (b, 0, 0)), cst((n_grp, nsp))],
        out_specs=[pl.BlockSpec((None, SUBLANES, LANES), lambda b, s, pt: (b, 0, 0)),
                   pl.BlockSpec((None, N_SELECT, LANES), lambda b, s, pt: (b, 0, 0))],
        scratch_shapes=[pltpu.VMEM((n_grp, GROUP_W), F32), pltpu.VMEM((n_grp + SUBLANES, GROUP_W), F32)],
    )
    return pl.pallas_call(
        functools.partial(_samp_cmp_body, pages_per_step=pages_per_step, n_grp=n_grp, n_cmp=n_cmp,
                          n_sel=n_sel, n_top=min(N_SELECT, n_sel), t_pos=t_pos),
        grid_spec=grid_spec,
        out_shape=[jax.ShapeDtypeStruct((nb, SUBLANES, LANES), F32),
                   jax.ShapeDtypeStruct((nb, N_SELECT, LANES), jnp.int32)],
        compiler_params=_params("parallel", "arbitrary"),
        name="nsa_decode_compress",
    )(page_table, *([cache] * pages_per_step), cw["pe_lo"], cw["pe_hi"], cw["w_lo"], cw["w_hi"], cw["b1"],
      cw["w2"], cw["b2"], cw["gk"], cos, sin, j2, q4, ovl)


def _samp_sel_body(pt_ref, idx_ref, *refs, n_past_blk, t_pos, win_pos0):
    del pt_ref
    blk_refs = refs[:N_SELECT]
    q4_ref, nslc_ref, win_ref, nwin_ref, gate_ref, ocmp_ref, o_ref = refs[N_SELECT:]
    b = pl.program_id(0)
    q4 = q4_ref[...].astype(BF16)

    n_keys = N_SELECT * SEL_BLOCK
    lane_k = lax.broadcasted_iota(jnp.int32, (1, n_keys), 1)
    jvec = jnp.zeros((1, n_keys), jnp.int32)
    parts = []
    for k in range(N_SELECT):
        j = idx_ref[b, k]
        blk = blk_refs[k][...]
        parts.append(jnp.where(j == n_past_blk, nslc_ref[...], blk[0:SUBLANES]))
        parts.append(blk[SUBLANES:])
        jvec = jnp.where(lane_k // SEL_BLOCK == k, j, jvec)
    kv = jnp.concatenate(parts, axis=0).astype(BF16)
    kpos = jvec * SEL_BLOCK + (lane_k & (SEL_BLOCK - 1))
    p_sel = _masked_softmax_rows(_dot_nt(q4, kv) * ATT_SCALE, (kpos <= t_pos) & (jvec >= 0))
    o_sel = _dot(p_sel.astype(BF16), kv)

    wb = win_ref[...].astype(BF16)
    nw = nwin_ref[...].astype(BF16)
    s_w = _dot_nt(q4, wb) * ATT_SCALE
    s_n = _dot_nt(q4, nw) * ATT_SCALE
    wpos = win_pos0 + lax.broadcasted_iota(jnp.int32, s_w.shape, 1)
    rel = t_pos - wpos
    m_w = (rel >= 0) & (rel < WINDOW) & (wpos >= win_pos0)
    m_n = lax.broadcasted_iota(jnp.int32, s_n.shape, 1) == 0
    s_w = jnp.where(m_w, s_w, NEG_INF)
    s_n = jnp.where(m_n, s_n, NEG_INF)
    mx = jnp.maximum(jnp.max(s_w, axis=-1, keepdims=True), jnp.max(s_n, axis=-1, keepdims=True))
    e_w = jnp.exp(s_w - mx) * m_w.astype(F32)
    e_n = jnp.exp(s_n - mx) * m_n.astype(F32)
    den = jnp.sum(e_w, axis=-1, keepdims=True) + jnp.sum(e_n, axis=-1, keepdims=True)
    o_win = (_dot(e_w.astype(BF16), wb) + _dot(e_n.astype(BF16), nw)) / jnp.maximum(den, 1e-30)
    o_cmp = ocmp_ref[...]
    g = gate_ref[...]
    lane = lax.broadcasted_iota(jnp.int32, (1, LANES), 1)
    outs = []
    for h in range(N_HEADS):
        outs.append(g[0:1, 3 * h:3 * h + 1] * o_cmp[h:h + 1] + g[0:1, 3 * h + 1:3 * h + 2] * o_sel[h:h + 1]
                    + g[0:1, 3 * h + 2:3 * h + 3] * o_win[h:h + 1])
    o_ref[...] = jnp.broadcast_to(_unstack_heads(outs, lane), o_ref.shape)


def _samp_sel(cache_blk, layer, page_table, top_idx, q4, nslc8, win, nwin8, gates8, ocmp, t_pos, win_pos0):
    nb, n_pages = page_table.shape
    blk_per_page = PAGE_SIZE // SEL_BLOCK
    n_past_blk = n_pages * blk_per_page
    wlen = win.shape[2]

    def blk_spec(k):
        def blk_map(b, pt, idx):
            j = jnp.clip(idx[b, k], 0, n_past_blk - 1)
            return (layer, pt[b, j // blk_per_page], j % blk_per_page, 0, 0)
        return pl.BlockSpec((None, None, None, SEL_BLOCK, LANES), blk_map)

    row8 = pl.BlockSpec((None, SUBLANES, LANES), lambda b, pt, idx: (b, 0, 0))
    grid_spec = pltpu.PrefetchScalarGridSpec(
        num_scalar_prefetch=2,
        grid=(nb,),
        in_specs=[blk_spec(k) for k in range(N_SELECT)] + [
            row8, row8, pl.BlockSpec((None, None, wlen, LANES), lambda b, pt, idx: (layer, b, 0, 0)),
            row8, row8, row8],
        out_specs=pl.BlockSpec((None, SUBLANES, GROUP_W), lambda b, pt, idx: (b, 0, 0)),
    )
    return pl.pallas_call(
        functools.partial(_samp_sel_body, n_past_blk=n_past_blk, t_pos=t_pos, win_pos0=win_pos0),
        grid_spec=grid_spec,
        out_shape=jax.ShapeDtypeStruct((nb, SUBLANES, GROUP_W), F32),
        compiler_params=_params("parallel"),
        name="nsa_decode_select",
    )(page_table, top_idx, *([cache_blk] * N_SELECT), q4, nslc8, win, nwin8, gates8, ocmp)


def _block_diag(blocks):
    n, a, b = blocks.shape
    eye = jnp.eye(n, dtype=blocks.dtype)
    return jnp.einsum("nab,nm->namb", blocks, eye).reshape(n * a, n * b)


def _rope_tables(pos):
    half = HEAD_D // 2
    inv = ROPE_THETA ** (-jnp.arange(half, dtype=F32) / half)
    ang = pos.astype(F32)[:, None] * inv[None, :]
    cos, sin = jnp.cos(ang), jnp.sin(ang)
    cos128 = jnp.concatenate([cos, cos, cos, cos], axis=1)
    sin128 = jnp.concatenate([-sin, sin, -sin, sin], axis=1)
    return cos128, sin128


def _seg_ones(n_seg):
    return _block_diag(jnp.ones((n_seg, HEAD_D, HEAD_D), F32)).astype(BF16)


def _w_in_stack(w_in):
    a_w, b_w, c_w = 2 * GROUP_W, 4 * GROUP_W, 3 * GROUP_W
    return jnp.concatenate(
        [w_in[..., a_w:a_w + b_w], w_in[..., :a_w], w_in[..., a_w + b_w:a_w + b_w + c_w], w_in[..., a_w + b_w + c_w:],
         jnp.zeros(w_in.shape[:-1] + (PROJ_PAD - w_in.shape[-1],), F32)], axis=-1).astype(BF16)


def _layer_weights(lp):
    lora = HEAD_D
    w1 = lp["nsa_cmp_w1"].reshape(2, 2, CMP_STRIDE, HEAD_D, 2 * HEAD_D)
    pe = lp["nsa_cmp_pe"].reshape(2, 2, CMP_STRIDE, HEAD_D)

    def cmp_half(hf):
        wk = jnp.zeros((CMP_STRIDE, 2 * HEAD_D, 4 * HEAD_D), F32)
        wk = wk.at[:, :HEAD_D, :2 * HEAD_D].set(w1[0, hf]).at[:, HEAD_D:, 2 * HEAD_D:].set(w1[1, hf])
        pk = jnp.concatenate([pe[0, hf], pe[1, hf]], axis=1)
        return wk.reshape(CMP_STRIDE * 2 * HEAD_D, 4 * HEAD_D).astype(BF16), pk.reshape(1, CMP_STRIDE * 2 * HEAD_D)

    w_lo, pe_lo = cmp_half(0)
    w_hi, pe_hi = cmp_half(1)
    ones64 = jnp.ones((HEAD_D,), F32)
    cw = dict(
        w_lo=w_lo, w_hi=w_hi, pe_lo=pe_lo, pe_hi=pe_hi,
        b1=lp["nsa_cmp_b1"].reshape(1, 4 * HEAD_D),
        w2=_block_diag(lp["nsa_cmp_w2"]).astype(BF16),
        b2=lp["nsa_cmp_b2"].reshape(1, 2 * HEAD_D),
        gk=jnp.concatenate([lp["nsa_norm_k"][0], ones64]).reshape(1, LANES),
    )
    zeros_lora = jnp.zeros((lora, GROUP_W), F32)
    return dict(
        norm_ffn=lp["norm_ffn"].reshape(2, 1, D_MODEL),
        norm_mix=lp["norm_mix"].reshape(1, D_MODEL),
        lru_cw=lp["lru_conv_w"], lru_cb=lp["lru_conv_b"].reshape(1, GROUP_W),
        lru_wri=jnp.concatenate([_block_diag(lp["lru_w_r"]), _block_diag(lp["lru_w_i"])], axis=1).astype(BF16),
        lru_bri=jnp.concatenate([lp["lru_b_r"], lp["lru_b_i"]]).reshape(1, 2 * GROUP_W),
        lru_lam=lp["lru_lambda"].reshape(1, GROUP_W), sconv_w=lp["sconv_w"],
        mu=lp["rwkv_mu"].reshape(1, 4 * GROUP_W), w0=lp["rwkv_w0"].reshape(1, GROUP_W),
        w2p=jnp.concatenate([lp["rwkv_w2"], zeros_lora], axis=0).astype(BF16),
        a0=lp["rwkv_a0"].reshape(1, GROUP_W),
        a2p=jnp.concatenate([zeros_lora, lp["rwkv_a2"]], axis=0).astype(BF16),
        g2=lp["rwkv_g2"].astype(BF16),
        k_k=lp["rwkv_k_k"].reshape(1, GROUP_W), k_a=lp["rwkv_k_a"].reshape(1, GROUP_W),
        r_k=lp["rwkv_r_k"].reshape(1, GROUP_W),
        ln_g=lp["rwkv_ln_g"].reshape(1, GROUP_W), ln_b=lp["rwkv_ln_b"].reshape(1, GROUP_W),
        gq=jnp.tile(lp["nsa_norm_q"], 2).reshape(1, LANES),
        gk2=jnp.stack([jnp.concatenate([lp["nsa_norm_k"][1], ones64]),
                       jnp.concatenate([lp["nsa_norm_k"][2], ones64])]),
        cw=cw,
        out_norm=lp["out_norm"].reshape(N_HEADS, GROUP_W),
    )


def _overlap(n_grp, n_sel_pad, n_sel):
    cs = jnp.arange(n_grp)[:, None] * CMP_STRIDE
    js = jnp.arange(n_sel_pad)[None, :] * SEL_BLOCK
    ov = (cs < js + SEL_BLOCK) & (cs + CMP_LEN > js) & (jnp.arange(n_sel_pad)[None, :] < n_sel)
    return ov.astype(BF16)


def _pad_rows_front(x, rows):
    return jnp.pad(x, ((0, 0), (rows - x.shape[1], 0), (0, 0)))


def _round_up(x, m):
    return (x + m - 1) // m * m


def _mixers_recurrent(lw, proj3, cbuf8, h0, sbuf8, shift8, s_heads, j4, tt, chunk, pos0, t_valid):
    rw = (lw["mu"], lw["w0"], lw["w2p"], lw["a0"], lw["a2p"], lw["g2"], lw["k_k"], lw["k_a"], lw["r_k"],
          lw["ln_g"], lw["ln_b"])
    lru = (lw["lru_cw"], lw["lru_cb"], lw["lru_wri"], lw["lru_bri"], lw["lru_lam"], lw["sconv_w"])
    return _mixers(proj3, shift8, s_heads, rw, cbuf8, h0, sbuf8, lru, j4, tt, chunk, t_valid, pos0,
                   n_seq=math.gcd(proj3.shape[0], max(1, RWKV_CHAINS * chunk // tt)))


def _prompt_layer(lw, layer, x, nb, t, consts):
    m = nb * t
    tm = 512
    j2, j4 = consts["j2"], consts["j4"]
    ffn_w = consts["ffn_w"]
    x1 = _ffn(x, lw["norm_ffn"][0], ffn_w, layer, 0, tm)
    proj, q_r, ncmp, nslc, nwin, gates, slc_bf, win_bf, slc_t, win_t = _proj_prep(
        x1, lw["norm_mix"], consts["w_in"], layer, consts["cos_p"], consts["sin_p"], lw["gq"], lw["gk2"], j2, tm,
        rows_per_table=t, with_bf16=True)
    proj3 = proj.reshape(nb, t, PROJ_KEEP)
    zeros8 = jnp.zeros((nb, SUBLANES, GROUP_W), F32)
    ya, yb, yc, hlast, ulast, s_out = _mixers_recurrent(
        lw, proj3, zeros8, jnp.zeros((nb, 1, GROUP_W), F32), zeros8,
        jnp.zeros((nb, SUBLANES, 4 * GROUP_W), F32), jnp.zeros((nb, N_HEADS, HEAD_D, HEAD_D), F32), j4,
        tt=256, chunk=HEAD_D, pos0=0, t_valid=t)
    kvc, kvc_t = _compress(ncmp.reshape(nb, t, LANES), lw["cw"], consts["cos_cp"], consts["sin_cp"], j2)
    n_cmp = (t - CMP_LEN) // CMP_STRIDE + 1
    n_sel = -(-t // SEL_BLOCK)
    yd = _nsa_attn_t(q_r.reshape(nb, t, GROUP_W), gates.reshape(nb, t, LANES), kvc, kvc_t,
                     slc_bf.reshape(nb, t, LANES), slc_t, win_bf.reshape(nb, t, LANES), win_t,
                     consts["ovl_t_p"], consts["expand_p"], tq=256, tk=512, n_cmp=n_cmp, n_sel=n_sel)
    x3 = _ffn(x1, lw["norm_ffn"][1], ffn_w, layer, 1, tm,
              mix=(ya.reshape(m, GROUP_W), yb.reshape(m, GROUP_W), yc.reshape(m, GROUP_W), yd.reshape(m, GROUP_W),
                   lw["out_norm"], consts["w_out"]))
    xa = proj3[:, :, COL_XA * GROUP_W:(COL_XA + 1) * GROUP_W]
    wlen = min(WINDOW, t)
    states = (hlast[:, SUBLANES - 1], xa[:, t - 3:], s_out, proj3[:, t - 1, :4 * GROUP_W],
              ulast[:, SUBLANES - 2:], nwin.reshape(nb, t, LANES)[:, t - wlen:],
              ncmp.reshape(nb, t // PAGE_SIZE, PAGE_SIZE, LANES), nslc.reshape(nb, t // PAGE_SIZE, PAGE_SIZE, LANES))
    return x3, states


def _sample_layer(lw, layer, x, st, caches, page_table, consts):
    nb = x.shape[0]
    n_pages = page_table.shape[1]
    past_len = n_pages * PAGE_SIZE
    j2, j4 = consts["j2"], consts["j4"]
    lru_h, lru_conv, rwkv_s, rwkv_shift, sconv, win = st
    cache_cmp, cache_slc = caches
    ffn_w = consts["ffn_w"]
    x1 = _ffn(x, lw["norm_ffn"][0], ffn_w, layer, 0, nb)
    proj, q_r, ncmp, nslc, nwin, gates = _proj_prep(
        x1, lw["norm_mix"], consts["w_in"], layer, consts["cos_s"], consts["sin_s"], lw["gq"], lw["gk2"], j2, nb,
        rows_per_table=nb, with_bf16=False)
    proj3 = jnp.pad(proj[:, None, :], ((0, 0), (0, SUBLANES - 1), (0, 0)))
    ya, yb, yc, hlast, ulast, s_out = _mixers_recurrent(
        lw, proj3, _pad_rows_front(lru_conv, SUBLANES), lru_h[:, None, :], _pad_rows_front(sconv, SUBLANES),
        _pad_rows_front(rwkv_shift[:, None, :], SUBLANES), rwkv_s, j4,
        tt=SUBLANES, chunk=SUBLANES, pos0=past_len, t_valid=1)

    def row8(a):
        return jnp.pad(a[:, None, :], ((0, 0), (0, SUBLANES - 1), (0, 0)))

    q4 = jnp.pad(q_r.reshape(nb, N_HEADS, HEAD_D), ((0, 0), (0, SUBLANES - N_HEADS), (0, LANES - HEAD_D)))
    depth, n_phys = cache_cmp.shape[:2]
    n_cmp = (past_len + 1 - CMP_LEN) // CMP_STRIDE + 1
    n_sel = -(-(past_len + 1) // SEL_BLOCK)
    ocmp, top_idx = _samp_cmp(cache_cmp, layer, page_table, lw["cw"], consts["cos_cs"], consts["sin_cs"], j2, q4,
                              consts["ovl_s"], n_cmp, n_sel, t_pos=past_len,
                              pages_per_step=consts["pages_per_step"])
    wlen = win.shape[1]
    yd8 = _samp_sel(cache_slc.reshape(depth, n_phys, PAGE_SIZE // SEL_BLOCK, SEL_BLOCK, LANES), layer, page_table,
                    top_idx[:, :, 0], q4, row8(nslc), consts["win_all"], row8(nwin), row8(gates), ocmp,
                    t_pos=past_len, win_pos0=past_len - wlen)
    x3 = _ffn(x1, lw["norm_ffn"][1], ffn_w, layer, 1, nb,
              mix=(ya[:, 0], yb[:, 0], yc[:, 0], yd8[:, 0], lw["out_norm"], consts["w_out"]))
    xa = proj[:, COL_XA * GROUP_W:(COL_XA + 1) * GROUP_W]
    new_win = jnp.concatenate([win, nwin[:, None, :]], axis=1)
    states = (hlast[:, 0], jnp.concatenate([lru_conv[:, 1:], xa[:, None, :]], axis=1), s_out,
              proj[:, :4 * GROUP_W], jnp.concatenate([sconv[:, 1:], ulast[:, 0:1]], axis=1),
              new_win[:, -min(WINDOW, wlen + 1):], ncmp[:, None, :], nslc[:, None, :])
    return x3, states


def kernel(x_prompt, x_sample, state_lru_h, state_lru_conv, state_rwkv_S, state_rwkv_shift, state_sconv, state_nsa_win, cache_nsa_cmp, cache_nsa_slc, page_table, norm_ffn, ffn_w_gate, ffn_w_up, ffn_w_down, norm_mix, w_in, lru_conv_w, lru_conv_b, lru_w_r, lru_b_r, lru_w_i, lru_b_i, lru_lambda, rwkv_mu, rwkv_w0, rwkv_w2, rwkv_a0, rwkv_a2, rwkv_g2, rwkv_k_k, rwkv_k_a, rwkv_r_k, rwkv_ln_g, rwkv_ln_b, sconv_w, nsa_norm_q, nsa_norm_k, nsa_cmp_pe, nsa_cmp_w1, nsa_cmp_b1, nsa_cmp_w2, nsa_cmp_b2, out_norm, w_out):
    params = dict(norm_ffn=norm_ffn, norm_mix=norm_mix, lru_conv_w=lru_conv_w, lru_conv_b=lru_conv_b, lru_w_r=lru_w_r,
                  lru_b_r=lru_b_r, lru_w_i=lru_w_i, lru_b_i=lru_b_i, lru_lambda=lru_lambda, rwkv_mu=rwkv_mu,
                  rwkv_w0=rwkv_w0, rwkv_w2=rwkv_w2, rwkv_a0=rwkv_a0, rwkv_a2=rwkv_a2, rwkv_g2=rwkv_g2,
                  rwkv_k_k=rwkv_k_k, rwkv_k_a=rwkv_k_a, rwkv_r_k=rwkv_r_k, rwkv_ln_g=rwkv_ln_g,
                  rwkv_ln_b=rwkv_ln_b, sconv_w=sconv_w, nsa_norm_q=nsa_norm_q, nsa_norm_k=nsa_norm_k,
                  nsa_cmp_pe=nsa_cmp_pe, nsa_cmp_w1=nsa_cmp_w1, nsa_cmp_b1=nsa_cmp_b1, nsa_cmp_w2=nsa_cmp_w2,
                  nsa_cmp_b2=nsa_cmp_b2, out_norm=out_norm)
    depth = norm_mix.shape[0]
    bp, tp, d = x_prompt.shape
    bs, ts, _ = x_sample.shape
    n_pages = page_table.shape[1]
    past_len = n_pages * PAGE_SIZE
    assert d == D_MODEL and ts == 1 and tp % 512 == 0 and tp >= WINDOW + 256 and past_len >= WINDOW
    assert bs % SUBLANES == 0 or bs < SUBLANES

    n_grp_p = tp // CMP_STRIDE
    n_sel_p = -(-tp // SEL_BLOCK)
    nsp_p = _round_up(n_sel_p, LANES)
    n_grp_s = past_len // CMP_STRIDE
    n_sel_s = -(-(past_len + 1) // SEL_BLOCK)
    nsp_s = _round_up(n_sel_s, LANES)
    cos_p, sin_p = _rope_tables(jnp.arange(tp))
    cos_s, sin_s = _rope_tables(jnp.full((bs,), past_len))
    cos_cp, sin_cp = _rope_tables(jnp.arange(n_grp_p) * CMP_STRIDE + CMP_LEN - 1)
    cos_cs, sin_cs = _rope_tables(jnp.arange(n_grp_s) * CMP_STRIDE + CMP_LEN - 1)
    expand_p = ((jnp.arange(tp)[:, None] // SEL_BLOCK) == jnp.arange(nsp_p)[None, :]).astype(BF16)
    pages_per_step = math.gcd(n_pages, 64)
    consts = dict(
        j2=_seg_ones(2), j4=_seg_ones(N_HEADS), cos_p=cos_p, sin_p=sin_p, cos_s=cos_s, sin_s=sin_s,
        cos_cp=cos_cp, sin_cp=sin_cp, cos_cs=cos_cs, sin_cs=sin_cs,
        ovl_t_p=_overlap(n_grp_p, nsp_p, n_sel_p).T,
        expand_p=expand_p,
        ovl_s=_overlap(n_grp_s, nsp_s, n_sel_s),
        pages_per_step=pages_per_step,
        ffn_w=(ffn_w_gate.astype(BF16), ffn_w_up.astype(BF16), ffn_w_down.astype(BF16)),
        w_out=w_out.astype(BF16),
        w_in=_w_in_stack(w_in),
        win_all=state_nsa_win,
    )

    yp = x_prompt.reshape(bp * tp, d)
    ys = x_sample.reshape(bs, d)
    sp_all, ss_all = [], []
    for l in range(depth):
        lw = _layer_weights({name: arr[l] for name, arr in params.items()})
        yp, sp = _prompt_layer(lw, l, yp, bp, tp, consts)
        ys, ss = _sample_layer(lw, l, ys, (state_lru_h[l], state_lru_conv[l], state_rwkv_S[l], state_rwkv_shift[l],
                                           state_sconv[l], state_nsa_win[l]),
                               (cache_nsa_cmp, cache_nsa_slc), page_table, consts)
        sp_all.append(sp)
        ss_all.append(ss)
    outs = [yp.reshape(bp, tp, d), ys.reshape(bs, ts, d)]
    for i in range(8):
        outs.append(jnp.stack([s[i] for s in sp_all]))
        outs.append(jnp.stack([s[i] for s in ss_all]))
    return tuple(outs)
```
